```python
import jax, jax.numpy as jnp
from jax import lax
import numpy as np

D_MODEL = 2048
BATCH = 8
SEQ = 8192
DEPTH = 2

MIX_WIDTH = 2 * D_MODEL
A_WIDTH = D_MODEL // 1 if False else MIX_WIDTH // 2
A_GROUPS = 8
A_CHUNK = 128
B_WIDTH = MIX_WIDTH // 2
B_HEAD_DIM = 64
B_HEADS = B_WIDTH // B_HEAD_DIM
B_GROUPS = 8
B_STATE = 128
B_CONV = 4
B_CHUNK = 128
B_XBC = B_WIDTH + 2 * B_GROUPS * B_STATE
C_WIDTH = MIX_WIDTH // 2
C_CONV = 3
D_WIDTH = MIX_WIDTH // 2
D_HEAD_DIM = 128
D_HEADS = D_WIDTH // D_HEAD_DIM
D_PATTERNS = ((128, 1), (512, 4), (2048, 16))

EPS = 1e-5
N_EVEN = (DEPTH + 1) // 2
N_ODD = DEPTH // 2
IN_EVEN = 3 * A_WIDTH + B_WIDTH + B_XBC + B_HEADS
IN_ODD = 4 * C_WIDTH + 4 * D_WIDTH

kernel_name = "hybrid_gmlp_ssd_shortconv_dilated_attn"


def rms_norm(x, g):
    xf = x.astype(jnp.float32)
    y = xf * lax.rsqrt(jnp.mean(xf * xf, axis=-1, keepdims=True) + EPS)
    return (y * g.astype(jnp.float32)).astype(x.dtype)


def layer_norm(x, g, b):
    xf = x.astype(jnp.float32)
    mu = jnp.mean(xf, axis=-1, keepdims=True)
    xc = xf - mu
    y = xc * lax.rsqrt(jnp.mean(xc * xc, axis=-1, keepdims=True) + EPS)
    return (y * g.astype(jnp.float32) + b.astype(jnp.float32)).astype(x.dtype)


def causal_dwconv(x, w):
    K, C = w.shape
    return lax.conv_general_dilated(
        x, w[:, None, :].astype(x.dtype), window_strides=(1,),
        padding=[(K - 1, 0)], dimension_numbers=("NWC", "WIO", "NWC"),
        feature_group_count=C)


def gmlp_branch(h, ln_g, ln_b, ws, bs):
    Bb, S, _ = h.shape
    u, v, z = jnp.split(h, 3, axis=-1)
    v = layer_norm(v, ln_g, ln_b)
    G, Q, _ = ws.shape
    causal = jnp.tril(jnp.ones((Q, Q), dtype=bool))
    ws_c = jnp.where(causal, ws, jnp.zeros_like(ws))
    vc = v.reshape(Bb, S // Q, Q, G, A_WIDTH // G)
    mixed = jnp.einsum("gts,bcsgd->bctgd", ws_c, vc) + bs.T[None, None, :, :, None]
    return jax.nn.silu(z) * (u * mixed.reshape(Bb, S, A_WIDTH))


def segsum(x):
    T = x.shape[-1]
    cs = jnp.cumsum(x, axis=-1)
    seg = cs[..., :, None] - cs[..., None, :]
    return jnp.where(jnp.tril(jnp.ones((T, T), dtype=bool)), seg, -jnp.inf)


def ssd_scan(x, dt, a, bm, cm):
    Bb, S, H, P = x.shape
    G, N = bm.shape[2], bm.shape[3]
    R = H // G
    Q = B_CHUNK
    nc = S // Q
    xdt = (x * dt[..., None]).reshape(Bb, nc, Q, G, R, P)
    adt = (dt * a).reshape(Bb, nc, Q, G, R).transpose(0, 3, 4, 1, 2)
    bc = bm.reshape(Bb, nc, Q, G, N)
    cc = cm.reshape(Bb, nc, Q, G, N)
    a_cs = jnp.cumsum(adt, axis=-1)
    L = jnp.exp(segsum(adt))
    cb = jnp.einsum("bclgn,bcsgn->bcgls", cc, bc)
    y_diag = jnp.einsum("bcgls,bgrcls,bcsgrp->bclgrp", cb, L, xdt)
    decay_states = jnp.exp(a_cs[..., -1:] - a_cs)
    states = jnp.einsum("bclgn,bgrcl,bclgrp->bcgrpn", bc, decay_states, xdt)
    chunk_decay = jnp.exp(a_cs[..., -1])

    def step(hstate, inp):
        s_c, d_c = inp
        return hstate * d_c[..., None, None] + s_c, hstate

    _, prev = lax.scan(step, jnp.zeros_like(states[:, 0]),
                       (jnp.moveaxis(states, 1, 0), jnp.moveaxis(chunk_decay, -1, 0)))
    prev = jnp.moveaxis(prev, 0, 1)
    y_off = jnp.einsum("bclgn,bcgrpn,bgrcl->bclgrp", cc, prev, jnp.exp(a_cs))
    return (y_diag + y_off).reshape(Bb, S, H, P)


def ssd_branch(h, conv_w, conv_b, dt_bias, a_log, d_skip, norm_g):
    Bb, S, _ = h.shape
    z = h[..., :B_WIDTH]
    xbc = h[..., B_WIDTH:B_WIDTH + B_XBC]
    dt_raw = h[..., B_WIDTH + B_XBC:]
    xbc = jax.nn.silu(causal_dwconv(xbc, conv_w) + conv_b.astype(xbc.dtype))
    gn = B_GROUPS * B_STATE
    xs = xbc[..., :B_WIDTH].astype(jnp.float32).reshape(Bb, S, B_HEADS, B_HEAD_DIM)
    bm = xbc[..., B_WIDTH:B_WIDTH + gn].astype(jnp.float32).reshape(Bb, S, B_GROUPS, B_STATE)
    cm = xbc[..., B_WIDTH + gn:].astype(jnp.float32).reshape(Bb, S, B_GROUPS, B_STATE)
    dt = jax.nn.softplus(dt_raw.astype(jnp.float32) + dt_bias.astype(jnp.float32))
    a = -jnp.exp(a_log.astype(jnp.float32))
    y = ssd_scan(xs, dt, a, bm, cm) + d_skip.astype(jnp.float32)[:, None] * xs
    y = y.reshape(Bb, S, B_WIDTH) * jax.nn.silu(z.astype(jnp.float32))
    yg = y.reshape(Bb, S, B_GROUPS, B_WIDTH // B_GROUPS)
    yg = yg * lax.rsqrt(jnp.mean(yg * yg, axis=-1, keepdims=True) + EPS)
    return (yg.reshape(Bb, S, B_WIDTH) * norm_g.astype(jnp.float32)).astype(h.dtype)


def shortconv_branch(h, conv_w):
    bg, cg, hx, z = jnp.split(h, 4, axis=-1)
    return jax.nn.silu(z) * (bg * causal_dwconv(cg * hx, conv_w))


def dilated_window_attention(q, k, v, dil, n_back):
    Bb, S, H, E = q.shape
    M = S // dil
    nb = -(-M // n_back)
    Mp = nb * n_back

    def to_blocks(t):
        t = t.reshape(Bb, M, dil, H, E)
        t = jnp.pad(t, ((0, 0), (0, Mp - M), (0, 0), (0, 0), (0, 0)))
        return t.reshape(Bb, nb, n_back, dil, H, E)

    def with_prev(t):
        prev = jnp.pad(t, ((0, 0), (1, 0), (0, 0), (0, 0), (0, 0), (0, 0)))[:, :-1]
        return jnp.concatenate([prev, t], axis=2)

    qb = to_blocks(q)
    kw = with_prev(to_blocks(k))
    vw = with_prev(to_blocks(v))
    s = jnp.einsum("bnarhe,bnjrhe->bnrhaj", qb, kw,
                   preferred_element_type=jnp.float32) * (E ** -0.5)
    a_idx = jnp.arange(n_back)[:, None]
    j_idx = jnp.arange(2 * n_back)[None, :]
    band = (j_idx >= a_idx) & (j_idx <= a_idx + n_back)
    key_ok = (jnp.arange(nb)[:, None] > 0) | (j_idx >= n_back)
    mask = (band[None] & key_ok[:, None, :])[None, :, None, None]
    s = jnp.where(mask, s, -jnp.inf)
    mx = jnp.max(s, axis=-1, keepdims=True)
    p = jnp.exp(s - mx)
    l = jnp.sum(p, axis=-1, keepdims=True)
    o = jnp.einsum("bnrhaj,bnjrhe->bnarhe", p / l, vw.astype(jnp.float32))
    lse = (mx + jnp.log(l))[..., 0].transpose(0, 1, 4, 2, 3)
    o = o.reshape(Bb, Mp, dil, H, E)[:, :M].reshape(Bb, S, H, E)
    lse = lse.reshape(Bb, Mp, dil, H)[:, :M].reshape(Bb, S, H)
    return o, lse


def dilated_attention_branch(h):
    Bb, S, _ = h.shape
    q, k, v, z = jnp.split(h, 4, axis=-1)
    q = q.reshape(Bb, S, D_HEADS, D_HEAD_DIM)
    k = k.reshape(Bb, S, D_HEADS, D_HEAD_DIM)
    v = v.reshape(Bb, S, D_HEADS, D_HEAD_DIM)
    outs, lses = [], []
    for window, dil in D_PATTERNS:
        o, lse = dilated_window_attention(q, k, v, dil, window // dil)
        outs.append(o)
        lses.append(lse)
    wts = jax.nn.softmax(jnp.stack(lses, axis=0), axis=0)
    o = jnp.sum(wts[..., None] * jnp.stack(outs, axis=0), axis=0)
    return jax.nn.silu(z) * o.reshape(Bb, S, D_WIDTH).astype(h.dtype)


def even_layer(x, norm_g, w_in, ln_g, ln_b, ws, bs, conv_w, conv_b, dt_bias,
               a_log, d_skip, ssd_norm_g, w_out):
    h = jnp.einsum("bsd,df->bsf", rms_norm(x, norm_g), w_in)
    ya = gmlp_branch(h[..., :3 * A_WIDTH], ln_g, ln_b, ws, bs)
    yb = ssd_branch(h[..., 3 * A_WIDTH:], conv_w, conv_b, dt_bias, a_log, d_skip, ssd_norm_g)
    y = jnp.concatenate([ya, yb.astype(ya.dtype)], axis=-1)
    return x + jnp.einsum("bsf,fd->bsd", y, w_out).astype(x.dtype)


def odd_layer(x, norm_g, w_in, sconv_w, w_out):
    h = jnp.einsum("bsd,df->bsf", rms_norm(x, norm_g), w_in)
    yc = shortconv_branch(h[..., :4 * C_WIDTH], sconv_w)
    yd = dilated_attention_branch(h[..., 4 * C_WIDTH:])
    y = jnp.concatenate([yc, yd.astype(yc.dtype)], axis=-1)
    return x + jnp.einsum("bsf,fd->bsd", y, w_out).astype(x.dtype)


def _fwd_setup_inputs(seed: int = 0) -> dict:
    key = jax.random.key(seed)
    ks = jax.random.split(key, 20)
    f32 = jnp.float32
    nrm = lambda k, shape, scale: jax.random.normal(k, shape, f32) * scale
    x = jax.random.normal(ks[0], (BATCH, SEQ, D_MODEL), f32)
    even_norm_g = 1.0 + nrm(ks[1], (N_EVEN, D_MODEL), 0.02)
    even_w_in = nrm(ks[2], (N_EVEN, D_MODEL, IN_EVEN), D_MODEL ** -0.5)
    gmlp_ln_g = 1.0 + nrm(ks[3], (N_EVEN, A_WIDTH), 0.02)
    gmlp_ln_b = nrm(ks[4], (N_EVEN, A_WIDTH), 0.02)
    gmlp_ws = nrm(ks[5], (N_EVEN, A_GROUPS, A_CHUNK, A_CHUNK), A_CHUNK ** -0.5)
    gmlp_bs = 1.0 + nrm(ks[6], (N_EVEN, A_GROUPS, A_CHUNK), 0.1)
    ssd_conv_w = nrm(ks[7], (N_EVEN, B_CONV, B_XBC), B_CONV ** -0.5)
    ssd_conv_b = nrm(ks[8], (N_EVEN, B_XBC), 0.02)
    dt0 = jnp.exp(jax.random.uniform(ks[9], (N_EVEN, B_HEADS), f32,
                                     np.log(1e-3).astype(np.float32), np.log(1e-1).astype(np.float32)))
    ssd_dt_bias = dt0 + jnp.log(-jnp.expm1(-dt0))
    ssd_a_log = jnp.log(jax.random.uniform(ks[10], (N_EVEN, B_HEADS), f32, 1.0, 16.0))
    ssd_d = 1.0 + nrm(ks[11], (N_EVEN, B_HEADS), 0.1)
    ssd_norm_g = 1.0 + nrm(ks[12], (N_EVEN, B_WIDTH), 0.02)
    even_w_out = nrm(ks[13], (N_EVEN, MIX_WIDTH, D_MODEL), MIX_WIDTH ** -0.5)
    odd_norm_g = 1.0 + nrm(ks[14], (N_ODD, D_MODEL), 0.02)
    odd_w_in = nrm(ks[15], (N_ODD, D_MODEL, IN_ODD), D_MODEL ** -0.5)
    sconv_w = nrm(ks[16], (N_ODD, C_CONV, C_WIDTH), C_CONV ** -0.5)
    odd_w_out = nrm(ks[17], (N_ODD, MIX_WIDTH, D_MODEL), MIX_WIDTH ** -0.5)
    final_norm_g = 1.0 + nrm(ks[18], (D_MODEL,), 0.02)
    return {"x": x, "even_norm_g": even_norm_g, "even_w_in": even_w_in,
            "gmlp_ln_g": gmlp_ln_g, "gmlp_ln_b": gmlp_ln_b, "gmlp_ws": gmlp_ws,
            "gmlp_bs": gmlp_bs, "ssd_conv_w": ssd_conv_w, "ssd_conv_b": ssd_conv_b,
            "ssd_dt_bias": ssd_dt_bias, "ssd_a_log": ssd_a_log, "ssd_d": ssd_d,
            "ssd_norm_g": ssd_norm_g, "even_w_out": even_w_out,
            "odd_norm_g": odd_norm_g, "odd_w_in": odd_w_in, "sconv_w": sconv_w,
            "odd_w_out": odd_w_out, "final_norm_g": final_norm_g}


def _fwd_reference(x, even_norm_g, even_w_in, gmlp_ln_g, gmlp_ln_b, gmlp_ws, gmlp_bs,
              ssd_conv_w, ssd_conv_b, ssd_dt_bias, ssd_a_log, ssd_d, ssd_norm_g,
              even_w_out, odd_norm_g, odd_w_in, sconv_w, odd_w_out, final_norm_g):
    for layer in range(DEPTH):
        i = layer // 2
        if layer % 2 == 0:
            x = even_layer(x, even_norm_g[i], even_w_in[i], gmlp_ln_g[i], gmlp_ln_b[i],
                           gmlp_ws[i], gmlp_bs[i], ssd_conv_w[i], ssd_conv_b[i],
                           ssd_dt_bias[i], ssd_a_log[i], ssd_d[i], ssd_norm_g[i],
                           even_w_out[i])
        else:
            x = odd_layer(x, odd_norm_g[i], odd_w_in[i], sconv_w[i], odd_w_out[i])
    return rms_norm(x, final_norm_g)


import jax as _jax
import jax.numpy as _jnp

TWIN_FORMAT = 'train_step'
FWD_PARAMS = ['x', 'even_norm_g', 'even_w_in', 'gmlp_ln_g', 'gmlp_ln_b', 'gmlp_ws', 'gmlp_bs', 'ssd_conv_w', 'ssd_conv_b', 'ssd_dt_bias', 'ssd_a_log', 'ssd_d', 'ssd_norm_g', 'even_w_out', 'odd_norm_g', 'odd_w_in', 'sconv_w', 'odd_w_out', 'final_norm_g']
TWIN_WEIGHTS = ['even_norm_g', 'even_w_in', 'gmlp_ln_g', 'gmlp_ln_b', 'gmlp_ws', 'gmlp_bs', 'ssd_conv_w', 'ssd_conv_b', 'ssd_dt_bias', 'ssd_a_log', 'ssd_d', 'ssd_norm_g', 'even_w_out', 'odd_norm_g', 'odd_w_in', 'sconv_w', 'odd_w_out', 'final_norm_g']
TWIN_DIFF_INPUT = 'x'
TWIN_INPUTS = ['x', 'even_norm_g', 'even_w_in', 'gmlp_ln_g', 'gmlp_ln_b', 'gmlp_ws', 'gmlp_bs', 'ssd_conv_w', 'ssd_conv_b', 'ssd_dt_bias', 'ssd_a_log', 'ssd_d', 'ssd_norm_g', 'even_w_out', 'odd_norm_g', 'odd_w_in', 'sconv_w', 'odd_w_out', 'final_norm_g', 'loss_target', 'm_even_norm_g', 'm_even_w_in', 'm_gmlp_ln_g', 'm_gmlp_ln_b', 'm_gmlp_ws', 'm_gmlp_bs', 'm_ssd_conv_w', 'm_ssd_conv_b', 'm_ssd_dt_bias', 'm_ssd_a_log', 'm_ssd_d', 'm_ssd_norm_g', 'm_even_w_out', 'm_odd_norm_g', 'm_odd_w_in', 'm_sconv_w', 'm_odd_w_out', 'm_final_norm_g', 'v_even_norm_g', 'v_even_w_in', 'v_gmlp_ln_g', 'v_gmlp_ln_b', 'v_gmlp_ws', 'v_gmlp_bs', 'v_ssd_conv_w', 'v_ssd_conv_b', 'v_ssd_dt_bias', 'v_ssd_a_log', 'v_ssd_d', 'v_ssd_norm_g', 'v_even_w_out', 'v_odd_norm_g', 'v_odd_w_in', 'v_sconv_w', 'v_odd_w_out', 'v_final_norm_g']
TWIN_OUTPUTS = ['loss', 'grad_x', 'grad_even_norm_g', 'grad_even_w_in', 'grad_gmlp_ln_g', 'grad_gmlp_ln_b', 'grad_gmlp_ws', 'grad_gmlp_bs', 'grad_ssd_conv_w', 'grad_ssd_conv_b', 'grad_ssd_dt_bias', 'grad_ssd_a_log', 'grad_ssd_d', 'grad_ssd_norm_g', 'grad_even_w_out', 'grad_odd_norm_g', 'grad_odd_w_in', 'grad_sconv_w', 'grad_odd_w_out', 'grad_final_norm_g', 'delta_even_norm_g', 'delta_even_w_in', 'delta_gmlp_ln_g', 'delta_gmlp_ln_b', 'delta_gmlp_ws', 'delta_gmlp_bs', 'delta_ssd_conv_w', 'delta_ssd_conv_b', 'delta_ssd_dt_bias', 'delta_ssd_a_log', 'delta_ssd_d', 'delta_ssd_norm_g', 'delta_even_w_out', 'delta_odd_norm_g', 'delta_odd_w_in', 'delta_sconv_w', 'delta_odd_w_out', 'delta_final_norm_g', 'new_m_even_norm_g', 'new_m_even_w_in', 'new_m_gmlp_ln_g', 'new_m_gmlp_ln_b', 'new_m_gmlp_ws', 'new_m_gmlp_bs', 'new_m_ssd_conv_w', 'new_m_ssd_conv_b', 'new_m_ssd_dt_bias', 'new_m_ssd_a_log', 'new_m_ssd_d', 'new_m_ssd_norm_g', 'new_m_even_w_out', 'new_m_odd_norm_g', 'new_m_odd_w_in', 'new_m_sconv_w', 'new_m_odd_w_out', 'new_m_final_norm_g', 'new_v_even_norm_g', 'new_v_even_w_in', 'new_v_gmlp_ln_g', 'new_v_gmlp_ln_b', 'new_v_gmlp_ws', 'new_v_gmlp_bs', 'new_v_ssd_conv_w', 'new_v_ssd_conv_b', 'new_v_ssd_dt_bias', 'new_v_ssd_a_log', 'new_v_ssd_d', 'new_v_ssd_norm_g', 'new_v_even_w_out', 'new_v_odd_norm_g', 'new_v_odd_w_in', 'new_v_sconv_w', 'new_v_odd_w_out', 'new_v_final_norm_g']
TWIN_LEAF_KINDS = {'loss': 'loss', 'grad_x': 'grad_x', 'grad_even_norm_g': 'grad_w', 'grad_even_w_in': 'grad_w', 'grad_gmlp_ln_g': 'grad_w', 'grad_gmlp_ln_b': 'grad_w', 'grad_gmlp_ws': 'grad_w', 'grad_gmlp_bs': 'grad_w', 'grad_ssd_conv_w': 'grad_w', 'grad_ssd_conv_b': 'grad_w', 'grad_ssd_dt_bias': 'grad_w', 'grad_ssd_a_log': 'grad_w', 'grad_ssd_d': 'grad_w', 'grad_ssd_norm_g': 'grad_w', 'grad_even_w_out': 'grad_w', 'grad_odd_norm_g': 'grad_w', 'grad_odd_w_in': 'grad_w', 'grad_sconv_w': 'grad_w', 'grad_odd_w_out': 'grad_w', 'grad_final_norm_g': 'grad_w', 'delta_even_norm_g': 'delta_w', 'delta_even_w_in': 'delta_w', 'delta_gmlp_ln_g': 'delta_w', 'delta_gmlp_ln_b': 'delta_w', 'delta_gmlp_ws': 'delta_w', 'delta_gmlp_bs': 'delta_w', 'delta_ssd_conv_w': 'delta_w', 'delta_ssd_conv_b': 'delta_w', 'delta_ssd_dt_bias': 'delta_w', 'delta_ssd_a_log': 'delta_w', 'delta_ssd_d': 'delta_w', 'delta_ssd_norm_g': 'delta_w', 'delta_even_w_out': 'delta_w', 'delta_odd_norm_g': 'delta_w', 'delta_odd_w_in': 'delta_w', 'delta_sconv_w': 'delta_w', 'delta_odd_w_out': 'delta_w', 'delta_final_norm_g': 'delta_w', 'new_m_even_norm_g': 'new_m', 'new_m_even_w_in': 'new_m', 'new_m_gmlp_ln_g': 'new_m', 'new_m_gmlp_ln_b': 'new_m', 'new_m_gmlp_ws': 'new_m', 'new_m_gmlp_bs': 'new_m', 'new_m_ssd_conv_w': 'new_m', 'new_m_ssd_conv_b': 'new_m', 'new_m_ssd_dt_bias': 'new_m', 'new_m_ssd_a_log': 'new_m', 'new_m_ssd_d': 'new_m', 'new_m_ssd_norm_g': 'new_m', 'new_m_even_w_out': 'new_m', 'new_m_odd_norm_g': 'new_m', 'new_m_odd_w_in': 'new_m', 'new_m_sconv_w': 'new_m', 'new_m_odd_w_out': 'new_m', 'new_m_final_norm_g': 'new_m', 'new_v_even_norm_g': 'new_v', 'new_v_even_w_in': 'new_v', 'new_v_gmlp_ln_g': 'new_v', 'new_v_gmlp_ln_b': 'new_v', 'new_v_gmlp_ws': 'new_v', 'new_v_gmlp_bs': 'new_v', 'new_v_ssd_conv_w': 'new_v', 'new_v_ssd_conv_b': 'new_v', 'new_v_ssd_dt_bias': 'new_v', 'new_v_ssd_a_log': 'new_v', 'new_v_ssd_d': 'new_v', 'new_v_ssd_norm_g': 'new_v', 'new_v_even_w_out': 'new_v', 'new_v_odd_norm_g': 'new_v', 'new_v_odd_w_in': 'new_v', 'new_v_sconv_w': 'new_v', 'new_v_odd_w_out': 'new_v', 'new_v_final_norm_g': 'new_v'}


def _forward(args):
    return _fwd_reference(*[args[k] for k in FWD_PARAMS])


def _output_shape():
    def fwd():
        inp = _fwd_setup_inputs(0)
        return _fwd_reference(*[inp[k] for k in FWD_PARAMS])
    out = _jax.eval_shape(fwd)
    return out.shape, out.dtype

N_MICROBATCH = 1
ADAM_LR = 0.001
ADAM_B1 = 0.9
ADAM_B2 = 0.999
ADAM_EPS = 1e-08
ADAM_WD = 0.01
ADAM_STEP = 10
PER_EXAMPLE_BATCH_AXIS = {'x': 0, 'loss_target': 0}
SHARED_INPUTS = []
_WEIGHT_DTYPES = {'even_norm_g': _jnp.float32, 'even_w_in': _jnp.float32, 'gmlp_ln_g': _jnp.float32, 'gmlp_ln_b': _jnp.float32, 'gmlp_ws': _jnp.float32, 'gmlp_bs': _jnp.float32, 'ssd_conv_w': _jnp.float32, 'ssd_conv_b': _jnp.float32, 'ssd_dt_bias': _jnp.float32, 'ssd_a_log': _jnp.float32, 'ssd_d': _jnp.float32, 'ssd_norm_g': _jnp.float32, 'even_w_out': _jnp.float32, 'odd_norm_g': _jnp.float32, 'odd_w_in': _jnp.float32, 'sconv_w': _jnp.float32, 'odd_w_out': _jnp.float32, 'final_norm_g': _jnp.float32}
MOMENT_SCALE = {'even_norm_g': 1.430736e-01, 'even_w_in': 5.786213e-02, 'gmlp_ln_g': 3.271292e-02, 'gmlp_ln_b': 3.048251e-02, 'gmlp_ws': 4.506280e-02, 'gmlp_bs': 7.401142e-02, 'ssd_conv_w': 5.713238e-02, 'ssd_conv_b': 7.738843e-02, 'ssd_dt_bias': 3.989004e-01, 'ssd_a_log': 1.916311e-01, 'ssd_d': 5.620882e-01, 'ssd_norm_g': 7.627340e-02, 'even_w_out': 9.405674e-02, 'odd_norm_g': 7.708111e-02, 'odd_w_in': 2.719009e-02, 'sconv_w': 3.846777e-02, 'odd_w_out': 3.847343e-02, 'final_norm_g': 3.199052e+01}


def _to_microbatches(a, axis):
    t = _jnp.moveaxis(a, axis, 0)
    t = t.reshape((N_MICROBATCH, t.shape[0] // N_MICROBATCH) + t.shape[1:])
    return _jnp.moveaxis(t, 1, axis + 1)


def setup_inputs(seed: int = 0) -> dict:
    inp = _fwd_setup_inputs(seed)
    key = _jax.random.fold_in(_jax.random.key(seed), 7919)
    shape, _ = _output_shape()
    out = dict(inp)
    out["loss_target"] = _jax.random.normal(_jax.random.fold_in(key, 0), shape, _jnp.float32)
    for i, name in enumerate(TWIN_WEIGHTS):
        w = inp[name].astype(_jnp.float32)
        if MOMENT_SCALE is None:
            s = _jnp.sqrt(_jnp.mean(_jnp.square(w)) + 1e-30)
        else:
            s = MOMENT_SCALE[name]
        km, kv = _jax.random.split(_jax.random.fold_in(key, i + 1))
        out[name] = w
        out["m_" + name] = s * _jax.random.normal(km, w.shape, _jnp.float32)
        out["v_" + name] = (s * s) * _jax.random.uniform(kv, w.shape, _jnp.float32, 0.5, 1.5)
    if N_MICROBATCH > 1:
        for name, axis in PER_EXAMPLE_BATCH_AXIS.items():
            out[name] = _to_microbatches(out[name], axis)
    return {'x': out['x'], 'even_norm_g': out['even_norm_g'], 'even_w_in': out['even_w_in'], 'gmlp_ln_g': out['gmlp_ln_g'], 'gmlp_ln_b': out['gmlp_ln_b'], 'gmlp_ws': out['gmlp_ws'], 'gmlp_bs': out['gmlp_bs'], 'ssd_conv_w': out['ssd_conv_w'], 'ssd_conv_b': out['ssd_conv_b'], 'ssd_dt_bias': out['ssd_dt_bias'], 'ssd_a_log': out['ssd_a_log'], 'ssd_d': out['ssd_d'], 'ssd_norm_g': out['ssd_norm_g'], 'even_w_out': out['even_w_out'], 'odd_norm_g': out['odd_norm_g'], 'odd_w_in': out['odd_w_in'], 'sconv_w': out['sconv_w'], 'odd_w_out': out['odd_w_out'], 'final_norm_g': out['final_norm_g'], 'loss_target': out['loss_target'], 'm_even_norm_g': out['m_even_norm_g'], 'm_even_w_in': out['m_even_w_in'], 'm_gmlp_ln_g': out['m_gmlp_ln_g'], 'm_gmlp_ln_b': out['m_gmlp_ln_b'], 'm_gmlp_ws': out['m_gmlp_ws'], 'm_gmlp_bs': out['m_gmlp_bs'], 'm_ssd_conv_w': out['m_ssd_conv_w'], 'm_ssd_conv_b': out['m_ssd_conv_b'], 'm_ssd_dt_bias': out['m_ssd_dt_bias'], 'm_ssd_a_log': out['m_ssd_a_log'], 'm_ssd_d': out['m_ssd_d'], 'm_ssd_norm_g': out['m_ssd_norm_g'], 'm_even_w_out': out['m_even_w_out'], 'm_odd_norm_g': out['m_odd_norm_g'], 'm_odd_w_in': out['m_odd_w_in'], 'm_sconv_w': out['m_sconv_w'], 'm_odd_w_out': out['m_odd_w_out'], 'm_final_norm_g': out['m_final_norm_g'], 'v_even_norm_g': out['v_even_norm_g'], 'v_even_w_in': out['v_even_w_in'], 'v_gmlp_ln_g': out['v_gmlp_ln_g'], 'v_gmlp_ln_b': out['v_gmlp_ln_b'], 'v_gmlp_ws': out['v_gmlp_ws'], 'v_gmlp_bs': out['v_gmlp_bs'], 'v_ssd_conv_w': out['v_ssd_conv_w'], 'v_ssd_conv_b': out['v_ssd_conv_b'], 'v_ssd_dt_bias': out['v_ssd_dt_bias'], 'v_ssd_a_log': out['v_ssd_a_log'], 'v_ssd_d': out['v_ssd_d'], 'v_ssd_norm_g': out['v_ssd_norm_g'], 'v_even_w_out': out['v_even_w_out'], 'v_odd_norm_g': out['v_odd_norm_g'], 'v_odd_w_in': out['v_odd_w_in'], 'v_sconv_w': out['v_sconv_w'], 'v_odd_w_out': out['v_odd_w_out'], 'v_final_norm_g': out['v_final_norm_g']}


def _loss(weights, diff, rest, loss_target):
    with _jax.named_scope("forward"):
        args = {**rest, TWIN_DIFF_INPUT: diff, **{k: w.astype(_WEIGHT_DTYPES[k]) for k, w in weights.items()}}
        y = _forward(args)
    with _jax.named_scope("loss_head"):
        err = _jnp.square(y.astype(_jnp.float32) - loss_target)
        return 0.5 * _jnp.sum(_jnp.mean(err, axis=-1)) if err.ndim else 0.5 * err


def _adamw(w, g, m, v):
    m = ADAM_B1 * m + (1.0 - ADAM_B1) * g
    v = ADAM_B2 * v + (1.0 - ADAM_B2) * _jnp.square(g)
    m_hat = m / (1.0 - ADAM_B1 ** ADAM_STEP)
    v_hat = v / (1.0 - ADAM_B2 ** ADAM_STEP)
    delta = -ADAM_LR * (m_hat / (_jnp.sqrt(v_hat) + ADAM_EPS) + ADAM_WD * w)
    return delta, m, v


def reference(x, even_norm_g, even_w_in, gmlp_ln_g, gmlp_ln_b, gmlp_ws, gmlp_bs, ssd_conv_w, ssd_conv_b, ssd_dt_bias, ssd_a_log, ssd_d, ssd_norm_g, even_w_out, odd_norm_g, odd_w_in, sconv_w, odd_w_out, final_norm_g, loss_target, m_even_norm_g, m_even_w_in, m_gmlp_ln_g, m_gmlp_ln_b, m_gmlp_ws, m_gmlp_bs, m_ssd_conv_w, m_ssd_conv_b, m_ssd_dt_bias, m_ssd_a_log, m_ssd_d, m_ssd_norm_g, m_even_w_out, m_odd_norm_g, m_odd_w_in, m_sconv_w, m_odd_w_out, m_final_norm_g, v_even_norm_g, v_even_w_in, v_gmlp_ln_g, v_gmlp_ln_b, v_gmlp_ws, v_gmlp_bs, v_ssd_conv_w, v_ssd_conv_b, v_ssd_dt_bias, v_ssd_a_log, v_ssd_d, v_ssd_norm_g, v_even_w_out, v_odd_norm_g, v_odd_w_in, v_sconv_w, v_odd_w_out, v_final_norm_g):
    given = dict(x=x, even_norm_g=even_norm_g, even_w_in=even_w_in, gmlp_ln_g=gmlp_ln_g, gmlp_ln_b=gmlp_ln_b, gmlp_ws=gmlp_ws, gmlp_bs=gmlp_bs, ssd_conv_w=ssd_conv_w, ssd_conv_b=ssd_conv_b, ssd_dt_bias=ssd_dt_bias, ssd_a_log=ssd_a_log, ssd_d=ssd_d, ssd_norm_g=ssd_norm_g, even_w_out=even_w_out, odd_norm_g=odd_norm_g, odd_w_in=odd_w_in, sconv_w=sconv_w, odd_w_out=odd_w_out, final_norm_g=final_norm_g, loss_target=loss_target, m_even_norm_g=m_even_norm_g, m_even_w_in=m_even_w_in, m_gmlp_ln_g=m_gmlp_ln_g, m_gmlp_ln_b=m_gmlp_ln_b, m_gmlp_ws=m_gmlp_ws, m_gmlp_bs=m_gmlp_bs, m_ssd_conv_w=m_ssd_conv_w, m_ssd_conv_b=m_ssd_conv_b, m_ssd_dt_bias=m_ssd_dt_bias, m_ssd_a_log=m_ssd_a_log, m_ssd_d=m_ssd_d, m_ssd_norm_g=m_ssd_norm_g, m_even_w_out=m_even_w_out, m_odd_norm_g=m_odd_norm_g, m_odd_w_in=m_odd_w_in, m_sconv_w=m_sconv_w, m_odd_w_out=m_odd_w_out, m_final_norm_g=m_final_norm_g, v_even_norm_g=v_even_norm_g, v_even_w_in=v_even_w_in, v_gmlp_ln_g=v_gmlp_ln_g, v_gmlp_ln_b=v_gmlp_ln_b, v_gmlp_ws=v_gmlp_ws, v_gmlp_bs=v_gmlp_bs, v_ssd_conv_w=v_ssd_conv_w, v_ssd_conv_b=v_ssd_conv_b, v_ssd_dt_bias=v_ssd_dt_bias, v_ssd_a_log=v_ssd_a_log, v_ssd_d=v_ssd_d, v_ssd_norm_g=v_ssd_norm_g, v_even_w_out=v_even_w_out, v_odd_norm_g=v_odd_norm_g, v_odd_w_in=v_odd_w_in, v_sconv_w=v_sconv_w, v_odd_w_out=v_odd_w_out, v_final_norm_g=v_final_norm_g)
    weights = {n: given[n] for n in TWIN_WEIGHTS}
    shared = {n: given[n] for n in SHARED_INPUTS}
    per_example = {n: given[n] for n in ['x']}
    grad_fn = _jax.value_and_grad(_loss, argnums=(0, 1))

    def one_microbatch(ex, loss_target):
        ex = dict(ex)
        diff = ex.pop(TWIN_DIFF_INPUT)
        return grad_fn(weights, diff, {**shared, **ex}, loss_target)

    if N_MICROBATCH == 1:
        loss, (grad_w, grad_x) = one_microbatch(per_example, given["loss_target"])
    else:
        def body(carry, xs):
            loss_sum, grad_sum = carry
            l_k, (gw_k, gx_k) = one_microbatch(xs[0], xs[1])
            with _jax.named_scope("update"):
                return (loss_sum + l_k, _jax.tree.map(_jnp.add, grad_sum, gw_k)), gx_k

        init = (_jnp.zeros((), _jnp.float32), _jax.tree.map(_jnp.zeros_like, weights))
        (loss, grad_w), grad_x = _jax.lax.scan(body, init, (per_example, given["loss_target"]))
    with _jax.named_scope("update"):
        delta_w, new_m, new_v = {}, {}, {}
        for n in TWIN_WEIGHTS:
            delta_w[n], new_m[n], new_v[n] = _adamw(weights[n], grad_w[n], given["m_" + n], given["v_" + n])
    return (loss, grad_x, *[grad_w[n] for n in TWIN_WEIGHTS], *[delta_w[n] for n in TWIN_WEIGHTS],
            *[new_m[n] for n in TWIN_WEIGHTS], *[new_v[n] for n in TWIN_WEIGHTS])
```

```python
import functools

import jax
import jax.numpy as jnp
from jax import lax
from jax.experimental import pallas as pl
from jax.experimental.pallas import tpu as pltpu

F32, BF16 = jnp.float32, jnp.bfloat16
MESH = pl.DeviceIdType.MESH
ANY = pl.BlockSpec(memory_space=pl.ANY)

N_DEV = 8
D_MODEL = 2048
WIDTH = 2048
CHUNK = 128
A_GROUPS = 8
B_HEADS, B_HEAD_DIM, B_GROUPS, B_STATE, B_CONV = 32, 64, 8, 128, 4
B_GROUP_W = WIDTH // B_GROUPS
B_XBC = WIDTH + 2 * B_GROUPS * B_STATE
C_CONV = 3
D_HEADS, D_HEAD_DIM = 16, 128
D_PATTERNS = ((128, 1), (512, 4), (2048, 16))
D_BACK = 2048 // CHUNK
F_EVEN = 3 * WIDTH + WIDTH + B_XBC
F_EVEN_ALL = F_EVEN + B_HEADS
F_ODD = 8 * WIDTH
EPS = 1e-5
NEG = -1e30

ADAM_LR, ADAM_B1, ADAM_B2, ADAM_EPS, ADAM_WD, ADAM_STEP = 0.001, 0.9, 0.999, 1e-08, 0.01, 10

VMEM_LIMIT_V7X = 56 * 1024 * 1024
SUBLANES, LANES = 8, 128


def _cp(*sem):
    return pltpu.CompilerParams(dimension_semantics=sem, vmem_limit_bytes=VMEM_LIMIT_V7X)


def _sig(x):
    return 1.0 / (1.0 + jnp.exp(-x))


def _silu(x):
    return x * _sig(x)


def _dsilu(x):
    s = _sig(x)
    return s * (1.0 + x * (1.0 - s))


def _softplus(x):
    return jnp.maximum(x, 0.0) + jnp.log(1.0 + jnp.exp(-jnp.abs(x)))


def _dot(a, b):
    return jnp.dot(a.astype(BF16), b.astype(BF16), preferred_element_type=F32)


def _dot_nt(a, b):
    return lax.dot_general(a.astype(BF16), b.astype(BF16), (((1,), (1,)), ((), ())), preferred_element_type=F32)


def _dot_tn(a, b):
    return lax.dot_general(a.astype(BF16), b.astype(BF16), (((0,), (0,)), ((), ())), preferred_element_type=F32)


def _split3(x):
    hi = x.astype(BF16)
    r1 = x - hi.astype(F32)
    mid = r1.astype(BF16)
    lo = (r1 - mid.astype(F32)).astype(BF16)
    return hi, mid, lo


def _dot_sel(x, sel):
    return sum(jnp.dot(p, sel, preferred_element_type=F32) for p in _split3(x))


def _sel_dot(sel, x):
    return sum(jnp.dot(sel, p, preferred_element_type=F32) for p in _split3(x))


def _iota(shape, axis):
    return lax.broadcasted_iota(jnp.int32, shape, axis)


def _shift_down(cur, halo, j):
    if j == 0:
        return cur
    r = pltpu.roll(cur, j, 0)
    top = jnp.where(_iota(halo.shape, 0) < j, pltpu.roll(halo, j, 0), r[0:SUBLANES])
    return jnp.concatenate([top, r[SUBLANES:]], axis=0)


def _shift_up(cur, halo, j):
    if j == 0:
        return cur
    n = cur.shape[0]
    r = pltpu.roll(cur, n - j, 0)
    bot = jnp.where(_iota(halo.shape, 0) >= SUBLANES - j, pltpu.roll(halo, SUBLANES - j, 0), r[n - SUBLANES:])
    return jnp.concatenate([r[:n - SUBLANES], bot], axis=0)


def _mm(a_srcs, b, *, ta=False, tb=False, n, tm, tn, tk, out_dtype, out_rows=None, add=None, name):
    nsrc = len(a_srcs)
    feat = [a.shape[-1] for a, _ in a_srcs]
    rows = a_srcs[0][0].shape[-2]
    m, k_len = (sum(feat), rows) if ta else (rows, sum(feat))
    ftile = tm if ta else tk
    cnt = [f // ftile for f in feat]
    start = [sum(cnt[:s]) for s in range(nsrc)]
    nk = k_len // tk
    grid = (m // tm, n // tn, nk)

    def a_spec(s):
        lead, st, c = a_srcs[s][1], start[s], cnt[s]

        def im(i, j, k):
            f = jnp.clip((i if ta else k) - st, 0, c - 1) if nsrc > 1 else (i if ta else k)
            pos = (k, f) if ta else (i, f)
            return pos if lead is None else (lead,) + pos

        blk = (tk, tm) if ta else (tm, tk)
        return pl.BlockSpec(blk if lead is None else (None,) + blk, im)

    b_spec = pl.BlockSpec((tn, tk), lambda i, j, k: (j, k)) if tb else pl.BlockSpec((tk, tn), lambda i, j, k: (k, j))
    io_spec = pl.BlockSpec((tm, tn), lambda i, j, k: (i, j))
    dims = (((0 if ta else 1,), (1 if tb else 0,)), ((), ()))

    def body(*refs):
        a_refs, b_ref = refs[:nsrc], refs[nsrc]
        add_ref = refs[nsrc + 1] if add is not None else None
        o_ref = refs[nsrc + 1 + (add is not None)]
        acc_ref = refs[-1] if nk > 1 else None
        k = pl.program_id(2)
        f = pl.program_id(0) if ta else k

        def finish(r):
            if add_ref is not None:
                r = r + add_ref[...]
            o_ref[...] = r.astype(out_dtype)

        def contribute(a_ref):
            p = lax.dot_general(a_ref[...].astype(BF16), b_ref[...].astype(BF16), dims, preferred_element_type=F32)
            if nk == 1:
                finish(p)
            else:
                @pl.when(k == 0)
                def _():
                    acc_ref[...] = p

                @pl.when(k > 0)
                def _():
                    acc_ref[...] += p

        for s in range(nsrc):
            if nsrc == 1:
                contribute(a_refs[s])
            else:
                pl.when((f >= start[s]) & (f < start[s] + cnt[s]))(functools.partial(contribute, a_refs[s]))
        if nk > 1:
            @pl.when(k == nk - 1)
            def _():
                finish(acc_ref[...])

    return pl.pallas_call(
        body, grid=grid, name=name,
        in_specs=[a_spec(s) for s in range(nsrc)] + [b_spec] + ([io_spec] if add is not None else []),
        out_specs=io_spec,
        out_shape=jax.ShapeDtypeStruct((out_rows or m, n), out_dtype),
        scratch_shapes=[pltpu.VMEM((tm, tn), F32)] if nk > 1 else [],
        compiler_params=_cp("parallel", "parallel", "arbitrary"),
    )(*[a for a, _ in a_srcs], b, *([add] if add is not None else []))


def _rms_fwd(x, g, name):
    t, tb = x.shape[0], 512

    def body(x_ref, g_ref, o_ref):
        xv = x_ref[...]
        r = lax.rsqrt(jnp.mean(xv * xv, axis=-1, keepdims=True) + EPS)
        o_ref[...] = (xv * r * g_ref[...]).astype(BF16)

    row = pl.BlockSpec((tb, D_MODEL), lambda i: (i, 0))
    return pl.pallas_call(
        body, grid=(t // tb,), name=name,
        in_specs=[row, pl.BlockSpec((1, D_MODEL), lambda i: (0, 0))], out_specs=row,
        out_shape=jax.ShapeDtypeStruct((t, D_MODEL), BF16), compiler_params=_cp("parallel"),
    )(x, g)


def _rms_bwd(x, g, dxn, dres, name):
    t, tb = x.shape[0], 256

    def body(x_ref, g_ref, dxn_ref, dres_ref, dx_ref, dg_ref):
        xv = x_ref[...]
        r = lax.rsqrt(jnp.mean(xv * xv, axis=-1, keepdims=True) + EPS)
        nv = xv * r
        dy = dxn_ref[...]
        dn = dy * g_ref[...]
        dx_ref[...] = dres_ref[...] + r * (dn - nv * jnp.mean(dn * nv, axis=-1, keepdims=True))
        part = jnp.sum(dy * nv, axis=0, keepdims=True)

        @pl.when(pl.program_id(0) == 0)
        def _():
            dg_ref[...] = part

        @pl.when(pl.program_id(0) > 0)
        def _():
            dg_ref[...] += part

    row = pl.BlockSpec((tb, D_MODEL), lambda i: (i, 0))
    vec = pl.BlockSpec((1, D_MODEL), lambda i: (0, 0))
    return pl.pallas_call(
        body, grid=(t // tb,), name=name,
        in_specs=[row, vec, row, row], out_specs=[row, vec],
        out_shape=[jax.ShapeDtypeStruct((t, D_MODEL), F32), jax.ShapeDtypeStruct((1, D_MODEL), F32)],
        compiler_params=_cp("arbitrary"),
    )(x, g, dxn, dres)


def _loss_head(x, g, target, name):
    t, tb = x.shape[0], 256

    def body(x_ref, g_ref, t_ref, loss_ref, dx_ref, dg_ref):
        xv, gv = x_ref[...], g_ref[...]
        r = lax.rsqrt(jnp.mean(xv * xv, axis=-1, keepdims=True) + EPS)
        nv = xv * r
        err = nv * gv - t_ref[...]
        lpart = 0.5 * jnp.sum(jnp.mean(err * err, axis=-1, keepdims=True), axis=0, keepdims=True)
        dy = err * (1.0 / D_MODEL)
        dn = dy * gv
        dx_ref[...] = r * (dn - nv * jnp.mean(dn * nv, axis=-1, keepdims=True))
        gpart = jnp.sum(dy * nv, axis=0, keepdims=True)

        @pl.when(pl.program_id(0) == 0)
        def _():
            dg_ref[...] = gpart
            loss_ref[...] = jnp.broadcast_to(lpart, (1, LANES))

        @pl.when(pl.program_id(0) > 0)
        def _():
            dg_ref[...] += gpart
            loss_ref[...] += jnp.broadcast_to(lpart, (1, LANES))

    row = pl.BlockSpec((tb, D_MODEL), lambda i: (i, 0))
    vec = pl.BlockSpec((1, D_MODEL), lambda i: (0, 0))
    return pl.pallas_call(
        body, grid=(t // tb,), name=name,
        in_specs=[row, vec, row], out_specs=[pl.BlockSpec((1, LANES), lambda i: (0, 0)), row, vec],
        out_shape=[jax.ShapeDtypeStruct((1, LANES), F32), jax.ShapeDtypeStruct((t, D_MODEL), F32),
                   jax.ShapeDtypeStruct((1, D_MODEL), F32)],
        compiler_params=_cp("arbitrary"),
    )(x, g, target)


A_GW = WIDTH // A_GROUPS


def _gmlp_common(v, lg, lb):
    xc = v - jnp.mean(v, axis=-1, keepdims=True)
    rs = lax.rsqrt(jnp.mean(xc * xc, axis=-1, keepdims=True) + EPS)
    vh = xc * rs
    return rs, vh, (vh * lg + lb).astype(BF16)


def _gmlp_fwd(h, ln_g, ln_b, ws, bs_t, name):
    t, tb = h.shape[0], 256

    def body(u_ref, v_ref, z_ref, lg_ref, lb_ref, ws_ref, bst_ref, y_ref):
        _, _, vn = _gmlp_common(v_ref[...], lg_ref[...], lb_ref[...])
        causal = _iota((CHUNK, CHUNK), 1) <= _iota((CHUNK, CHUNK), 0)
        for g in range(A_GROUPS):
            w = jnp.where(causal, ws_ref[g], 0.0).astype(BF16)
            cols = slice(g * A_GW, (g + 1) * A_GW)
            for c in range(tb // CHUNK):
                rows = slice(c * CHUNK, (c + 1) * CHUNK)
                mixed = jnp.dot(w, vn[rows, cols], preferred_element_type=F32) + bst_ref[:, g:g + 1]
                y_ref[rows, cols] = (_silu(z_ref[rows, cols]) * (u_ref[rows, cols] * mixed)).astype(BF16)

    col = lambda j: pl.BlockSpec((tb, WIDTH), lambda i: (i, j))
    full = lambda a: pl.BlockSpec(a.shape, lambda i: (0,) * a.ndim)
    return pl.pallas_call(
        body, grid=(t // tb,), name=name,
        in_specs=[col(0), col(1), col(2), full(ln_g), full(ln_b), full(ws), full(bs_t)],
        out_specs=col(0), out_shape=jax.ShapeDtypeStruct((t, 2 * WIDTH), BF16),
        compiler_params=_cp("parallel"),
    )(h, h, h, ln_g, ln_b, ws, bs_t)


def _gmlp_bwd(h, dy, ln_g, ln_b, ws, ws_t, bs_t, name):
    t, tb = h.shape[0], 256

    def body(u_ref, v_ref, z_ref, dy_ref, lg_ref, lb_ref, ws_ref, wst_ref, bst_ref,
             dh_ref, dws_ref, dbst_ref, dlg_ref, dlb_ref, dvn_ref):
        @pl.when(pl.program_id(0) == 0)
        def _():
            dws_ref[...] = jnp.zeros_like(dws_ref)
            dbst_ref[...] = jnp.zeros_like(dbst_ref)
            dlg_ref[...] = jnp.zeros_like(dlg_ref)
            dlb_ref[...] = jnp.zeros_like(dlb_ref)

        rs, vh, vn = _gmlp_common(v_ref[...], lg_ref[...], lb_ref[...])
        row, lane = _iota((CHUNK, CHUNK), 0), _iota((CHUNK, CHUNK), 1)
        for g in range(A_GROUPS):
            w = jnp.where(lane <= row, ws_ref[g], 0.0).astype(BF16)
            wt = jnp.where(row <= lane, wst_ref[g], 0.0).astype(BF16)
            cols = slice(g * A_GW, (g + 1) * A_GW)
            dws_acc = jnp.zeros((CHUNK, CHUNK), F32)
            dbs_acc = jnp.zeros((CHUNK, 1), F32)
            for c in range(tb // CHUNK):
                rows = slice(c * CHUNK, (c + 1) * CHUNK)
                vnb = vn[rows, cols]
                mixed = jnp.dot(w, vnb, preferred_element_type=F32) + bst_ref[:, g:g + 1]
                u, z, dyv = u_ref[rows, cols], z_ref[rows, cols], dy_ref[rows, cols]
                sz = _silu(z)
                dh_ref[rows, cols] = (dyv * sz * mixed).astype(BF16)
                dh_ref[rows, slice(2 * WIDTH + g * A_GW, 2 * WIDTH + (g + 1) * A_GW)] = (
                    dyv * (u * mixed) * _dsilu(z)).astype(BF16)
                dm = dyv * sz * u
                dws_acc += _dot_nt(dm, vnb)
                dbs_acc += jnp.sum(dm, axis=1, keepdims=True)
                dvn_ref[rows, cols] = jnp.dot(wt, dm.astype(BF16), preferred_element_type=F32)
            dws_ref[g] += jnp.where(lane <= row, dws_acc, 0.0)
            dbst_ref[...] += jnp.where(lane == g, dbs_acc, 0.0)
        dvn = dvn_ref[...]
        dlg_ref[...] += jnp.sum(dvn * vh, axis=0, keepdims=True)
        dlb_ref[...] += jnp.sum(dvn, axis=0, keepdims=True)
        dvh = dvn * lg_ref[...]
        dv = rs * (dvh - jnp.mean(dvh, axis=-1, keepdims=True) - vh * jnp.mean(dvh * vh, axis=-1, keepdims=True))
        dh_ref[:, WIDTH:2 * WIDTH] = dv.astype(BF16)

    col = lambda j: pl.BlockSpec((tb, WIDTH), lambda i: (i, j))
    full = lambda a: pl.BlockSpec(a.shape, lambda i: (0,) * a.ndim)
    acc = lambda shape: pl.BlockSpec(shape, lambda i: (0,) * len(shape))
    return pl.pallas_call(
        body, grid=(t // tb,), name=name,
        in_specs=[col(0), col(1), col(2), col(0), full(ln_g), full(ln_b), full(ws), full(ws_t), full(bs_t)],
        out_specs=[pl.BlockSpec((tb, 3 * WIDTH), lambda i: (i, 0)), acc((A_GROUPS, CHUNK, CHUNK)),
                   acc((CHUNK, LANES)), acc((1, WIDTH)), acc((1, WIDTH))],
        out_shape=[jax.ShapeDtypeStruct((t, F_EVEN), BF16), jax.ShapeDtypeStruct((A_GROUPS, CHUNK, CHUNK), F32),
                   jax.ShapeDtypeStruct((CHUNK, LANES), F32), jax.ShapeDtypeStruct((1, WIDTH), F32),
                   jax.ShapeDtypeStruct((1, WIDTH), F32)],
        scratch_shapes=[pltpu.VMEM((tb, WIDTH), F32)],
        compiler_params=_cp("arbitrary"),
    )(h, h, h, dy, ln_g, ln_b, ws, ws_t, bs_t)


def _halo_prev(tb, j):
    return lambda i: (jnp.maximum(i * (tb // SUBLANES) - 1, 0), j)


def _halo_next(tb, j, t):
    return lambda i: (jnp.minimum((i + 1) * (tb // SUBLANES), t // SUBLANES - 1), j)


def _row_select(parts, width):
    row = _iota((SUBLANES, width), 0)
    out = jnp.zeros((SUBLANES, width), F32)
    for k, p in enumerate(parts):
        out = jnp.where(row == k, p, out)
    return out


def _sconv_fwd(h, w, name):
    t, tb = h.shape[0], 256

    def body(bg_ref, cg_ref, hx_ref, z_ref, cgh_ref, hxh_ref, w_ref, y_ref):
        p = cg_ref[...] * hx_ref[...]
        ph = jnp.where(pl.program_id(0) > 0, cgh_ref[...] * hxh_ref[...], 0.0)
        cv = w_ref[2:3, :] * p + w_ref[1:2, :] * _shift_down(p, ph, 1) + w_ref[0:1, :] * _shift_down(p, ph, 2)
        y_ref[...] = (_silu(z_ref[...]) * (bg_ref[...] * cv)).astype(BF16)

    col = lambda j: pl.BlockSpec((tb, WIDTH), lambda i: (i, j))
    halo = lambda j: pl.BlockSpec((SUBLANES, WIDTH), _halo_prev(tb, j))
    return pl.pallas_call(
        body, grid=(t // tb,), name=name,
        in_specs=[col(0), col(1), col(2), col(3), halo(1), halo(2), pl.BlockSpec(w.shape, lambda i: (0, 0))],
        out_specs=col(0), out_shape=jax.ShapeDtypeStruct((t, 2 * WIDTH), BF16),
        compiler_params=_cp("parallel"),
    )(h, h, h, h, h, h, w)


def _sconv_bwd(h, dy, w, name):
    t, tb = h.shape[0], 256
    nb = t // tb

    def body(bg_ref, cg_ref, hx_ref, z_ref, dy_ref, cgh_ref, hxh_ref, bgn_ref, zn_ref, dyn_ref, w_ref, dh_ref, dw_ref):
        i = pl.program_id(0)
        bg, cg, hx, z, dyv = bg_ref[...], cg_ref[...], hx_ref[...], z_ref[...], dy_ref[...]
        p = cg * hx
        ph = jnp.where(i > 0, cgh_ref[...] * hxh_ref[...], 0.0)
        p1, p2 = _shift_down(p, ph, 1), _shift_down(p, ph, 2)
        cv = w_ref[2:3, :] * p + w_ref[1:2, :] * p1 + w_ref[0:1, :] * p2
        sz = _silu(z)
        dcv = dyv * sz * bg
        dcvn = jnp.where(i < nb - 1, dyn_ref[...] * _silu(zn_ref[...]) * bgn_ref[...], 0.0)
        dp = w_ref[2:3, :] * dcv + w_ref[1:2, :] * _shift_up(dcv, dcvn, 1) + w_ref[0:1, :] * _shift_up(dcv, dcvn, 2)
        dh_ref[:, 0:WIDTH] = (dyv * sz * cv).astype(BF16)
        dh_ref[:, WIDTH:2 * WIDTH] = (dp * hx).astype(BF16)
        dh_ref[:, 2 * WIDTH:3 * WIDTH] = (dp * cg).astype(BF16)
        dh_ref[:, 3 * WIDTH:4 * WIDTH] = (dyv * (bg * cv) * _dsilu(z)).astype(BF16)
        part = _row_select([jnp.sum(dcv * q, axis=0, keepdims=True) for q in (p2, p1, p)], WIDTH)

        @pl.when(i == 0)
        def _():
            dw_ref[...] = part

        @pl.when(i > 0)
        def _():
            dw_ref[...] += part

    col = lambda j: pl.BlockSpec((tb, WIDTH), lambda i: (i, j))
    prev = lambda j: pl.BlockSpec((SUBLANES, WIDTH), _halo_prev(tb, j))
    nxt = lambda j: pl.BlockSpec((SUBLANES, WIDTH), _halo_next(tb, j, t))
    return pl.pallas_call(
        body, grid=(nb,), name=name,
        in_specs=[col(0), col(1), col(2), col(3), col(0), prev(1), prev(2), nxt(0), nxt(3), nxt(0),
                  pl.BlockSpec(w.shape, lambda i: (0, 0))],
        out_specs=[pl.BlockSpec((tb, 4 * WIDTH), lambda i: (i, 0)), pl.BlockSpec((SUBLANES, WIDTH), lambda i: (0, 0))],
        out_shape=[jax.ShapeDtypeStruct((t, 4 * WIDTH), BF16), jax.ShapeDtypeStruct((SUBLANES, WIDTH), F32)],
        compiler_params=_cp("arbitrary"),
    )(h, h, h, h, dy, h, h, h, h, dy, w)


XBC_COL0 = 4 * WIDTH


def _ssd_conv_fwd(h, w, b, name):
    t, tb = h.shape[0], 256

    def body(x_ref, xh_ref, w_ref, b_ref, o_ref):
        xv = x_ref[...]
        xh = jnp.where(pl.program_id(0) > 0, xh_ref[...], 0.0)
        acc = b_ref[...] + w_ref[3:4, :] * xv
        for j in range(1, B_CONV):
            acc = acc + w_ref[B_CONV - 1 - j:B_CONV - j, :] * _shift_down(xv, xh, j)
        o_ref[...] = acc

    cb = XBC_COL0 // B_XBC
    return pl.pallas_call(
        body, grid=(t // tb,), name=name,
        in_specs=[pl.BlockSpec((tb, B_XBC), lambda i: (i, cb)), pl.BlockSpec((SUBLANES, B_XBC), _halo_prev(tb, cb)),
                  pl.BlockSpec(w.shape, lambda i: (0, 0)), pl.BlockSpec(b.shape, lambda i: (0, 0))],
        out_specs=pl.BlockSpec((tb, B_XBC), lambda i: (i, 0)), out_shape=jax.ShapeDtypeStruct((t, B_XBC), F32),
        compiler_params=_cp("parallel"),
    )(h, h, w, b)


def _ssd_conv_bwd(h, dh, dpx, dpb, dpc, w, name):
    t, tb, tc = h.shape[0], 256, 1024
    nb = t // tb
    r8 = tb // SUBLANES
    last8 = t // SUBLANES - 1

    def body(dpx_ref, dpb_ref, dpc_ref, nx_ref, nb_ref, nc_ref, x_ref, xh_ref, w_ref, dh_in, dh_ref, dw_ref, db_ref):
        j, i = pl.program_id(0), pl.program_id(1)
        pick = lambda a, b_, c: jnp.where(j < 2, a[...], jnp.where(j == 2, b_[...], c[...]))
        dp = pick(dpx_ref, dpb_ref, dpc_ref)
        dn = jnp.where(i < nb - 1, pick(nx_ref, nb_ref, nc_ref), 0.0)
        xv = x_ref[...]
        xh = jnp.where(i > 0, xh_ref[...], 0.0)
        dx = w_ref[3:4, :] * dp
        for s in range(1, B_CONV):
            dx = dx + w_ref[B_CONV - 1 - s:B_CONV - s, :] * _shift_up(dp, dn, s)
        dh_ref[...] = dx.astype(BF16)
        wpart = _row_select([jnp.sum(dp * _shift_down(xv, xh, B_CONV - 1 - k), axis=0, keepdims=True)
                             for k in range(B_CONV)], tc)
        bpart = jnp.sum(dp, axis=0, keepdims=True)

        @pl.when(i == 0)
        def _():
            dw_ref[...] = wpart
            db_ref[...] = bpart

        @pl.when(i > 0)
        def _():
            dw_ref[...] += wpart
            db_ref[...] += bpart

    def src(blk_rows, rowf, sel, colf):
        return pl.BlockSpec((blk_rows, tc), lambda j, i: (jnp.where(sel(j), rowf(i), 0), colf(j)))

    cur = lambda i: i
    nxt = lambda i: jnp.minimum((i + 1) * r8, last8)
    is_x, is_b, is_c = (lambda j: j < 2), (lambda j: j == 2), (lambda j: j == 3)
    xcol, zero = (lambda j: jnp.minimum(j, 1)), (lambda j: 0)
    c0 = XBC_COL0 // tc
    return pl.pallas_call(
        body, grid=(B_XBC // tc, nb), name=name,
        in_specs=[src(tb, cur, is_x, xcol), src(tb, cur, is_b, zero), src(tb, cur, is_c, zero),
                  src(SUBLANES, nxt, is_x, xcol), src(SUBLANES, nxt, is_b, zero), src(SUBLANES, nxt, is_c, zero),
                  pl.BlockSpec((tb, tc), lambda j, i: (i, c0 + j)),
                  pl.BlockSpec((SUBLANES, tc), lambda j, i: (jnp.maximum(i * r8 - 1, 0), c0 + j)),
                  pl.BlockSpec((B_CONV, tc), lambda j, i: (0, j)), ANY],
        out_specs=[pl.BlockSpec((tb, tc), lambda j, i: (i, c0 + j)), pl.BlockSpec((SUBLANES, tc), lambda j, i: (0, j)),
                   pl.BlockSpec((1, tc), lambda j, i: (0, j))],
        out_shape=[jax.ShapeDtypeStruct(dh.shape, dh.dtype), jax.ShapeDtypeStruct((SUBLANES, B_XBC), F32),
                   jax.ShapeDtypeStruct((1, B_XBC), F32)],
        input_output_aliases={9: 0},
        compiler_params=_cp("arbitrary", "arbitrary"),
    )(dpx, dpb, dpc, dpx, dpb, dpc, h, h, w, dh)


HEADS_PER_GROUP = B_HEADS // B_GROUPS


def _ssd_group_terms(g, px, pb, pc, dtr, alog, dtb):
    xs, bm, cm = _silu(px), _silu(pb), _silu(pc)
    dt = _softplus(dtr + dtb)
    a = -jnp.exp(alog)
    head_of_lane = _iota((LANES, B_GROUP_W), 1) // B_HEAD_DIM + HEADS_PER_GROUP * g
    expand = (_iota((LANES, B_GROUP_W), 0) == head_of_lane).astype(BF16)
    local = _iota((LANES, LANES), 1)
    sel = ((_iota((LANES, LANES), 0) == local + HEADS_PER_GROUP * g) & (local < HEADS_PER_GROUP)).astype(BF16)
    tri = (_iota((CHUNK, CHUNK), 1) <= _iota((CHUNK, CHUNK), 0)).astype(BF16)
    dt_e = _dot_sel(dt, expand)
    a_e = _sel_dot(tri, _dot_sel(dt * a, expand))
    a4 = _sel_dot(tri, _dot_sel(dt * a, sel))
    sub = _iota((LANES, LANES), 0)
    sel_t = ((local == sub + HEADS_PER_GROUP * g) & (sub < HEADS_PER_GROUP)).astype(BF16)
    return dict(xs=xs, bm=bm, cm=cm, dt=dt, a=a, sel=sel, sel_t=sel_t, dt_e=dt_e, a_e=a_e, a4=a4, a4t=a4.T)


def _ssd_decay(tm, j, transposed):
    col, row = tm["a4"][:, j:j + 1], tm["a4t"][j:j + 1, :]
    lane, sub = _iota((CHUNK, CHUNK), 1), _iota((CHUNK, CHUNK), 0)
    if transposed:
        return jnp.where(sub <= lane, jnp.exp(jnp.minimum(row - col, 0.0)), 0.0)
    return jnp.where(lane <= sub, jnp.exp(jnp.minimum(col - row, 0.0)), 0.0)


def _ssd_specs(nc, rev):
    ch = (lambda c: nc - 1 - c) if rev else (lambda c: c)
    gw = lambda off: pl.BlockSpec((CHUNK, B_GROUP_W), lambda c, g: (ch(c), off + g))
    gn = lambda off: pl.BlockSpec((CHUNK, B_STATE), lambda c, g: (ch(c), off + g))
    tok = pl.BlockSpec((CHUNK, LANES), lambda c, g: (ch(c), 0))
    vec = pl.BlockSpec((1, LANES), lambda c, g: (0, 0))
    gvec = pl.BlockSpec((1, B_GROUP_W), lambda c, g: (0, g))
    st = pl.BlockSpec((None, B_STATE, B_GROUP_W), lambda c, g: (ch(c), 0, g))
    return gw, gn, tok, vec, gvec, st


def _ssd_scan_fwd(pre, hdt, h, y, alog, dtb, dfull, ng, name):
    t = pre.shape[0]
    nc = t // CHUNK

    def body(px_ref, pb_ref, pc_ref, dtr_ref, z_ref, alog_ref, dtb_ref, df_ref, ng_ref, y_in,
             y_ref, ypre_ref, st_ref, state_ref):
        c, g = pl.program_id(0), pl.program_id(1)

        @pl.when(c == 0)
        def _():
            state_ref[g] = jnp.zeros((B_STATE, B_GROUP_W), F32)

        tm = _ssd_group_terms(g, px_ref[...], pb_ref[...], pc_ref[...], dtr_ref[...], alog_ref[...], dtb_ref[...])
        xs, bm, cm, a_e = tm["xs"], tm["bm"], tm["cm"], tm["a_e"]
        xdt = xs * tm["dt_e"]
        cb = _dot_nt(cm, bm)
        head = _iota((CHUNK, B_GROUP_W), 1) // B_HEAD_DIM
        yd = jnp.zeros((CHUNK, B_GROUP_W), F32)
        for j in range(HEADS_PER_GROUP):
            yd = jnp.where(head == j, _dot(cb * _ssd_decay(tm, j, False), xdt), yd)
        st = state_ref[g]
        st_ref[...] = st
        yv = yd + jnp.exp(a_e) * _dot(cm, st) + df_ref[...] * xs
        a_last = a_e[CHUNK - 1:CHUNK, :]
        state_ref[g] = st * jnp.exp(a_last) + _dot_tn(bm, xdt * jnp.exp(a_last - a_e))
        ypre_ref[...] = yv
        yz = yv * _silu(z_ref[...])
        r = lax.rsqrt(jnp.mean(yz * yz, axis=-1, keepdims=True) + EPS)
        y_ref[...] = (yz * r * ng_ref[...]).astype(BF16)

    gw, gn, tok, vec, gvec, st = _ssd_specs(nc, False)
    return pl.pallas_call(
        body, grid=(nc, B_GROUPS), name=name,
        in_specs=[gw(0), gn(WIDTH // B_STATE), gn((WIDTH + B_GROUPS * B_STATE) // B_STATE), tok,
                  gw(3 * WIDTH // B_GROUP_W), vec, vec, gvec, gvec, ANY],
        out_specs=[gw(WIDTH // B_GROUP_W), gw(0), st],
        out_shape=[jax.ShapeDtypeStruct(y.shape, y.dtype), jax.ShapeDtypeStruct((t, WIDTH), F32),
                   jax.ShapeDtypeStruct((nc, B_STATE, WIDTH), F32)],
        scratch_shapes=[pltpu.VMEM((B_GROUPS, B_STATE, B_GROUP_W), F32)],
        input_output_aliases={9: 0},
        compiler_params=_cp("arbitrary", "arbitrary"),
    )(pre, pre, pre, hdt, h, alog, dtb, dfull, ng, y)


def _ssd_scan_bwd(pre, hdt, h, dy, ypre, states, dh, alog, dtb, dfull, ng, name):
    t = pre.shape[0]
    nc = t // CHUNK

    def body(px_ref, pb_ref, pc_ref, dtr_ref, z_ref, dy_ref, ypre_ref, st_ref, alog_ref, dtb_ref, df_ref, ng_ref,
             dh_in, dz_ref, dpx_ref, dpb_ref, dpc_ref, ddt_ref, dbias_ref, dalog_ref, dd_ref, dng_ref, dstate_ref):
        c, g = pl.program_id(0), pl.program_id(1)

        @pl.when(c == 0)
        def _():
            dstate_ref[g] = jnp.zeros((B_STATE, B_GROUP_W), F32)
            dd_ref[g] = jnp.zeros((1, B_GROUP_W), F32)
            dng_ref[g] = jnp.zeros((1, B_GROUP_W), F32)

        @pl.when((c == 0) & (g == 0))
        def _():
            dbias_ref[...] = jnp.zeros_like(dbias_ref)
            dalog_ref[...] = jnp.zeros_like(dalog_ref)

        px, pb, pc, dtr = px_ref[...], pb_ref[...], pc_ref[...], dtr_ref[...]
        tm = _ssd_group_terms(g, px, pb, pc, dtr, alog_ref[...], dtb_ref[...])
        xs, bm, cm, a_e, dt_e = tm["xs"], tm["bm"], tm["cm"], tm["a_e"], tm["dt_e"]
        xdt = xs * dt_e
        head = _iota((CHUNK, B_GROUP_W), 1) // B_HEAD_DIM

        z, yv, ngv = z_ref[...], ypre_ref[...], ng_ref[...]
        sz = _silu(z)
        yz = yv * sz
        r = lax.rsqrt(jnp.mean(yz * yz, axis=-1, keepdims=True) + EPS)
        dyn = dy_ref[...]
        dng_ref[g] += jnp.sum(dyn * yz * r, axis=0, keepdims=True)
        q = dyn * ngv
        dyz = r * q - yz * (r * r * r) * jnp.mean(q * yz, axis=-1, keepdims=True)
        dyv = dyz * sz
        dz_ref[...] = (dyz * yv * _dsilu(z)).astype(BF16)
        dd_ref[g] += jnp.sum(dyv * xs, axis=0, keepdims=True)
        dxs = df_ref[...] * dyv

        st = st_ref[...]
        ea = jnp.exp(a_e)
        ead = ea * dyv
        dcm = _dot_nt(ead, st)
        da_e = dyv * (ea * _dot(cm, st))

        dsn = dstate_ref[g]
        a_last = a_e[CHUNK - 1:CHUNK, :]
        ea_last = jnp.exp(a_last)
        dstate_ref[g] = dsn * ea_last + _dot_tn(cm, ead)
        wdec = jnp.exp(a_last - a_e)
        xw = xdt * wdec
        dxw = _dot(bm, dsn)
        dxdt = dxw * wdec
        dbm = _dot_nt(xw, dsn)
        zc = dxw * xw
        da_last = jnp.sum(zc, axis=0, keepdims=True) + jnp.sum(dsn * st, axis=0, keepdims=True) * ea_last
        da_e = da_e - zc + jnp.where(_iota((CHUNK, B_GROUP_W), 0) == CHUNK - 1, da_last, 0.0)

        cb, cbt = _dot_nt(cm, bm), _dot_nt(bm, cm)
        dcb, dcbt = jnp.zeros((CHUNK, CHUNK), F32), jnp.zeros((CHUNK, CHUNK), F32)
        da4 = jnp.zeros((CHUNK, LANES), F32)
        lane = _iota((CHUNK, LANES), 1)
        for j in range(HEADS_PER_GROUP):
            mine = head == j
            gm = _dot_nt(jnp.where(mine, dyv, 0.0), xdt)
            gmt = _dot_nt(jnp.where(mine, xdt, 0.0), dyv)
            dec, dect = _ssd_decay(tm, j, False), _ssd_decay(tm, j, True)
            dcb += gm * dec
            dcbt += gmt * dect
            da_j = (jnp.sum(gm * cb * dec, axis=1, keepdims=True) - jnp.sum(gmt * cbt * dect, axis=1, keepdims=True))
            da4 = jnp.where(lane == j, da_j, da4)
            dxdt = dxdt + jnp.where(mine, _dot(cbt * dect, dyv), 0.0)
        dcm = dcm + _dot(dcb, bm)
        dbm = dbm + _dot(dcbt, cm)

        gather = (_iota((B_GROUP_W, LANES), 0) // B_HEAD_DIM == _iota((B_GROUP_W, LANES), 1)).astype(BF16)
        da4 = da4 + _dot_sel(da_e, gather)
        rtri = (_iota((CHUNK, CHUNK), 1) >= _iota((CHUNK, CHUNK), 0)).astype(BF16)
        dadt4 = _sel_dot(rtri, da4)
        dt4 = _dot_sel(tm["dt"], tm["sel"])
        a4row = _dot_sel(jnp.broadcast_to(tm["a"], (SUBLANES, LANES)), tm["sel"])[0:1, :]
        ddt4 = dadt4 * a4row + _dot_sel(dxdt * xs, gather)
        dxs = dxs + dxdt * dt_e
        sel_t = tm["sel_t"]
        ddt = _dot_sel(ddt4, sel_t) * _sig(dtr + dtb_ref[...])
        da_heads = _dot_sel(jnp.broadcast_to(jnp.sum(dadt4 * dt4, axis=0, keepdims=True), (SUBLANES, LANES)), sel_t)[0:1, :]

        @pl.when(g == 0)
        def _():
            ddt_ref[...] = ddt

        @pl.when(g > 0)
        def _():
            ddt_ref[...] += ddt

        dbias_ref[...] += jnp.sum(ddt, axis=0, keepdims=True)
        dalog_ref[...] += da_heads * tm["a"]
        dpx_ref[...] = dxs * _dsilu(px)
        dpb_ref[...] = dbm * _dsilu(pb)
        dpc_ref[...] = dcm * _dsilu(pc)

    gw, gn, tok, vec, gvec, st = _ssd_specs(nc, True)
    acc = lambda shape: pl.BlockSpec(shape, lambda c, g: (0,) * len(shape))
    rev = lambda c: nc - 1 - c
    return pl.pallas_call(
        body, grid=(nc, B_GROUPS), name=name,
        in_specs=[gw(0), gn(WIDTH // B_STATE), gn((WIDTH + B_GROUPS * B_STATE) // B_STATE), tok,
                  gw(3 * WIDTH // B_GROUP_W), gw(WIDTH // B_GROUP_W), gw(0), st, vec, vec, gvec, gvec, ANY],
        out_specs=[gw(3 * WIDTH // B_GROUP_W), gw(0),
                   pl.BlockSpec((CHUNK, B_STATE), lambda c, g: (rev(c), g)),
                   pl.BlockSpec((CHUNK, B_STATE), lambda c, g: (rev(c), g)),
                   tok, vec, vec, acc((B_GROUPS, 1, B_GROUP_W)), acc((B_GROUPS, 1, B_GROUP_W))],
        out_shape=[jax.ShapeDtypeStruct(dh.shape, dh.dtype), jax.ShapeDtypeStruct((t, WIDTH), F32),
                   jax.ShapeDtypeStruct((t, B_GROUPS * B_STATE), F32), jax.ShapeDtypeStruct((t, B_GROUPS * B_STATE), F32),
                   jax.ShapeDtypeStruct((t, LANES), F32), jax.ShapeDtypeStruct((1, LANES), F32),
                   jax.ShapeDtypeStruct((1, LANES), F32), jax.ShapeDtypeStruct((B_GROUPS, 1, B_GROUP_W), F32),
                   jax.ShapeDtypeStruct((B_GROUPS, 1, B_GROUP_W), F32)],
        scratch_shapes=[pltpu.VMEM((B_GROUPS, B_STATE, B_GROUP_W), F32)],
        input_output_aliases={12: 0},
        compiler_params=_cp("arbitrary", "arbitrary"),
    )(pre, pre, pre, hdt, h, dy, ypre, states, alog, dtb, dfull, ng, dh)


Q_COL, K_COL, V_COL, Z_COL = [(4 + i) * WIDTH // D_HEAD_DIM for i in range(4)]
ATT_SCALE = D_HEAD_DIM ** -0.5


def _attn_multiplicity():
    d = jnp.arange(D_BACK + 1)[:, None, None]
    i = jnp.arange(CHUNK)[None, :, None]
    j = jnp.arange(CHUNK)[None, None, :]
    delta = d * CHUNK + i - j
    return sum(((delta >= 0) & (delta <= w) & (delta % dil == 0)).astype(F32) for w, dil in D_PATTERNS)


def _attn_fwd(h, y, mult, name):
    t = h.shape[0]
    nq = t // CHUNK

    def body(q_ref, k_ref, v_ref, z_ref, m_ref, y_in, y_ref, o_ref, lse_ref):
        qb = pl.program_id(1)
        q = q_ref[...].astype(BF16)

        def step(d, carry):
            m, l, acc = carry
            ks = pl.multiple_of((qb - d) * CHUNK, CHUNK)
            cmul = m_ref[d]
            s = jnp.where(cmul > 0.0, _dot_nt(q, k_ref[pl.ds(ks, CHUNK), :]) * ATT_SCALE, NEG)
            m_new = jnp.maximum(m, jnp.max(s, axis=1, keepdims=True))
            p = cmul * jnp.exp(s - m_new)
            alpha = jnp.exp(m - m_new)
            return m_new, alpha * l + jnp.sum(p, axis=1, keepdims=True), alpha * acc + _dot(p, v_ref[pl.ds(ks, CHUNK), :])

        init = (jnp.full((CHUNK, 1), NEG, F32), jnp.zeros((CHUNK, 1), F32), jnp.zeros((CHUNK, D_HEAD_DIM), F32))
        m, l, acc = lax.fori_loop(0, jnp.minimum(qb, D_BACK) + 1, step, init)
        o = acc / l
        o_ref[...] = o
        y_ref[...] = (_silu(z_ref[...]) * o).astype(BF16)
        lse_ref[pl.ds(qb, 1), :] = jnp.broadcast_to(m + jnp.log(l), (CHUNK, CHUNK)).T[0:1, :]

    blk = lambda off: pl.BlockSpec((CHUNK, D_HEAD_DIM), lambda hd, qb: (qb, off + hd))
    seq = lambda off: pl.BlockSpec((t, D_HEAD_DIM), lambda hd, qb: (0, off + hd))
    return pl.pallas_call(
        body, grid=(D_HEADS, nq), name=name,
        in_specs=[blk(Q_COL), seq(K_COL), seq(V_COL), blk(Z_COL), pl.BlockSpec(mult.shape, lambda hd, qb: (0, 0, 0)), ANY],
        out_specs=[blk(WIDTH // D_HEAD_DIM), blk(0), pl.BlockSpec((None, nq, CHUNK), lambda hd, qb: (hd, 0, 0))],
        out_shape=[jax.ShapeDtypeStruct(y.shape, y.dtype), jax.ShapeDtypeStruct((t, WIDTH), F32),
                   jax.ShapeDtypeStruct((D_HEADS, nq, CHUNK), F32)],
        input_output_aliases={5: 0},
        compiler_params=_cp("arbitrary", "arbitrary"),
    )(h, h, h, h, mult, y)


def _attn_bwd(h, dy, o, lse, mult, name):
    t = h.shape[0]
    nq = t // CHUNK

    def body(q_ref, k_ref, v_ref, z_ref, dy_ref, o_ref, lse_ref, m_ref, dqz_ref, dkv_ref, acc_ref):
        qb = pl.program_id(1)

        @pl.when(qb == 0)
        def _():
            acc_ref[...] = jnp.zeros_like(acc_ref)

        q = q_ref[...].astype(BF16)
        z, ov, dyv = z_ref[...], o_ref[...], dy_ref[...]
        do = dyv * _silu(z)
        dob = do.astype(BF16)
        dqz_ref[1] = (dyv * ov * _dsilu(z)).astype(BF16)
        delta = jnp.sum(do * ov, axis=1, keepdims=True)
        lse_col = jnp.broadcast_to(lse_ref[pl.ds(qb, 1), :], (CHUNK, CHUNK)).T

        def step(d, dq):
            ks = pl.multiple_of((qb - d) * CHUNK, CHUNK)
            k = k_ref[pl.ds(ks, CHUNK), :].astype(BF16)
            cmul = m_ref[d]
            s = jnp.where(cmul > 0.0, _dot_nt(q, k) * ATT_SCALE, NEG)
            p = cmul * jnp.exp(s - lse_col)
            ds = p * (_dot_nt(dob, v_ref[pl.ds(ks, CHUNK), :]) - delta) * ATT_SCALE
            acc_ref[0, pl.ds(ks, CHUNK), :] += _dot_tn(ds, q)
            acc_ref[1, pl.ds(ks, CHUNK), :] += _dot_tn(p, dob)
            return dq + _dot(ds, k)

        dq = lax.fori_loop(0, jnp.minimum(qb, D_BACK) + 1, step, jnp.zeros((CHUNK, D_HEAD_DIM), F32))
        dqz_ref[0] = dq.astype(BF16)

        @pl.when(qb == nq - 1)
        def _():
            dkv_ref[...] = acc_ref[...].astype(BF16)

    blk = lambda off: pl.BlockSpec((CHUNK, D_HEAD_DIM), lambda hd, qb: (qb, off + hd))
    seq = lambda off: pl.BlockSpec((t, D_HEAD_DIM), lambda hd, qb: (0, off + hd))
    return pl.pallas_call(
        body, grid=(D_HEADS, nq), name=name,
        in_specs=[blk(Q_COL), seq(K_COL), seq(V_COL), blk(Z_COL), blk(WIDTH // D_HEAD_DIM), blk(0),
                  pl.BlockSpec((None, nq, CHUNK), lambda hd, qb: (hd, 0, 0)),
                  pl.BlockSpec(mult.shape, lambda hd, qb: (0, 0, 0))],
        out_specs=[pl.BlockSpec((2, CHUNK, D_HEAD_DIM), lambda hd, qb: (0, qb, hd)),
                   pl.BlockSpec((2, t, D_HEAD_DIM), lambda hd, qb: (0, 0, hd))],
        out_shape=[jax.ShapeDtypeStruct((2, t, WIDTH), BF16), jax.ShapeDtypeStruct((2, t, WIDTH), BF16)],
        scratch_shapes=[pltpu.VMEM((2, t, D_HEAD_DIM), F32)],
        compiler_params=_cp("arbitrary", "arbitrary"),
    )(h, h, h, h, dy, o, lse, mult)


def _place():
    x, y, c = lax.axis_index("x"), lax.axis_index("y"), lax.axis_index("c")
    return x, y, c, 4 * x + 2 * y + c


def _all_gather(blocks, name):
    n = len(blocks)

    def body(*refs):
        ins, outs = refs[:n], refs[n:2 * n]
        send_sems, recv_sems, local_sems = refs[2 * n:]
        x, y, c, me = _place()
        sibling = (x, y, 1 - c)
        chips = [(1 - x, y), (x, 1 - y), (1 - x, 1 - y)]
        index = lambda px, py, pc: 4 * px + 2 * py + pc

        def copy(a, k, block, to, src=None):
            dst = outs[a].at[block]
            return pltpu.make_async_remote_copy(
                src_ref=dst if src is None else src, dst_ref=dst, send_sem=send_sems.at[a, k],
                recv_sem=recv_sems.at[a, k], device_id=to, device_id_type=MESH)

        started = []
        for a in range(n):
            mine = pltpu.make_async_copy(ins[a], outs[a].at[me], local_sems.at[a])
            mine.start()
            started.append(mine)
        sent = []
        for a in range(n):
            sent.append(copy(a, 0, me, sibling, src=ins[a]))
            sent += [copy(a, 1 + j, me, (*chip, c), src=ins[a]) for j, chip in enumerate(chips)]
        for cp in sent:
            cp.start()
        for j, chip in enumerate(chips):
            for a in range(n):
                copy(a, 1 + j, index(*chip, c), (x, y, c)).wait_recv()
                passed = copy(a, 4 + j, index(*chip, c), sibling)
                passed.start()
                sent.append(passed)
        for a in range(n):
            copy(a, 0, index(x, y, 1 - c), (x, y, c)).wait_recv()
        for j, chip in enumerate(chips):
            for a in range(n):
                copy(a, 4 + j, index(*chip, 1 - c), (x, y, c)).wait_recv()
        for cp in sent:
            cp.wait_send()
        for mine in started:
            mine.wait()

    return pl.pallas_call(
        body, name=name, in_specs=[ANY] * n, out_specs=[ANY] * n,
        out_shape=[jax.ShapeDtypeStruct((N_DEV,) + b.shape, b.dtype) for b in blocks],
        scratch_shapes=[pltpu.SemaphoreType.DMA((n, 7)), pltpu.SemaphoreType.DMA((n, 7)), pltpu.SemaphoreType.DMA((n,))],
    )(*blocks)


def _all_to_all(parts, name):
    n = len(parts)
    flips = [(fx, fy, fc) for fx in (0, 1) for fy in (0, 1) for fc in (0, 1)][1:]

    def body(*refs):
        ins, outs = refs[:n], refs[n:2 * n]
        send_sems, recv_sems, local_sems = refs[2 * n:]
        x, y, c, me = _place()
        flip = lambda v, f: 1 - v if f else v
        peers = [(flip(x, fx), flip(y, fy), flip(c, fc)) for fx, fy, fc in flips]

        def copy(a, k, sending):
            px, py, pc = peers[k]
            there = 4 * px + 2 * py + pc
            return pltpu.make_async_remote_copy(
                src_ref=ins[a].at[there], dst_ref=outs[a].at[me if sending else there], send_sem=send_sems.at[a, k],
                recv_sem=recv_sems.at[a, k], device_id=peers[k], device_id_type=MESH)

        local = [pltpu.make_async_copy(ins[a].at[me], outs[a].at[me], local_sems.at[a]) for a in range(n)]
        for cp in local:
            cp.start()
        for k in range(7):
            for a in range(n):
                copy(a, k, True).start()
        for k in range(7):
            for a in range(n):
                copy(a, k, False).wait()
        for cp in local:
            cp.wait()

    return pl.pallas_call(
        body, name=name, in_specs=[ANY] * n, out_specs=[ANY] * n,
        out_shape=[jax.ShapeDtypeStruct(p.shape, p.dtype) for p in parts],
        scratch_shapes=[pltpu.SemaphoreType.DMA((n, 7)), pltpu.SemaphoreType.DMA((n, 7)), pltpu.SemaphoreType.DMA((n,))],
    )(*parts)


def _sum_parts(parts, tc, name):
    _, r, c = parts.shape

    def body(p_ref, o_ref):
        acc = p_ref[0].astype(F32)
        for d in range(1, N_DEV):
            acc = acc + p_ref[d].astype(F32)
        o_ref[...] = acc

    return pl.pallas_call(
        body, grid=(c // tc,), name=name,
        in_specs=[pl.BlockSpec((N_DEV, r, tc), lambda j: (0, 0, j))], out_specs=pl.BlockSpec((r, tc), lambda j: (0, j)),
        out_shape=jax.ShapeDtypeStruct((r, c), F32), compiler_params=_cp("parallel"),
    )(parts)


def _adamw(w, g, m, v, tr, name):
    r, c = w.shape

    def body(w_ref, g_ref, m_ref, v_ref, d_ref, m2_ref, v2_ref):
        gv = g_ref[...]
        m2 = ADAM_B1 * m_ref[...] + (1.0 - ADAM_B1) * gv
        v2 = ADAM_B2 * v_ref[...] + (1.0 - ADAM_B2) * (gv * gv)
        m_hat = m2 / (1.0 - ADAM_B1 ** ADAM_STEP)
        v_hat = v2 / (1.0 - ADAM_B2 ** ADAM_STEP)
        d_ref[...] = -ADAM_LR * (m_hat / (jnp.sqrt(v_hat) + ADAM_EPS) + ADAM_WD * w_ref[...])
        m2_ref[...] = m2
        v2_ref[...] = v2

    spec = pl.BlockSpec((tr, c), lambda i: (i, 0))
    return pl.pallas_call(
        body, grid=(r // tr,), name=name, in_specs=[spec] * 4, out_specs=[spec] * 3,
        out_shape=[jax.ShapeDtypeStruct((r, c), F32)] * 3, compiler_params=_cp("parallel"),
    )(w, g, m, v)


PACK_ROWS = SUBLANES * LANES


def _pack(arrays):
    flat = [jnp.pad(a.reshape(-1), (0, -a.size % PACK_ROWS)) for a in arrays]
    return jnp.concatenate(flat).reshape(-1, LANES)


def _unpack(packed, shapes):
    flat, out, pos = packed.reshape(-1), [], 0
    for s in shapes:
        size = 1
        for d in s:
            size *= d
        out.append(flat[pos:pos + size].reshape(s))
        pos += size + (-size % PACK_ROWS)
    return out


SMALL = ["even_norm_g", "gmlp_ln_g", "gmlp_ln_b", "gmlp_ws", "gmlp_bs", "ssd_conv_w", "ssd_conv_b", "ssd_dt_bias",
         "ssd_a_log", "ssd_d", "ssd_norm_g", "odd_norm_g", "sconv_w", "final_norm_g"]
ORDER = ["even_norm_g", "even_w_in", "gmlp_ln_g", "gmlp_ln_b", "gmlp_ws", "gmlp_bs", "ssd_conv_w", "ssd_conv_b",
         "ssd_dt_bias", "ssd_a_log", "ssd_d", "ssd_norm_g", "even_w_out", "odd_norm_g", "odd_w_in", "sconv_w",
         "odd_w_out", "final_norm_g"]


def kernel(x, even_norm_g, even_w_in, gmlp_ln_g, gmlp_ln_b, gmlp_ws, gmlp_bs, ssd_conv_w, ssd_conv_b, ssd_dt_bias, ssd_a_log, ssd_d, ssd_norm_g, even_w_out, odd_norm_g, odd_w_in, sconv_w, odd_w_out, final_norm_g, loss_target, m_even_norm_g, m_even_w_in, m_gmlp_ln_g, m_gmlp_ln_b, m_gmlp_ws, m_gmlp_bs, m_ssd_conv_w, m_ssd_conv_b, m_ssd_dt_bias, m_ssd_a_log, m_ssd_d, m_ssd_norm_g, m_even_w_out, m_odd_norm_g, m_odd_w_in, m_sconv_w, m_odd_w_out, m_final_norm_g, v_even_norm_g, v_even_w_in, v_gmlp_ln_g, v_gmlp_ln_b, v_gmlp_ws, v_gmlp_bs, v_ssd_conv_w, v_ssd_conv_b, v_ssd_dt_bias, v_ssd_a_log, v_ssd_d, v_ssd_norm_g, v_even_w_out, v_odd_norm_g, v_odd_w_in, v_sconv_w, v_odd_w_out, v_final_norm_g):
    w = dict(even_norm_g=even_norm_g, even_w_in=even_w_in, gmlp_ln_g=gmlp_ln_g, gmlp_ln_b=gmlp_ln_b, gmlp_ws=gmlp_ws,
             gmlp_bs=gmlp_bs, ssd_conv_w=ssd_conv_w, ssd_conv_b=ssd_conv_b, ssd_dt_bias=ssd_dt_bias,
             ssd_a_log=ssd_a_log, ssd_d=ssd_d, ssd_norm_g=ssd_norm_g, even_w_out=even_w_out, odd_norm_g=odd_norm_g,
             odd_w_in=odd_w_in, sconv_w=sconv_w, odd_w_out=odd_w_out, final_norm_g=final_norm_g)
    m1 = dict(even_norm_g=m_even_norm_g, even_w_in=m_even_w_in, gmlp_ln_g=m_gmlp_ln_g, gmlp_ln_b=m_gmlp_ln_b,
              gmlp_ws=m_gmlp_ws, gmlp_bs=m_gmlp_bs, ssd_conv_w=m_ssd_conv_w, ssd_conv_b=m_ssd_conv_b,
              ssd_dt_bias=m_ssd_dt_bias, ssd_a_log=m_ssd_a_log, ssd_d=m_ssd_d, ssd_norm_g=m_ssd_norm_g,
              even_w_out=m_even_w_out, odd_norm_g=m_odd_norm_g, odd_w_in=m_odd_w_in, sconv_w=m_sconv_w,
              odd_w_out=m_odd_w_out, final_norm_g=m_final_norm_g)
    m2 = dict(even_norm_g=v_even_norm_g, even_w_in=v_even_w_in, gmlp_ln_g=v_gmlp_ln_g, gmlp_ln_b=v_gmlp_ln_b,
              gmlp_ws=v_gmlp_ws, gmlp_bs=v_gmlp_bs, ssd_conv_w=v_ssd_conv_w, ssd_conv_b=v_ssd_conv_b,
              ssd_dt_bias=v_ssd_dt_bias, ssd_a_log=v_ssd_a_log, ssd_d=v_ssd_d, ssd_norm_g=v_ssd_norm_g,
              even_w_out=v_even_w_out, odd_norm_g=v_odd_norm_g, odd_w_in=v_odd_w_in, sconv_w=v_sconv_w,
              odd_w_out=v_odd_w_out, final_norm_g=v_final_norm_g)
    _, _, _, me = _place()
    xs = x[0]
    shard = WIDTH // N_DEV

    small_blk = jnp.concatenate([
        ssd_conv_w[0], jnp.pad(sconv_w[0], ((0, 0), (0, shard))), jnp.pad(odd_norm_g, ((0, 0), (0, shard)))], axis=0)
    g_wte, g_wto, g_woe, g_woo, g_small = _all_gather(
        [even_w_in[0].T.astype(BF16), odd_w_in[0].T.astype(BF16), even_w_out[0].astype(BF16),
         odd_w_out[0].astype(BF16), small_blk], "gather_weights")
    wte = g_wte.reshape(F_EVEN_ALL, D_MODEL)
    wte_dt = jnp.pad(wte[F_EVEN:], ((0, LANES - B_HEADS), (0, 0)))
    wto = g_wto.reshape(F_ODD, D_MODEL)
    woe = g_woe.reshape(2 * WIDTH, D_MODEL)
    woo = g_woo.reshape(2 * WIDTH, D_MODEL)
    conv_w = g_small[:, 0:B_CONV, :].transpose(1, 0, 2).reshape(B_CONV, B_XBC)
    sconv_full = g_small[:, B_CONV:B_CONV + C_CONV, :shard].transpose(1, 0, 2).reshape(C_CONV, WIDTH)
    odd_g = g_small[:, B_CONV + C_CONV, :shard].reshape(1, WIDTH)

    pad_heads = lambda a: jnp.pad(a, ((0, 0), (0, LANES - B_HEADS)))
    alog, dtb = pad_heads(ssd_a_log), pad_heads(ssd_dt_bias)
    d_full = jnp.repeat(ssd_d, B_HEAD_DIM, axis=1)
    ws, bs_t = gmlp_ws[0], gmlp_bs[0].T
    ws_t = jnp.swapaxes(ws, 1, 2)
    mult = _attn_multiplicity()
    proj = dict(tm=1024, tn=1024, tk=D_MODEL, out_dtype=F32)
    out_proj = dict(n=D_MODEL, tm=1024, tn=1024, tk=1024, out_dtype=F32)
    back_proj = dict(tb=True, n=2 * WIDTH, tm=512, tn=1024, tk=D_MODEL, out_dtype=F32)
    dw_out = dict(ta=True, n=D_MODEL, tm=512, tn=D_MODEL, tk=512, out_dtype=BF16)
    dx_in = dict(n=D_MODEL, tm=512, tn=D_MODEL, tk=1024, out_dtype=F32)
    dw_in = dict(ta=True, n=D_MODEL, tm=1024, tn=D_MODEL, tk=512, out_dtype=BF16)

    xn0 = _rms_fwd(xs, even_norm_g, "norm_even")
    h0 = _mm([(xn0, None)], wte, tb=True, n=F_EVEN, name="proj_even", **proj)
    hdt = _mm([(xn0, None)], wte_dt, tb=True, n=LANES, tm=1024, tn=LANES, tk=D_MODEL, out_dtype=F32, name="proj_dt")
    y0 = _gmlp_fwd(h0, gmlp_ln_g, gmlp_ln_b, ws, bs_t, "gmlp_fwd")
    pre = _ssd_conv_fwd(h0, conv_w, ssd_conv_b, "ssd_conv_fwd")
    y0, ypre, states = _ssd_scan_fwd(pre, hdt, h0, y0, alog, dtb, d_full, ssd_norm_g, "ssd_scan_fwd")
    x1 = _mm([(y0, None)], woe, add=xs, name="out_even", **out_proj)
    xn1 = _rms_fwd(x1, odd_g, "norm_odd")
    h1 = _mm([(xn1, None)], wto, tb=True, n=F_ODD, name="proj_odd", **proj)
    y1 = _sconv_fwd(h1, sconv_full, "sconv_fwd")
    y1, att_o, att_lse = _attn_fwd(h1, y1, mult, "attn_fwd")
    x2 = _mm([(y1, None)], woo, add=x1, name="out_odd", **out_proj)
    loss_part, dx2, g_final = _loss_head(x2, final_norm_g.reshape(1, D_MODEL), loss_target[0], "loss_head")

    dy1 = _mm([(dx2, None)], woo, name="dy_odd", **back_proj)
    gw_woo = _mm([(y1, None)], dx2, name="dw_out_odd", **dw_out)
    dh1c, g_sconv = _sconv_bwd(h1, dy1, sconv_full, "sconv_bwd")
    dqz, dkv = _attn_bwd(h1, dy1, att_o, att_lse, mult, "attn_bwd")
    dh1 = [(dh1c, None), (dqz, 0), (dkv, 0), (dkv, 1), (dqz, 1)]
    dxn1 = _mm(dh1, wto, name="dx_odd", **dx_in)
    gw_wto = _mm(dh1, xn1, name="dw_in_odd", **dw_in)
    dx1, g_odd = _rms_bwd(x1, odd_g, dxn1, dx2, "norm_odd_bwd")

    dy0 = _mm([(dx1, None)], woe, name="dy_even", **back_proj)
    gw_woe = _mm([(y0, None)], dx1, name="dw_out_even", **dw_out)
    dh0, g_ws, g_bs_t, g_ln_g, g_ln_b = _gmlp_bwd(h0, dy0, gmlp_ln_g, gmlp_ln_b, ws, ws_t, bs_t, "gmlp_bwd")
    dh0, dpx, dpb, dpc, ddt, g_dtb, g_alog, g_dd, g_ng = _ssd_scan_bwd(
        pre, hdt, h0, dy0, ypre, states, dh0, alog, dtb, d_full, ssd_norm_g, "ssd_scan_bwd")
    dh0, g_conv_w, g_conv_b = _ssd_conv_bwd(h0, dh0, dpx, dpb, dpc, conv_w, "ssd_conv_bwd")
    dxn0_dt = _mm([(ddt, None)], wte_dt, n=D_MODEL, tm=1024, tn=D_MODEL, tk=LANES, out_dtype=F32, name="dx_dt")
    dxn0 = _mm([(dh0, None)], wte, add=dxn0_dt, name="dx_even", **dx_in)
    gw_main = _mm([(dh0, None)], xn0, out_rows=F_EVEN_ALL, name="dw_in_even", **dw_in)
    gw_dt = _mm([(ddt, None)], xn0, ta=True, n=D_MODEL, tm=LANES, tn=D_MODEL, tk=512, out_dtype=BF16, name="dw_dt")
    gw_wte = lax.dynamic_update_slice(gw_main, gw_dt[:B_HEADS], (F_EVEN, 0))
    grad_x, g_even = _rms_bwd(xs, even_norm_g, dxn0, dx1, "norm_even_bwd")

    small_parts = dict(
        even_norm_g=g_even, gmlp_ln_g=g_ln_g, gmlp_ln_b=g_ln_b, gmlp_ws=g_ws, gmlp_bs=g_bs_t[:, :A_GROUPS].T,
        ssd_conv_w=g_conv_w[:B_CONV], ssd_conv_b=g_conv_b, ssd_dt_bias=g_dtb[:, :B_HEADS], ssd_a_log=g_alog[:, :B_HEADS],
        ssd_d=g_dd.reshape(B_HEADS, B_HEAD_DIM).sum(axis=1), ssd_norm_g=g_ng, odd_norm_g=g_odd,
        sconv_w=g_sconv[:C_CONV], final_norm_g=g_final)
    full_shapes = dict(
        even_norm_g=(1, D_MODEL), gmlp_ln_g=(1, WIDTH), gmlp_ln_b=(1, WIDTH), gmlp_ws=(1, A_GROUPS, CHUNK, CHUNK),
        gmlp_bs=(1, A_GROUPS, CHUNK), ssd_conv_w=(1, B_CONV, B_XBC), ssd_conv_b=(1, B_XBC), ssd_dt_bias=(1, B_HEADS),
        ssd_a_log=(1, B_HEADS), ssd_d=(1, B_HEADS), ssd_norm_g=(1, WIDTH), odd_norm_g=(1, D_MODEL),
        sconv_w=(1, C_CONV, WIDTH), final_norm_g=(D_MODEL,))
    (gathered_small,) = _all_gather([_pack([small_parts[k] for k in SMALL])], "gather_small_grads")
    small_sum = _sum_parts(gathered_small, LANES, "sum_small_grads")
    grads = dict(zip(SMALL, _unpack(small_sum, [full_shapes[k] for k in SMALL])))
    grads["ssd_conv_w"] = lax.dynamic_slice_in_dim(grads["ssd_conv_w"], me * 2 * shard, 2 * shard, axis=2)
    grads["odd_norm_g"] = lax.dynamic_slice_in_dim(grads["odd_norm_g"], me * shard, shard, axis=1)
    grads["sconv_w"] = lax.dynamic_slice_in_dim(grads["sconv_w"], me * shard, shard, axis=2)

    rows_e, rows_o, rows_out = F_EVEN_ALL // N_DEV, F_ODD // N_DEV, 2 * WIDTH // N_DEV
    r_wte, r_wto, r_woe, r_woo = _all_to_all(
        [gw_wte.reshape(N_DEV, rows_e, D_MODEL), gw_wto.reshape(N_DEV, rows_o, D_MODEL),
         gw_woe.reshape(N_DEV, rows_out, D_MODEL), gw_woo.reshape(N_DEV, rows_out, D_MODEL)], "scatter_grads")
    grads["even_w_in"] = _sum_parts(r_wte, 256, "sum_even_w_in").T[None]
    grads["odd_w_in"] = _sum_parts(r_wto, 256, "sum_odd_w_in").T[None]
    grads["even_w_out"] = _sum_parts(r_woe, 512, "sum_even_w_out")[None]
    grads["odd_w_out"] = _sum_parts(r_woo, 512, "sum_odd_w_out")[None]

    delta, new_m, new_v = {}, {}, {}
    for k in ("even_w_in", "odd_w_in", "even_w_out", "odd_w_out"):
        d_k, m_k, v_k = _adamw(w[k][0], grads[k][0], m1[k][0], m2[k][0], 128, "adamw_" + k)
        delta[k], new_m[k], new_v[k] = d_k[None], m_k[None], v_k[None]
    packed = [_pack([src[k] for k in SMALL]) for src in (w, grads, m1, m2)]
    small_out = _adamw(*packed, packed[0].shape[0], "adamw_small")
    shapes = [w[k].shape for k in SMALL]
    for dst, arr in zip((delta, new_m, new_v), small_out):
        dst.update(zip(SMALL, _unpack(arr, shapes)))

    loss = lax.psum(loss_part[0, 0], ("x", "y", "c"))
    return (loss, grad_x[None], *[grads[k] for k in ORDER], *[delta[k] for k in ORDER],
            *[new_m[k] for k in ORDER], *[new_v[k] for k in ORDER])
```

```python
import functools

import jax
import jax.numpy as jnp
from jax import lax
from jax.experimental import pallas as pl
from jax.experimental.pallas import tpu as pltpu

F32, BF16 = jnp.float32, jnp.bfloat16
MESH = pl.DeviceIdType.MESH
ANY = pl.BlockSpec(memory_space=pl.ANY)

N_DEV = 8
D_MODEL = 2048
WIDTH = 2048
CHUNK = 128
A_GROUPS = 8
B_HEADS, B_HEAD_DIM, B_GROUPS, B_STATE, B_CONV = 32, 64, 8, 128, 4
B_GROUP_W = WIDTH // B_GROUPS
B_XBC = WIDTH + 2 * B_GROUPS * B_STATE
C_CONV = 3
D_HEADS, D_HEAD_DIM = 16, 128
D_PATTERNS = ((128, 1), (512, 4), (2048, 16))
F_EVEN = 3 * WIDTH + WIDTH + B_XBC
F_EVEN_ALL = F_EVEN + B_HEADS
F_ODD = 8 * WIDTH
EPS = 1e-5
NEG = -1e30

ADAM_LR, ADAM_B1, ADAM_B2, ADAM_EPS, ADAM_WD, ADAM_STEP = 0.001, 0.9, 0.999, 1e-08, 0.01, 10

VMEM_LIMIT_V7X = 56 * 1024 * 1024
SUBLANES, LANES = 8, 128


def _cp(*sem):
    return pltpu.CompilerParams(dimension_semantics=sem, vmem_limit_bytes=VMEM_LIMIT_V7X)


def _sig(x):
    return 1.0 / (1.0 + jnp.exp(-x))


def _silu(x):
    return x * _sig(x)


def _dsilu(x):
    s = _sig(x)
    return s * (1.0 + x * (1.0 - s))


def _softplus(x):
    return jnp.maximum(x, 0.0) + jnp.log(1.0 + jnp.exp(-jnp.abs(x)))


def _dot(a, b):
    return jnp.dot(a.astype(BF16), b.astype(BF16), preferred_element_type=F32)


def _dot_nt(a, b):
    return lax.dot_general(a.astype(BF16), b.astype(BF16), (((1,), (1,)), ((), ())), preferred_element_type=F32)


def _dot_tn(a, b):
    return lax.dot_general(a.astype(BF16), b.astype(BF16), (((0,), (0,)), ((), ())), preferred_element_type=F32)


def _split3(x):
    hi = x.astype(BF16)
    r1 = x - hi.astype(F32)
    mid = r1.astype(BF16)
    lo = (r1 - mid.astype(F32)).astype(BF16)
    return hi, mid, lo


def _dot_sel(x, sel):
    return sum(jnp.dot(p, sel, preferred_element_type=F32) for p in _split3(x))


def _sel_dot(sel, x):
    return sum(jnp.dot(sel, p, preferred_element_type=F32) for p in _split3(x))


def _iota(shape, axis):
    return lax.broadcasted_iota(jnp.int32, shape, axis)


def _shift_down(cur, halo, j):
    if j == 0:
        return cur
    r = pltpu.roll(cur, j, 0)
    top = jnp.where(_iota(halo.shape, 0) < j, pltpu.roll(halo, j, 0), r[0:SUBLANES])
    return jnp.concatenate([top, r[SUBLANES:]], axis=0)


def _shift_up(cur, halo, j):
    if j == 0:
        return cur
    n = cur.shape[0]
    r = pltpu.roll(cur, n - j, 0)
    bot = jnp.where(_iota(halo.shape, 0) >= SUBLANES - j, pltpu.roll(halo, SUBLANES - j, 0), r[n - SUBLANES:])
    return jnp.concatenate([r[:n - SUBLANES], bot], axis=0)


def _mm(a_srcs, b, *, ta=False, tb=False, n, tm, tn, tk, out_dtype, out_rows=None, add=None, name):
    nsrc = len(a_srcs)
    feat = [a.shape[-1] for a, _ in a_srcs]
    rows = a_srcs[0][0].shape[-2]
    m, k_len = (sum(feat), rows) if ta else (rows, sum(feat))
    ftile = tm if ta else tk
    cnt = [f // ftile for f in feat]
    start = [sum(cnt[:s]) for s in range(nsrc)]
    nk = k_len // tk
    grid = (m // tm, n // tn, nk)

    def a_spec(s):
        lead, st, c = a_srcs[s][1], start[s], cnt[s]

        def im(i, j, k):
            f = jnp.clip((i if ta else k) - st, 0, c - 1) if nsrc > 1 else (i if ta else k)
            pos = (k, f) if ta else (i, f)
            return pos if lead is None else (lead,) + pos

        blk = (tk, tm) if ta else (tm, tk)
        return pl.BlockSpec(blk if lead is None else (None,) + blk, im)

    b_spec = pl.BlockSpec((tn, tk), lambda i, j, k: (j, k)) if tb else pl.BlockSpec((tk, tn), lambda i, j, k: (k, j))
    io_spec = pl.BlockSpec((tm, tn), lambda i, j, k: (i, j))
    dims = (((0 if ta else 1,), (1 if tb else 0,)), ((), ()))

    def body(*refs):
        a_refs, b_ref = refs[:nsrc], refs[nsrc]
        add_ref = refs[nsrc + 1] if add is not None else None
        o_ref = refs[nsrc + 1 + (add is not None)]
        acc_ref = refs[-1] if nk > 1 else None
        k = pl.program_id(2)
        f = pl.program_id(0) if ta else k

        def finish(r):
            if add_ref is not None:
                r = r + add_ref[...]
            o_ref[...] = r.astype(out_dtype)

        def contribute(a_ref):
            p = lax.dot_general(a_ref[...].astype(BF16), b_ref[...].astype(BF16), dims, preferred_element_type=F32)
            if nk == 1:
                finish(p)
            else:
                @pl.when(k == 0)
                def _():
                    acc_ref[...] = p

                @pl.when(k > 0)
                def _():
                    acc_ref[...] += p

        for s in range(nsrc):
            if nsrc == 1:
                contribute(a_refs[s])
            else:
                pl.when((f >= start[s]) & (f < start[s] + cnt[s]))(functools.partial(contribute, a_refs[s]))
        if nk > 1:
            @pl.when(k == nk - 1)
            def _():
                finish(acc_ref[...])

    return pl.pallas_call(
        body, grid=grid, name=name,
        in_specs=[a_spec(s) for s in range(nsrc)] + [b_spec] + ([io_spec] if add is not None else []),
        out_specs=io_spec,
        out_shape=jax.ShapeDtypeStruct((out_rows or m, n), out_dtype),
        scratch_shapes=[pltpu.VMEM((tm, tn), F32)] if nk > 1 else [],
        compiler_params=_cp("parallel", "parallel", "arbitrary"),
    )(*[a for a, _ in a_srcs], b, *([add] if add is not None else []))


def _rms_fwd(x, g, name):
    t, tb = x.shape[0], 512

    def body(x_ref, g_ref, o_ref):
        xv = x_ref[...]
        r = lax.rsqrt(jnp.mean(xv * xv, axis=-1, keepdims=True) + EPS)
        o_ref[...] = (xv * r * g_ref[...]).astype(BF16)

    row = pl.BlockSpec((tb, D_MODEL), lambda i: (i, 0))
    return pl.pallas_call(
        body, grid=(t // tb,), name=name,
        in_specs=[row, pl.BlockSpec((1, D_MODEL), lambda i: (0, 0))], out_specs=row,
        out_shape=jax.ShapeDtypeStruct((t, D_MODEL), BF16), compiler_params=_cp("parallel"),
    )(x, g)


def _rms_bwd(x, g, dxn, dres, name):
    t, tb = x.shape[0], 256

    def body(x_ref, g_ref, dxn_ref, dres_ref, dx_ref, dg_ref):
        xv = x_ref[...]
        r = lax.rsqrt(jnp.mean(xv * xv, axis=-1, keepdims=True) + EPS)
        nv = xv * r
        dy = dxn_ref[...]
        dn = dy * g_ref[...]
        dx_ref[...] = dres_ref[...] + r * (dn - nv * jnp.mean(dn * nv, axis=-1, keepdims=True))
        part = jnp.sum(dy * nv, axis=0, keepdims=True)

        @pl.when(pl.program_id(0) == 0)
        def _():
            dg_ref[...] = part

        @pl.when(pl.program_id(0) > 0)
        def _():
            dg_ref[...] += part

    row = pl.BlockSpec((tb, D_MODEL), lambda i: (i, 0))
    vec = pl.BlockSpec((1, D_MODEL), lambda i: (0, 0))
    return pl.pallas_call(
        body, grid=(t // tb,), name=name,
        in_specs=[row, vec, row, row], out_specs=[row, vec],
        out_shape=[jax.ShapeDtypeStruct((t, D_MODEL), F32), jax.ShapeDtypeStruct((1, D_MODEL), F32)],
        compiler_params=_cp("arbitrary"),
    )(x, g, dxn, dres)


def _loss_head(x, g, target, name):
    t, tb = x.shape[0], 256

    def body(x_ref, g_ref, t_ref, loss_ref, dx_ref, dg_ref):
        xv, gv = x_ref[...], g_ref[...]
        r = lax.rsqrt(jnp.mean(xv * xv, axis=-1, keepdims=True) + EPS)
        nv = xv * r
        err = nv * gv - t_ref[...]
        lpart = 0.5 * jnp.sum(jnp.mean(err * err, axis=-1, keepdims=True), axis=0, keepdims=True)
        dy = err * (1.0 / D_MODEL)
        dn = dy * gv
        dx_ref[...] = r * (dn - nv * jnp.mean(dn * nv, axis=-1, keepdims=True))
        gpart = jnp.sum(dy * nv, axis=0, keepdims=True)

        @pl.when(pl.program_id(0) == 0)
        def _():
            dg_ref[...] = gpart
            loss_ref[...] = jnp.broadcast_to(lpart, (1, LANES))

        @pl.when(pl.program_id(0) > 0)
        def _():
            dg_ref[...] += gpart
            loss_ref[...] += jnp.broadcast_to(lpart, (1, LANES))

    row = pl.BlockSpec((tb, D_MODEL), lambda i: (i, 0))
    vec = pl.BlockSpec((1, D_MODEL), lambda i: (0, 0))
    return pl.pallas_call(
        body, grid=(t // tb,), name=name,
        in_specs=[row, vec, row], out_specs=[pl.BlockSpec((1, LANES), lambda i: (0, 0)), row, vec],
        out_shape=[jax.ShapeDtypeStruct((1, LANES), F32), jax.ShapeDtypeStruct((t, D_MODEL), F32),
                   jax.ShapeDtypeStruct((1, D_MODEL), F32)],
        compiler_params=_cp("arbitrary"),
    )(x, g, target)


A_GW = WIDTH // A_GROUPS


def _gmlp_common(v, lg, lb):
    xc = v - jnp.mean(v, axis=-1, keepdims=True)
    rs = lax.rsqrt(jnp.mean(xc * xc, axis=-1, keepdims=True) + EPS)
    vh = xc * rs
    return rs, vh, (vh * lg + lb).astype(BF16)


def _gmlp_fwd(h, ln_g, ln_b, ws, bs_t, name):
    t, tb = h.shape[0], 256

    def body(u_ref, v_ref, z_ref, lg_ref, lb_ref, ws_ref, bst_ref, y_ref):
        _, _, vn = _gmlp_common(v_ref[...], lg_ref[...], lb_ref[...])
        causal = _iota((CHUNK, CHUNK), 1) <= _iota((CHUNK, CHUNK), 0)
        for g in range(A_GROUPS):
            w = jnp.where(causal, ws_ref[g], 0.0).astype(BF16)
            cols = slice(g * A_GW, (g + 1) * A_GW)
            for c in range(tb // CHUNK):
                rows = slice(c * CHUNK, (c + 1) * CHUNK)
                mixed = jnp.dot(w, vn[rows, cols], preferred_element_type=F32) + bst_ref[:, g:g + 1]
                y_ref[rows, cols] = (_silu(z_ref[rows, cols]) * (u_ref[rows, cols] * mixed)).astype(BF16)

    col = lambda j: pl.BlockSpec((tb, WIDTH), lambda i: (i, j))
    full = lambda a: pl.BlockSpec(a.shape, lambda i: (0,) * a.ndim)
    return pl.pallas_call(
        body, grid=(t // tb,), name=name,
        in_specs=[col(0), col(1), col(2), full(ln_g), full(ln_b), full(ws), full(bs_t)],
        out_specs=col(0), out_shape=jax.ShapeDtypeStruct((t, 2 * WIDTH), BF16),
        compiler_params=_cp("parallel"),
    )(h, h, h, ln_g, ln_b, ws, bs_t)


def _gmlp_bwd(h, dy, ln_g, ln_b, ws, ws_t, bs_t, name):
    t, tb = h.shape[0], 256

    def body(u_ref, v_ref, z_ref, dy_ref, lg_ref, lb_ref, ws_ref, wst_ref, bst_ref,
             dh_ref, dws_ref, dbst_ref, dlg_ref, dlb_ref, dvn_ref):
        @pl.when(pl.program_id(0) == 0)
        def _():
            dws_ref[...] = jnp.zeros_like(dws_ref)
            dbst_ref[...] = jnp.zeros_like(dbst_ref)
            dlg_ref[...] = jnp.zeros_like(dlg_ref)
            dlb_ref[...] = jnp.zeros_like(dlb_ref)

        rs, vh, vn = _gmlp_common(v_ref[...], lg_ref[...], lb_ref[...])
        row, lane = _iota((CHUNK, CHUNK), 0), _iota((CHUNK, CHUNK), 1)
        for g in range(A_GROUPS):
            w = jnp.where(lane <= row, ws_ref[g], 0.0).astype(BF16)
            wt = jnp.where(row <= lane, wst_ref[g], 0.0).astype(BF16)
            cols = slice(g * A_GW, (g + 1) * A_GW)
            dws_acc = jnp.zeros((CHUNK, CHUNK), F32)
            dbs_acc = jnp.zeros((CHUNK, 1), F32)
            for c in range(tb // CHUNK):
                rows = slice(c * CHUNK, (c + 1) * CHUNK)
                vnb = vn[rows, cols]
                mixed = jnp.dot(w, vnb, preferred_element_type=F32) + bst_ref[:, g:g + 1]
                u, z, dyv = u_ref[rows, cols], z_ref[rows, cols], dy_ref[rows, cols]
                sz = _silu(z)
                dh_ref[rows, cols] = (dyv * sz * mixed).astype(BF16)
                dh_ref[rows, slice(2 * WIDTH + g * A_GW, 2 * WIDTH + (g + 1) * A_GW)] = (
                    dyv * (u * mixed) * _dsilu(z)).astype(BF16)
                dm = dyv * sz * u
                dws_acc += _dot_nt(dm, vnb)
                dbs_acc += jnp.sum(dm, axis=1, keepdims=True)
                dvn_ref[rows, cols] = jnp.dot(wt, dm.astype(BF16), preferred_element_type=F32)
            dws_ref[g] += jnp.where(lane <= row, dws_acc, 0.0)
            dbst_ref[...] += jnp.where(lane == g, dbs_acc, 0.0)
        dvn = dvn_ref[...]
        dlg_ref[...] += jnp.sum(dvn * vh, axis=0, keepdims=True)
        dlb_ref[...] += jnp.sum(dvn, axis=0, keepdims=True)
        dvh = dvn * lg_ref[...]
        dv = rs * (dvh - jnp.mean(dvh, axis=-1, keepdims=True) - vh * jnp.mean(dvh * vh, axis=-1, keepdims=True))
        dh_ref[:, WIDTH:2 * WIDTH] = dv.astype(BF16)

    col = lambda j: pl.BlockSpec((tb, WIDTH), lambda i: (i, j))
    full = lambda a: pl.BlockSpec(a.shape, lambda i: (0,) * a.ndim)
    acc = lambda shape: pl.BlockSpec(shape, lambda i: (0,) * len(shape))
    return pl.pallas_call(
        body, grid=(t // tb,), name=name,
        in_specs=[col(0), col(1), col(2), col(0), full(ln_g), full(ln_b), full(ws), full(ws_t), full(bs_t)],
        out_specs=[pl.BlockSpec((tb, 3 * WIDTH), lambda i: (i, 0)), acc((A_GROUPS, CHUNK, CHUNK)),
                   acc((CHUNK, LANES)), acc((1, WIDTH)), acc((1, WIDTH))],
        out_shape=[jax.ShapeDtypeStruct((t, F_EVEN), BF16), jax.ShapeDtypeStruct((A_GROUPS, CHUNK, CHUNK), F32),
                   jax.ShapeDtypeStruct((CHUNK, LANES), F32), jax.ShapeDtypeStruct((1, WIDTH), F32),
                   jax.ShapeDtypeStruct((1, WIDTH), F32)],
        scratch_shapes=[pltpu.VMEM((tb, WIDTH), F32)],
        compiler_params=_cp("arbitrary"),
    )(h, h, h, dy, ln_g, ln_b, ws, ws_t, bs_t)


def _halo_prev(tb, j):
    return lambda i: (jnp.maximum(i * (tb // SUBLANES) - 1, 0), j)


def _halo_next(tb, j, t):
    return lambda i: (jnp.minimum((i + 1) * (tb // SUBLANES), t // SUBLANES - 1), j)


def _row_select(parts, width):
    row = _iota((SUBLANES, width), 0)
    out = jnp.zeros((SUBLANES, width), F32)
    for k, p in enumerate(parts):
        out = jnp.where(row == k, p, out)
    return out


def _sconv_fwd(h, w, name):
    t, tb = h.shape[0], 256

    def body(bg_ref, cg_ref, hx_ref, z_ref, cgh_ref, hxh_ref, w_ref, y_ref):
        p = cg_ref[...] * hx_ref[...]
        ph = jnp.where(pl.program_id(0) > 0, cgh_ref[...] * hxh_ref[...], 0.0)
        cv = w_ref[2:3, :] * p + w_ref[1:2, :] * _shift_down(p, ph, 1) + w_ref[0:1, :] * _shift_down(p, ph, 2)
        y_ref[...] = (_silu(z_ref[...]) * (bg_ref[...] * cv)).astype(BF16)

    col = lambda j: pl.BlockSpec((tb, WIDTH), lambda i: (i, j))
    halo = lambda j: pl.BlockSpec((SUBLANES, WIDTH), _halo_prev(tb, j))
    return pl.pallas_call(
        body, grid=(t // tb,), name=name,
        in_specs=[col(0), col(1), col(2), col(3), halo(1), halo(2), pl.BlockSpec(w.shape, lambda i: (0, 0))],
        out_specs=col(0), out_shape=jax.ShapeDtypeStruct((t, 2 * WIDTH), BF16),
        compiler_params=_cp("parallel"),
    )(h, h, h, h, h, h, w)


def _sconv_bwd(h, dy, w, name):
    t, tb = h.shape[0], 256
    nb = t // tb

    def body(bg_ref, cg_ref, hx_ref, z_ref, dy_ref, cgh_ref, hxh_ref, bgn_ref, zn_ref, dyn_ref, w_ref, dh_ref, dw_ref):
        i = pl.program_id(0)
        bg, cg, hx, z, dyv = bg_ref[...], cg_ref[...], hx_ref[...], z_ref[...], dy_ref[...]
        p = cg * hx
        ph = jnp.where(i > 0, cgh_ref[...] * hxh_ref[...], 0.0)
        p1, p2 = _shift_down(p, ph, 1), _shift_down(p, ph, 2)
        cv = w_ref[2:3, :] * p + w_ref[1:2, :] * p1 + w_ref[0:1, :] * p2
        sz = _silu(z)
        dcv = dyv * sz * bg
        dcvn = jnp.where(i < nb - 1, dyn_ref[...] * _silu(zn_ref[...]) * bgn_ref[...], 0.0)
        dp = w_ref[2:3, :] * dcv + w_ref[1:2, :] * _shift_up(dcv, dcvn, 1) + w_ref[0:1, :] * _shift_up(dcv, dcvn, 2)
        dh_ref[:, 0:WIDTH] = (dyv * sz * cv).astype(BF16)
        dh_ref[:, WIDTH:2 * WIDTH] = (dp * hx).astype(BF16)
        dh_ref[:, 2 * WIDTH:3 * WIDTH] = (dp * cg).astype(BF16)
        dh_ref[:, 3 * WIDTH:4 * WIDTH] = (dyv * (bg * cv) * _dsilu(z)).astype(BF16)
        part = _row_select([jnp.sum(dcv * q, axis=0, keepdims=True) for q in (p2, p1, p)], WIDTH)

        @pl.when(i == 0)
        def _():
            dw_ref[...] = part

        @pl.when(i > 0)
        def _():
            dw_ref[...] += part

    col = lambda j: pl.BlockSpec((tb, WIDTH), lambda i: (i, j))
    prev = lambda j: pl.BlockSpec((SUBLANES, WIDTH), _halo_prev(tb, j))
    nxt = lambda j: pl.BlockSpec((SUBLANES, WIDTH), _halo_next(tb, j, t))
    return pl.pallas_call(
        body, grid=(nb,), name=name,
        in_specs=[col(0), col(1), col(2), col(3), col(0), prev(1), prev(2), nxt(0), nxt(3), nxt(0),
                  pl.BlockSpec(w.shape, lambda i: (0, 0))],
        out_specs=[pl.BlockSpec((tb, 4 * WIDTH), lambda i: (i, 0)), pl.BlockSpec((SUBLANES, WIDTH), lambda i: (0, 0))],
        out_shape=[jax.ShapeDtypeStruct((t, 4 * WIDTH), BF16), jax.ShapeDtypeStruct((SUBLANES, WIDTH), F32)],
        compiler_params=_cp("arbitrary"),
    )(h, h, h, h, dy, h, h, h, h, dy, w)


XBC_COL0 = 4 * WIDTH


def _ssd_conv_fwd(h, w, b, name):
    t, tb = h.shape[0], 256

    def body(x_ref, xh_ref, w_ref, b_ref, o_ref):
        xv = x_ref[...]
        xh = jnp.where(pl.program_id(0) > 0, xh_ref[...], 0.0)
        acc = b_ref[...] + w_ref[3:4, :] * xv
        for j in range(1, B_CONV):
            acc = acc + w_ref[B_CONV - 1 - j:B_CONV - j, :] * _shift_down(xv, xh, j)
        o_ref[...] = acc

    cb = XBC_COL0 // B_XBC
    return pl.pallas_call(
        body, grid=(t // tb,), name=name,
        in_specs=[pl.BlockSpec((tb, B_XBC), lambda i: (i, cb)), pl.BlockSpec((SUBLANES, B_XBC), _halo_prev(tb, cb)),
                  pl.BlockSpec(w.shape, lambda i: (0, 0)), pl.BlockSpec(b.shape, lambda i: (0, 0))],
        out_specs=pl.BlockSpec((tb, B_XBC), lambda i: (i, 0)), out_shape=jax.ShapeDtypeStruct((t, B_XBC), F32),
        compiler_params=_cp("parallel"),
    )(h, h, w, b)


def _ssd_conv_bwd(h, dh, dpx, dpb, dpc, w, name):
    t, tb, tc = h.shape[0], 256, 1024
    nb = t // tb
    r8 = tb // SUBLANES
    last8 = t // SUBLANES - 1

    def body(dpx_ref, dpb_ref, dpc_ref, nx_ref, nb_ref, nc_ref, x_ref, xh_ref, w_ref, dh_in, dh_ref, dw_ref, db_ref):
        j, i = pl.program_id(0), pl.program_id(1)
        pick = lambda a, b_, c: jnp.where(j < 2, a[...], jnp.where(j == 2, b_[...], c[...]))
        dp = pick(dpx_ref, dpb_ref, dpc_ref)
        dn = jnp.where(i < nb - 1, pick(nx_ref, nb_ref, nc_ref), 0.0)
        xv = x_ref[...]
        xh = jnp.where(i > 0, xh_ref[...], 0.0)
        dx = w_ref[3:4, :] * dp
        for s in range(1, B_CONV):
            dx = dx + w_ref[B_CONV - 1 - s:B_CONV - s, :] * _shift_up(dp, dn, s)
        dh_ref[...] = dx.astype(BF16)
        wpart = _row_select([jnp.sum(dp * _shift_down(xv, xh, B_CONV - 1 - k), axis=0, keepdims=True)
                             for k in range(B_CONV)], tc)
        bpart = jnp.sum(dp, axis=0, keepdims=True)

        @pl.when(i == 0)
        def _():
            dw_ref[...] = wpart
            db_ref[...] = bpart

        @pl.when(i > 0)
        def _():
            dw_ref[...] += wpart
            db_ref[...] += bpart

    def src(blk_rows, rowf, sel, colf):
        return pl.BlockSpec((blk_rows, tc), lambda j, i: (jnp.where(sel(j), rowf(i), 0), colf(j)))

    cur = lambda i: i
    nxt = lambda i: jnp.minimum((i + 1) * r8, last8)
    is_x, is_b, is_c = (lambda j: j < 2), (lambda j: j == 2), (lambda j: j == 3)
    xcol, zero = (lambda j: jnp.minimum(j, 1)), (lambda j: 0)
    c0 = XBC_COL0 // tc
    return pl.pallas_call(
        body, grid=(B_XBC // tc, nb), name=name,
        in_specs=[src(tb, cur, is_x, xcol), src(tb, cur, is_b, zero), src(tb, cur, is_c, zero),
                  src(SUBLANES, nxt, is_x, xcol), src(SUBLANES, nxt, is_b, zero), src(SUBLANES, nxt, is_c, zero),
                  pl.BlockSpec((tb, tc), lambda j, i: (i, c0 + j)),
                  pl.BlockSpec((SUBLANES, tc), lambda j, i: (jnp.maximum(i * r8 - 1, 0), c0 + j)),
                  pl.BlockSpec((B_CONV, tc), lambda j, i: (0, j)), ANY],
        out_specs=[pl.BlockSpec((tb, tc), lambda j, i: (i, c0 + j)), pl.BlockSpec((SUBLANES, tc), lambda j, i: (0, j)),
                   pl.BlockSpec((1, tc), lambda j, i: (0, j))],
        out_shape=[jax.ShapeDtypeStruct(dh.shape, dh.dtype), jax.ShapeDtypeStruct((SUBLANES, B_XBC), F32),
                   jax.ShapeDtypeStruct((1, B_XBC), F32)],
        input_output_aliases={9: 0},
        compiler_params=_cp("arbitrary", "arbitrary"),
    )(dpx, dpb, dpc, dpx, dpb, dpc, h, h, w, dh)


HEADS_PER_GROUP = B_HEADS // B_GROUPS


def _ssd_group_terms(g, px, pb, pc, dtr, alog, dtb):
    xs, bm, cm = _silu(px), _silu(pb), _silu(pc)
    dt = _softplus(dtr + dtb)
    a = -jnp.exp(alog)
    head_of_lane = _iota((LANES, B_GROUP_W), 1) // B_HEAD_DIM + HEADS_PER_GROUP * g
    expand = (_iota((LANES, B_GROUP_W), 0) == head_of_lane).astype(BF16)
    local = _iota((LANES, LANES), 1)
    sel = ((_iota((LANES, LANES), 0) == local + HEADS_PER_GROUP * g) & (local < HEADS_PER_GROUP)).astype(BF16)
    tri = (_iota((CHUNK, CHUNK), 1) <= _iota((CHUNK, CHUNK), 0)).astype(BF16)
    dt_e = _dot_sel(dt, expand)
    a_e = _sel_dot(tri, _dot_sel(dt * a, expand))
    a4 = _sel_dot(tri, _dot_sel(dt * a, sel))
    sub = _iota((LANES, LANES), 0)
    sel_t = ((local == sub + HEADS_PER_GROUP * g) & (sub < HEADS_PER_GROUP)).astype(BF16)
    return dict(xs=xs, bm=bm, cm=cm, dt=dt, a=a, sel=sel, sel_t=sel_t, dt_e=dt_e, a_e=a_e, a4=a4, a4t=a4.T)


def _ssd_decay(tm, j, transposed):
    col, row = tm["a4"][:, j:j + 1], tm["a4t"][j:j + 1, :]
    lane, sub = _iota((CHUNK, CHUNK), 1), _iota((CHUNK, CHUNK), 0)
    if transposed:
        return jnp.where(sub <= lane, jnp.exp(jnp.minimum(row - col, 0.0)), 0.0)
    return jnp.where(lane <= sub, jnp.exp(jnp.minimum(col - row, 0.0)), 0.0)


def _ssd_specs(nc, rev):
    ch = (lambda c: nc - 1 - c) if rev else (lambda c: c)
    gw = lambda off: pl.BlockSpec((CHUNK, B_GROUP_W), lambda c, g: (ch(c), off + g))
    gn = lambda off: pl.BlockSpec((CHUNK, B_STATE), lambda c, g: (ch(c), off + g))
    tok = pl.BlockSpec((CHUNK, LANES), lambda c, g: (ch(c), 0))
    vec = pl.BlockSpec((1, LANES), lambda c, g: (0, 0))
    gvec = pl.BlockSpec((1, B_GROUP_W), lambda c, g: (0, g))
    st = pl.BlockSpec((None, B_STATE, B_GROUP_W), lambda c, g: (ch(c), 0, g))
    return gw, gn, tok, vec, gvec, st


def _ssd_scan_fwd(pre, hdt, h, y, alog, dtb, dfull, ng, name):
    t = pre.shape[0]
    nc = t // CHUNK

    def body(px_ref, pb_ref, pc_ref, dtr_ref, z_ref, alog_ref, dtb_ref, df_ref, ng_ref, y_in,
             y_ref, ypre_ref, st_ref, state_ref):
        c, g = pl.program_id(0), pl.program_id(1)

        @pl.when(c == 0)
        def _():
            state_ref[g] = jnp.zeros((B_STATE, B_GROUP_W), F32)

        tm = _ssd_group_terms(g, px_ref[...], pb_ref[...], pc_ref[...], dtr_ref[...], alog_ref[...], dtb_ref[...])
        xs, bm, cm, a_e = tm["xs"], tm["bm"], tm["cm"], tm["a_e"]
        xdt = xs * tm["dt_e"]
        cb = _dot_nt(cm, bm)
        head = _iota((CHUNK, B_GROUP_W), 1) // B_HEAD_DIM
        yd = jnp.zeros((CHUNK, B_GROUP_W), F32)
        for j in range(HEADS_PER_GROUP):
            yd = jnp.where(head == j, _dot(cb * _ssd_decay(tm, j, False), xdt), yd)
        st = state_ref[g]
        st_ref[...] = st
        yv = yd + jnp.exp(a_e) * _dot(cm, st) + df_ref[...] * xs
        a_last = a_e[CHUNK - 1:CHUNK, :]
        state_ref[g] = st * jnp.exp(a_last) + _dot_tn(bm, xdt * jnp.exp(a_last - a_e))
        ypre_ref[...] = yv
        yz = yv * _silu(z_ref[...])
        r = lax.rsqrt(jnp.mean(yz * yz, axis=-1, keepdims=True) + EPS)
        y_ref[...] = (yz * r * ng_ref[...]).astype(BF16)

    gw, gn, tok, vec, gvec, st = _ssd_specs(nc, False)
    return pl.pallas_call(
        body, grid=(nc, B_GROUPS), name=name,
        in_specs=[gw(0), gn(WIDTH // B_STATE), gn((WIDTH + B_GROUPS * B_STATE) // B_STATE), tok,
                  gw(3 * WIDTH // B_GROUP_W), vec, vec, gvec, gvec, ANY],
        out_specs=[gw(WIDTH // B_GROUP_W), gw(0), st],
        out_shape=[jax.ShapeDtypeStruct(y.shape, y.dtype), jax.ShapeDtypeStruct((t, WIDTH), F32),
                   jax.ShapeDtypeStruct((nc, B_STATE, WIDTH), F32)],
        scratch_shapes=[pltpu.VMEM((B_GROUPS, B_STATE, B_GROUP_W), F32)],
        input_output_aliases={9: 0},
        compiler_params=_cp("arbitrary", "arbitrary"),
    )(pre, pre, pre, hdt, h, alog, dtb, dfull, ng, y)


def _ssd_scan_bwd(pre, hdt, h, dy, ypre, states, dh, alog, dtb, dfull, ng, name):
    t = pre.shape[0]
    nc = t // CHUNK

    def body(px_ref, pb_ref, pc_ref, dtr_ref, z_ref, dy_ref, ypre_ref, st_ref, alog_ref, dtb_ref, df_ref, ng_ref,
             dh_in, dz_ref, dpx_ref, dpb_ref, dpc_ref, ddt_ref, dbias_ref, dalog_ref, dd_ref, dng_ref, dstate_ref):
        c, g = pl.program_id(0), pl.program_id(1)

        @pl.when(c == 0)
        def _():
            dstate_ref[g] = jnp.zeros((B_STATE, B_GROUP_W), F32)
            dd_ref[g] = jnp.zeros((1, B_GROUP_W), F32)
            dng_ref[g] = jnp.zeros((1, B_GROUP_W), F32)

        @pl.when((c == 0) & (g == 0))
        def _():
            dbias_ref[...] = jnp.zeros_like(dbias_ref)
            dalog_ref[...] = jnp.zeros_like(dalog_ref)

        px, pb, pc, dtr = px_ref[...], pb_ref[...], pc_ref[...], dtr_ref[...]
        tm = _ssd_group_terms(g, px, pb, pc, dtr, alog_ref[...], dtb_ref[...])
        xs, bm, cm, a_e, dt_e = tm["xs"], tm["bm"], tm["cm"], tm["a_e"], tm["dt_e"]
        xdt = xs * dt_e
        head = _iota((CHUNK, B_GROUP_W), 1) // B_HEAD_DIM

        z, yv, ngv = z_ref[...], ypre_ref[...], ng_ref[...]
        sz = _silu(z)
        yz = yv * sz
        r = lax.rsqrt(jnp.mean(yz * yz, axis=-1, keepdims=True) + EPS)
        dyn = dy_ref[...]
        dng_ref[g] += jnp.sum(dyn * yz * r, axis=0, keepdims=True)
        q = dyn * ngv
        dyz = r * q - yz * (r * r * r) * jnp.mean(q * yz, axis=-1, keepdims=True)
        dyv = dyz * sz
        dz_ref[...] = (dyz * yv * _dsilu(z)).astype(BF16)
        dd_ref[g] += jnp.sum(dyv * xs, axis=0, keepdims=True)
        dxs = df_ref[...] * dyv

        st = st_ref[...]
        ea = jnp.exp(a_e)
        ead = ea * dyv
        dcm = _dot_nt(ead, st)
        da_e = dyv * (ea * _dot(cm, st))

        dsn = dstate_ref[g]
        a_last = a_e[CHUNK - 1:CHUNK, :]
        ea_last = jnp.exp(a_last)
        dstate_ref[g] = dsn * ea_last + _dot_tn(cm, ead)
        wdec = jnp.exp(a_last - a_e)
        xw = xdt * wdec
        dxw = _dot(bm, dsn)
        dxdt = dxw * wdec
        dbm = _dot_nt(xw, dsn)
        zc = dxw * xw
        da_last = jnp.sum(zc, axis=0, keepdims=True) + jnp.sum(dsn * st, axis=0, keepdims=True) * ea_last
        da_e = da_e - zc + jnp.where(_iota((CHUNK, B_GROUP_W), 0) == CHUNK - 1, da_last, 0.0)

        cb, cbt = _dot_nt(cm, bm), _dot_nt(bm, cm)
        dcb, dcbt = jnp.zeros((CHUNK, CHUNK), F32), jnp.zeros((CHUNK, CHUNK), F32)
        da4 = jnp.zeros((CHUNK, LANES), F32)
        lane = _iota((CHUNK, LANES), 1)
        for j in range(HEADS_PER_GROUP):
            mine = head == j
            gm = _dot_nt(jnp.where(mine, dyv, 0.0), xdt)
            gmt = _dot_nt(jnp.where(mine, xdt, 0.0), dyv)
            dec, dect = _ssd_decay(tm, j, False), _ssd_decay(tm, j, True)
            dcb += gm * dec
            dcbt += gmt * dect
            da_j = (jnp.sum(gm * cb * dec, axis=1, keepdims=True) - jnp.sum(gmt * cbt * dect, axis=1, keepdims=True))
            da4 = jnp.where(lane == j, da_j, da4)
            dxdt = dxdt + jnp.where(mine, _dot(cbt * dect, dyv), 0.0)
        dcm = dcm + _dot(dcb, bm)
        dbm = dbm + _dot(dcbt, cm)

        gather = (_iota((B_GROUP_W, LANES), 0) // B_HEAD_DIM == _iota((B_GROUP_W, LANES), 1)).astype(BF16)
        da4 = da4 + _dot_sel(da_e, gather)
        rtri = (_iota((CHUNK, CHUNK), 1) >= _iota((CHUNK, CHUNK), 0)).astype(BF16)
        dadt4 = _sel_dot(rtri, da4)
        dt4 = _dot_sel(tm["dt"], tm["sel"])
        a4row = _dot_sel(jnp.broadcast_to(tm["a"], (SUBLANES, LANES)), tm["sel"])[0:1, :]
        ddt4 = dadt4 * a4row + _dot_sel(dxdt * xs, gather)
        dxs = dxs + dxdt * dt_e
        sel_t = tm["sel_t"]
        ddt = _dot_sel(ddt4, sel_t) * _sig(dtr + dtb_ref[...])
        da_heads = _dot_sel(jnp.broadcast_to(jnp.sum(dadt4 * dt4, axis=0, keepdims=True), (SUBLANES, LANES)), sel_t)[0:1, :]

        @pl.when(g == 0)
        def _():
            ddt_ref[...] = ddt

        @pl.when(g > 0)
        def _():
            ddt_ref[...] += ddt

        dbias_ref[...] += jnp.sum(ddt, axis=0, keepdims=True)
        dalog_ref[...] += da_heads * tm["a"]
        dpx_ref[...] = dxs * _dsilu(px)
        dpb_ref[...] = dbm * _dsilu(pb)
        dpc_ref[...] = dcm * _dsilu(pc)

    gw, gn, tok, vec, gvec, st = _ssd_specs(nc, True)
    acc = lambda shape: pl.BlockSpec(shape, lambda c, g: (0,) * len(shape))
    rev = lambda c: nc - 1 - c
    return pl.pallas_call(
        body, grid=(nc, B_GROUPS), name=name,
        in_specs=[gw(0), gn(WIDTH // B_STATE), gn((WIDTH + B_GROUPS * B_STATE) // B_STATE), tok,
                  gw(3 * WIDTH // B_GROUP_W), gw(WIDTH // B_GROUP_W), gw(0), st, vec, vec, gvec, gvec, ANY],
        out_specs=[gw(3 * WIDTH // B_GROUP_W), gw(0),
                   pl.BlockSpec((CHUNK, B_STATE), lambda c, g: (rev(c), g)),
                   pl.BlockSpec((CHUNK, B_STATE), lambda c, g: (rev(c), g)),
                   tok, vec, vec, acc((B_GROUPS, 1, B_GROUP_W)), acc((B_GROUPS, 1, B_GROUP_W))],
        out_shape=[jax.ShapeDtypeStruct(dh.shape, dh.dtype), jax.ShapeDtypeStruct((t, WIDTH), F32),
                   jax.ShapeDtypeStruct((t, B_GROUPS * B_STATE), F32), jax.ShapeDtypeStruct((t, B_GROUPS * B_STATE), F32),
                   jax.ShapeDtypeStruct((t, LANES), F32), jax.ShapeDtypeStruct((1, LANES), F32),
                   jax.ShapeDtypeStruct((1, LANES), F32), jax.ShapeDtypeStruct((B_GROUPS, 1, B_GROUP_W), F32),
                   jax.ShapeDtypeStruct((B_GROUPS, 1, B_GROUP_W), F32)],
        scratch_shapes=[pltpu.VMEM((B_GROUPS, B_STATE, B_GROUP_W), F32)],
        input_output_aliases={12: 0},
        compiler_params=_cp("arbitrary", "arbitrary"),
    )(pre, pre, pre, hdt, h, dy, ypre, states, alog, dtb, dfull, ng, dh)


Q_COL, K_COL, V_COL, Z_COL = [(4 + i) * WIDTH // D_HEAD_DIM for i in range(4)]
ATT_SCALE = D_HEAD_DIM ** -0.5


SPAN = 2048


def _att_blocks():
    out = []
    for _, dil in D_PATTERNS:
        nbl = SPAN // (CHUNK * dil)
        for r in range(dil):
            for bl in range(nbl):
                st = r + dil * CHUNK * bl
                out.append((dil, st, bl > 0, st - dil * CHUNK if bl > 0 else r + dil * CHUNK * (nbl - 1)))
    return out


def _rows(start, dil):
    return pl.ds(start, CHUNK) if dil == 1 else pl.ds(start, CHUNK, stride=dil)


def _att_keys(kc_ref, kp_ref, blk):
    dil, st, inside, pst = blk
    prev = (kc_ref if inside else kp_ref)[_rows(pst, dil), :]
    return jnp.concatenate([prev, kc_ref[_rows(st, dil), :]], axis=0).astype(BF16)


def _att_band(span_index):
    lane, sub = _iota((CHUNK, 2 * CHUNK), 1), _iota((CHUNK, 2 * CHUNK), 0)
    band = (lane >= sub) & (lane <= sub + CHUNK)
    return band, band & ((lane >= CHUNK) | (span_index > 0))


def _att_specs(t):
    blk = lambda off: pl.BlockSpec((SPAN, D_HEAD_DIM), lambda hd, sb: (sb, off + hd))
    prev = lambda off: pl.BlockSpec((SPAN, D_HEAD_DIM), lambda hd, sb: (jnp.maximum(sb - 1, 0), off + hd))
    lse = pl.BlockSpec((None, SPAN // CHUNK, CHUNK), lambda hd, sb: (hd, sb, 0))
    return blk, prev, lse


def _attn2_fwd(h, y, name):
    t = h.shape[0]

    def body(q_ref, kc_ref, vc_ref, kp_ref, vp_ref, z_ref, y_in, y_ref, o_ref, lse_ref, m_ref, l_ref):
        band, band_first = _att_band(pl.program_id(1))
        for blk in _att_blocks():
            dil, st, inside, _ = blk
            rows = _rows(st, dil)
            k2, v2 = _att_keys(kc_ref, kp_ref, blk), _att_keys(vc_ref, vp_ref, blk)
            s = jnp.where(band if inside else band_first, _dot_nt(q_ref[rows, :], k2) * ATT_SCALE, NEG)
            m_b = jnp.max(s, axis=1, keepdims=True)
            p = jnp.exp(s - m_b)
            l_b = jnp.sum(p, axis=1, keepdims=True)
            o_b = _dot(p, v2)
            wide = lambda a: jnp.broadcast_to(a, (CHUNK, D_HEAD_DIM))
            if dil == 1:
                m_ref[rows, :], l_ref[rows, :], o_ref[rows, :] = wide(m_b), wide(l_b), o_b
            else:
                m_o = m_ref[rows, :]
                m_n = jnp.maximum(m_o, m_b)
                a_o, a_b = jnp.exp(m_o - m_n), jnp.exp(m_b - m_n)
                m_ref[rows, :] = m_n
                l_ref[rows, :] = a_o * l_ref[rows, :] + a_b * l_b
                o_ref[rows, :] = a_o * o_ref[rows, :] + a_b * o_b
        l = l_ref[...]
        o = o_ref[...] / l
        o_ref[...] = o
        y_ref[...] = (_silu(z_ref[...]) * o).astype(BF16)
        for i in range(SPAN // CHUNK):
            blk_rows = slice(i * CHUNK, (i + 1) * CHUNK)
            lse_ref[i:i + 1, :] = (m_ref[blk_rows, :] + jnp.log(l[blk_rows, :])).T[0:1, :]

    blk, prev, lse_spec = _att_specs(t)
    return pl.pallas_call(
        body, grid=(D_HEADS, t // SPAN), name=name,
        in_specs=[blk(Q_COL), blk(K_COL), blk(V_COL), prev(K_COL), prev(V_COL), blk(Z_COL), ANY],
        out_specs=[blk(WIDTH // D_HEAD_DIM), blk(0), lse_spec],
        out_shape=[jax.ShapeDtypeStruct(y.shape, y.dtype), jax.ShapeDtypeStruct((t, WIDTH), F32),
                   jax.ShapeDtypeStruct((D_HEADS, t // CHUNK, CHUNK), F32)],
        scratch_shapes=[pltpu.VMEM((SPAN, D_HEAD_DIM), F32)] * 2,
        input_output_aliases={6: 0},
        compiler_params=_cp("parallel", "parallel"),
    )(h, h, h, h, h, h, y)


def _attn2_bwd(h, dy, o, lse, name):
    t = h.shape[0]
    ns = t // SPAN

    def body(q_ref, kc_ref, vc_ref, kp_ref, vp_ref, z_ref, dy_ref, o_ref, lse_ref, dqz_ref, dkv_ref,
             acc_ref, dq_ref, do_ref, delta_ref, lsec_ref):
        sb = pl.program_id(1)

        @pl.when(sb == 0)
        def _():
            acc_ref[...] = jnp.zeros_like(acc_ref)

        z, ov, dyv = z_ref[...], o_ref[...], dy_ref[...]
        do = dyv * _silu(z)
        do_ref[...] = do
        dqz_ref[1] = (dyv * ov * _dsilu(z)).astype(BF16)
        delta_ref[...] = jnp.broadcast_to(jnp.sum(do * ov, axis=1, keepdims=True), (SPAN, D_HEAD_DIM))
        for i in range(SPAN // CHUNK):
            lsec_ref[i * CHUNK:(i + 1) * CHUNK, :] = jnp.broadcast_to(lse_ref[i:i + 1, :], (CHUNK, CHUNK)).T
        band, band_first = _att_band(sb)
        here, before = sb * SPAN, jnp.maximum(sb - 1, 0) * SPAN
        for blk in _att_blocks():
            dil, st, inside, pst = blk
            rows = _rows(st, dil)
            q = q_ref[rows, :].astype(BF16)
            k2, v2 = _att_keys(kc_ref, kp_ref, blk), _att_keys(vc_ref, vp_ref, blk)
            dob = do_ref[rows, :].astype(BF16)
            s = jnp.where(band if inside else band_first, _dot_nt(q, k2) * ATT_SCALE, NEG)
            p = jnp.exp(s - lsec_ref[rows, :][:, 0:1])
            ds = p * (_dot_nt(dob, v2) - delta_ref[rows, :][:, 0:1]) * ATT_SCALE
            dq_b = _dot(ds, k2)
            if dil == 1:
                dq_ref[rows, :] = dq_b
            else:
                dq_ref[rows, :] += dq_b
            dk2, dv2 = _dot_tn(ds, q), _dot_tn(p, dob)
            own = _rows(pl.multiple_of(here + st, CHUNK) if dil == 1 else here + st, dil)
            pbase = (here if inside else before) + pst
            prv = _rows(pl.multiple_of(pbase, CHUNK) if dil == 1 else pbase, dil)
            acc_ref[0, own, :] += dk2[CHUNK:]
            acc_ref[1, own, :] += dv2[CHUNK:]
            acc_ref[0, prv, :] += dk2[:CHUNK]
            acc_ref[1, prv, :] += dv2[:CHUNK]
        dqz_ref[0] = dq_ref[...].astype(BF16)

        @pl.when(sb == ns - 1)
        def _():
            dkv_ref[...] = acc_ref[...].astype(BF16)

    blk, prev, lse_spec = _att_specs(t)
    span = lambda: pltpu.VMEM((SPAN, D_HEAD_DIM), F32)
    return pl.pallas_call(
        body, grid=(D_HEADS, ns), name=name,
        in_specs=[blk(Q_COL), blk(K_COL), blk(V_COL), prev(K_COL), prev(V_COL), blk(Z_COL), blk(WIDTH // D_HEAD_DIM),
                  blk(0), lse_spec],
        out_specs=[pl.BlockSpec((2, SPAN, D_HEAD_DIM), lambda hd, sb: (0, sb, hd)),
                   pl.BlockSpec((2, t, D_HEAD_DIM), lambda hd, sb: (0, 0, hd))],
        out_shape=[jax.ShapeDtypeStruct((2, t, WIDTH), BF16), jax.ShapeDtypeStruct((2, t, WIDTH), BF16)],
        scratch_shapes=[pltpu.VMEM((2, t, D_HEAD_DIM), F32), span(), span(), span(), span()],
        compiler_params=_cp("arbitrary", "arbitrary"),
    )(h, h, h, h, h, h, dy, o, lse)


def _place():
    x, y, c = lax.axis_index("x"), lax.axis_index("y"), lax.axis_index("c")
    return x, y, c, 4 * x + 2 * y + c


def _all_gather(blocks, name):
    n = len(blocks)

    def body(*refs):
        ins, outs = refs[:n], refs[n:2 * n]
        send_sems, recv_sems, local_sems = refs[2 * n:]
        x, y, c, me = _place()
        sibling = (x, y, 1 - c)
        chips = [(1 - x, y), (x, 1 - y), (1 - x, 1 - y)]
        index = lambda px, py, pc: 4 * px + 2 * py + pc

        def copy(a, k, block, to, src=None):
            dst = outs[a].at[block]
            return pltpu.make_async_remote_copy(
                src_ref=dst if src is None else src, dst_ref=dst, send_sem=send_sems.at[a, k],
                recv_sem=recv_sems.at[a, k], device_id=to, device_id_type=MESH)

        started = []
        for a in range(n):
            mine = pltpu.make_async_copy(ins[a], outs[a].at[me], local_sems.at[a])
            mine.start()
            started.append(mine)
        sent = []
        for a in range(n):
            sent.append(copy(a, 0, me, sibling, src=ins[a]))
            sent += [copy(a, 1 + j, me, (*chip, c), src=ins[a]) for j, chip in enumerate(chips)]
        for cp in sent:
            cp.start()
        for j, chip in enumerate(chips):
            for a in range(n):
                copy(a, 1 + j, index(*chip, c), (x, y, c)).wait_recv()
                passed = copy(a, 4 + j, index(*chip, c), sibling)
                passed.start()
                sent.append(passed)
        for a in range(n):
            copy(a, 0, index(x, y, 1 - c), (x, y, c)).wait_recv()
        for j, chip in enumerate(chips):
            for a in range(n):
                copy(a, 4 + j, index(*chip, 1 - c), (x, y, c)).wait_recv()
        for cp in sent:
            cp.wait_send()
        for mine in started:
            mine.wait()

    return pl.pallas_call(
        body, name=name, in_specs=[ANY] * n, out_specs=[ANY] * n,
        out_shape=[jax.ShapeDtypeStruct((N_DEV,) + b.shape, b.dtype) for b in blocks],
        scratch_shapes=[pltpu.SemaphoreType.DMA((n, 7)), pltpu.SemaphoreType.DMA((n, 7)), pltpu.SemaphoreType.DMA((n,))],
    )(*blocks)


def _all_to_all(parts, name):
    n = len(parts)
    flips = [(fx, fy, fc) for fx in (0, 1) for fy in (0, 1) for fc in (0, 1)][1:]

    def body(*refs):
        ins, outs = refs[:n], refs[n:2 * n]
        send_sems, recv_sems, local_sems = refs[2 * n:]
        x, y, c, me = _place()
        flip = lambda v, f: 1 - v if f else v
        peers = [(flip(x, fx), flip(y, fy), flip(c, fc)) for fx, fy, fc in flips]

        def copy(a, k, sending):
            px, py, pc = peers[k]
            there = 4 * px + 2 * py + pc
            return pltpu.make_async_remote_copy(
                src_ref=ins[a].at[there], dst_ref=outs[a].at[me if sending else there], send_sem=send_sems.at[a, k],
                recv_sem=recv_sems.at[a, k], device_id=peers[k], device_id_type=MESH)

        local = [pltpu.make_async_copy(ins[a].at[me], outs[a].at[me], local_sems.at[a]) for a in range(n)]
        for cp in local:
            cp.start()
        for k in range(7):
            for a in range(n):
                copy(a, k, True).start()
        for k in range(7):
            for a in range(n):
                copy(a, k, False).wait()
        for cp in local:
            cp.wait()

    return pl.pallas_call(
        body, name=name, in_specs=[ANY] * n, out_specs=[ANY] * n,
        out_shape=[jax.ShapeDtypeStruct(p.shape, p.dtype) for p in parts],
        scratch_shapes=[pltpu.SemaphoreType.DMA((n, 7)), pltpu.SemaphoreType.DMA((n, 7)), pltpu.SemaphoreType.DMA((n,))],
    )(*parts)


def _sum_parts(parts, tc, name):
    _, r, c = parts.shape

    def body(p_ref, o_ref):
        acc = p_ref[0].astype(F32)
        for d in range(1, N_DEV):
            acc = acc + p_ref[d].astype(F32)
        o_ref[...] = acc

    return pl.pallas_call(
        body, grid=(c // tc,), name=name,
        in_specs=[pl.BlockSpec((N_DEV, r, tc), lambda j: (0, 0, j))], out_specs=pl.BlockSpec((r, tc), lambda j: (0, j)),
        out_shape=jax.ShapeDtypeStruct((r, c), F32), compiler_params=_cp("parallel"),
    )(parts)


def _adamw(w, g, m, v, tr, name):
    r, c = w.shape

    def body(w_ref, g_ref, m_ref, v_ref, d_ref, m2_ref, v2_ref):
        gv = g_ref[...]
        m2 = ADAM_B1 * m_ref[...] + (1.0 - ADAM_B1) * gv
        v2 = ADAM_B2 * v_ref[...] + (1.0 - ADAM_B2) * (gv * gv)
        m_hat = m2 / (1.0 - ADAM_B1 ** ADAM_STEP)
        v_hat = v2 / (1.0 - ADAM_B2 ** ADAM_STEP)
        d_ref[...] = -ADAM_LR * (m_hat / (jnp.sqrt(v_hat) + ADAM_EPS) + ADAM_WD * w_ref[...])
        m2_ref[...] = m2
        v2_ref[...] = v2

    spec = pl.BlockSpec((tr, c), lambda i: (i, 0))
    return pl.pallas_call(
        body, grid=(r // tr,), name=name, in_specs=[spec] * 4, out_specs=[spec] * 3,
        out_shape=[jax.ShapeDtypeStruct((r, c), F32)] * 3, compiler_params=_cp("parallel"),
    )(w, g, m, v)


PACK_ROWS = SUBLANES * LANES


def _pack(arrays):
    flat = [jnp.pad(a.reshape(-1), (0, -a.size % PACK_ROWS)) for a in arrays]
    return jnp.concatenate(flat).reshape(-1, LANES)


def _unpack(packed, shapes):
    flat, out, pos = packed.reshape(-1), [], 0
    for s in shapes:
        size = 1
        for d in s:
            size *= d
        out.append(flat[pos:pos + size].reshape(s))
        pos += size + (-size % PACK_ROWS)
    return out


SMALL = ["even_norm_g", "gmlp_ln_g", "gmlp_ln_b", "gmlp_ws", "gmlp_bs", "ssd_conv_w", "ssd_conv_b", "ssd_dt_bias",
         "ssd_a_log", "ssd_d", "ssd_norm_g", "odd_norm_g", "sconv_w", "final_norm_g"]
ORDER = ["even_norm_g", "even_w_in", "gmlp_ln_g", "gmlp_ln_b", "gmlp_ws", "gmlp_bs", "ssd_conv_w", "ssd_conv_b",
         "ssd_dt_bias", "ssd_a_log", "ssd_d", "ssd_norm_g", "even_w_out", "odd_norm_g", "odd_w_in", "sconv_w",
         "odd_w_out", "final_norm_g"]


def kernel(x, even_norm_g, even_w_in, gmlp_ln_g, gmlp_ln_b, gmlp_ws, gmlp_bs, ssd_conv_w, ssd_conv_b, ssd_dt_bias, ssd_a_log, ssd_d, ssd_norm_g, even_w_out, odd_norm_g, odd_w_in, sconv_w, odd_w_out, final_norm_g, loss_target, m_even_norm_g, m_even_w_in, m_gmlp_ln_g, m_gmlp_ln_b, m_gmlp_ws, m_gmlp_bs, m_ssd_conv_w, m_ssd_conv_b, m_ssd_dt_bias, m_ssd_a_log, m_ssd_d, m_ssd_norm_g, m_even_w_out, m_odd_norm_g, m_odd_w_in, m_sconv_w, m_odd_w_out, m_final_norm_g, v_even_norm_g, v_even_w_in, v_gmlp_ln_g, v_gmlp_ln_b, v_gmlp_ws, v_gmlp_bs, v_ssd_conv_w, v_ssd_conv_b, v_ssd_dt_bias, v_ssd_a_log, v_ssd_d, v_ssd_norm_g, v_even_w_out, v_odd_norm_g, v_odd_w_in, v_sconv_w, v_odd_w_out, v_final_norm_g):
    w = dict(even_norm_g=even_norm_g, even_w_in=even_w_in, gmlp_ln_g=gmlp_ln_g, gmlp_ln_b=gmlp_ln_b, gmlp_ws=gmlp_ws,
             gmlp_bs=gmlp_bs, ssd_conv_w=ssd_conv_w, ssd_conv_b=ssd_conv_b, ssd_dt_bias=ssd_dt_bias,
             ssd_a_log=ssd_a_log, ssd_d=ssd_d, ssd_norm_g=ssd_norm_g, even_w_out=even_w_out, odd_norm_g=odd_norm_g,
             odd_w_in=odd_w_in, sconv_w=sconv_w, odd_w_out=odd_w_out, final_norm_g=final_norm_g)
    m1 = dict(even_norm_g=m_even_norm_g, even_w_in=m_even_w_in, gmlp_ln_g=m_gmlp_ln_g, gmlp_ln_b=m_gmlp_ln_b,
              gmlp_ws=m_gmlp_ws, gmlp_bs=m_gmlp_bs, ssd_conv_w=m_ssd_conv_w, ssd_conv_b=m_ssd_conv_b,
              ssd_dt_bias=m_ssd_dt_bias, ssd_a_log=m_ssd_a_log, ssd_d=m_ssd_d, ssd_norm_g=m_ssd_norm_g,
              even_w_out=m_even_w_out, odd_norm_g=m_odd_norm_g, odd_w_in=m_odd_w_in, sconv_w=m_sconv_w,
              odd_w_out=m_odd_w_out, final_norm_g=m_final_norm_g)
    m2 = dict(even_norm_g=v_even_norm_g, even_w_in=v_even_w_in, gmlp_ln_g=v_gmlp_ln_g, gmlp_ln_b=v_gmlp_ln_b,
              gmlp_ws=v_gmlp_ws, gmlp_bs=v_gmlp_bs, ssd_conv_w=v_ssd_conv_w, ssd_conv_b=v_ssd_conv_b,
              ssd_dt_bias=v_ssd_dt_bias, ssd_a_log=v_ssd_a_log, ssd_d=v_ssd_d, ssd_norm_g=v_ssd_norm_g,
              even_w_out=v_even_w_out, odd_norm_g=v_odd_norm_g, odd_w_in=v_odd_w_in, sconv_w=v_sconv_w,
              odd_w_out=v_odd_w_out, final_norm_g=v_final_norm_g)
    _, _, _, me = _place()
    xs = x[0]
    shard = WIDTH // N_DEV

    small_blk = jnp.concatenate([
        ssd_conv_w[0], jnp.pad(sconv_w[0], ((0, 0), (0, shard))), jnp.pad(odd_norm_g, ((0, 0), (0, shard)))], axis=0)
    g_wte, g_wto, g_woe, g_woo, g_small = _all_gather(
        [even_w_in[0].T.astype(BF16), odd_w_in[0].T.astype(BF16), even_w_out[0].astype(BF16),
         odd_w_out[0].astype(BF16), small_blk], "gather_weights")
    wte = g_wte.reshape(F_EVEN_ALL, D_MODEL)
    wte_dt = jnp.pad(wte[F_EVEN:], ((0, LANES - B_HEADS), (0, 0)))
    wto = g_wto.reshape(F_ODD, D_MODEL)
    woe = g_woe.reshape(2 * WIDTH, D_MODEL)
    woo = g_woo.reshape(2 * WIDTH, D_MODEL)
    conv_w = g_small[:, 0:B_CONV, :].transpose(1, 0, 2).reshape(B_CONV, B_XBC)
    sconv_full = g_small[:, B_CONV:B_CONV + C_CONV, :shard].transpose(1, 0, 2).reshape(C_CONV, WIDTH)
    odd_g = g_small[:, B_CONV + C_CONV, :shard].reshape(1, WIDTH)

    pad_heads = lambda a: jnp.pad(a, ((0, 0), (0, LANES - B_HEADS)))
    alog, dtb = pad_heads(ssd_a_log), pad_heads(ssd_dt_bias)
    d_full = jnp.repeat(ssd_d, B_HEAD_DIM, axis=1)
    ws, bs_t = gmlp_ws[0], gmlp_bs[0].T
    ws_t = jnp.swapaxes(ws, 1, 2)
    proj = dict(tm=1024, tn=1024, tk=D_MODEL, out_dtype=F32)
    out_proj = dict(n=D_MODEL, tm=1024, tn=1024, tk=1024, out_dtype=F32)
    back_proj = dict(tb=True, n=2 * WIDTH, tm=512, tn=1024, tk=D_MODEL, out_dtype=F32)
    dw_out = dict(ta=True, n=D_MODEL, tm=512, tn=D_MODEL, tk=512, out_dtype=BF16)
    dx_in = dict(n=D_MODEL, tm=512, tn=D_MODEL, tk=1024, out_dtype=F32)
    dw_in = dict(ta=True, n=D_MODEL, tm=1024, tn=D_MODEL, tk=512, out_dtype=BF16)

    xn0 = _rms_fwd(xs, even_norm_g, "norm_even")
    h0 = _mm([(xn0, None)], wte, tb=True, n=F_EVEN, name="proj_even", **proj)
    hdt = _mm([(xn0, None)], wte_dt, tb=True, n=LANES, tm=1024, tn=LANES, tk=D_MODEL, out_dtype=F32, name="proj_dt")
    y0 = _gmlp_fwd(h0, gmlp_ln_g, gmlp_ln_b, ws, bs_t, "gmlp_fwd")
    pre = _ssd_conv_fwd(h0, conv_w, ssd_conv_b, "ssd_conv_fwd")
    y0, ypre, states = _ssd_scan_fwd(pre, hdt, h0, y0, alog, dtb, d_full, ssd_norm_g, "ssd_scan_fwd")
    x1 = _mm([(y0, None)], woe, add=xs, name="out_even", **out_proj)
    xn1 = _rms_fwd(x1, odd_g, "norm_odd")
    h1 = _mm([(xn1, None)], wto, tb=True, n=F_ODD, name="proj_odd", **proj)
    y1 = _sconv_fwd(h1, sconv_full, "sconv_fwd")
    y1, att_o, att_lse = _attn2_fwd(h1, y1, "attn_fwd")
    x2 = _mm([(y1, None)], woo, add=x1, name="out_odd", **out_proj)
    loss_part, dx2, g_final = _loss_head(x2, final_norm_g.reshape(1, D_MODEL), loss_target[0], "loss_head")

    dy1 = _mm([(dx2, None)], woo, name="dy_odd", **back_proj)
    gw_woo = _mm([(y1, None)], dx2, name="dw_out_odd", **dw_out)
    dh1c, g_sconv = _sconv_bwd(h1, dy1, sconv_full, "sconv_bwd")
    dqz, dkv = _attn2_bwd(h1, dy1, att_o, att_lse, "attn_bwd")
    dh1 = [(dh1c, None), (dqz, 0), (dkv, 0), (dkv, 1), (dqz, 1)]
    dxn1 = _mm(dh1, wto, name="dx_odd", **dx_in)
    gw_wto = _mm(dh1, xn1, name="dw_in_odd", **dw_in)
    dx1, g_odd = _rms_bwd(x1, odd_g, dxn1, dx2, "norm_odd_bwd")

    dy0 = _mm([(dx1, None)], woe, name="dy_even", **back_proj)
    gw_woe = _mm([(y0, None)], dx1, name="dw_out_even", **dw_out)
    dh0, g_ws, g_bs_t, g_ln_g, g_ln_b = _gmlp_bwd(h0, dy0, gmlp_ln_g, gmlp_ln_b, ws, ws_t, bs_t, "gmlp_bwd")
    dh0, dpx, dpb, dpc, ddt, g_dtb, g_alog, g_dd, g_ng = _ssd_scan_bwd(
        pre, hdt, h0, dy0, ypre, states, dh0, alog, dtb, d_full, ssd_norm_g, "ssd_scan_bwd")
    dh0, g_conv_w, g_conv_b = _ssd_conv_bwd(h0, dh0, dpx, dpb, dpc, conv_w, "ssd_conv_bwd")
    dxn0_dt = _mm([(ddt, None)], wte_dt, n=D_MODEL, tm=1024, tn=D_MODEL, tk=LANES, out_dtype=F32, name="dx_dt")
    dxn0 = _mm([(dh0, None)], wte, add=dxn0_dt, name="dx_even", **dx_in)
    gw_main = _mm([(dh0, None)], xn0, out_rows=F_EVEN_ALL, name="dw_in_even", **dw_in)
    gw_dt = _mm([(ddt, None)], xn0, ta=True, n=D_MODEL, tm=LANES, tn=D_MODEL, tk=512, out_dtype=BF16, name="dw_dt")
    gw_wte = lax.dynamic_update_slice(gw_main, gw_dt[:B_HEADS], (F_EVEN, 0))
    grad_x, g_even = _rms_bwd(xs, even_norm_g, dxn0, dx1, "norm_even_bwd")

    small_parts = dict(
        even_norm_g=g_even, gmlp_ln_g=g_ln_g, gmlp_ln_b=g_ln_b, gmlp_ws=g_ws, gmlp_bs=g_bs_t[:, :A_GROUPS].T,
        ssd_conv_w=g_conv_w[:B_CONV], ssd_conv_b=g_conv_b, ssd_dt_bias=g_dtb[:, :B_HEADS], ssd_a_log=g_alog[:, :B_HEADS],
        ssd_d=g_dd.reshape(B_HEADS, B_HEAD_DIM).sum(axis=1), ssd_norm_g=g_ng, odd_norm_g=g_odd,
        sconv_w=g_sconv[:C_CONV], final_norm_g=g_final)
    full_shapes = dict(
        even_norm_g=(1, D_MODEL), gmlp_ln_g=(1, WIDTH), gmlp_ln_b=(1, WIDTH), gmlp_ws=(1, A_GROUPS, CHUNK, CHUNK),
        gmlp_bs=(1, A_GROUPS, CHUNK), ssd_conv_w=(1, B_CONV, B_XBC), ssd_conv_b=(1, B_XBC), ssd_dt_bias=(1, B_HEADS),
        ssd_a_log=(1, B_HEADS), ssd_d=(1, B_HEADS), ssd_norm_g=(1, WIDTH), odd_norm_g=(1, D_MODEL),
        sconv_w=(1, C_CONV, WIDTH), final_norm_g=(D_MODEL,))
    (gathered_small,) = _all_gather([_pack([small_parts[k] for k in SMALL])], "gather_small_grads")
    small_sum = _sum_parts(gathered_small, LANES, "sum_small_grads")
    grads = dict(zip(SMALL, _unpack(small_sum, [full_shapes[k] for k in SMALL])))
    grads["ssd_conv_w"] = lax.dynamic_slice_in_dim(grads["ssd_conv_w"], me * 2 * shard, 2 * shard, axis=2)
    grads["odd_norm_g"] = lax.dynamic_slice_in_dim(grads["odd_norm_g"], me * shard, shard, axis=1)
    grads["sconv_w"] = lax.dynamic_slice_in_dim(grads["sconv_w"], me * shard, shard, axis=2)

    rows_e, rows_o, rows_out = F_EVEN_ALL // N_DEV, F_ODD // N_DEV, 2 * WIDTH // N_DEV
    r_wte, r_wto, r_woe, r_woo = _all_to_all(
        [gw_wte.reshape(N_DEV, rows_e, D_MODEL), gw_wto.reshape(N_DEV, rows_o, D_MODEL),
         gw_woe.reshape(N_DEV, rows_out, D_MODEL), gw_woo.reshape(N_DEV, rows_out, D_MODEL)], "scatter_grads")
    grads["even_w_in"] = _sum_parts(r_wte, 256, "sum_even_w_in").T[None]
    grads["odd_w_in"] = _sum_parts(r_wto, 256, "sum_odd_w_in").T[None]
    grads["even_w_out"] = _sum_parts(r_woe, 512, "sum_even_w_out")[None]
    grads["odd_w_out"] = _sum_parts(r_woo, 512, "sum_odd_w_out")[None]

    delta, new_m, new_v = {}, {}, {}
    for k in ("even_w_in", "odd_w_in", "even_w_out", "odd_w_out"):
        d_k, m_k, v_k = _adamw(w[k][0], grads[k][0], m1[k][0], m2[k][0], 128, "adamw_" + k)
        delta[k], new_m[k], new_v[k] = d_k[None], m_k[None], v_k[None]
    packed = [_pack([src[k] for k in SMALL]) for src in (w, grads, m1, m2)]
    small_out = _adamw(*packed, packed[0].shape[0], "adamw_small")
    shapes = [w[k].shape for k in SMALL]
    for dst, arr in zip((delta, new_m, new_v), small_out):
        dst.update(zip(SMALL, _unpack(arr, shapes)))

    loss = lax.psum(loss_part[0, 0], ("x", "y", "c"))
    return (loss, grad_x[None], *[grads[k] for k in ORDER], *[delta[k] for k in ORDER],
            *[new_m[k] for k in ORDER], *[new_v[k] for k in ORDER])
```

```python
import functools

import jax
import jax.numpy as jnp
from jax import lax
from jax.experimental import pallas as pl
from jax.experimental.pallas import tpu as pltpu

F32, BF16 = jnp.float32, jnp.bfloat16
MESH = pl.DeviceIdType.MESH
ANY = pl.BlockSpec(memory_space=pl.ANY)

N_DEV = 8
D_MODEL = 2048
WIDTH = 2048
CHUNK = 128
A_GROUPS = 8
B_HEADS, B_HEAD_DIM, B_GROUPS, B_STATE, B_CONV = 32, 64, 8, 128, 4
B_GROUP_W = WIDTH // B_GROUPS
B_XBC = WIDTH + 2 * B_GROUPS * B_STATE
C_CONV = 3
D_HEADS, D_HEAD_DIM = 16, 128
D_PATTERNS = ((128, 1), (512, 4), (2048, 16))
F_EVEN = 3 * WIDTH + WIDTH + B_XBC
F_EVEN_ALL = F_EVEN + B_HEADS
F_ODD = 8 * WIDTH
EPS = 1e-5
NEG = -1e30

ADAM_LR, ADAM_B1, ADAM_B2, ADAM_EPS, ADAM_WD, ADAM_STEP = 0.001, 0.9, 0.999, 1e-08, 0.01, 10

VMEM_LIMIT_V7X = 56 * 1024 * 1024
SUBLANES, LANES = 8, 128


def _cp(*sem):
    return pltpu.CompilerParams(dimension_semantics=sem, vmem_limit_bytes=VMEM_LIMIT_V7X)


def _sig(x):
    return 1.0 / (1.0 + jnp.exp(-x))


def _silu(x):
    return x * _sig(x)


def _dsilu(x):
    s = _sig(x)
    return s * (1.0 + x * (1.0 - s))


def _softplus(x):
    return jnp.maximum(x, 0.0) + jnp.log(1.0 + jnp.exp(-jnp.abs(x)))


def _dot(a, b):
    return jnp.dot(a.astype(BF16), b.astype(BF16), preferred_element_type=F32)


def _dot_nt(a, b):
    return lax.dot_general(a.astype(BF16), b.astype(BF16), (((1,), (1,)), ((), ())), preferred_element_type=F32)


def _dot_tn(a, b):
    return lax.dot_general(a.astype(BF16), b.astype(BF16), (((0,), (0,)), ((), ())), preferred_element_type=F32)


def _split3(x):
    hi = x.astype(BF16)
    r1 = x - hi.astype(F32)
    mid = r1.astype(BF16)
    lo = (r1 - mid.astype(F32)).astype(BF16)
    return hi, mid, lo


def _dot_sel(x, sel):
    return sum(jnp.dot(p, sel, preferred_element_type=F32) for p in _split3(x))


def _sel_dot(sel, x):
    return sum(jnp.dot(sel, p, preferred_element_type=F32) for p in _split3(x))


def _iota(shape, axis):
    return lax.broadcasted_iota(jnp.int32, shape, axis)


def _shift_down(cur, halo, j):
    if j == 0:
        return cur
    r = pltpu.roll(cur, j, 0)
    top = jnp.where(_iota(halo.shape, 0) < j, pltpu.roll(halo, j, 0), r[0:SUBLANES])
    return jnp.concatenate([top, r[SUBLANES:]], axis=0)


def _shift_up(cur, halo, j):
    if j == 0:
        return cur
    n = cur.shape[0]
    r = pltpu.roll(cur, n - j, 0)
    bot = jnp.where(_iota(halo.shape, 0) >= SUBLANES - j, pltpu.roll(halo, SUBLANES - j, 0), r[n - SUBLANES:])
    return jnp.concatenate([r[:n - SUBLANES], bot], axis=0)


def _mm(a, b, *, ta=False, tb=False, n, tm, tn, tk, out_dtype, out_rows=None, add=None, carry=None, name):
    width, rows = a.shape[-1], a.shape[-2]
    feat = width * (a.shape[0] if a.ndim == 3 else 1)
    m, k_len = (feat, rows) if ta else (rows, feat)
    per_part = width // (tm if ta else tk)
    grid = (m // tm, n // tn, k_len // tk)
    nk = grid[2]

    def a_index(i, j, k):
        f = i if ta else k
        pos = (k,) if ta else (i,)
        return pos + (f,) if a.ndim == 2 else (f // per_part,) + pos + (f % per_part,)

    a_block = (tk, tm) if ta else (tm, tk)
    a_spec = pl.BlockSpec(a_block if a.ndim == 2 else (None,) + a_block, a_index)
    b_spec = pl.BlockSpec((tn, tk), lambda i, j, k: (j, k)) if tb else pl.BlockSpec((tk, tn), lambda i, j, k: (k, j))
    io_spec = pl.BlockSpec((tm, tn), lambda i, j, k: (i, j))
    dims = (((0 if ta else 1,), (1 if tb else 0,)), ((), ()))
    has_add, nc = add is not None, len(carry.arrays) if carry else 0

    def body(*refs):
        a_ref, b_ref = refs[0], refs[1]
        add_ref = refs[2] if has_add else None
        pos = 2 + has_add
        c_in, o_ref, c_out = refs[pos:pos + nc], refs[pos + nc], refs[pos + nc + 1:pos + 2 * nc + 1]
        pos += 2 * nc + 1
        acc_ref = refs[pos] if nk > 1 else None
        c_sems = refs[pos + (nk > 1):]
        i, j, k = pl.program_id(0), pl.program_id(1), pl.program_id(2)
        if carry:
            pl.when((i == 0) & (j == 0) & (k == 0))(lambda: carry.start(c_in, c_out, c_sems))

        def finish(r):
            if add_ref is not None:
                r = r + add_ref[...]
            o_ref[...] = r.astype(out_dtype)

        p = lax.dot_general(a_ref[...].astype(BF16), b_ref[...].astype(BF16), dims, preferred_element_type=F32)
        if nk == 1:
            finish(p)
        else:
            @pl.when(k == 0)
            def _():
                acc_ref[...] = p

            @pl.when((k > 0) & (k < nk - 1))
            def _():
                acc_ref[...] += p

            @pl.when(k == nk - 1)
            def _():
                finish(acc_ref[...] + p)

        if carry:
            pl.when((i == grid[0] - 1) & (j == grid[1] - 1) & (k == nk - 1))(lambda: carry.finish(c_in, c_out, c_sems))

    out = pl.pallas_call(
        body, grid=grid, name=name,
        in_specs=[a_spec, b_spec] + [io_spec] * has_add + [ANY] * nc,
        out_specs=[io_spec] + [ANY] * nc,
        out_shape=[jax.ShapeDtypeStruct((out_rows or m, n), out_dtype)] + (carry.out_shape if carry else []),
        scratch_shapes=([pltpu.VMEM((tm, tn), F32)] if nk > 1 else []) + (carry.scratch if carry else []),
        compiler_params=_cp(*(("arbitrary",) * 3 if carry else ("parallel", "parallel", "arbitrary"))),
    )(a, b, *([add] if has_add else []), *(carry.arrays if carry else []))
    return (out[0], out[1:]) if carry else out[0]


def _rms_fwd(x, g, name):
    t, tb = x.shape[0], 512

    def body(x_ref, g_ref, o_ref):
        xv = x_ref[...]
        r = lax.rsqrt(jnp.mean(xv * xv, axis=-1, keepdims=True) + EPS)
        o_ref[...] = (xv * r * g_ref[...]).astype(BF16)

    row = pl.BlockSpec((tb, D_MODEL), lambda i: (i, 0))
    return pl.pallas_call(
        body, grid=(t // tb,), name=name,
        in_specs=[row, pl.BlockSpec((1, D_MODEL), lambda i: (0, 0))], out_specs=row,
        out_shape=jax.ShapeDtypeStruct((t, D_MODEL), BF16), compiler_params=_cp("parallel"),
    )(x, g)


def _rms_bwd(x, g, dxn, dres, bf16_copy, name):
    t, tb = x.shape[0], 256

    def body(x_ref, g_ref, dxn_ref, dres_ref, dx_ref, dg_ref, *dxb_ref):
        xv = x_ref[...]
        r = lax.rsqrt(jnp.mean(xv * xv, axis=-1, keepdims=True) + EPS)
        nv = xv * r
        dy = dxn_ref[...]
        dn = dy * g_ref[...]
        dx = dres_ref[...] + r * (dn - nv * jnp.mean(dn * nv, axis=-1, keepdims=True))
        dx_ref[...] = dx
        for ref in dxb_ref:
            ref[...] = dx.astype(BF16)
        part = jnp.sum(dy * nv, axis=0, keepdims=True)

        @pl.when(pl.program_id(0) == 0)
        def _():
            dg_ref[...] = part

        @pl.when(pl.program_id(0) > 0)
        def _():
            dg_ref[...] += part

    row = pl.BlockSpec((tb, D_MODEL), lambda i: (i, 0))
    vec = pl.BlockSpec((1, D_MODEL), lambda i: (0, 0))
    return pl.pallas_call(
        body, grid=(t // tb,), name=name,
        in_specs=[row, vec, row, row], out_specs=[row, vec] + [row] * bf16_copy,
        out_shape=[jax.ShapeDtypeStruct((t, D_MODEL), F32), jax.ShapeDtypeStruct((1, D_MODEL), F32)]
        + [jax.ShapeDtypeStruct((t, D_MODEL), BF16)] * bf16_copy,
        compiler_params=_cp("arbitrary"),
    )(x, g, dxn, dres)


def _loss_head(x, g, target, name):
    t, tb = x.shape[0], 256

    def body(x_ref, g_ref, t_ref, loss_ref, dx_ref, dg_ref, dxb_ref):
        xv, gv = x_ref[...], g_ref[...]
        r = lax.rsqrt(jnp.mean(xv * xv, axis=-1, keepdims=True) + EPS)
        nv = xv * r
        err = nv * gv - t_ref[...]
        lpart = 0.5 * jnp.sum(jnp.mean(err * err, axis=-1, keepdims=True), axis=0, keepdims=True)
        dy = err * (1.0 / D_MODEL)
        dn = dy * gv
        dx = r * (dn - nv * jnp.mean(dn * nv, axis=-1, keepdims=True))
        dx_ref[...] = dx
        dxb_ref[...] = dx.astype(BF16)
        gpart = jnp.sum(dy * nv, axis=0, keepdims=True)

        @pl.when(pl.program_id(0) == 0)
        def _():
            dg_ref[...] = gpart
            loss_ref[...] = jnp.broadcast_to(lpart, (1, LANES))

        @pl.when(pl.program_id(0) > 0)
        def _():
            dg_ref[...] += gpart
            loss_ref[...] += jnp.broadcast_to(lpart, (1, LANES))

    row = pl.BlockSpec((tb, D_MODEL), lambda i: (i, 0))
    vec = pl.BlockSpec((1, D_MODEL), lambda i: (0, 0))
    return pl.pallas_call(
        body, grid=(t // tb,), name=name,
        in_specs=[row, vec, row], out_specs=[pl.BlockSpec((1, LANES), lambda i: (0, 0)), row, vec, row],
        out_shape=[jax.ShapeDtypeStruct((1, LANES), F32), jax.ShapeDtypeStruct((t, D_MODEL), F32),
                   jax.ShapeDtypeStruct((1, D_MODEL), F32), jax.ShapeDtypeStruct((t, D_MODEL), BF16)],
        compiler_params=_cp("arbitrary"),
    )(x, g, target)


A_GW = WIDTH // A_GROUPS


def _gmlp_common(v, lg, lb):
    xc = v - jnp.mean(v, axis=-1, keepdims=True)
    rs = lax.rsqrt(jnp.mean(xc * xc, axis=-1, keepdims=True) + EPS)
    vh = xc * rs
    return rs, vh, (vh * lg + lb).astype(BF16)


def _gmlp_fwd(h, ln_g, ln_b, ws, bs_t, name):
    t, tb = h.shape[0], 256

    def body(u_ref, v_ref, z_ref, lg_ref, lb_ref, ws_ref, bst_ref, y_ref):
        _, _, vn = _gmlp_common(v_ref[...], lg_ref[...], lb_ref[...])
        causal = _iota((CHUNK, CHUNK), 1) <= _iota((CHUNK, CHUNK), 0)
        for g in range(A_GROUPS):
            w = jnp.where(causal, ws_ref[g], 0.0).astype(BF16)
            cols = slice(g * A_GW, (g + 1) * A_GW)
            for c in range(tb // CHUNK):
                rows = slice(c * CHUNK, (c + 1) * CHUNK)
                mixed = jnp.dot(w, vn[rows, cols], preferred_element_type=F32) + bst_ref[:, g:g + 1]
                y_ref[rows, cols] = (_silu(z_ref[rows, cols]) * (u_ref[rows, cols] * mixed)).astype(BF16)

    col = lambda j: pl.BlockSpec((tb, WIDTH), lambda i: (i, j))
    full = lambda a: pl.BlockSpec(a.shape, lambda i: (0,) * a.ndim)
    return pl.pallas_call(
        body, grid=(t // tb,), name=name,
        in_specs=[col(0), col(1), col(2), full(ln_g), full(ln_b), full(ws), full(bs_t)],
        out_specs=col(0), out_shape=jax.ShapeDtypeStruct((t, 2 * WIDTH), BF16),
        compiler_params=_cp("parallel"),
    )(h, h, h, ln_g, ln_b, ws, bs_t)


def _gmlp_bwd(h, dy, ln_g, ln_b, ws, ws_t, bs_t, name):
    t, tb = h.shape[0], 256

    def body(u_ref, v_ref, z_ref, dy_ref, lg_ref, lb_ref, ws_ref, wst_ref, bst_ref,
             dh_ref, dws_ref, dbst_ref, dlg_ref, dlb_ref, dvn_ref):
        @pl.when(pl.program_id(0) == 0)
        def _():
            dws_ref[...] = jnp.zeros_like(dws_ref)
            dbst_ref[...] = jnp.zeros_like(dbst_ref)
            dlg_ref[...] = jnp.zeros_like(dlg_ref)
            dlb_ref[...] = jnp.zeros_like(dlb_ref)

        rs, vh, vn = _gmlp_common(v_ref[...], lg_ref[...], lb_ref[...])
        row, lane = _iota((CHUNK, CHUNK), 0), _iota((CHUNK, CHUNK), 1)
        for g in range(A_GROUPS):
            w = jnp.where(lane <= row, ws_ref[g], 0.0).astype(BF16)
            wt = jnp.where(row <= lane, wst_ref[g], 0.0).astype(BF16)
            cols = slice(g * A_GW, (g + 1) * A_GW)
            dws_acc = jnp.zeros((CHUNK, CHUNK), F32)
            dbs_acc = jnp.zeros((CHUNK, 1), F32)
            for c in range(tb // CHUNK):
                rows = slice(c * CHUNK, (c + 1) * CHUNK)
                vnb = vn[rows, cols]
                mixed = jnp.dot(w, vnb, preferred_element_type=F32) + bst_ref[:, g:g + 1]
                u, z, dyv = u_ref[rows, cols], z_ref[rows, cols], dy_ref[rows, cols]
                sz = _silu(z)
                dh_ref[rows, cols] = (dyv * sz * mixed).astype(BF16)
                dh_ref[rows, slice(2 * WIDTH + g * A_GW, 2 * WIDTH + (g + 1) * A_GW)] = (
                    dyv * (u * mixed) * _dsilu(z)).astype(BF16)
                dm = dyv * sz * u
                dws_acc += _dot_nt(dm, vnb)
                dbs_acc += jnp.sum(dm, axis=1, keepdims=True)
                dvn_ref[rows, cols] = jnp.dot(wt, dm.astype(BF16), preferred_element_type=F32)
            dws_ref[g] += jnp.where(lane <= row, dws_acc, 0.0)
            dbst_ref[...] += jnp.where(lane == g, dbs_acc, 0.0)
        dvn = dvn_ref[...]
        dlg_ref[...] += jnp.sum(dvn * vh, axis=0, keepdims=True)
        dlb_ref[...] += jnp.sum(dvn, axis=0, keepdims=True)
        dvh = dvn * lg_ref[...]
        dv = rs * (dvh - jnp.mean(dvh, axis=-1, keepdims=True) - vh * jnp.mean(dvh * vh, axis=-1, keepdims=True))
        dh_ref[:, WIDTH:2 * WIDTH] = dv.astype(BF16)

    col = lambda j: pl.BlockSpec((tb, WIDTH), lambda i: (i, j))
    full = lambda a: pl.BlockSpec(a.shape, lambda i: (0,) * a.ndim)
    acc = lambda shape: pl.BlockSpec(shape, lambda i: (0,) * len(shape))
    return pl.pallas_call(
        body, grid=(t // tb,), name=name,
        in_specs=[col(0), col(1), col(2), col(0), full(ln_g), full(ln_b), full(ws), full(ws_t), full(bs_t)],
        out_specs=[pl.BlockSpec((tb, 3 * WIDTH), lambda i: (i, 0)), acc((A_GROUPS, CHUNK, CHUNK)),
                   acc((CHUNK, LANES)), acc((1, WIDTH)), acc((1, WIDTH))],
        out_shape=[jax.ShapeDtypeStruct((t, F_EVEN), BF16), jax.ShapeDtypeStruct((A_GROUPS, CHUNK, CHUNK), F32),
                   jax.ShapeDtypeStruct((CHUNK, LANES), F32), jax.ShapeDtypeStruct((1, WIDTH), F32),
                   jax.ShapeDtypeStruct((1, WIDTH), F32)],
        scratch_shapes=[pltpu.VMEM((tb, WIDTH), F32)],
        compiler_params=_cp("arbitrary"),
    )(h, h, h, dy, ln_g, ln_b, ws, ws_t, bs_t)


def _halo_prev(tb, j):
    return lambda i: (jnp.maximum(i * (tb // SUBLANES) - 1, 0), j)


def _halo_next(tb, j, t):
    return lambda i: (jnp.minimum((i + 1) * (tb // SUBLANES), t // SUBLANES - 1), j)


def _row_select(parts, width):
    row = _iota((SUBLANES, width), 0)
    out = jnp.zeros((SUBLANES, width), F32)
    for k, p in enumerate(parts):
        out = jnp.where(row == k, p, out)
    return out


def _sconv_fwd(h, w, name):
    t, tb = h.shape[0], 256

    def body(bg_ref, cg_ref, hx_ref, z_ref, cgh_ref, hxh_ref, w_ref, y_ref):
        p = cg_ref[...] * hx_ref[...]
        ph = jnp.where(pl.program_id(0) > 0, cgh_ref[...] * hxh_ref[...], 0.0)
        cv = w_ref[2:3, :] * p + w_ref[1:2, :] * _shift_down(p, ph, 1) + w_ref[0:1, :] * _shift_down(p, ph, 2)
        y_ref[...] = (_silu(z_ref[...]) * (bg_ref[...] * cv)).astype(BF16)

    col = lambda j: pl.BlockSpec((tb, WIDTH), lambda i: (i, j))
    halo = lambda j: pl.BlockSpec((SUBLANES, WIDTH), _halo_prev(tb, j))
    return pl.pallas_call(
        body, grid=(t // tb,), name=name,
        in_specs=[col(0), col(1), col(2), col(3), halo(1), halo(2), pl.BlockSpec(w.shape, lambda i: (0, 0))],
        out_specs=col(0), out_shape=jax.ShapeDtypeStruct((t, 2 * WIDTH), BF16),
        compiler_params=_cp("parallel"),
    )(h, h, h, h, h, h, w)


def _sconv_bwd(h, dy, w, name):
    t, tb = h.shape[0], 256
    nb = t // tb

    def body(bg_ref, cg_ref, hx_ref, z_ref, dy_ref, cgh_ref, hxh_ref, bgn_ref, zn_ref, dyn_ref, w_ref, dh_ref, dw_ref):
        i = pl.program_id(0)
        bg, cg, hx, z, dyv = bg_ref[...], cg_ref[...], hx_ref[...], z_ref[...], dy_ref[...]
        p = cg * hx
        ph = jnp.where(i > 0, cgh_ref[...] * hxh_ref[...], 0.0)
        p1, p2 = _shift_down(p, ph, 1), _shift_down(p, ph, 2)
        cv = w_ref[2:3, :] * p + w_ref[1:2, :] * p1 + w_ref[0:1, :] * p2
        sz = _silu(z)
        dcv = dyv * sz * bg
        dcvn = jnp.where(i < nb - 1, dyn_ref[...] * _silu(zn_ref[...]) * bgn_ref[...], 0.0)
        dp = w_ref[2:3, :] * dcv + w_ref[1:2, :] * _shift_up(dcv, dcvn, 1) + w_ref[0:1, :] * _shift_up(dcv, dcvn, 2)
        dh_ref[0] = (dyv * sz * cv).astype(BF16)
        dh_ref[1] = (dp * hx).astype(BF16)
        dh_ref[2] = (dp * cg).astype(BF16)
        dh_ref[3] = (dyv * (bg * cv) * _dsilu(z)).astype(BF16)
        part = _row_select([jnp.sum(dcv * q, axis=0, keepdims=True) for q in (p2, p1, p)], WIDTH)

        @pl.when(i == 0)
        def _():
            dw_ref[...] = part

        @pl.when(i > 0)
        def _():
            dw_ref[...] += part

    col = lambda j: pl.BlockSpec((tb, WIDTH), lambda i: (i, j))
    prev = lambda j: pl.BlockSpec((SUBLANES, WIDTH), _halo_prev(tb, j))
    nxt = lambda j: pl.BlockSpec((SUBLANES, WIDTH), _halo_next(tb, j, t))
    return pl.pallas_call(
        body, grid=(nb,), name=name,
        in_specs=[col(0), col(1), col(2), col(3), col(0), prev(1), prev(2), nxt(0), nxt(3), nxt(0),
                  pl.BlockSpec(w.shape, lambda i: (0, 0))],
        out_specs=[pl.BlockSpec((4, tb, WIDTH), lambda i: (0, i, 0)), pl.BlockSpec((SUBLANES, WIDTH), lambda i: (0, 0))],
        out_shape=[jax.ShapeDtypeStruct((8, t, WIDTH), BF16), jax.ShapeDtypeStruct((SUBLANES, WIDTH), F32)],
        compiler_params=_cp("arbitrary"),
    )(h, h, h, h, dy, h, h, h, h, dy, w)


XBC_COL0 = 4 * WIDTH


def _ssd_conv_fwd(h, w, b, name):
    t, tb = h.shape[0], 256

    def body(x_ref, xh_ref, w_ref, b_ref, o_ref):
        xv = x_ref[...]
        xh = jnp.where(pl.program_id(0) > 0, xh_ref[...], 0.0)
        acc = b_ref[...] + w_ref[3:4, :] * xv
        for j in range(1, B_CONV):
            acc = acc + w_ref[B_CONV - 1 - j:B_CONV - j, :] * _shift_down(xv, xh, j)
        o_ref[...] = acc

    cb = XBC_COL0 // B_XBC
    return pl.pallas_call(
        body, grid=(t // tb,), name=name,
        in_specs=[pl.BlockSpec((tb, B_XBC), lambda i: (i, cb)), pl.BlockSpec((SUBLANES, B_XBC), _halo_prev(tb, cb)),
                  pl.BlockSpec(w.shape, lambda i: (0, 0)), pl.BlockSpec(b.shape, lambda i: (0, 0))],
        out_specs=pl.BlockSpec((tb, B_XBC), lambda i: (i, 0)), out_shape=jax.ShapeDtypeStruct((t, B_XBC), F32),
        compiler_params=_cp("parallel"),
    )(h, h, w, b)


def _ssd_conv_bwd(h, dh, dpx, dpb, dpc, w, name):
    t, tb, tc = h.shape[0], 256, 1024
    nb = t // tb
    r8 = tb // SUBLANES
    last8 = t // SUBLANES - 1

    def body(dpx_ref, dpb_ref, dpc_ref, nx_ref, nb_ref, nc_ref, x_ref, xh_ref, w_ref, dh_in, dh_ref, dw_ref, db_ref):
        j, i = pl.program_id(0), pl.program_id(1)
        pick = lambda a, b_, c: jnp.where(j < 2, a[...], jnp.where(j == 2, b_[...], c[...]))
        dp = pick(dpx_ref, dpb_ref, dpc_ref)
        dn = jnp.where(i < nb - 1, pick(nx_ref, nb_ref, nc_ref), 0.0)
        xv = x_ref[...]
        xh = jnp.where(i > 0, xh_ref[...], 0.0)
        dx = w_ref[3:4, :] * dp
        for s in range(1, B_CONV):
            dx = dx + w_ref[B_CONV - 1 - s:B_CONV - s, :] * _shift_up(dp, dn, s)
        dh_ref[...] = dx.astype(BF16)
        wpart = _row_select([jnp.sum(dp * _shift_down(xv, xh, B_CONV - 1 - k), axis=0, keepdims=True)
                             for k in range(B_CONV)], tc)
        bpart = jnp.sum(dp, axis=0, keepdims=True)

        @pl.when(i == 0)
        def _():
            dw_ref[...] = wpart
            db_ref[...] = bpart

        @pl.when(i > 0)
        def _():
            dw_ref[...] += wpart
            db_ref[...] += bpart

    def src(blk_rows, rowf, sel, colf):
        return pl.BlockSpec((blk_rows, tc), lambda j, i: (jnp.where(sel(j), rowf(i), 0), colf(j)))

    cur = lambda i: i
    nxt = lambda i: jnp.minimum((i + 1) * r8, last8)
    is_x, is_b, is_c = (lambda j: j < 2), (lambda j: j == 2), (lambda j: j == 3)
    xcol, zero = (lambda j: jnp.minimum(j, 1)), (lambda j: 0)
    c0 = XBC_COL0 // tc
    return pl.pallas_call(
        body, grid=(B_XBC // tc, nb), name=name,
        in_specs=[src(tb, cur, is_x, xcol), src(tb, cur, is_b, zero), src(tb, cur, is_c, zero),
                  src(SUBLANES, nxt, is_x, xcol), src(SUBLANES, nxt, is_b, zero), src(SUBLANES, nxt, is_c, zero),
                  pl.BlockSpec((tb, tc), lambda j, i: (i, c0 + j)),
                  pl.BlockSpec((SUBLANES, tc), lambda j, i: (jnp.maximum(i * r8 - 1, 0), c0 + j)),
                  pl.BlockSpec((B_CONV, tc), lambda j, i: (0, j)), ANY],
        out_specs=[pl.BlockSpec((tb, tc), lambda j, i: (i, c0 + j)), pl.BlockSpec((SUBLANES, tc), lambda j, i: (0, j)),
                   pl.BlockSpec((1, tc), lambda j, i: (0, j))],
        out_shape=[jax.ShapeDtypeStruct(dh.shape, dh.dtype), jax.ShapeDtypeStruct((SUBLANES, B_XBC), F32),
                   jax.ShapeDtypeStruct((1, B_XBC), F32)],
        input_output_aliases={9: 0},
        compiler_params=_cp("arbitrary", "arbitrary"),
    )(dpx, dpb, dpc, dpx, dpb, dpc, h, h, w, dh)


HEADS_PER_GROUP = B_HEADS // B_GROUPS


def _ssd_group_terms(g, px, pb, pc, dtr, alog, dtb):
    xs, bm, cm = _silu(px), _silu(pb), _silu(pc)
    dt = _softplus(dtr + dtb)
    a = -jnp.exp(alog)
    head_of_lane = _iota((LANES, B_GROUP_W), 1) // B_HEAD_DIM + HEADS_PER_GROUP * g
    expand = (_iota((LANES, B_GROUP_W), 0) == head_of_lane).astype(BF16)
    local = _iota((LANES, LANES), 1)
    sel = ((_iota((LANES, LANES), 0) == local + HEADS_PER_GROUP * g) & (local < HEADS_PER_GROUP)).astype(BF16)
    tri = (_iota((CHUNK, CHUNK), 1) <= _iota((CHUNK, CHUNK), 0)).astype(BF16)
    dt_e = _dot_sel(dt, expand)
    a_e = _sel_dot(tri, _dot_sel(dt * a, expand))
    a4 = _sel_dot(tri, _dot_sel(dt * a, sel))
    sub = _iota((LANES, LANES), 0)
    sel_t = ((local == sub + HEADS_PER_GROUP * g) & (sub < HEADS_PER_GROUP)).astype(BF16)
    return dict(xs=xs, bm=bm, cm=cm, dt=dt, a=a, sel=sel, sel_t=sel_t, dt_e=dt_e, a_e=a_e, a4=a4, a4t=a4.T)


def _ssd_decay(tm, j, transposed):
    col, row = tm["a4"][:, j:j + 1], tm["a4t"][j:j + 1, :]
    lane, sub = _iota((CHUNK, CHUNK), 1), _iota((CHUNK, CHUNK), 0)
    if transposed:
        return jnp.where(sub <= lane, jnp.exp(jnp.minimum(row - col, 0.0)), 0.0)
    return jnp.where(lane <= sub, jnp.exp(jnp.minimum(col - row, 0.0)), 0.0)


def _ssd_specs(nc, rev):
    ch = (lambda c: nc - 1 - c) if rev else (lambda c: c)
    gw = lambda off: pl.BlockSpec((CHUNK, B_GROUP_W), lambda c, g: (ch(c), off + g))
    gn = lambda off: pl.BlockSpec((CHUNK, B_STATE), lambda c, g: (ch(c), off + g))
    tok = pl.BlockSpec((CHUNK, LANES), lambda c, g: (ch(c), 0))
    vec = pl.BlockSpec((1, LANES), lambda c, g: (0, 0))
    gvec = pl.BlockSpec((1, B_GROUP_W), lambda c, g: (0, g))
    st = pl.BlockSpec((None, B_STATE, B_GROUP_W), lambda c, g: (ch(c), 0, g))
    return gw, gn, tok, vec, gvec, st


def _ssd_scan_fwd(pre, hdt, h, y, alog, dtb, dfull, ng, name):
    t = pre.shape[0]
    nc = t // CHUNK

    def body(px_ref, pb_ref, pc_ref, dtr_ref, z_ref, alog_ref, dtb_ref, df_ref, ng_ref, y_in,
             y_ref, ypre_ref, st_ref, state_ref):
        c, g = pl.program_id(0), pl.program_id(1)

        @pl.when(c == 0)
        def _():
            state_ref[g] = jnp.zeros((B_STATE, B_GROUP_W), F32)

        tm = _ssd_group_terms(g, px_ref[...], pb_ref[...], pc_ref[...], dtr_ref[...], alog_ref[...], dtb_ref[...])
        xs, bm, cm, a_e = tm["xs"], tm["bm"], tm["cm"], tm["a_e"]
        xdt = xs * tm["dt_e"]
        cb = _dot_nt(cm, bm)
        head = _iota((CHUNK, B_GROUP_W), 1) // B_HEAD_DIM
        yd = jnp.zeros((CHUNK, B_GROUP_W), F32)
        for j in range(HEADS_PER_GROUP):
            yd = jnp.where(head == j, _dot(cb * _ssd_decay(tm, j, False), xdt), yd)
        st = state_ref[g]
        st_ref[...] = st
        yv = yd + jnp.exp(a_e) * _dot(cm, st) + df_ref[...] * xs
        a_last = a_e[CHUNK - 1:CHUNK, :]
        state_ref[g] = st * jnp.exp(a_last) + _dot_tn(bm, xdt * jnp.exp(a_last - a_e))
        ypre_ref[...] = yv
        yz = yv * _silu(z_ref[...])
        r = lax.rsqrt(jnp.mean(yz * yz, axis=-1, keepdims=True) + EPS)
        y_ref[...] = (yz * r * ng_ref[...]).astype(BF16)

    gw, gn, tok, vec, gvec, st = _ssd_specs(nc, False)
    return pl.pallas_call(
        body, grid=(nc, B_GROUPS), name=name,
        in_specs=[gw(0), gn(WIDTH // B_STATE), gn((WIDTH + B_GROUPS * B_STATE) // B_STATE), tok,
                  gw(3 * WIDTH // B_GROUP_W), vec, vec, gvec, gvec, ANY],
        out_specs=[gw(WIDTH // B_GROUP_W), gw(0), st],
        out_shape=[jax.ShapeDtypeStruct(y.shape, y.dtype), jax.ShapeDtypeStruct((t, WIDTH), F32),
                   jax.ShapeDtypeStruct((nc, B_STATE, WIDTH), F32)],
        scratch_shapes=[pltpu.VMEM((B_GROUPS, B_STATE, B_GROUP_W), F32)],
        input_output_aliases={9: 0},
        compiler_params=_cp("arbitrary", "arbitrary"),
    )(pre, pre, pre, hdt, h, alog, dtb, dfull, ng, y)


def _ssd_scan_bwd(pre, hdt, h, dy, ypre, states, dh, alog, dtb, dfull, ng, carry, name):
    t = pre.shape[0]
    nc = t // CHUNK

    n_carried = len(carry.arrays) if carry else 0

    def body(*refs):
        (px_ref, pb_ref, pc_ref, dtr_ref, z_ref, dy_ref, ypre_ref, st_ref, alog_ref, dtb_ref, df_ref, ng_ref,
         _) = refs[:13]
        refs = refs[13:]
        c_in, refs = refs[:n_carried], refs[n_carried:]
        dz_ref, dpx_ref, dpb_ref, dpc_ref, ddt_ref, dbias_ref, dalog_ref, dd_ref, dng_ref = refs[:9]
        c_out, dstate_ref, c_sems = refs[9:9 + n_carried], refs[9 + n_carried], refs[10 + n_carried:]
        c, g = pl.program_id(0), pl.program_id(1)
        if carry:
            pl.when((c == 0) & (g == 0))(lambda: carry.start(c_in, c_out, c_sems))

        @pl.when(c == 0)
        def _():
            dstate_ref[g] = jnp.zeros((B_STATE, B_GROUP_W), F32)
            dd_ref[g] = jnp.zeros((1, B_GROUP_W), F32)
            dng_ref[g] = jnp.zeros((1, B_GROUP_W), F32)

        @pl.when((c == 0) & (g == 0))
        def _():
            dbias_ref[...] = jnp.zeros_like(dbias_ref)
            dalog_ref[...] = jnp.zeros_like(dalog_ref)

        px, pb, pc, dtr = px_ref[...], pb_ref[...], pc_ref[...], dtr_ref[...]
        tm = _ssd_group_terms(g, px, pb, pc, dtr, alog_ref[...], dtb_ref[...])
        xs, bm, cm, a_e, dt_e = tm["xs"], tm["bm"], tm["cm"], tm["a_e"], tm["dt_e"]
        xdt = xs * dt_e
        head = _iota((CHUNK, B_GROUP_W), 1) // B_HEAD_DIM

        z, yv, ngv = z_ref[...], ypre_ref[...], ng_ref[...]
        sz = _silu(z)
        yz = yv * sz
        r = lax.rsqrt(jnp.mean(yz * yz, axis=-1, keepdims=True) + EPS)
        dyn = dy_ref[...]
        dng_ref[g] += jnp.sum(dyn * yz * r, axis=0, keepdims=True)
        q = dyn * ngv
        dyz = r * q - yz * (r * r * r) * jnp.mean(q * yz, axis=-1, keepdims=True)
        dyv = dyz * sz
        dz_ref[...] = (dyz * yv * _dsilu(z)).astype(BF16)
        dd_ref[g] += jnp.sum(dyv * xs, axis=0, keepdims=True)
        dxs = df_ref[...] * dyv

        st = st_ref[...]
        ea = jnp.exp(a_e)
        ead = ea * dyv
        dcm = _dot_nt(ead, st)
        da_e = dyv * (ea * _dot(cm, st))

        dsn = dstate_ref[g]
        a_last = a_e[CHUNK - 1:CHUNK, :]
        ea_last = jnp.exp(a_last)
        dstate_ref[g] = dsn * ea_last + _dot_tn(cm, ead)
        wdec = jnp.exp(a_last - a_e)
        xw = xdt * wdec
        dxw = _dot(bm, dsn)
        dxdt = dxw * wdec
        dbm = _dot_nt(xw, dsn)
        zc = dxw * xw
        da_last = jnp.sum(zc, axis=0, keepdims=True) + jnp.sum(dsn * st, axis=0, keepdims=True) * ea_last
        da_e = da_e - zc + jnp.where(_iota((CHUNK, B_GROUP_W), 0) == CHUNK - 1, da_last, 0.0)

        cb, cbt = _dot_nt(cm, bm), _dot_nt(bm, cm)
        dcb, dcbt = jnp.zeros((CHUNK, CHUNK), F32), jnp.zeros((CHUNK, CHUNK), F32)
        da4 = jnp.zeros((CHUNK, LANES), F32)
        lane = _iota((CHUNK, LANES), 1)
        for j in range(HEADS_PER_GROUP):
            mine = head == j
            gm = _dot_nt(jnp.where(mine, dyv, 0.0), xdt)
            gmt = _dot_nt(jnp.where(mine, xdt, 0.0), dyv)
            dec, dect = _ssd_decay(tm, j, False), _ssd_decay(tm, j, True)
            dcb += gm * dec
            dcbt += gmt * dect
            da_j = (jnp.sum(gm * cb * dec, axis=1, keepdims=True) - jnp.sum(gmt * cbt * dect, axis=1, keepdims=True))
            da4 = jnp.where(lane == j, da_j, da4)
            dxdt = dxdt + jnp.where(mine, _dot(cbt * dect, dyv), 0.0)
        dcm = dcm + _dot(dcb, bm)
        dbm = dbm + _dot(dcbt, cm)

        gather = (_iota((B_GROUP_W, LANES), 0) // B_HEAD_DIM == _iota((B_GROUP_W, LANES), 1)).astype(BF16)
        da4 = da4 + _dot_sel(da_e, gather)
        rtri = (_iota((CHUNK, CHUNK), 1) >= _iota((CHUNK, CHUNK), 0)).astype(BF16)
        dadt4 = _sel_dot(rtri, da4)
        dt4 = _dot_sel(tm["dt"], tm["sel"])
        a4row = _dot_sel(jnp.broadcast_to(tm["a"], (SUBLANES, LANES)), tm["sel"])[0:1, :]
        ddt4 = dadt4 * a4row + _dot_sel(dxdt * xs, gather)
        dxs = dxs + dxdt * dt_e
        sel_t = tm["sel_t"]
        ddt = _dot_sel(ddt4, sel_t) * _sig(dtr + dtb_ref[...])
        da_heads = _dot_sel(jnp.broadcast_to(jnp.sum(dadt4 * dt4, axis=0, keepdims=True), (SUBLANES, LANES)), sel_t)[0:1, :]

        @pl.when(g == 0)
        def _():
            ddt_ref[...] = ddt

        @pl.when(g > 0)
        def _():
            ddt_ref[...] += ddt

        dbias_ref[...] += jnp.sum(ddt, axis=0, keepdims=True)
        dalog_ref[...] += da_heads * tm["a"]
        dpx_ref[...] = dxs * _dsilu(px)
        dpb_ref[...] = dbm * _dsilu(pb)
        dpc_ref[...] = dcm * _dsilu(pc)
        if carry:
            pl.when((c == nc - 1) & (g == B_GROUPS - 1))(lambda: carry.finish(c_in, c_out, c_sems))

    gw, gn, tok, vec, gvec, st = _ssd_specs(nc, True)
    acc = lambda shape: pl.BlockSpec(shape, lambda c, g: (0,) * len(shape))
    rev = lambda c: nc - 1 - c
    out = pl.pallas_call(
        body, grid=(nc, B_GROUPS), name=name,
        in_specs=[gw(0), gn(WIDTH // B_STATE), gn((WIDTH + B_GROUPS * B_STATE) // B_STATE), tok,
                  gw(3 * WIDTH // B_GROUP_W), gw(WIDTH // B_GROUP_W), gw(0), st, vec, vec, gvec, gvec, ANY]
        + [ANY] * n_carried,
        out_specs=[gw(3 * WIDTH // B_GROUP_W), gw(0),
                   pl.BlockSpec((CHUNK, B_STATE), lambda c, g: (rev(c), g)),
                   pl.BlockSpec((CHUNK, B_STATE), lambda c, g: (rev(c), g)),
                   tok, vec, vec, acc((B_GROUPS, 1, B_GROUP_W)), acc((B_GROUPS, 1, B_GROUP_W))] + [ANY] * n_carried,
        out_shape=[jax.ShapeDtypeStruct(dh.shape, dh.dtype), jax.ShapeDtypeStruct((t, WIDTH), F32),
                   jax.ShapeDtypeStruct((t, B_GROUPS * B_STATE), F32), jax.ShapeDtypeStruct((t, B_GROUPS * B_STATE), F32),
                   jax.ShapeDtypeStruct((t, LANES), F32), jax.ShapeDtypeStruct((1, LANES), F32),
                   jax.ShapeDtypeStruct((1, LANES), F32), jax.ShapeDtypeStruct((B_GROUPS, 1, B_GROUP_W), F32),
                   jax.ShapeDtypeStruct((B_GROUPS, 1, B_GROUP_W), F32)] + (carry.out_shape if carry else []),
        scratch_shapes=[pltpu.VMEM((B_GROUPS, B_STATE, B_GROUP_W), F32)] + (carry.scratch if carry else []),
        input_output_aliases={12: 0},
        compiler_params=_cp("arbitrary", "arbitrary"),
    )(pre, pre, pre, hdt, h, dy, ypre, states, alog, dtb, dfull, ng, dh, *(carry.arrays if carry else []))
    return out[:9], out[9:]


Q_COL, K_COL, V_COL, Z_COL = [(4 + i) * WIDTH // D_HEAD_DIM for i in range(4)]
ATT_SCALE = D_HEAD_DIM ** -0.5


SPAN = 2048


def _att_blocks():
    out = []
    for _, dil in D_PATTERNS:
        nbl = SPAN // (CHUNK * dil)
        for r in range(dil):
            for bl in range(nbl):
                st = r + dil * CHUNK * bl
                out.append((dil, st, bl > 0, st - dil * CHUNK if bl > 0 else r + dil * CHUNK * (nbl - 1)))
    return out


def _rows(start, dil):
    return pl.ds(start, CHUNK) if dil == 1 else pl.ds(start, CHUNK, stride=dil)


def _att_keys(kc_ref, kp_ref, blk):
    dil, st, inside, pst = blk
    prev = (kc_ref if inside else kp_ref)[_rows(pst, dil), :]
    return jnp.concatenate([prev, kc_ref[_rows(st, dil), :]], axis=0).astype(BF16)


def _att_band(span_index):
    lane, sub = _iota((CHUNK, 2 * CHUNK), 1), _iota((CHUNK, 2 * CHUNK), 0)
    band = (lane >= sub) & (lane <= sub + CHUNK)
    return band, band & ((lane >= CHUNK) | (span_index > 0))


def _att_specs(t):
    blk = lambda off: pl.BlockSpec((SPAN, D_HEAD_DIM), lambda hd, sb: (sb, off + hd))
    prev = lambda off: pl.BlockSpec((SPAN, D_HEAD_DIM), lambda hd, sb: (jnp.maximum(sb - 1, 0), off + hd))
    lse = pl.BlockSpec((None, SPAN // CHUNK, CHUNK), lambda hd, sb: (hd, sb, 0))
    return blk, prev, lse


def _attn2_fwd(h, y, name):
    t = h.shape[0]

    def body(q_ref, kc_ref, vc_ref, kp_ref, vp_ref, z_ref, y_in, y_ref, o_ref, lse_ref, m_ref, l_ref):
        band, band_first = _att_band(pl.program_id(1))
        for blk in _att_blocks():
            dil, st, inside, _ = blk
            rows = _rows(st, dil)
            k2, v2 = _att_keys(kc_ref, kp_ref, blk), _att_keys(vc_ref, vp_ref, blk)
            s = jnp.where(band if inside else band_first, _dot_nt(q_ref[rows, :], k2) * ATT_SCALE, NEG)
            m_b = jnp.max(s, axis=1, keepdims=True)
            p = jnp.exp(s - m_b)
            l_b = jnp.sum(p, axis=1, keepdims=True)
            o_b = _dot(p, v2)
            wide = lambda a: jnp.broadcast_to(a, (CHUNK, D_HEAD_DIM))
            if dil == 1:
                m_ref[rows, :], l_ref[rows, :], o_ref[rows, :] = wide(m_b), wide(l_b), o_b
            else:
                m_o = m_ref[rows, :]
                m_n = jnp.maximum(m_o, m_b)
                a_o, a_b = jnp.exp(m_o - m_n), jnp.exp(m_b - m_n)
                m_ref[rows, :] = m_n
                l_ref[rows, :] = a_o * l_ref[rows, :] + a_b * l_b
                o_ref[rows, :] = a_o * o_ref[rows, :] + a_b * o_b
        l = l_ref[...]
        o = o_ref[...] / l
        o_ref[...] = o
        y_ref[...] = (_silu(z_ref[...]) * o).astype(BF16)
        for i in range(SPAN // CHUNK):
            blk_rows = slice(i * CHUNK, (i + 1) * CHUNK)
            lse_ref[i:i + 1, :] = (m_ref[blk_rows, :] + jnp.log(l[blk_rows, :])).T[0:1, :]

    blk, prev, lse_spec = _att_specs(t)
    return pl.pallas_call(
        body, grid=(D_HEADS, t // SPAN), name=name,
        in_specs=[blk(Q_COL), blk(K_COL), blk(V_COL), prev(K_COL), prev(V_COL), blk(Z_COL), ANY],
        out_specs=[blk(WIDTH // D_HEAD_DIM), blk(0), lse_spec],
        out_shape=[jax.ShapeDtypeStruct(y.shape, y.dtype), jax.ShapeDtypeStruct((t, WIDTH), F32),
                   jax.ShapeDtypeStruct((D_HEADS, t // CHUNK, CHUNK), F32)],
        scratch_shapes=[pltpu.VMEM((SPAN, D_HEAD_DIM), F32)] * 2,
        input_output_aliases={6: 0},
        compiler_params=_cp("parallel", "parallel"),
    )(h, h, h, h, h, h, y)


def _attn2_bwd(h, dy, o, lse, dh, name):
    t = h.shape[0]
    ns = t // SPAN

    def body(q_ref, kc_ref, vc_ref, kp_ref, vp_ref, z_ref, dy_ref, o_ref, lse_ref, dh_in, dh_ref,
             acc_ref, dq_ref, do_ref, delta_ref, lsec_ref):
        sb = pl.program_id(1)

        @pl.when(sb == 0)
        def _():
            acc_ref[...] = jnp.zeros_like(acc_ref)

        here, before = pl.multiple_of(sb * SPAN, SPAN), jnp.maximum(sb - 1, 0) * SPAN
        z, ov, dyv = z_ref[...], o_ref[...], dy_ref[...]
        do = dyv * _silu(z)
        do_ref[...] = do
        dh_ref[3, pl.ds(here, SPAN), :] = (dyv * ov * _dsilu(z)).astype(BF16)
        delta_ref[...] = jnp.broadcast_to(jnp.sum(do * ov, axis=1, keepdims=True), (SPAN, D_HEAD_DIM))
        for i in range(SPAN // CHUNK):
            lsec_ref[i * CHUNK:(i + 1) * CHUNK, :] = jnp.broadcast_to(lse_ref[i:i + 1, :], (CHUNK, CHUNK)).T
        band, band_first = _att_band(sb)
        for blk in _att_blocks():
            dil, st, inside, pst = blk
            rows = _rows(st, dil)
            q = q_ref[rows, :].astype(BF16)
            k2, v2 = _att_keys(kc_ref, kp_ref, blk), _att_keys(vc_ref, vp_ref, blk)
            dob = do_ref[rows, :].astype(BF16)
            s = jnp.where(band if inside else band_first, _dot_nt(q, k2) * ATT_SCALE, NEG)
            p = jnp.exp(s - lsec_ref[rows, :][:, 0:1])
            ds = p * (_dot_nt(dob, v2) - delta_ref[rows, :][:, 0:1]) * ATT_SCALE
            dq_b = _dot(ds, k2)
            if dil == 1:
                dq_ref[rows, :] = dq_b
            else:
                dq_ref[rows, :] += dq_b
            dk2, dv2 = _dot_tn(ds, q), _dot_tn(p, dob)
            own = _rows(pl.multiple_of(here + st, CHUNK) if dil == 1 else here + st, dil)
            pbase = (here if inside else before) + pst
            prv = _rows(pl.multiple_of(pbase, CHUNK) if dil == 1 else pbase, dil)
            acc_ref[0, own, :] += dk2[CHUNK:]
            acc_ref[1, own, :] += dv2[CHUNK:]
            acc_ref[0, prv, :] += dk2[:CHUNK]
            acc_ref[1, prv, :] += dv2[:CHUNK]
        dh_ref[0, pl.ds(here, SPAN), :] = dq_ref[...].astype(BF16)

        @pl.when(sb == ns - 1)
        def _():
            dh_ref[1] = acc_ref[0].astype(BF16)
            dh_ref[2] = acc_ref[1].astype(BF16)

    blk, prev, lse_spec = _att_specs(t)
    span = lambda: pltpu.VMEM((SPAN, D_HEAD_DIM), F32)
    return pl.pallas_call(
        body, grid=(D_HEADS, ns), name=name,
        in_specs=[blk(Q_COL), blk(K_COL), blk(V_COL), prev(K_COL), prev(V_COL), blk(Z_COL), blk(WIDTH // D_HEAD_DIM),
                  blk(0), lse_spec, ANY],
        out_specs=pl.BlockSpec((4, t, D_HEAD_DIM), lambda hd, sb: (1, 0, hd)),
        out_shape=jax.ShapeDtypeStruct(dh.shape, dh.dtype),
        scratch_shapes=[pltpu.VMEM((2, t, D_HEAD_DIM), F32), span(), span(), span(), span()],
        input_output_aliases={9: 0},
        compiler_params=_cp("arbitrary", "arbitrary"),
    )(h, h, h, h, h, h, dy, o, lse, dh)


def _place():
    x, y, c = lax.axis_index("x"), lax.axis_index("y"), lax.axis_index("c")
    return x, y, c, 4 * x + 2 * y + c


class _Exchange:
    def __init__(self, arrays, out_shape):
        n = len(arrays)
        self.arrays, self.out_shape = list(arrays), out_shape
        self.scratch = [pltpu.SemaphoreType.DMA((n, 7)), pltpu.SemaphoreType.DMA((n, 7)), pltpu.SemaphoreType.DMA((n,))]


class _AllGather(_Exchange):
    def __init__(self, blocks):
        super().__init__(blocks, [jax.ShapeDtypeStruct((N_DEV,) + b.shape, b.dtype) for b in blocks])

    def _plan(self, ins, outs, sems):
        send_sems, recv_sems, local_sems = sems
        x, y, c, me = _place()
        chips = [(1 - x, y), (x, 1 - y), (1 - x, 1 - y)]

        def copy(a, k, block, to, src=None):
            dst = outs[a].at[block]
            return pltpu.make_async_remote_copy(
                src_ref=dst if src is None else src, dst_ref=dst, send_sem=send_sems.at[a, k],
                recv_sem=recv_sems.at[a, k], device_id=to, device_id_type=MESH)

        n = len(ins)
        index = lambda px, py, pc: 4 * px + 2 * py + pc
        mine = [pltpu.make_async_copy(ins[a], outs[a].at[me], local_sems.at[a]) for a in range(n)]
        first = [copy(a, 0, me, (x, y, 1 - c), src=ins[a]) for a in range(n)]
        first += [copy(a, 1 + j, me, (*chip, c), src=ins[a]) for j, chip in enumerate(chips) for a in range(n)]
        arrive = lambda a, k, px, py, pc: copy(a, k, index(px, py, pc), (x, y, c))
        over_ici = [[arrive(a, 1 + j, *chip, c) for a in range(n)] for j, chip in enumerate(chips)]
        passed = [[copy(a, 4 + j, index(*chip, c), (x, y, 1 - c)) for a in range(n)] for j, chip in enumerate(chips)]
        from_sibling = [arrive(a, 0, x, y, 1 - c) for a in range(n)]
        from_sibling += [arrive(a, 4 + j, *chip, 1 - c) for j, chip in enumerate(chips) for a in range(n)]
        return mine, first, over_ici, passed, from_sibling

    def start(self, ins, outs, sems):
        mine, first, _, _, _ = self._plan(ins, outs, sems)
        for cp in mine + first:
            cp.start()

    def finish(self, ins, outs, sems):
        mine, first, over_ici, passed, from_sibling = self._plan(ins, outs, sems)
        for landed, onward in zip(over_ici, passed):
            for cp, fwd in zip(landed, onward):
                cp.wait_recv()
                fwd.start()
        for cp in from_sibling:
            cp.wait_recv()
        for cp in first + [fwd for onward in passed for fwd in onward]:
            cp.wait_send()
        for cp in mine:
            cp.wait()


class _AllToAll(_Exchange):
    def __init__(self, parts):
        super().__init__(parts, [jax.ShapeDtypeStruct(p.shape, p.dtype) for p in parts])

    def _plan(self, ins, outs, sems):
        send_sems, recv_sems, local_sems = sems
        x, y, c, me = _place()
        flip = lambda v, f: 1 - v if f else v
        peers = [(flip(x, fx), flip(y, fy), flip(c, fc)) for fx in (0, 1) for fy in (0, 1) for fc in (0, 1)][1:]

        def copy(a, k, sending):
            px, py, pc = peers[k]
            there = 4 * px + 2 * py + pc
            return pltpu.make_async_remote_copy(
                src_ref=ins[a].at[there], dst_ref=outs[a].at[me if sending else there], send_sem=send_sems.at[a, k],
                recv_sem=recv_sems.at[a, k], device_id=peers[k], device_id_type=MESH)

        n = len(ins)
        local = [pltpu.make_async_copy(ins[a].at[me], outs[a].at[me], local_sems.at[a]) for a in range(n)]
        return local, [[copy(a, k, sending) for k in range(7) for a in range(n)] for sending in (True, False)]

    def start(self, ins, outs, sems):
        local, (sends, _) = self._plan(ins, outs, sems)
        for cp in local + sends:
            cp.start()

    def finish(self, ins, outs, sems):
        local, (_, both_ways) = self._plan(ins, outs, sems)
        for cp in both_ways + local:
            cp.wait()


def _exchange(ex, name):
    n = len(ex.arrays)

    def body(*refs):
        ins, outs, sems = refs[:n], refs[n:2 * n], refs[2 * n:]
        ex.start(ins, outs, sems)
        ex.finish(ins, outs, sems)

    return pl.pallas_call(body, name=name, in_specs=[ANY] * n, out_specs=[ANY] * n, out_shape=ex.out_shape,
                          scratch_shapes=ex.scratch)(*ex.arrays)


def _sum_parts(parts, tc, name):
    _, r, c = parts.shape

    def body(p_ref, o_ref):
        acc = p_ref[0].astype(F32)
        for d in range(1, N_DEV):
            acc = acc + p_ref[d].astype(F32)
        o_ref[...] = acc

    return pl.pallas_call(
        body, grid=(c // tc,), name=name,
        in_specs=[pl.BlockSpec((N_DEV, r, tc), lambda j: (0, 0, j))], out_specs=pl.BlockSpec((r, tc), lambda j: (0, j)),
        out_shape=jax.ShapeDtypeStruct((r, c), F32), compiler_params=_cp("parallel"),
    )(parts)


def _adamw(w, g, m, v, tr, name):
    r, c = w.shape

    def body(w_ref, g_ref, m_ref, v_ref, d_ref, m2_ref, v2_ref):
        gv = g_ref[...]
        m2 = ADAM_B1 * m_ref[...] + (1.0 - ADAM_B1) * gv
        v2 = ADAM_B2 * v_ref[...] + (1.0 - ADAM_B2) * (gv * gv)
        m_hat = m2 / (1.0 - ADAM_B1 ** ADAM_STEP)
        v_hat = v2 / (1.0 - ADAM_B2 ** ADAM_STEP)
        d_ref[...] = -ADAM_LR * (m_hat / (jnp.sqrt(v_hat) + ADAM_EPS) + ADAM_WD * w_ref[...])
        m2_ref[...] = m2
        v2_ref[...] = v2

    spec = pl.BlockSpec((tr, c), lambda i: (i, 0))
    return pl.pallas_call(
        body, grid=(r // tr,), name=name, in_specs=[spec] * 4, out_specs=[spec] * 3,
        out_shape=[jax.ShapeDtypeStruct((r, c), F32)] * 3, compiler_params=_cp("parallel"),
    )(w, g, m, v)


PACK_ROWS = SUBLANES * LANES


def _pack(arrays):
    flat = [jnp.pad(a.reshape(-1), (0, -a.size % PACK_ROWS)) for a in arrays]
    return jnp.concatenate(flat).reshape(-1, LANES)


def _unpack(packed, shapes):
    flat, out, pos = packed.reshape(-1), [], 0
    for s in shapes:
        size = 1
        for d in s:
            size *= d
        out.append(flat[pos:pos + size].reshape(s))
        pos += size + (-size % PACK_ROWS)
    return out


SMALL = ["even_norm_g", "gmlp_ln_g", "gmlp_ln_b", "gmlp_ws", "gmlp_bs", "ssd_conv_w", "ssd_conv_b", "ssd_dt_bias",
         "ssd_a_log", "ssd_d", "ssd_norm_g", "odd_norm_g", "sconv_w", "final_norm_g"]
ORDER = ["even_norm_g", "even_w_in", "gmlp_ln_g", "gmlp_ln_b", "gmlp_ws", "gmlp_bs", "ssd_conv_w", "ssd_conv_b",
         "ssd_dt_bias", "ssd_a_log", "ssd_d", "ssd_norm_g", "even_w_out", "odd_norm_g", "odd_w_in", "sconv_w",
         "odd_w_out", "final_norm_g"]


def kernel(x, even_norm_g, even_w_in, gmlp_ln_g, gmlp_ln_b, gmlp_ws, gmlp_bs, ssd_conv_w, ssd_conv_b, ssd_dt_bias, ssd_a_log, ssd_d, ssd_norm_g, even_w_out, odd_norm_g, odd_w_in, sconv_w, odd_w_out, final_norm_g, loss_target, m_even_norm_g, m_even_w_in, m_gmlp_ln_g, m_gmlp_ln_b, m_gmlp_ws, m_gmlp_bs, m_ssd_conv_w, m_ssd_conv_b, m_ssd_dt_bias, m_ssd_a_log, m_ssd_d, m_ssd_norm_g, m_even_w_out, m_odd_norm_g, m_odd_w_in, m_sconv_w, m_odd_w_out, m_final_norm_g, v_even_norm_g, v_even_w_in, v_gmlp_ln_g, v_gmlp_ln_b, v_gmlp_ws, v_gmlp_bs, v_ssd_conv_w, v_ssd_conv_b, v_ssd_dt_bias, v_ssd_a_log, v_ssd_d, v_ssd_norm_g, v_even_w_out, v_odd_norm_g, v_odd_w_in, v_sconv_w, v_odd_w_out, v_final_norm_g):
    w = dict(even_norm_g=even_norm_g, even_w_in=even_w_in, gmlp_ln_g=gmlp_ln_g, gmlp_ln_b=gmlp_ln_b, gmlp_ws=gmlp_ws,
             gmlp_bs=gmlp_bs, ssd_conv_w=ssd_conv_w, ssd_conv_b=ssd_conv_b, ssd_dt_bias=ssd_dt_bias,
             ssd_a_log=ssd_a_log, ssd_d=ssd_d, ssd_norm_g=ssd_norm_g, even_w_out=even_w_out, odd_norm_g=odd_norm_g,
             odd_w_in=odd_w_in, sconv_w=sconv_w, odd_w_out=odd_w_out, final_norm_g=final_norm_g)
    m1 = dict(even_norm_g=m_even_norm_g, even_w_in=m_even_w_in, gmlp_ln_g=m_gmlp_ln_g, gmlp_ln_b=m_gmlp_ln_b,
              gmlp_ws=m_gmlp_ws, gmlp_bs=m_gmlp_bs, ssd_conv_w=m_ssd_conv_w, ssd_conv_b=m_ssd_conv_b,
              ssd_dt_bias=m_ssd_dt_bias, ssd_a_log=m_ssd_a_log, ssd_d=m_ssd_d, ssd_norm_g=m_ssd_norm_g,
              even_w_out=m_even_w_out, odd_norm_g=m_odd_norm_g, odd_w_in=m_odd_w_in, sconv_w=m_sconv_w,
              odd_w_out=m_odd_w_out, final_norm_g=m_final_norm_g)
    m2 = dict(even_norm_g=v_even_norm_g, even_w_in=v_even_w_in, gmlp_ln_g=v_gmlp_ln_g, gmlp_ln_b=v_gmlp_ln_b,
              gmlp_ws=v_gmlp_ws, gmlp_bs=v_gmlp_bs, ssd_conv_w=v_ssd_conv_w, ssd_conv_b=v_ssd_conv_b,
              ssd_dt_bias=v_ssd_dt_bias, ssd_a_log=v_ssd_a_log, ssd_d=v_ssd_d, ssd_norm_g=v_ssd_norm_g,
              even_w_out=v_even_w_out, odd_norm_g=v_odd_norm_g, odd_w_in=v_odd_w_in, sconv_w=v_sconv_w,
              odd_w_out=v_odd_w_out, final_norm_g=v_final_norm_g)
    _, _, _, me = _place()
    xs = x[0]
    shard = WIDTH // N_DEV

    small_blk = jnp.concatenate([
        ssd_conv_w[0], jnp.pad(sconv_w[0], ((0, 0), (0, shard))), jnp.pad(odd_norm_g, ((0, 0), (0, shard)))], axis=0)
    g_wte, g_small = _exchange(_AllGather([even_w_in[0].T.astype(BF16), small_blk]), "gather_even_w_in")
    later_weights = _AllGather([odd_w_in[0].T.astype(BF16), even_w_out[0].astype(BF16), odd_w_out[0].astype(BF16)])
    wte = g_wte.reshape(F_EVEN_ALL, D_MODEL)
    wte_dt = jnp.pad(wte[F_EVEN:], ((0, LANES - B_HEADS), (0, 0)))
    conv_w = g_small[:, 0:B_CONV, :].transpose(1, 0, 2).reshape(B_CONV, B_XBC)
    sconv_full = g_small[:, B_CONV:B_CONV + C_CONV, :shard].transpose(1, 0, 2).reshape(C_CONV, WIDTH)
    odd_g = g_small[:, B_CONV + C_CONV, :shard].reshape(1, WIDTH)

    pad_heads = lambda a: jnp.pad(a, ((0, 0), (0, LANES - B_HEADS)))
    alog, dtb = pad_heads(ssd_a_log), pad_heads(ssd_dt_bias)
    d_full = jnp.repeat(ssd_d, B_HEAD_DIM, axis=1)
    ws, bs_t = gmlp_ws[0], gmlp_bs[0].T
    ws_t = jnp.swapaxes(ws, 1, 2)
    proj = dict(tb=True, tm=1024, tn=1024, tk=D_MODEL, out_dtype=F32)
    out_proj = dict(n=D_MODEL, tm=1024, tn=1024, tk=2048, out_dtype=F32)
    dw_out = dict(ta=True, n=D_MODEL, tm=1024, tn=D_MODEL, tk=1024, out_dtype=BF16)
    dx_in = dict(n=D_MODEL, tm=1024, tn=1024, tk=2048, out_dtype=F32)
    dw_in = dict(ta=True, n=D_MODEL, tm=1024, tn=D_MODEL, tk=1024, out_dtype=BF16)
    slabs = lambda g, rows: g.reshape(N_DEV, rows // N_DEV, D_MODEL)

    xn0 = _rms_fwd(xs, even_norm_g, "norm_even")
    h0, (g_wto, g_woe, g_woo) = _mm(xn0, wte, n=F_EVEN, carry=later_weights, name="proj_even", **proj)
    wto, woe, woo = g_wto.reshape(F_ODD, D_MODEL), g_woe.reshape(2 * WIDTH, D_MODEL), g_woo.reshape(2 * WIDTH, D_MODEL)
    hdt = _mm(xn0, wte_dt, tb=True, n=LANES, tm=1024, tn=LANES, tk=D_MODEL, out_dtype=F32, name="proj_dt")
    y0 = _gmlp_fwd(h0, gmlp_ln_g, gmlp_ln_b, ws, bs_t, "gmlp_fwd")
    pre = _ssd_conv_fwd(h0, conv_w, ssd_conv_b, "ssd_conv_fwd")
    y0, ypre, states = _ssd_scan_fwd(pre, hdt, h0, y0, alog, dtb, d_full, ssd_norm_g, "ssd_scan_fwd")
    x1 = _mm(y0, woe, add=xs, name="out_even", **out_proj)
    xn1 = _rms_fwd(x1, odd_g, "norm_odd")
    h1 = _mm(xn1, wto, n=F_ODD, name="proj_odd", **proj)
    y1 = _sconv_fwd(h1, sconv_full, "sconv_fwd")
    y1, att_o, att_lse = _attn2_fwd(h1, y1, "attn_fwd")
    x2 = _mm(y1, woo, add=x1, name="out_odd", **out_proj)
    loss_part, dx2, g_final, dx2_b = _loss_head(x2, final_norm_g.reshape(1, D_MODEL), loss_target[0], "loss_head")

    dy1 = _mm(dx2_b, woo, n=2 * WIDTH, name="dy_odd", **proj)
    gw_woo = _mm(y1, dx2_b, name="dw_out_odd", **dw_out)
    dh1, g_sconv = _sconv_bwd(h1, dy1, sconv_full, "sconv_bwd")
    dh1 = _attn2_bwd(h1, dy1, att_o, att_lse, dh1, "attn_bwd")
    dxn1 = _mm(dh1, wto, name="dx_odd", **dx_in)
    gw_wto = _mm(dh1, xn1, name="dw_in_odd", **dw_in)
    dx1, g_odd, dx1_b = _rms_bwd(x1, odd_g, dxn1, dx2, True, "norm_odd_bwd")

    dy0 = _mm(dx1_b, woe, n=2 * WIDTH, name="dy_even", **proj)
    gw_woe = _mm(y0, dx1_b, name="dw_out_even", **dw_out)
    dh0, g_ws, g_bs_t, g_ln_g, g_ln_b = _gmlp_bwd(h0, dy0, gmlp_ln_g, gmlp_ln_b, ws, ws_t, bs_t, "gmlp_bwd")
    early_grads = _AllToAll([slabs(gw_wto, F_ODD), slabs(gw_woe, 2 * WIDTH), slabs(gw_woo, 2 * WIDTH)])
    (dh0, dpx, dpb, dpc, ddt, g_dtb, g_alog, g_dd, g_ng), (r_wto, r_woe, r_woo) = _ssd_scan_bwd(
        pre, hdt, h0, dy0, ypre, states, dh0, alog, dtb, d_full, ssd_norm_g, early_grads, "ssd_scan_bwd")
    dh0, g_conv_w, g_conv_b = _ssd_conv_bwd(h0, dh0, dpx, dpb, dpc, conv_w, "ssd_conv_bwd")
    gw_main = _mm(dh0, xn0, out_rows=F_EVEN_ALL, name="dw_in_even", **dw_in)
    gw_dt = _mm(ddt, xn0, ta=True, n=D_MODEL, tm=LANES, tn=D_MODEL, tk=1024, out_dtype=BF16, name="dw_dt")
    gw_wte = lax.dynamic_update_slice(gw_main, gw_dt[:B_HEADS], (F_EVEN, 0))
    dxn0_dt = _mm(ddt, wte_dt, n=D_MODEL, tm=1024, tn=D_MODEL, tk=LANES, out_dtype=F32, name="dx_dt")
    dxn0, (r_wte,) = _mm(dh0, wte, add=dxn0_dt, carry=_AllToAll([slabs(gw_wte, F_EVEN_ALL)]), name="dx_even", **dx_in)
    grad_x, g_even = _rms_bwd(xs, even_norm_g, dxn0, dx1, False, "norm_even_bwd")

    small_parts = dict(
        even_norm_g=g_even, gmlp_ln_g=g_ln_g, gmlp_ln_b=g_ln_b, gmlp_ws=g_ws, gmlp_bs=g_bs_t[:, :A_GROUPS].T,
        ssd_conv_w=g_conv_w[:B_CONV], ssd_conv_b=g_conv_b, ssd_dt_bias=g_dtb[:, :B_HEADS], ssd_a_log=g_alog[:, :B_HEADS],
        ssd_d=g_dd.reshape(B_HEADS, B_HEAD_DIM).sum(axis=1), ssd_norm_g=g_ng, odd_norm_g=g_odd,
        sconv_w=g_sconv[:C_CONV], final_norm_g=g_final)
    full_shapes = dict(
        even_norm_g=(1, D_MODEL), gmlp_ln_g=(1, WIDTH), gmlp_ln_b=(1, WIDTH), gmlp_ws=(1, A_GROUPS, CHUNK, CHUNK),
        gmlp_bs=(1, A_GROUPS, CHUNK), ssd_conv_w=(1, B_CONV, B_XBC), ssd_conv_b=(1, B_XBC), ssd_dt_bias=(1, B_HEADS),
        ssd_a_log=(1, B_HEADS), ssd_d=(1, B_HEADS), ssd_norm_g=(1, WIDTH), odd_norm_g=(1, D_MODEL),
        sconv_w=(1, C_CONV, WIDTH), final_norm_g=(D_MODEL,))
    (gathered_small,) = _exchange(_AllGather([_pack([small_parts[k] for k in SMALL])]), "gather_small_grads")
    small_sum = _sum_parts(gathered_small, LANES, "sum_small_grads")
    grads = dict(zip(SMALL, _unpack(small_sum, [full_shapes[k] for k in SMALL])))
    grads["ssd_conv_w"] = lax.dynamic_slice_in_dim(grads["ssd_conv_w"], me * 2 * shard, 2 * shard, axis=2)
    grads["odd_norm_g"] = lax.dynamic_slice_in_dim(grads["odd_norm_g"], me * shard, shard, axis=1)
    grads["sconv_w"] = lax.dynamic_slice_in_dim(grads["sconv_w"], me * shard, shard, axis=2)

    grads["even_w_in"] = _sum_parts(r_wte, 256, "sum_even_w_in").T[None]
    grads["odd_w_in"] = _sum_parts(r_wto, 256, "sum_odd_w_in").T[None]
    grads["even_w_out"] = _sum_parts(r_woe, 512, "sum_even_w_out")[None]
    grads["odd_w_out"] = _sum_parts(r_woo, 512, "sum_odd_w_out")[None]

    delta, new_m, new_v = {}, {}, {}
    for k in ("even_w_in", "odd_w_in", "even_w_out", "odd_w_out"):
        d_k, m_k, v_k = _adamw(w[k][0], grads[k][0], m1[k][0], m2[k][0], 128, "adamw_" + k)
        delta[k], new_m[k], new_v[k] = d_k[None], m_k[None], v_k[None]
    packed = [_pack([src[k] for k in SMALL]) for src in (w, grads, m1, m2)]
    small_out = _adamw(*packed, packed[0].shape[0], "adamw_small")
    shapes = [w[k].shape for k in SMALL]
    for dst, arr in zip((delta, new_m, new_v), small_out):
        dst.update(zip(SMALL, _unpack(arr, shapes)))

    loss = lax.psum(loss_part[0, 0], ("x", "y", "c"))
    return (loss, grad_x[None], *[grads[k] for k in ORDER], *[delta[k] for k in ORDER],
            *[new_m[k] for k in ORDER], *[new_v[k] for k in ORDER])
```

```python
import functools

import jax
import jax.numpy as jnp
from jax import lax
from jax.experimental import pallas as pl
from jax.experimental.pallas import tpu as pltpu

F32, BF16 = jnp.float32, jnp.bfloat16
MESH = pl.DeviceIdType.MESH
ANY = pl.BlockSpec(memory_space=pl.ANY)

N_DEV = 8
D_MODEL = 2048
WIDTH = 2048
CHUNK = 128
A_GROUPS = 8
B_HEADS, B_HEAD_DIM, B_GROUPS, B_STATE, B_CONV = 32, 64, 8, 128, 4
B_GROUP_W = WIDTH // B_GROUPS
B_XBC = WIDTH + 2 * B_GROUPS * B_STATE
C_CONV = 3
D_HEADS, D_HEAD_DIM = 16, 128
D_PATTERNS = ((128, 1), (512, 4), (2048, 16))
F_EVEN = 3 * WIDTH + WIDTH + B_XBC
F_EVEN_ALL = F_EVEN + B_HEADS
F_ODD = 8 * WIDTH
EPS = 1e-5
NEG = -1e30

ADAM_LR, ADAM_B1, ADAM_B2, ADAM_EPS, ADAM_WD, ADAM_STEP = 0.001, 0.9, 0.999, 1e-08, 0.01, 10

VMEM_LIMIT_V7X = 56 * 1024 * 1024
SUBLANES, LANES = 8, 128


def _cp(*sem):
    return pltpu.CompilerParams(dimension_semantics=sem, vmem_limit_bytes=VMEM_LIMIT_V7X)


def _sig(x):
    return 0.5 * jnp.tanh(0.5 * x) + 0.5


def _silu(x):
    return x * _sig(x)


def _dsilu(x):
    s = _sig(x)
    return s * (1.0 + x * (1.0 - s))


def _softplus(x):
    return jnp.maximum(x, 0.0) + jnp.log(1.0 + jnp.exp(-jnp.abs(x)))


def _dot(a, b):
    return jnp.dot(a.astype(BF16), b.astype(BF16), preferred_element_type=F32)


def _dot_nt(a, b):
    return lax.dot_general(a.astype(BF16), b.astype(BF16), (((1,), (1,)), ((), ())), preferred_element_type=F32)


def _dot_tn(a, b):
    return lax.dot_general(a.astype(BF16), b.astype(BF16), (((0,), (0,)), ((), ())), preferred_element_type=F32)


def _split3(x):
    hi = x.astype(BF16)
    r1 = x - hi.astype(F32)
    mid = r1.astype(BF16)
    lo = (r1 - mid.astype(F32)).astype(BF16)
    return hi, mid, lo


def _dot_sel(x, sel):
    return sum(jnp.dot(p, sel, preferred_element_type=F32) for p in _split3(x))


def _sel_dot(sel, x):
    return sum(jnp.dot(sel, p, preferred_element_type=F32) for p in _split3(x))


def _iota(shape, axis):
    return lax.broadcasted_iota(jnp.int32, shape, axis)


def _shift_down(cur, halo, j):
    if j == 0:
        return cur
    r = pltpu.roll(cur, j, 0)
    top = jnp.where(_iota(halo.shape, 0) < j, pltpu.roll(halo, j, 0), r[0:SUBLANES])
    return jnp.concatenate([top, r[SUBLANES:]], axis=0)


def _shift_up(cur, halo, j):
    if j == 0:
        return cur
    n = cur.shape[0]
    r = pltpu.roll(cur, n - j, 0)
    bot = jnp.where(_iota(halo.shape, 0) >= SUBLANES - j, pltpu.roll(halo, SUBLANES - j, 0), r[n - SUBLANES:])
    return jnp.concatenate([r[:n - SUBLANES], bot], axis=0)


def _mm(a, b, *, ta=False, tb=False, n, tm, tn, tk, out_dtype, out_rows=None, add=None, carry=None, name):
    width, rows = a.shape[-1], a.shape[-2]
    feat = width * (a.shape[0] if a.ndim == 3 else 1)
    m, k_len = (feat, rows) if ta else (rows, feat)
    per_part = width // (tm if ta else tk)
    grid = (m // tm, n // tn, k_len // tk)
    nk = grid[2]

    def a_index(i, j, k):
        f = i if ta else k
        pos = (k,) if ta else (i,)
        return pos + (f,) if a.ndim == 2 else (f // per_part,) + pos + (f % per_part,)

    a_block = (tk, tm) if ta else (tm, tk)
    a_spec = pl.BlockSpec(a_block if a.ndim == 2 else (None,) + a_block, a_index)
    b_spec = pl.BlockSpec((tn, tk), lambda i, j, k: (j, k)) if tb else pl.BlockSpec((tk, tn), lambda i, j, k: (k, j))
    io_spec = pl.BlockSpec((tm, tn), lambda i, j, k: (i, j))
    dims = (((0 if ta else 1,), (1 if tb else 0,)), ((), ()))
    has_add, nc = add is not None, len(carry.arrays) if carry else 0

    def body(*refs):
        a_ref, b_ref = refs[0], refs[1]
        add_ref = refs[2] if has_add else None
        pos = 2 + has_add
        c_in, o_ref, c_out = refs[pos:pos + nc], refs[pos + nc], refs[pos + nc + 1:pos + 2 * nc + 1]
        pos += 2 * nc + 1
        acc_ref = refs[pos] if nk > 1 else None
        c_sems = refs[pos + (nk > 1):]
        i, j, k = pl.program_id(0), pl.program_id(1), pl.program_id(2)
        if carry:
            pl.when((i == 0) & (j == 0) & (k == 0))(lambda: carry.start(c_in, c_out, c_sems))

        def finish(r):
            if add_ref is not None:
                r = r + add_ref[...]
            o_ref[...] = r.astype(out_dtype)

        p = lax.dot_general(a_ref[...].astype(BF16), b_ref[...].astype(BF16), dims, preferred_element_type=F32)
        if nk == 1:
            finish(p)
        else:
            @pl.when(k == 0)
            def _():
                acc_ref[...] = p

            @pl.when((k > 0) & (k < nk - 1))
            def _():
                acc_ref[...] += p

            @pl.when(k == nk - 1)
            def _():
                finish(acc_ref[...] + p)

        if carry:
            pl.when((i == grid[0] - 1) & (j == grid[1] - 1) & (k == nk - 1))(lambda: carry.finish(c_in, c_out, c_sems))

    out = pl.pallas_call(
        body, grid=grid, name=name,
        in_specs=[a_spec, b_spec] + [io_spec] * has_add + [ANY] * nc,
        out_specs=[io_spec] + [ANY] * nc,
        out_shape=[jax.ShapeDtypeStruct((out_rows or m, n), out_dtype)] + (carry.out_shape if carry else []),
        scratch_shapes=([pltpu.VMEM((tm, tn), F32)] if nk > 1 else []) + (carry.scratch if carry else []),
        compiler_params=_cp(*(("arbitrary",) * 3 if carry else ("parallel", "parallel", "arbitrary"))),
    )(a, b, *([add] if has_add else []), *(carry.arrays if carry else []))
    return (out[0], out[1:]) if carry else out[0]


def _rms_fwd(x, g, name):
    t, tb = x.shape[0], 512

    def body(x_ref, g_ref, o_ref):
        xv = x_ref[...]
        r = lax.rsqrt(jnp.mean(xv * xv, axis=-1, keepdims=True) + EPS)
        o_ref[...] = (xv * r * g_ref[...]).astype(BF16)

    row = pl.BlockSpec((tb, D_MODEL), lambda i: (i, 0))
    return pl.pallas_call(
        body, grid=(t // tb,), name=name,
        in_specs=[row, pl.BlockSpec((1, D_MODEL), lambda i: (0, 0))], out_specs=row,
        out_shape=jax.ShapeDtypeStruct((t, D_MODEL), BF16), compiler_params=_cp("parallel"),
    )(x, g)


def _rms_bwd(x, g, dxn, dres, bf16_copy, name):
    t, tb = x.shape[0], 256

    def body(x_ref, g_ref, dxn_ref, dres_ref, dx_ref, dg_ref, *dxb_ref):
        xv = x_ref[...]
        r = lax.rsqrt(jnp.mean(xv * xv, axis=-1, keepdims=True) + EPS)
        nv = xv * r
        dy = dxn_ref[...]
        dn = dy * g_ref[...]
        dx = dres_ref[...] + r * (dn - nv * jnp.mean(dn * nv, axis=-1, keepdims=True))
        dx_ref[...] = dx
        for ref in dxb_ref:
            ref[...] = dx.astype(BF16)
        part = jnp.sum(dy * nv, axis=0, keepdims=True)

        @pl.when(pl.program_id(0) == 0)
        def _():
            dg_ref[...] = part

        @pl.when(pl.program_id(0) > 0)
        def _():
            dg_ref[...] += part

    row = pl.BlockSpec((tb, D_MODEL), lambda i: (i, 0))
    vec = pl.BlockSpec((1, D_MODEL), lambda i: (0, 0))
    return pl.pallas_call(
        body, grid=(t // tb,), name=name,
        in_specs=[row, vec, row, row], out_specs=[row, vec] + [row] * bf16_copy,
        out_shape=[jax.ShapeDtypeStruct((t, D_MODEL), F32), jax.ShapeDtypeStruct((1, D_MODEL), F32)]
        + [jax.ShapeDtypeStruct((t, D_MODEL), BF16)] * bf16_copy,
        compiler_params=_cp("arbitrary"),
    )(x, g, dxn, dres)


def _loss_head(x, g, target, name):
    t, tb = x.shape[0], 256

    def body(x_ref, g_ref, t_ref, loss_ref, dx_ref, dg_ref, dxb_ref):
        xv, gv = x_ref[...], g_ref[...]
        r = lax.rsqrt(jnp.mean(xv * xv, axis=-1, keepdims=True) + EPS)
        nv = xv * r
        err = nv * gv - t_ref[...]
        lpart = 0.5 * jnp.sum(jnp.mean(err * err, axis=-1, keepdims=True), axis=0, keepdims=True)
        dy = err * (1.0 / D_MODEL)
        dn = dy * gv
        dx = r * (dn - nv * jnp.mean(dn * nv, axis=-1, keepdims=True))
        dx_ref[...] = dx
        dxb_ref[...] = dx.astype(BF16)
        gpart = jnp.sum(dy * nv, axis=0, keepdims=True)

        @pl.when(pl.program_id(0) == 0)
        def _():
            dg_ref[...] = gpart
            loss_ref[...] = jnp.broadcast_to(lpart, (1, LANES))

        @pl.when(pl.program_id(0) > 0)
        def _():
            dg_ref[...] += gpart
            loss_ref[...] += jnp.broadcast_to(lpart, (1, LANES))

    row = pl.BlockSpec((tb, D_MODEL), lambda i: (i, 0))
    vec = pl.BlockSpec((1, D_MODEL), lambda i: (0, 0))
    return pl.pallas_call(
        body, grid=(t // tb,), name=name,
        in_specs=[row, vec, row], out_specs=[pl.BlockSpec((1, LANES), lambda i: (0, 0)), row, vec, row],
        out_shape=[jax.ShapeDtypeStruct((1, LANES), F32), jax.ShapeDtypeStruct((t, D_MODEL), F32),
                   jax.ShapeDtypeStruct((1, D_MODEL), F32), jax.ShapeDtypeStruct((t, D_MODEL), BF16)],
        compiler_params=_cp("arbitrary"),
    )(x, g, target)


A_GW = WIDTH // A_GROUPS


def _gmlp_common(v, lg, lb):
    xc = v - jnp.mean(v, axis=-1, keepdims=True)
    rs = lax.rsqrt(jnp.mean(xc * xc, axis=-1, keepdims=True) + EPS)
    vh = xc * rs
    return rs, vh, (vh * lg + lb).astype(BF16)


def _gmlp_fwd(h, ln_g, ln_b, ws, bs_t, name):
    t, tb = h.shape[0], 256

    def body(u_ref, v_ref, z_ref, lg_ref, lb_ref, ws_ref, bst_ref, y_ref):
        _, _, vn = _gmlp_common(v_ref[...], lg_ref[...], lb_ref[...])
        causal = _iota((CHUNK, CHUNK), 1) <= _iota((CHUNK, CHUNK), 0)
        for g in range(A_GROUPS):
            w = jnp.where(causal, ws_ref[g], 0.0).astype(BF16)
            cols = slice(g * A_GW, (g + 1) * A_GW)
            for c in range(tb // CHUNK):
                rows = slice(c * CHUNK, (c + 1) * CHUNK)
                mixed = jnp.dot(w, vn[rows, cols], preferred_element_type=F32) + bst_ref[:, g:g + 1]
                y_ref[rows, cols] = (_silu(z_ref[rows, cols]) * (u_ref[rows, cols] * mixed)).astype(BF16)

    col = lambda j: pl.BlockSpec((tb, WIDTH), lambda i: (i, j))
    full = lambda a: pl.BlockSpec(a.shape, lambda i: (0,) * a.ndim)
    return pl.pallas_call(
        body, grid=(t // tb,), name=name,
        in_specs=[col(0), col(1), col(2), full(ln_g), full(ln_b), full(ws), full(bs_t)],
        out_specs=col(0), out_shape=jax.ShapeDtypeStruct((t, 2 * WIDTH), BF16),
        compiler_params=_cp("parallel"),
    )(h, h, h, ln_g, ln_b, ws, bs_t)


def _gmlp_bwd(h, dy, ln_g, ln_b, ws, ws_t, bs_t, name):
    t, tb = h.shape[0], 256

    def body(u_ref, v_ref, z_ref, dy_ref, lg_ref, lb_ref, ws_ref, wst_ref, bst_ref,
             dh_ref, dws_ref, dbst_ref, dlg_ref, dlb_ref, dvn_ref):
        @pl.when(pl.program_id(0) == 0)
        def _():
            dws_ref[...] = jnp.zeros_like(dws_ref)
            dbst_ref[...] = jnp.zeros_like(dbst_ref)
            dlg_ref[...] = jnp.zeros_like(dlg_ref)
            dlb_ref[...] = jnp.zeros_like(dlb_ref)

        rs, vh, vn = _gmlp_common(v_ref[...], lg_ref[...], lb_ref[...])
        row, lane = _iota((CHUNK, CHUNK), 0), _iota((CHUNK, CHUNK), 1)
        for g in range(A_GROUPS):
            w = jnp.where(lane <= row, ws_ref[g], 0.0).astype(BF16)
            wt = jnp.where(row <= lane, wst_ref[g], 0.0).astype(BF16)
            cols = slice(g * A_GW, (g + 1) * A_GW)
            dws_acc = jnp.zeros((CHUNK, CHUNK), F32)
            dbs_acc = jnp.zeros((CHUNK, 1), F32)
            for c in range(tb // CHUNK):
                rows = slice(c * CHUNK, (c + 1) * CHUNK)
                vnb = vn[rows, cols]
                mixed = jnp.dot(w, vnb, preferred_element_type=F32) + bst_ref[:, g:g + 1]
                u, z, dyv = u_ref[rows, cols], z_ref[rows, cols], dy_ref[rows, cols]
                sz = _silu(z)
                dh_ref[rows, cols] = (dyv * sz * mixed).astype(BF16)
                dh_ref[rows, slice(2 * WIDTH + g * A_GW, 2 * WIDTH + (g + 1) * A_GW)] = (
                    dyv * (u * mixed) * _dsilu(z)).astype(BF16)
                dm = dyv * sz * u
                dws_acc += _dot_nt(dm, vnb)
                dbs_acc += jnp.sum(dm, axis=1, keepdims=True)
                dvn_ref[rows, cols] = jnp.dot(wt, dm.astype(BF16), preferred_element_type=F32)
            dws_ref[g] += jnp.where(lane <= row, dws_acc, 0.0)
            dbst_ref[...] += jnp.where(lane == g, dbs_acc, 0.0)
        dvn = dvn_ref[...]
        dlg_ref[...] += jnp.sum(dvn * vh, axis=0, keepdims=True)
        dlb_ref[...] += jnp.sum(dvn, axis=0, keepdims=True)
        dvh = dvn * lg_ref[...]
        dv = rs * (dvh - jnp.mean(dvh, axis=-1, keepdims=True) - vh * jnp.mean(dvh * vh, axis=-1, keepdims=True))
        dh_ref[:, WIDTH:2 * WIDTH] = dv.astype(BF16)

    col = lambda j: pl.BlockSpec((tb, WIDTH), lambda i: (i, j))
    full = lambda a: pl.BlockSpec(a.shape, lambda i: (0,) * a.ndim)
    acc = lambda shape: pl.BlockSpec(shape, lambda i: (0,) * len(shape))
    return pl.pallas_call(
        body, grid=(t // tb,), name=name,
        in_specs=[col(0), col(1), col(2), col(0), full(ln_g), full(ln_b), full(ws), full(ws_t), full(bs_t)],
        out_specs=[pl.BlockSpec((tb, 3 * WIDTH), lambda i: (i, 0)), acc((A_GROUPS, CHUNK, CHUNK)),
                   acc((CHUNK, LANES)), acc((1, WIDTH)), acc((1, WIDTH))],
        out_shape=[jax.ShapeDtypeStruct((t, F_EVEN), BF16), jax.ShapeDtypeStruct((A_GROUPS, CHUNK, CHUNK), F32),
                   jax.ShapeDtypeStruct((CHUNK, LANES), F32), jax.ShapeDtypeStruct((1, WIDTH), F32),
                   jax.ShapeDtypeStruct((1, WIDTH), F32)],
        scratch_shapes=[pltpu.VMEM((tb, WIDTH), F32)],
        compiler_params=_cp("arbitrary"),
    )(h, h, h, dy, ln_g, ln_b, ws, ws_t, bs_t)


def _halo_prev(tb, j):
    return lambda i: (jnp.maximum(i * (tb // SUBLANES) - 1, 0), j)


def _halo_next(tb, j, t):
    return lambda i: (jnp.minimum((i + 1) * (tb // SUBLANES), t // SUBLANES - 1), j)


def _row_select(parts, width):
    row = _iota((SUBLANES, width), 0)
    out = jnp.zeros((SUBLANES, width), F32)
    for k, p in enumerate(parts):
        out = jnp.where(row == k, p, out)
    return out


def _sconv_fwd(h, w, name):
    t, tb = h.shape[0], 256

    def body(bg_ref, cg_ref, hx_ref, z_ref, cgh_ref, hxh_ref, w_ref, y_ref):
        p = cg_ref[...] * hx_ref[...]
        ph = jnp.where(pl.program_id(0) > 0, cgh_ref[...] * hxh_ref[...], 0.0)
        cv = w_ref[2:3, :] * p + w_ref[1:2, :] * _shift_down(p, ph, 1) + w_ref[0:1, :] * _shift_down(p, ph, 2)
        y_ref[...] = (_silu(z_ref[...]) * (bg_ref[...] * cv)).astype(BF16)

    col = lambda j: pl.BlockSpec((tb, WIDTH), lambda i: (i, j))
    halo = lambda j: pl.BlockSpec((SUBLANES, WIDTH), _halo_prev(tb, j))
    return pl.pallas_call(
        body, grid=(t // tb,), name=name,
        in_specs=[col(0), col(1), col(2), col(3), halo(1), halo(2), pl.BlockSpec(w.shape, lambda i: (0, 0))],
        out_specs=col(0), out_shape=jax.ShapeDtypeStruct((t, 2 * WIDTH), BF16),
        compiler_params=_cp("parallel"),
    )(h, h, h, h, h, h, w)


def _sconv_bwd(h, dy, w, name):
    t, tb = h.shape[0], 256
    nb = t // tb

    def body(bg_ref, cg_ref, hx_ref, z_ref, dy_ref, cgh_ref, hxh_ref, bgn_ref, zn_ref, dyn_ref, w_ref, dh_ref, dw_ref):
        i = pl.program_id(0)
        bg, cg, hx, z, dyv = bg_ref[...], cg_ref[...], hx_ref[...], z_ref[...], dy_ref[...]
        p = cg * hx
        ph = jnp.where(i > 0, cgh_ref[...] * hxh_ref[...], 0.0)
        p1, p2 = _shift_down(p, ph, 1), _shift_down(p, ph, 2)
        cv = w_ref[2:3, :] * p + w_ref[1:2, :] * p1 + w_ref[0:1, :] * p2
        sz = _silu(z)
        dcv = dyv * sz * bg
        dcvn = jnp.where(i < nb - 1, dyn_ref[...] * _silu(zn_ref[...]) * bgn_ref[...], 0.0)
        dp = w_ref[2:3, :] * dcv + w_ref[1:2, :] * _shift_up(dcv, dcvn, 1) + w_ref[0:1, :] * _shift_up(dcv, dcvn, 2)
        dh_ref[0] = (dyv * sz * cv).astype(BF16)
        dh_ref[1] = (dp * hx).astype(BF16)
        dh_ref[2] = (dp * cg).astype(BF16)
        dh_ref[3] = (dyv * (bg * cv) * _dsilu(z)).astype(BF16)
        part = _row_select([jnp.sum(dcv * q, axis=0, keepdims=True) for q in (p2, p1, p)], WIDTH)

        @pl.when(i == 0)
        def _():
            dw_ref[...] = part

        @pl.when(i > 0)
        def _():
            dw_ref[...] += part

    col = lambda j: pl.BlockSpec((tb, WIDTH), lambda i: (i, j))
    prev = lambda j: pl.BlockSpec((SUBLANES, WIDTH), _halo_prev(tb, j))
    nxt = lambda j: pl.BlockSpec((SUBLANES, WIDTH), _halo_next(tb, j, t))
    return pl.pallas_call(
        body, grid=(nb,), name=name,
        in_specs=[col(0), col(1), col(2), col(3), col(0), prev(1), prev(2), nxt(0), nxt(3), nxt(0),
                  pl.BlockSpec(w.shape, lambda i: (0, 0))],
        out_specs=[pl.BlockSpec((4, tb, WIDTH), lambda i: (0, i, 0)), pl.BlockSpec((SUBLANES, WIDTH), lambda i: (0, 0))],
        out_shape=[jax.ShapeDtypeStruct((8, t, WIDTH), BF16), jax.ShapeDtypeStruct((SUBLANES, WIDTH), F32)],
        compiler_params=_cp("arbitrary"),
    )(h, h, h, h, dy, h, h, h, h, dy, w)


XBC_COL0 = 4 * WIDTH


def _ssd_conv_fwd(h, w, b, name):
    t, tb = h.shape[0], 256

    def body(x_ref, xh_ref, w_ref, b_ref, o_ref):
        xv = x_ref[...]
        xh = jnp.where(pl.program_id(0) > 0, xh_ref[...], 0.0)
        acc = b_ref[...] + w_ref[3:4, :] * xv
        for j in range(1, B_CONV):
            acc = acc + w_ref[B_CONV - 1 - j:B_CONV - j, :] * _shift_down(xv, xh, j)
        o_ref[...] = acc

    cb = XBC_COL0 // B_XBC
    return pl.pallas_call(
        body, grid=(t // tb,), name=name,
        in_specs=[pl.BlockSpec((tb, B_XBC), lambda i: (i, cb)), pl.BlockSpec((SUBLANES, B_XBC), _halo_prev(tb, cb)),
                  pl.BlockSpec(w.shape, lambda i: (0, 0)), pl.BlockSpec(b.shape, lambda i: (0, 0))],
        out_specs=pl.BlockSpec((tb, B_XBC), lambda i: (i, 0)), out_shape=jax.ShapeDtypeStruct((t, B_XBC), F32),
        compiler_params=_cp("parallel"),
    )(h, h, w, b)


def _ssd_conv_bwd(h, dh, dpx, dpb, dpc, w, name):
    t, tb, tc = h.shape[0], 256, 1024
    nb = t // tb
    r8 = tb // SUBLANES
    last8 = t // SUBLANES - 1

    def body(dpx_ref, dpb_ref, dpc_ref, nx_ref, nb_ref, nc_ref, x_ref, xh_ref, w_ref, dh_in, dh_ref, dw_ref, db_ref):
        j, i = pl.program_id(0), pl.program_id(1)
        pick = lambda a, b_, c: jnp.where(j < 2, a[...], jnp.where(j == 2, b_[...], c[...]))
        dp = pick(dpx_ref, dpb_ref, dpc_ref)
        dn = jnp.where(i < nb - 1, pick(nx_ref, nb_ref, nc_ref), 0.0)
        xv = x_ref[...]
        xh = jnp.where(i > 0, xh_ref[...], 0.0)
        dx = w_ref[3:4, :] * dp
        for s in range(1, B_CONV):
            dx = dx + w_ref[B_CONV - 1 - s:B_CONV - s, :] * _shift_up(dp, dn, s)
        dh_ref[...] = dx.astype(BF16)
        wpart = _row_select([jnp.sum(dp * _shift_down(xv, xh, B_CONV - 1 - k), axis=0, keepdims=True)
                             for k in range(B_CONV)], tc)
        bpart = jnp.sum(dp, axis=0, keepdims=True)

        @pl.when(i == 0)
        def _():
            dw_ref[...] = wpart
            db_ref[...] = bpart

        @pl.when(i > 0)
        def _():
            dw_ref[...] += wpart
            db_ref[...] += bpart

    def src(blk_rows, rowf, sel, colf):
        return pl.BlockSpec((blk_rows, tc), lambda j, i: (jnp.where(sel(j), rowf(i), 0), colf(j)))

    cur = lambda i: i
    nxt = lambda i: jnp.minimum((i + 1) * r8, last8)
    is_x, is_b, is_c = (lambda j: j < 2), (lambda j: j == 2), (lambda j: j == 3)
    xcol, zero = (lambda j: jnp.minimum(j, 1)), (lambda j: 0)
    c0 = XBC_COL0 // tc
    return pl.pallas_call(
        body, grid=(B_XBC // tc, nb), name=name,
        in_specs=[src(tb, cur, is_x, xcol), src(tb, cur, is_b, zero), src(tb, cur, is_c, zero),
                  src(SUBLANES, nxt, is_x, xcol), src(SUBLANES, nxt, is_b, zero), src(SUBLANES, nxt, is_c, zero),
                  pl.BlockSpec((tb, tc), lambda j, i: (i, c0 + j)),
                  pl.BlockSpec((SUBLANES, tc), lambda j, i: (jnp.maximum(i * r8 - 1, 0), c0 + j)),
                  pl.BlockSpec((B_CONV, tc), lambda j, i: (0, j)), ANY],
        out_specs=[pl.BlockSpec((tb, tc), lambda j, i: (i, c0 + j)), pl.BlockSpec((SUBLANES, tc), lambda j, i: (0, j)),
                   pl.BlockSpec((1, tc), lambda j, i: (0, j))],
        out_shape=[jax.ShapeDtypeStruct(dh.shape, dh.dtype), jax.ShapeDtypeStruct((SUBLANES, B_XBC), F32),
                   jax.ShapeDtypeStruct((1, B_XBC), F32)],
        input_output_aliases={9: 0},
        compiler_params=_cp("arbitrary", "arbitrary"),
    )(dpx, dpb, dpc, dpx, dpb, dpc, h, h, w, dh)


HEADS_PER_GROUP = B_HEADS // B_GROUPS


def _ssd_dt(hdt, alog, dtb, name):
    t = hdt.shape[0]
    nc = t // CHUNK

    def body(dtr_ref, alog_ref, dtb_ref, dt_ref, acs_ref, acst_ref):
        dt = _softplus(dtr_ref[...] + dtb_ref[...])
        tri = (_iota((CHUNK, CHUNK), 1) <= _iota((CHUNK, CHUNK), 0)).astype(BF16)
        acs = _sel_dot(tri, dt * -jnp.exp(alog_ref[...]))
        dt_ref[...] = dt
        acs_ref[...] = acs
        acst_ref[...] = acs.T

    tok = pl.BlockSpec((CHUNK, LANES), lambda c: (c, 0))
    vec = pl.BlockSpec((1, LANES), lambda c: (0, 0))
    return pl.pallas_call(
        body, grid=(nc,), name=name, in_specs=[tok, vec, vec],
        out_specs=[tok, tok, pl.BlockSpec((None, LANES, CHUNK), lambda c: (c, 0, 0))],
        out_shape=[jax.ShapeDtypeStruct((t, LANES), F32), jax.ShapeDtypeStruct((t, LANES), F32),
                   jax.ShapeDtypeStruct((nc, LANES, CHUNK), F32)],
        compiler_params=_cp("parallel"),
    )(hdt, alog, dtb)


def _to_group(m, g):
    return pltpu.roll(m, (LANES - HEADS_PER_GROUP * g) % LANES, 1)


def _from_group(m, g):
    return pltpu.roll(m, HEADS_PER_GROUP * g, 1)


def _ssd_group_terms(g, px, pb, pc, dt, acs, acst_ref):
    head = _iota((CHUNK, B_GROUP_W), 1) // B_HEAD_DIM

    def spread(m4):
        out = m4[:, HEADS_PER_GROUP - 1:HEADS_PER_GROUP]
        for j in range(HEADS_PER_GROUP - 2, -1, -1):
            out = jnp.where(head == j, m4[:, j:j + 1], out)
        return out

    dt4, a4 = _to_group(dt, g), _to_group(acs, g)
    rows = [acst_ref[pl.ds(HEADS_PER_GROUP * g + j, 1), :] for j in range(HEADS_PER_GROUP)]
    return dict(xs=_silu(px), bm=_silu(pb), cm=_silu(pc), dt4=dt4, a4=a4, rows=rows, dt_e=spread(dt4), a_e=spread(a4))


def _ssd_decay(tm, j, transposed):
    col, row = tm["a4"][:, j:j + 1], tm["rows"][j]
    lane, sub = _iota((CHUNK, CHUNK), 1), _iota((CHUNK, CHUNK), 0)
    if transposed:
        return jnp.where(sub <= lane, jnp.exp(jnp.minimum(row - col, 0.0)), 0.0)
    return jnp.where(lane <= sub, jnp.exp(jnp.minimum(col - row, 0.0)), 0.0)


GROUPS_PER_STEP = 2


def _ssd_specs(nc, rev):
    ch = (lambda c: nc - 1 - c) if rev else (lambda c: c)
    n = GROUPS_PER_STEP
    gw = lambda off: pl.BlockSpec((CHUNK, n * B_GROUP_W), lambda c, g: (ch(c), off // n + g))
    gn = lambda off: pl.BlockSpec((CHUNK, n * B_STATE), lambda c, g: (ch(c), off // n + g))
    tok = pl.BlockSpec((CHUNK, LANES), lambda c, g: (ch(c), 0))
    vec = pl.BlockSpec((1, LANES), lambda c, g: (0, 0))
    gvec = pl.BlockSpec((1, n * B_GROUP_W), lambda c, g: (0, g))
    st = pl.BlockSpec((None, B_STATE, n * B_GROUP_W), lambda c, g: (ch(c), 0, g))
    tokt = pl.BlockSpec((None, LANES, CHUNK), lambda c, g: (ch(c), 0, 0))
    return gw, gn, tok, tokt, vec, gvec, st


def _group_cols(u):
    return slice(u * B_GROUP_W, (u + 1) * B_GROUP_W), slice(u * B_STATE, (u + 1) * B_STATE)


def _ssd_scan_fwd(pre, dt, acs, acst, h, y, dfull, ng, carry, name):
    t = pre.shape[0]
    nc = t // CHUNK
    n_carried = len(carry.arrays) if carry else 0

    def body(*refs):
        px_ref, pb_ref, pc_ref, dt_ref, acs_ref, acst_ref, z_ref, df_ref, ng_ref, _ = refs[:10]
        refs = refs[10:]
        c_in, refs = refs[:n_carried], refs[n_carried:]
        y_ref, ypre_ref, st_ref = refs[:3]
        c_out, state_ref, c_sems = refs[3:3 + n_carried], refs[3 + n_carried], refs[4 + n_carried:]
        c, pair = pl.program_id(0), pl.program_id(1)
        if carry:
            pl.when((c == 0) & (pair == 0))(lambda: carry.start(c_in, c_out, c_sems))
        for u in range(GROUPS_PER_STEP):
            g = GROUPS_PER_STEP * pair + u
            wide, narrow = _group_cols(u)

            @pl.when(c == 0)
            def _():
                state_ref[g] = jnp.zeros((B_STATE, B_GROUP_W), F32)

            tm = _ssd_group_terms(g, px_ref[:, wide], pb_ref[:, narrow], pc_ref[:, narrow], dt_ref[...],
                                  acs_ref[...], acst_ref)
            xs, bm, cm, a_e = tm["xs"], tm["bm"], tm["cm"], tm["a_e"]
            xdt = xs * tm["dt_e"]
            cb = _dot_nt(cm, bm)
            head = _iota((CHUNK, B_GROUP_W), 1) // B_HEAD_DIM
            yd = jnp.zeros((CHUNK, B_GROUP_W), F32)
            for j in range(HEADS_PER_GROUP):
                yd = jnp.where(head == j, _dot(cb * _ssd_decay(tm, j, False), xdt), yd)
            st = state_ref[g]
            st_ref[:, wide] = st
            yv = yd + jnp.exp(a_e) * _dot(cm, st) + df_ref[:, wide] * xs
            a_last = a_e[CHUNK - 1:CHUNK, :]
            state_ref[g] = st * jnp.exp(a_last) + _dot_tn(bm, xdt * jnp.exp(a_last - a_e))
            ypre_ref[:, wide] = yv
            yz = yv * _silu(z_ref[:, wide])
            r = lax.rsqrt(jnp.mean(yz * yz, axis=-1, keepdims=True) + EPS)
            y_ref[:, wide] = (yz * r * ng_ref[:, wide]).astype(BF16)
        if carry:
            last = (c == nc - 1) & (pair == B_GROUPS // GROUPS_PER_STEP - 1)
            pl.when(last)(lambda: carry.finish(c_in, c_out, c_sems))

    gw, gn, tok, tokt, vec, gvec, st = _ssd_specs(nc, False)
    out = pl.pallas_call(
        body, grid=(nc, B_GROUPS // GROUPS_PER_STEP), name=name,
        in_specs=[gw(0), gn(WIDTH // B_STATE), gn((WIDTH + B_GROUPS * B_STATE) // B_STATE), tok, tok, tokt,
                  gw(3 * WIDTH // B_GROUP_W), gvec, gvec, ANY] + [ANY] * n_carried,
        out_specs=[gw(WIDTH // B_GROUP_W), gw(0), st] + [ANY] * n_carried,
        out_shape=[jax.ShapeDtypeStruct(y.shape, y.dtype), jax.ShapeDtypeStruct((t, WIDTH), F32),
                   jax.ShapeDtypeStruct((nc, B_STATE, WIDTH), F32)] + (carry.out_shape if carry else []),
        scratch_shapes=[pltpu.VMEM((B_GROUPS, B_STATE, B_GROUP_W), F32)] + (carry.scratch if carry else []),
        input_output_aliases={9: 0},
        compiler_params=_cp("arbitrary", "arbitrary"),
    )(pre, pre, pre, dt, acs, acst, h, dfull, ng, y, *(carry.arrays if carry else []))
    return out[:3], out[3:]


def _ssd_scan_bwd(pre, hdt, dt, acs, acst, h, dy, ypre, states, dh, alog, dtb, dfull, ng, carry, name):
    t = pre.shape[0]
    nc = t // CHUNK

    n_carried = len(carry.arrays) if carry else 0

    def one_group(c, g, px_ref, pb_ref, pc_ref, dtr_ref, dt_ref, acs_ref, acst_ref, z_ref, dy_ref, ypre_ref, st_ref,
                  alog_ref, dtb_ref, df_ref, ng_ref, dz_ref, dpx_ref, dpb_ref, dpc_ref, dd_ref, dng_ref, dstate_ref):
        @pl.when(c == 0)
        def _():
            dstate_ref[g] = jnp.zeros((B_STATE, B_GROUP_W), F32)
            dd_ref[g] = jnp.zeros((1, B_GROUP_W), F32)
            dng_ref[g] = jnp.zeros((1, B_GROUP_W), F32)

        px, pb, pc, dtr = px_ref[...], pb_ref[...], pc_ref[...], dtr_ref[...]
        tm = _ssd_group_terms(g, px, pb, pc, dt_ref[...], acs_ref[...], acst_ref)
        xs, bm, cm, a_e, dt_e = tm["xs"], tm["bm"], tm["cm"], tm["a_e"], tm["dt_e"]
        xdt = xs * dt_e
        head = _iota((CHUNK, B_GROUP_W), 1) // B_HEAD_DIM

        z, yv, ngv = z_ref[...], ypre_ref[...], ng_ref[...]
        sz = _silu(z)
        yz = yv * sz
        r = lax.rsqrt(jnp.mean(yz * yz, axis=-1, keepdims=True) + EPS)
        dyn = dy_ref[...]
        dng_ref[g] += jnp.sum(dyn * yz * r, axis=0, keepdims=True)
        q = dyn * ngv
        dyz = r * q - yz * (r * r * r) * jnp.mean(q * yz, axis=-1, keepdims=True)
        dyv = dyz * sz
        dz_ref[...] = (dyz * yv * _dsilu(z)).astype(BF16)
        dd_ref[g] += jnp.sum(dyv * xs, axis=0, keepdims=True)
        dxs = df_ref[...] * dyv

        st = st_ref[...]
        ea = jnp.exp(a_e)
        ead = ea * dyv
        dcm = _dot_nt(ead, st)
        da_e = dyv * (ea * _dot(cm, st))

        dsn = dstate_ref[g]
        a_last = a_e[CHUNK - 1:CHUNK, :]
        ea_last = jnp.exp(a_last)
        dstate_ref[g] = dsn * ea_last + _dot_tn(cm, ead)
        wdec = jnp.exp(a_last - a_e)
        xw = xdt * wdec
        dxw = _dot(bm, dsn)
        dxdt = dxw * wdec
        dbm = _dot_nt(xw, dsn)
        zc = dxw * xw
        da_last = jnp.sum(zc, axis=0, keepdims=True) + jnp.sum(dsn * st, axis=0, keepdims=True) * ea_last
        da_e = da_e - zc + jnp.where(_iota((CHUNK, B_GROUP_W), 0) == CHUNK - 1, da_last, 0.0)

        cb, cbt = _dot_nt(cm, bm), _dot_nt(bm, cm)
        dcb, dcbt = jnp.zeros((CHUNK, CHUNK), F32), jnp.zeros((CHUNK, CHUNK), F32)
        da4 = jnp.zeros((CHUNK, LANES), F32)
        lane = _iota((CHUNK, LANES), 1)
        for j in range(HEADS_PER_GROUP):
            mine = head == j
            gm = _dot_nt(jnp.where(mine, dyv, 0.0), xdt)
            gmt = _dot_nt(jnp.where(mine, xdt, 0.0), dyv)
            dec, dect = _ssd_decay(tm, j, False), _ssd_decay(tm, j, True)
            dcb += gm * dec
            dcbt += gmt * dect
            da_j = (jnp.sum(gm * cb * dec, axis=1, keepdims=True) - jnp.sum(gmt * cbt * dect, axis=1, keepdims=True))
            da4 = jnp.where(lane == j, da_j, da4)
            dxdt = dxdt + jnp.where(mine, _dot(cbt * dect, dyv), 0.0)
        dcm = dcm + _dot(dcb, bm)
        dbm = dbm + _dot(dcbt, cm)

        gather = (_iota((B_GROUP_W, LANES), 0) // B_HEAD_DIM == _iota((B_GROUP_W, LANES), 1)).astype(BF16)
        per_head = _dot_sel(jnp.concatenate([da_e, dxdt * xs], axis=0), gather)
        da4 = da4 + per_head[:CHUNK]
        rtri = (_iota((CHUNK, CHUNK), 1) >= _iota((CHUNK, CHUNK), 0)).astype(BF16)
        dadt4 = _sel_dot(rtri, da4)
        a_heads = -jnp.exp(alog_ref[...])
        rows8 = lambda v: jnp.broadcast_to(v, (SUBLANES, LANES))
        ddt4 = dadt4 * _to_group(rows8(a_heads), g)[0:1, :] + per_head[CHUNK:]
        dxs = dxs + dxdt * dt_e
        ddt = _from_group(ddt4, g) * _sig(dtr + dtb_ref[...])
        da_heads = _from_group(rows8(jnp.sum(dadt4 * tm["dt4"], axis=0, keepdims=True)), g)[0:1, :]

        dpx_ref[...] = dxs * _dsilu(px)
        dpb_ref[...] = dbm * _dsilu(pb)
        dpc_ref[...] = dcm * _dsilu(pc)
        return ddt, da_heads * a_heads

    def body(*refs):
        (px_ref, pb_ref, pc_ref, dtr_ref, dt_ref, acs_ref, acst_ref, z_ref, dy_ref, ypre_ref, st_ref, alog_ref,
         dtb_ref, df_ref, ng_ref, _) = refs[:16]
        refs = refs[16:]
        c_in, refs = refs[:n_carried], refs[n_carried:]
        dz_ref, dpx_ref, dpb_ref, dpc_ref, ddt_ref, dbias_ref, dalog_ref, dd_ref, dng_ref = refs[:9]
        c_out, dstate_ref, c_sems = refs[9:9 + n_carried], refs[9 + n_carried], refs[10 + n_carried:]
        c, pair = pl.program_id(0), pl.program_id(1)
        if carry:
            pl.when((c == 0) & (pair == 0))(lambda: carry.start(c_in, c_out, c_sems))
        ddt, dalog = 0.0, 0.0
        for u in range(GROUPS_PER_STEP):
            wide, narrow = _group_cols(u)
            view = lambda ref, cols: ref.at[:, cols]
            ddt_u, dalog_u = one_group(
                c, GROUPS_PER_STEP * pair + u, view(px_ref, wide), view(pb_ref, narrow), view(pc_ref, narrow), dtr_ref,
                dt_ref, acs_ref, acst_ref, view(z_ref, wide), view(dy_ref, wide), view(ypre_ref, wide), view(st_ref, wide),
                alog_ref, dtb_ref, view(df_ref, wide), view(ng_ref, wide), view(dz_ref, wide), view(dpx_ref, wide),
                view(dpb_ref, narrow), view(dpc_ref, narrow), dd_ref, dng_ref, dstate_ref)
            ddt, dalog = ddt + ddt_u, dalog + dalog_u
        first = (c == 0) & (pair == 0)
        bias_part = jnp.sum(ddt, axis=0, keepdims=True)

        @pl.when(pair == 0)
        def _():
            ddt_ref[...] = ddt

        @pl.when(pair > 0)
        def _():
            ddt_ref[...] += ddt

        @pl.when(first)
        def _():
            dbias_ref[...] = bias_part
            dalog_ref[...] = dalog

        @pl.when(jnp.logical_not(first))
        def _():
            dbias_ref[...] += bias_part
            dalog_ref[...] += dalog

        if carry:
            last = (c == nc - 1) & (pair == B_GROUPS // GROUPS_PER_STEP - 1)
            pl.when(last)(lambda: carry.finish(c_in, c_out, c_sems))

    gw, gn, tok, tokt, vec, gvec, st = _ssd_specs(nc, True)
    acc = lambda shape: pl.BlockSpec(shape, lambda c, g: (0,) * len(shape))
    rev = lambda c: nc - 1 - c
    out = pl.pallas_call(
        body, grid=(nc, B_GROUPS // GROUPS_PER_STEP), name=name,
        in_specs=[gw(0), gn(WIDTH // B_STATE), gn((WIDTH + B_GROUPS * B_STATE) // B_STATE), tok, tok, tok, tokt,
                  gw(3 * WIDTH // B_GROUP_W), gw(WIDTH // B_GROUP_W), gw(0), st, vec, vec, gvec, gvec, ANY]
        + [ANY] * n_carried,
        out_specs=[gw(3 * WIDTH // B_GROUP_W), gw(0), gn(0), gn(0), tok, vec, vec, acc((B_GROUPS, 1, B_GROUP_W)), acc((B_GROUPS, 1, B_GROUP_W))] + [ANY] * n_carried,
        out_shape=[jax.ShapeDtypeStruct(dh.shape, dh.dtype), jax.ShapeDtypeStruct((t, WIDTH), F32),
                   jax.ShapeDtypeStruct((t, B_GROUPS * B_STATE), F32), jax.ShapeDtypeStruct((t, B_GROUPS * B_STATE), F32),
                   jax.ShapeDtypeStruct((t, LANES), F32), jax.ShapeDtypeStruct((1, LANES), F32),
                   jax.ShapeDtypeStruct((1, LANES), F32), jax.ShapeDtypeStruct((B_GROUPS, 1, B_GROUP_W), F32),
                   jax.ShapeDtypeStruct((B_GROUPS, 1, B_GROUP_W), F32)] + (carry.out_shape if carry else []),
        scratch_shapes=[pltpu.VMEM((B_GROUPS, B_STATE, B_GROUP_W), F32)] + (carry.scratch if carry else []),
        input_output_aliases={15: 0},
        compiler_params=_cp("arbitrary", "arbitrary"),
    )(pre, pre, pre, hdt, dt, acs, acst, h, dy, ypre, states, alog, dtb, dfull, ng, dh,
      *(carry.arrays if carry else []))
    return out[:9], out[9:]


Q_COL, K_COL, V_COL, Z_COL = [(4 + i) * WIDTH // D_HEAD_DIM for i in range(4)]
ATT_SCALE = D_HEAD_DIM ** -0.5


SPAN = 2048


def _att_blocks():
    out = []
    for _, dil in D_PATTERNS:
        nbl = SPAN // (CHUNK * dil)
        for r in range(dil):
            for bl in range(nbl):
                st = r + dil * CHUNK * bl
                out.append((dil, st, bl > 0, st - dil * CHUNK if bl > 0 else r + dil * CHUNK * (nbl - 1)))
    return out


def _rows(start, dil):
    return pl.ds(start, CHUNK) if dil == 1 else pl.ds(start, CHUNK, stride=dil)


def _att_keys(kc_ref, kp_ref, blk):
    dil, st, inside, pst = blk
    prev = (kc_ref if inside else kp_ref)[_rows(pst, dil), :]
    return jnp.concatenate([prev, kc_ref[_rows(st, dil), :]], axis=0).astype(BF16)


def _att_band(span_index):
    lane, sub = _iota((CHUNK, 2 * CHUNK), 1), _iota((CHUNK, 2 * CHUNK), 0)
    band = (lane >= sub) & (lane <= sub + CHUNK)
    return band, band & ((lane >= CHUNK) | (span_index > 0))


def _att_specs(t):
    blk = lambda off: pl.BlockSpec((SPAN, D_HEAD_DIM), lambda hd, sb: (sb, off + hd))
    prev = lambda off: pl.BlockSpec((SPAN, D_HEAD_DIM), lambda hd, sb: (jnp.maximum(sb - 1, 0), off + hd))
    lse = pl.BlockSpec((None, SPAN // CHUNK, CHUNK), lambda hd, sb: (hd, sb, 0))
    return blk, prev, lse


def _attn2_fwd(h, y, name):
    t = h.shape[0]

    def body(q_ref, kc_ref, vc_ref, kp_ref, vp_ref, z_ref, y_in, y_ref, o_ref, lse_ref, m_ref, l_ref):
        band, band_first = _att_band(pl.program_id(1))
        for blk in _att_blocks():
            dil, st, inside, _ = blk
            rows = _rows(st, dil)
            k2, v2 = _att_keys(kc_ref, kp_ref, blk), _att_keys(vc_ref, vp_ref, blk)
            s = jnp.where(band if inside else band_first, _dot_nt(q_ref[rows, :], k2) * ATT_SCALE, NEG)
            m_b = jnp.max(s, axis=1, keepdims=True)
            p = jnp.exp(s - m_b)
            l_b = jnp.sum(p, axis=1, keepdims=True)
            o_b = _dot(p, v2)
            wide = lambda a: jnp.broadcast_to(a, (CHUNK, D_HEAD_DIM))
            if dil == 1:
                m_ref[rows, :], l_ref[rows, :], o_ref[rows, :] = wide(m_b), wide(l_b), o_b
            else:
                m_o = m_ref[rows, :]
                m_n = jnp.maximum(m_o, m_b)
                a_o, a_b = jnp.exp(m_o - m_n), jnp.exp(m_b - m_n)
                m_ref[rows, :] = m_n
                l_ref[rows, :] = a_o * l_ref[rows, :] + a_b * l_b
                o_ref[rows, :] = a_o * o_ref[rows, :] + a_b * o_b
        l = l_ref[...]
        o = o_ref[...] / l
        o_ref[...] = o
        y_ref[...] = (_silu(z_ref[...]) * o).astype(BF16)
        for i in range(SPAN // CHUNK):
            blk_rows = slice(i * CHUNK, (i + 1) * CHUNK)
            lse_ref[i:i + 1, :] = (m_ref[blk_rows, :] + jnp.log(l[blk_rows, :])).T[0:1, :]

    blk, prev, lse_spec = _att_specs(t)
    return pl.pallas_call(
        body, grid=(D_HEADS, t // SPAN), name=name,
        in_specs=[blk(Q_COL), blk(K_COL), blk(V_COL), prev(K_COL), prev(V_COL), blk(Z_COL), ANY],
        out_specs=[blk(WIDTH // D_HEAD_DIM), blk(0), lse_spec],
        out_shape=[jax.ShapeDtypeStruct(y.shape, y.dtype), jax.ShapeDtypeStruct((t, WIDTH), F32),
                   jax.ShapeDtypeStruct((D_HEADS, t // CHUNK, CHUNK), F32)],
        scratch_shapes=[pltpu.VMEM((SPAN, D_HEAD_DIM), F32)] * 2,
        input_output_aliases={6: 0},
        compiler_params=_cp("parallel", "parallel"),
    )(h, h, h, h, h, h, y)


def _attn2_bwd(h, dy, o, lse, dh, name):
    t = h.shape[0]
    ns = t // SPAN

    def body(q_ref, kc_ref, vc_ref, kp_ref, vp_ref, z_ref, dy_ref, o_ref, lse_ref, dh_in, dh_ref,
             acc_ref, dq_ref, do_ref, delta_ref, lsec_ref):
        sb = pl.program_id(1)

        @pl.when(sb == 0)
        def _():
            acc_ref[...] = jnp.zeros_like(acc_ref)

        here, before = pl.multiple_of(sb * SPAN, SPAN), jnp.maximum(sb - 1, 0) * SPAN
        z, ov, dyv = z_ref[...], o_ref[...], dy_ref[...]
        do = dyv * _silu(z)
        do_ref[...] = do
        dh_ref[3, pl.ds(here, SPAN), :] = (dyv * ov * _dsilu(z)).astype(BF16)
        delta_ref[...] = jnp.broadcast_to(jnp.sum(do * ov, axis=1, keepdims=True), (SPAN, D_HEAD_DIM))
        for i in range(SPAN // CHUNK):
            lsec_ref[i * CHUNK:(i + 1) * CHUNK, :] = jnp.broadcast_to(lse_ref[i:i + 1, :], (CHUNK, CHUNK)).T
        band, band_first = _att_band(sb)
        for blk in _att_blocks():
            dil, st, inside, pst = blk
            rows = _rows(st, dil)
            q = q_ref[rows, :].astype(BF16)
            k2, v2 = _att_keys(kc_ref, kp_ref, blk), _att_keys(vc_ref, vp_ref, blk)
            dob = do_ref[rows, :].astype(BF16)
            s = jnp.where(band if inside else band_first, _dot_nt(q, k2) * ATT_SCALE, NEG)
            p = jnp.exp(s - lsec_ref[rows, :][:, 0:1])
            ds = p * (_dot_nt(dob, v2) - delta_ref[rows, :][:, 0:1]) * ATT_SCALE
            dq_b = _dot(ds, k2)
            if dil == 1:
                dq_ref[rows, :] = dq_b
            else:
                dq_ref[rows, :] += dq_b
            dk2, dv2 = _dot_tn(ds, q), _dot_tn(p, dob)
            own = _rows(pl.multiple_of(here + st, CHUNK) if dil == 1 else here + st, dil)
            pbase = (here if inside else before) + pst
            prv = _rows(pl.multiple_of(pbase, CHUNK) if dil == 1 else pbase, dil)
            acc_ref[0, own, :] += dk2[CHUNK:]
            acc_ref[1, own, :] += dv2[CHUNK:]
            acc_ref[0, prv, :] += dk2[:CHUNK]
            acc_ref[1, prv, :] += dv2[:CHUNK]
        dh_ref[0, pl.ds(here, SPAN), :] = dq_ref[...].astype(BF16)

        @pl.when(sb == ns - 1)
        def _():
            dh_ref[1] = acc_ref[0].astype(BF16)
            dh_ref[2] = acc_ref[1].astype(BF16)

    blk, prev, lse_spec = _att_specs(t)
    span = lambda: pltpu.VMEM((SPAN, D_HEAD_DIM), F32)
    return pl.pallas_call(
        body, grid=(D_HEADS, ns), name=name,
        in_specs=[blk(Q_COL), blk(K_COL), blk(V_COL), prev(K_COL), prev(V_COL), blk(Z_COL), blk(WIDTH // D_HEAD_DIM),
                  blk(0), lse_spec, ANY],
        out_specs=pl.BlockSpec((4, t, D_HEAD_DIM), lambda hd, sb: (1, 0, hd)),
        out_shape=jax.ShapeDtypeStruct(dh.shape, dh.dtype),
        scratch_shapes=[pltpu.VMEM((2, t, D_HEAD_DIM), F32), span(), span(), span(), span()],
        input_output_aliases={9: 0},
        compiler_params=_cp("arbitrary", "arbitrary"),
    )(h, h, h, h, h, h, dy, o, lse, dh)


def _place():
    x, y, c = lax.axis_index("x"), lax.axis_index("y"), lax.axis_index("c")
    return x, y, c, 4 * x + 2 * y + c


class _Exchange:
    def __init__(self, arrays, out_shape):
        n = len(arrays)
        self.arrays, self.out_shape = list(arrays), out_shape
        self.scratch = [pltpu.SemaphoreType.DMA((n, 7)), pltpu.SemaphoreType.DMA((n, 7)), pltpu.SemaphoreType.DMA((n,))]


class _AllGather(_Exchange):
    def __init__(self, blocks):
        super().__init__(blocks, [jax.ShapeDtypeStruct((N_DEV,) + b.shape, b.dtype) for b in blocks])

    def _plan(self, ins, outs, sems):
        send_sems, recv_sems, local_sems = sems
        x, y, c, me = _place()
        chips = [(1 - x, y), (x, 1 - y), (1 - x, 1 - y)]

        def copy(a, k, block, to, src=None):
            dst = outs[a].at[block]
            return pltpu.make_async_remote_copy(
                src_ref=dst if src is None else src, dst_ref=dst, send_sem=send_sems.at[a, k],
                recv_sem=recv_sems.at[a, k], device_id=to, device_id_type=MESH)

        n = len(ins)
        index = lambda px, py, pc: 4 * px + 2 * py + pc
        mine = [pltpu.make_async_copy(ins[a], outs[a].at[me], local_sems.at[a]) for a in range(n)]
        first = [copy(a, 0, me, (x, y, 1 - c), src=ins[a]) for a in range(n)]
        first += [copy(a, 1 + j, me, (*chip, c), src=ins[a]) for j, chip in enumerate(chips) for a in range(n)]
        arrive = lambda a, k, px, py, pc: copy(a, k, index(px, py, pc), (x, y, c))
        over_ici = [[arrive(a, 1 + j, *chip, c) for a in range(n)] for j, chip in enumerate(chips)]
        passed = [[copy(a, 4 + j, index(*chip, c), (x, y, 1 - c)) for a in range(n)] for j, chip in enumerate(chips)]
        from_sibling = [arrive(a, 0, x, y, 1 - c) for a in range(n)]
        from_sibling += [arrive(a, 4 + j, *chip, 1 - c) for j, chip in enumerate(chips) for a in range(n)]
        return mine, first, over_ici, passed, from_sibling

    def start(self, ins, outs, sems):
        mine, first, _, _, _ = self._plan(ins, outs, sems)
        for cp in mine + first:
            cp.start()

    def finish(self, ins, outs, sems):
        mine, first, over_ici, passed, from_sibling = self._plan(ins, outs, sems)
        for landed, onward in zip(over_ici, passed):
            for cp, fwd in zip(landed, onward):
                cp.wait_recv()
                fwd.start()
        for cp in from_sibling:
            cp.wait_recv()
        for cp in first + [fwd for onward in passed for fwd in onward]:
            cp.wait_send()
        for cp in mine:
            cp.wait()


class _AllToAll(_Exchange):
    def __init__(self, parts):
        super().__init__(parts, [jax.ShapeDtypeStruct(p.shape, p.dtype) for p in parts])

    def _plan(self, ins, outs, sems):
        send_sems, recv_sems, local_sems = sems
        x, y, c, me = _place()
        flip = lambda v, f: 1 - v if f else v
        peers = [(flip(x, fx), flip(y, fy), flip(c, fc)) for fx in (0, 1) for fy in (0, 1) for fc in (0, 1)][1:]

        def copy(a, k, sending):
            px, py, pc = peers[k]
            there = 4 * px + 2 * py + pc
            return pltpu.make_async_remote_copy(
                src_ref=ins[a].at[there], dst_ref=outs[a].at[me if sending else there], send_sem=send_sems.at[a, k],
                recv_sem=recv_sems.at[a, k], device_id=peers[k], device_id_type=MESH)

        n = len(ins)
        local = [pltpu.make_async_copy(ins[a].at[me], outs[a].at[me], local_sems.at[a]) for a in range(n)]
        return local, [[copy(a, k, sending) for k in range(7) for a in range(n)] for sending in (True, False)]

    def start(self, ins, outs, sems):
        local, (sends, _) = self._plan(ins, outs, sems)
        for cp in local + sends:
            cp.start()

    def finish(self, ins, outs, sems):
        local, (_, both_ways) = self._plan(ins, outs, sems)
        for cp in both_ways + local:
            cp.wait()


def _exchange(ex, name):
    n = len(ex.arrays)

    def body(*refs):
        ins, outs, sems = refs[:n], refs[n:2 * n], refs[2 * n:]
        ex.start(ins, outs, sems)
        ex.finish(ins, outs, sems)

    return pl.pallas_call(body, name=name, in_specs=[ANY] * n, out_specs=[ANY] * n, out_shape=ex.out_shape,
                          scratch_shapes=ex.scratch)(*ex.arrays)


def _sum_parts(parts, tc, name):
    _, r, c = parts.shape

    def body(p_ref, o_ref):
        acc = p_ref[0].astype(F32)
        for d in range(1, N_DEV):
            acc = acc + p_ref[d].astype(F32)
        o_ref[...] = acc

    return pl.pallas_call(
        body, grid=(c // tc,), name=name,
        in_specs=[pl.BlockSpec((N_DEV, r, tc), lambda j: (0, 0, j))], out_specs=pl.BlockSpec((r, tc), lambda j: (0, j)),
        out_shape=jax.ShapeDtypeStruct((r, c), F32), compiler_params=_cp("parallel"),
    )(parts)


def _adamw(w, g, m, v, tr, name):
    r, c = w.shape

    def body(w_ref, g_ref, m_ref, v_ref, d_ref, m2_ref, v2_ref):
        gv = g_ref[...]
        m2 = ADAM_B1 * m_ref[...] + (1.0 - ADAM_B1) * gv
        v2 = ADAM_B2 * v_ref[...] + (1.0 - ADAM_B2) * (gv * gv)
        m_hat = m2 / (1.0 - ADAM_B1 ** ADAM_STEP)
        v_hat = v2 / (1.0 - ADAM_B2 ** ADAM_STEP)
        d_ref[...] = -ADAM_LR * (m_hat / (jnp.sqrt(v_hat) + ADAM_EPS) + ADAM_WD * w_ref[...])
        m2_ref[...] = m2
        v2_ref[...] = v2

    spec = pl.BlockSpec((tr, c), lambda i: (i, 0))
    return pl.pallas_call(
        body, grid=(r // tr,), name=name, in_specs=[spec] * 4, out_specs=[spec] * 3,
        out_shape=[jax.ShapeDtypeStruct((r, c), F32)] * 3, compiler_params=_cp("parallel"),
    )(w, g, m, v)


PACK_ROWS = SUBLANES * LANES


def _pack(arrays):
    flat = [jnp.pad(a.reshape(-1), (0, -a.size % PACK_ROWS)) for a in arrays]
    return jnp.concatenate(flat).reshape(-1, LANES)


def _unpack(packed, shapes):
    flat, out, pos = packed.reshape(-1), [], 0
    for s in shapes:
        size = 1
        for d in s:
            size *= d
        out.append(flat[pos:pos + size].reshape(s))
        pos += size + (-size % PACK_ROWS)
    return out


SMALL = ["even_norm_g", "gmlp_ln_g", "gmlp_ln_b", "gmlp_ws", "gmlp_bs", "ssd_conv_w", "ssd_conv_b", "ssd_dt_bias",
         "ssd_a_log", "ssd_d", "ssd_norm_g", "odd_norm_g", "sconv_w", "final_norm_g"]
ORDER = ["even_norm_g", "even_w_in", "gmlp_ln_g", "gmlp_ln_b", "gmlp_ws", "gmlp_bs", "ssd_conv_w", "ssd_conv_b",
         "ssd_dt_bias", "ssd_a_log", "ssd_d", "ssd_norm_g", "even_w_out", "odd_norm_g", "odd_w_in", "sconv_w",
         "odd_w_out", "final_norm_g"]


def kernel(x, even_norm_g, even_w_in, gmlp_ln_g, gmlp_ln_b, gmlp_ws, gmlp_bs, ssd_conv_w, ssd_conv_b, ssd_dt_bias, ssd_a_log, ssd_d, ssd_norm_g, even_w_out, odd_norm_g, odd_w_in, sconv_w, odd_w_out, final_norm_g, loss_target, m_even_norm_g, m_even_w_in, m_gmlp_ln_g, m_gmlp_ln_b, m_gmlp_ws, m_gmlp_bs, m_ssd_conv_w, m_ssd_conv_b, m_ssd_dt_bias, m_ssd_a_log, m_ssd_d, m_ssd_norm_g, m_even_w_out, m_odd_norm_g, m_odd_w_in, m_sconv_w, m_odd_w_out, m_final_norm_g, v_even_norm_g, v_even_w_in, v_gmlp_ln_g, v_gmlp_ln_b, v_gmlp_ws, v_gmlp_bs, v_ssd_conv_w, v_ssd_conv_b, v_ssd_dt_bias, v_ssd_a_log, v_ssd_d, v_ssd_norm_g, v_even_w_out, v_odd_norm_g, v_odd_w_in, v_sconv_w, v_odd_w_out, v_final_norm_g):
    w = dict(even_norm_g=even_norm_g, even_w_in=even_w_in, gmlp_ln_g=gmlp_ln_g, gmlp_ln_b=gmlp_ln_b, gmlp_ws=gmlp_ws,
             gmlp_bs=gmlp_bs, ssd_conv_w=ssd_conv_w, ssd_conv_b=ssd_conv_b, ssd_dt_bias=ssd_dt_bias,
             ssd_a_log=ssd_a_log, ssd_d=ssd_d, ssd_norm_g=ssd_norm_g, even_w_out=even_w_out, odd_norm_g=odd_norm_g,
             odd_w_in=odd_w_in, sconv_w=sconv_w, odd_w_out=odd_w_out, final_norm_g=final_norm_g)
    m1 = dict(even_norm_g=m_even_norm_g, even_w_in=m_even_w_in, gmlp_ln_g=m_gmlp_ln_g, gmlp_ln_b=m_gmlp_ln_b,
              gmlp_ws=m_gmlp_ws, gmlp_bs=m_gmlp_bs, ssd_conv_w=m_ssd_conv_w, ssd_conv_b=m_ssd_conv_b,
              ssd_dt_bias=m_ssd_dt_bias, ssd_a_log=m_ssd_a_log, ssd_d=m_ssd_d, ssd_norm_g=m_ssd_norm_g,
              even_w_out=m_even_w_out, odd_norm_g=m_odd_norm_g, odd_w_in=m_odd_w_in, sconv_w=m_sconv_w,
              odd_w_out=m_odd_w_out, final_norm_g=m_final_norm_g)
    m2 = dict(even_norm_g=v_even_norm_g, even_w_in=v_even_w_in, gmlp_ln_g=v_gmlp_ln_g, gmlp_ln_b=v_gmlp_ln_b,
              gmlp_ws=v_gmlp_ws, gmlp_bs=v_gmlp_bs, ssd_conv_w=v_ssd_conv_w, ssd_conv_b=v_ssd_conv_b,
              ssd_dt_bias=v_ssd_dt_bias, ssd_a_log=v_ssd_a_log, ssd_d=v_ssd_d, ssd_norm_g=v_ssd_norm_g,
              even_w_out=v_even_w_out, odd_norm_g=v_odd_norm_g, odd_w_in=v_odd_w_in, sconv_w=v_sconv_w,
              odd_w_out=v_odd_w_out, final_norm_g=v_final_norm_g)
    _, _, _, me = _place()
    xs = x[0]
    shard = WIDTH // N_DEV

    small_blk = jnp.concatenate([
        ssd_conv_w[0], jnp.pad(sconv_w[0], ((0, 0), (0, shard))), jnp.pad(odd_norm_g, ((0, 0), (0, shard)))], axis=0)
    g_wte, g_small = _exchange(_AllGather([even_w_in[0].T.astype(BF16), small_blk]), "gather_even_w_in")
    out_weights = _AllGather([even_w_out[0].astype(BF16), odd_w_out[0].astype(BF16)])
    wte = g_wte.reshape(F_EVEN_ALL, D_MODEL)
    wte_dt = jnp.pad(wte[F_EVEN:], ((0, LANES - B_HEADS), (0, 0)))
    conv_w = g_small[:, 0:B_CONV, :].transpose(1, 0, 2).reshape(B_CONV, B_XBC)
    sconv_full = g_small[:, B_CONV:B_CONV + C_CONV, :shard].transpose(1, 0, 2).reshape(C_CONV, WIDTH)
    odd_g = g_small[:, B_CONV + C_CONV, :shard].reshape(1, WIDTH)

    pad_heads = lambda a: jnp.pad(a, ((0, 0), (0, LANES - B_HEADS)))
    alog, dtb = pad_heads(ssd_a_log), pad_heads(ssd_dt_bias)
    d_full = jnp.repeat(ssd_d, B_HEAD_DIM, axis=1)
    ws, bs_t = gmlp_ws[0], gmlp_bs[0].T
    ws_t = jnp.swapaxes(ws, 1, 2)
    proj = dict(tb=True, tm=1024, tn=1024, tk=D_MODEL, out_dtype=F32)
    out_proj = dict(n=D_MODEL, tm=1024, tn=1024, tk=2048, out_dtype=F32)
    dw_out = dict(ta=True, n=D_MODEL, tm=1024, tn=D_MODEL, tk=1024, out_dtype=BF16)
    dx_in = dict(n=D_MODEL, tm=1024, tn=1024, tk=2048, out_dtype=F32)
    dw_in = dict(ta=True, n=D_MODEL, tm=1024, tn=D_MODEL, tk=1024, out_dtype=BF16)
    slabs = lambda g, rows: g.reshape(N_DEV, rows // N_DEV, D_MODEL)

    xn0 = _rms_fwd(xs, even_norm_g, "norm_even")
    h0, (g_woe, g_woo) = _mm(xn0, wte, n=F_EVEN, carry=out_weights, name="proj_even", **proj)
    woe, woo = g_woe.reshape(2 * WIDTH, D_MODEL), g_woo.reshape(2 * WIDTH, D_MODEL)
    hdt = _mm(xn0, wte_dt, tb=True, n=LANES, tm=1024, tn=LANES, tk=D_MODEL, out_dtype=F32, name="proj_dt")
    y0 = _gmlp_fwd(h0, gmlp_ln_g, gmlp_ln_b, ws, bs_t, "gmlp_fwd")
    pre = _ssd_conv_fwd(h0, conv_w, ssd_conv_b, "ssd_conv_fwd")
    dt, acs, acst = _ssd_dt(hdt, alog, dtb, "ssd_dt")
    (y0, ypre, states), (g_wto,) = _ssd_scan_fwd(
        pre, dt, acs, acst, h0, y0, d_full, ssd_norm_g, _AllGather([odd_w_in[0].T.astype(BF16)]), "ssd_scan_fwd")
    wto = g_wto.reshape(F_ODD, D_MODEL)
    x1 = _mm(y0, woe, add=xs, name="out_even", **out_proj)
    xn1 = _rms_fwd(x1, odd_g, "norm_odd")
    h1 = _mm(xn1, wto, n=F_ODD, name="proj_odd", **proj)
    y1 = _sconv_fwd(h1, sconv_full, "sconv_fwd")
    y1, att_o, att_lse = _attn2_fwd(h1, y1, "attn_fwd")
    x2 = _mm(y1, woo, add=x1, name="out_odd", **out_proj)
    loss_part, dx2, g_final, dx2_b = _loss_head(x2, final_norm_g.reshape(1, D_MODEL), loss_target[0], "loss_head")

    dy1 = _mm(dx2_b, woo, n=2 * WIDTH, name="dy_odd", **proj)
    gw_woo = _mm(y1, dx2_b, name="dw_out_odd", **dw_out)
    dh1, g_sconv = _sconv_bwd(h1, dy1, sconv_full, "sconv_bwd")
    dh1 = _attn2_bwd(h1, dy1, att_o, att_lse, dh1, "attn_bwd")
    dxn1 = _mm(dh1, wto, name="dx_odd", **dx_in)
    gw_wto = _mm(dh1, xn1, name="dw_in_odd", **dw_in)
    dx1, g_odd, dx1_b = _rms_bwd(x1, odd_g, dxn1, dx2, True, "norm_odd_bwd")

    dy0 = _mm(dx1_b, woe, n=2 * WIDTH, name="dy_even", **proj)
    gw_woe = _mm(y0, dx1_b, name="dw_out_even", **dw_out)
    dh0, g_ws, g_bs_t, g_ln_g, g_ln_b = _gmlp_bwd(h0, dy0, gmlp_ln_g, gmlp_ln_b, ws, ws_t, bs_t, "gmlp_bwd")
    odd_grads = _AllToAll([slabs(gw_wto, F_ODD), slabs(gw_woo, 2 * WIDTH)])
    (dh0, dpx, dpb, dpc, ddt, g_dtb, g_alog, g_dd, g_ng), (r_wto, r_woo) = _ssd_scan_bwd(
        pre, hdt, dt, acs, acst, h0, dy0, ypre, states, dh0, alog, dtb, d_full, ssd_norm_g, odd_grads, "ssd_scan_bwd")
    dh0, g_conv_w, g_conv_b = _ssd_conv_bwd(h0, dh0, dpx, dpb, dpc, conv_w, "ssd_conv_bwd")
    gw_main, (r_woe,) = _mm(dh0, xn0, out_rows=F_EVEN_ALL, carry=_AllToAll([slabs(gw_woe, 2 * WIDTH)]),
                            name="dw_in_even", **dw_in)
    gw_dt = _mm(ddt, xn0, ta=True, n=D_MODEL, tm=LANES, tn=D_MODEL, tk=1024, out_dtype=BF16, name="dw_dt")
    gw_wte = lax.dynamic_update_slice(gw_main, gw_dt[:B_HEADS], (F_EVEN, 0))
    dxn0_dt = _mm(ddt, wte_dt, n=D_MODEL, tm=1024, tn=D_MODEL, tk=LANES, out_dtype=F32, name="dx_dt")
    dxn0, (r_wte,) = _mm(dh0, wte, add=dxn0_dt, carry=_AllToAll([slabs(gw_wte, F_EVEN_ALL)]), name="dx_even", **dx_in)
    grad_x, g_even = _rms_bwd(xs, even_norm_g, dxn0, dx1, False, "norm_even_bwd")

    small_parts = dict(
        even_norm_g=g_even, gmlp_ln_g=g_ln_g, gmlp_ln_b=g_ln_b, gmlp_ws=g_ws, gmlp_bs=g_bs_t[:, :A_GROUPS].T,
        ssd_conv_w=g_conv_w[:B_CONV], ssd_conv_b=g_conv_b, ssd_dt_bias=g_dtb[:, :B_HEADS], ssd_a_log=g_alog[:, :B_HEADS],
        ssd_d=g_dd.reshape(B_HEADS, B_HEAD_DIM).sum(axis=1), ssd_norm_g=g_ng, odd_norm_g=g_odd,
        sconv_w=g_sconv[:C_CONV], final_norm_g=g_final)
    full_shapes = dict(
        even_norm_g=(1, D_MODEL), gmlp_ln_g=(1, WIDTH), gmlp_ln_b=(1, WIDTH), gmlp_ws=(1, A_GROUPS, CHUNK, CHUNK),
        gmlp_bs=(1, A_GROUPS, CHUNK), ssd_conv_w=(1, B_CONV, B_XBC), ssd_conv_b=(1, B_XBC), ssd_dt_bias=(1, B_HEADS),
        ssd_a_log=(1, B_HEADS), ssd_d=(1, B_HEADS), ssd_norm_g=(1, WIDTH), odd_norm_g=(1, D_MODEL),
        sconv_w=(1, C_CONV, WIDTH), final_norm_g=(D_MODEL,))
    (gathered_small,) = _exchange(_AllGather([_pack([small_parts[k] for k in SMALL])]), "gather_small_grads")
    small_sum = _sum_parts(gathered_small, LANES, "sum_small_grads")
    grads = dict(zip(SMALL, _unpack(small_sum, [full_shapes[k] for k in SMALL])))
    grads["ssd_conv_w"] = lax.dynamic_slice_in_dim(grads["ssd_conv_w"], me * 2 * shard, 2 * shard, axis=2)
    grads["odd_norm_g"] = lax.dynamic_slice_in_dim(grads["odd_norm_g"], me * shard, shard, axis=1)
    grads["sconv_w"] = lax.dynamic_slice_in_dim(grads["sconv_w"], me * shard, shard, axis=2)

    grads["even_w_in"] = _sum_parts(r_wte, 256, "sum_even_w_in").T[None]
    grads["odd_w_in"] = _sum_parts(r_wto, 256, "sum_odd_w_in").T[None]
    grads["even_w_out"] = _sum_parts(r_woe, 512, "sum_even_w_out")[None]
    grads["odd_w_out"] = _sum_parts(r_woo, 512, "sum_odd_w_out")[None]

    delta, new_m, new_v = {}, {}, {}
    for k in ("even_w_in", "odd_w_in", "even_w_out", "odd_w_out"):
        d_k, m_k, v_k = _adamw(w[k][0], grads[k][0], m1[k][0], m2[k][0], 128, "adamw_" + k)
        delta[k], new_m[k], new_v[k] = d_k[None], m_k[None], v_k[None]
    packed = [_pack([src[k] for k in SMALL]) for src in (w, grads, m1, m2)]
    small_out = _adamw(*packed, packed[0].shape[0], "adamw_small")
    shapes = [w[k].shape for k in SMALL]
    for dst, arr in zip((delta, new_m, new_v), small_out):
        dst.update(zip(SMALL, _unpack(arr, shapes)))

    loss = lax.psum(loss_part[0, 0], ("x", "y", "c"))
    return (loss, grad_x[None], *[grads[k] for k in ORDER], *[delta[k] for k in ORDER],
            *[new_m[k] for k in ORDER], *[new_v[k] for k in ORDER])
```

```python
import functools

import jax
import jax.numpy as jnp
from jax import lax
from jax.experimental import pallas as pl
from jax.experimental.pallas import tpu as pltpu

F32, BF16 = jnp.float32, jnp.bfloat16
MESH = pl.DeviceIdType.MESH
ANY = pl.BlockSpec(memory_space=pl.ANY)

N_DEV = 8
D_MODEL = 2048
WIDTH = 2048
CHUNK = 128
A_GROUPS = 8
B_HEADS, B_HEAD_DIM, B_GROUPS, B_STATE, B_CONV = 32, 64, 8, 128, 4
B_GROUP_W = WIDTH // B_GROUPS
B_XBC = WIDTH + 2 * B_GROUPS * B_STATE
C_CONV = 3
D_HEADS, D_HEAD_DIM = 16, 128
D_PATTERNS = ((128, 1), (512, 4), (2048, 16))
F_EVEN = 3 * WIDTH + WIDTH + B_XBC
F_EVEN_ALL = F_EVEN + B_HEADS
F_ODD = 8 * WIDTH
EPS = 1e-5
NEG = -1e30

ADAM_LR, ADAM_B1, ADAM_B2, ADAM_EPS, ADAM_WD, ADAM_STEP = 0.001, 0.9, 0.999, 1e-08, 0.01, 10

VMEM_LIMIT_V7X = 56 * 1024 * 1024
SUBLANES, LANES = 8, 128


def _cp(*sem):
    return pltpu.CompilerParams(dimension_semantics=sem, vmem_limit_bytes=VMEM_LIMIT_V7X)


def _sig(x):
    return 0.5 * jnp.tanh(0.5 * x) + 0.5


def _silu(x):
    return x * _sig(x)


def _dsilu(x):
    s = _sig(x)
    return s * (1.0 + x * (1.0 - s))


def _softplus(x):
    return jnp.maximum(x, 0.0) + jnp.log(1.0 + jnp.exp(-jnp.abs(x)))


def _dot(a, b):
    return jnp.dot(a.astype(BF16), b.astype(BF16), preferred_element_type=F32)


def _dot_nt(a, b):
    return lax.dot_general(a.astype(BF16), b.astype(BF16), (((1,), (1,)), ((), ())), preferred_element_type=F32)


def _dot_tn(a, b):
    return lax.dot_general(a.astype(BF16), b.astype(BF16), (((0,), (0,)), ((), ())), preferred_element_type=F32)


def _split3(x):
    hi = x.astype(BF16)
    r1 = x - hi.astype(F32)
    mid = r1.astype(BF16)
    lo = (r1 - mid.astype(F32)).astype(BF16)
    return hi, mid, lo


def _dot_sel(x, sel):
    return sum(jnp.dot(p, sel, preferred_element_type=F32) for p in _split3(x))


def _sel_dot(sel, x):
    return sum(jnp.dot(sel, p, preferred_element_type=F32) for p in _split3(x))


def _iota(shape, axis):
    return lax.broadcasted_iota(jnp.int32, shape, axis)


def _shift_down(cur, halo, j):
    if j == 0:
        return cur
    r = pltpu.roll(cur, j, 0)
    top = jnp.where(_iota(halo.shape, 0) < j, pltpu.roll(halo, j, 0), r[0:SUBLANES])
    return jnp.concatenate([top, r[SUBLANES:]], axis=0)


def _shift_up(cur, halo, j):
    if j == 0:
        return cur
    n = cur.shape[0]
    r = pltpu.roll(cur, n - j, 0)
    bot = jnp.where(_iota(halo.shape, 0) >= SUBLANES - j, pltpu.roll(halo, SUBLANES - j, 0), r[n - SUBLANES:])
    return jnp.concatenate([r[:n - SUBLANES], bot], axis=0)


def _mm(a, b, *, ta=False, tb=False, n, tm, tn, tk, out_dtype, out_rows=None, add=None, carry=None, name):
    width, rows = a.shape[-1], a.shape[-2]
    feat = width * (a.shape[0] if a.ndim == 3 else 1)
    m, k_len = (feat, rows) if ta else (rows, feat)
    per_part = width // (tm if ta else tk)
    grid = (m // tm, n // tn, k_len // tk)
    nk = grid[2]

    def a_index(i, j, k):
        f = i if ta else k
        pos = (k,) if ta else (i,)
        return pos + (f,) if a.ndim == 2 else (f // per_part,) + pos + (f % per_part,)

    a_block = (tk, tm) if ta else (tm, tk)
    a_spec = pl.BlockSpec(a_block if a.ndim == 2 else (None,) + a_block, a_index)
    b_spec = pl.BlockSpec((tn, tk), lambda i, j, k: (j, k)) if tb else pl.BlockSpec((tk, tn), lambda i, j, k: (k, j))
    io_spec = pl.BlockSpec((tm, tn), lambda i, j, k: (i, j))
    dims = (((0 if ta else 1,), (1 if tb else 0,)), ((), ()))
    has_add, nc = add is not None, len(carry.arrays) if carry else 0

    def body(*refs):
        a_ref, b_ref = refs[0], refs[1]
        add_ref = refs[2] if has_add else None
        pos = 2 + has_add
        c_in, o_ref, c_out = refs[pos:pos + nc], refs[pos + nc], refs[pos + nc + 1:pos + 2 * nc + 1]
        pos += 2 * nc + 1
        acc_ref = refs[pos] if nk > 1 else None
        c_sems = refs[pos + (nk > 1):]
        i, j, k = pl.program_id(0), pl.program_id(1), pl.program_id(2)
        if carry:
            pl.when((i == 0) & (j == 0) & (k == 0))(lambda: carry.start(c_in, c_out, c_sems))

        def finish(r):
            if add_ref is not None:
                r = r + add_ref[...]
            o_ref[...] = r.astype(out_dtype)

        p = lax.dot_general(a_ref[...].astype(BF16), b_ref[...].astype(BF16), dims, preferred_element_type=F32)
        if nk == 1:
            finish(p)
        else:
            @pl.when(k == 0)
            def _():
                acc_ref[...] = p

            @pl.when((k > 0) & (k < nk - 1))
            def _():
                acc_ref[...] += p

            @pl.when(k == nk - 1)
            def _():
                finish(acc_ref[...] + p)

        if carry:
            pl.when((i == grid[0] - 1) & (j == grid[1] - 1) & (k == nk - 1))(lambda: carry.finish(c_in, c_out, c_sems))

    out = pl.pallas_call(
        body, grid=grid, name=name,
        in_specs=[a_spec, b_spec] + [io_spec] * has_add + [ANY] * nc,
        out_specs=[io_spec] + [ANY] * nc,
        out_shape=[jax.ShapeDtypeStruct((out_rows or m, n), out_dtype)] + (carry.out_shape if carry else []),
        scratch_shapes=([pltpu.VMEM((tm, tn), F32)] if nk > 1 else []) + (carry.scratch if carry else []),
        compiler_params=_cp(*(("arbitrary",) * 3 if carry else ("parallel", "parallel", "arbitrary"))),
    )(a, b, *([add] if has_add else []), *(carry.arrays if carry else []))
    return (out[0], out[1:]) if carry else out[0]


def _rms_fwd(x, g, name):
    t, tb = x.shape[0], 512

    def body(x_ref, g_ref, o_ref):
        xv = x_ref[...]
        r = lax.rsqrt(jnp.mean(xv * xv, axis=-1, keepdims=True) + EPS)
        o_ref[...] = (xv * r * g_ref[...]).astype(BF16)

    row = pl.BlockSpec((tb, D_MODEL), lambda i: (i, 0))
    return pl.pallas_call(
        body, grid=(t // tb,), name=name,
        in_specs=[row, pl.BlockSpec((1, D_MODEL), lambda i: (0, 0))], out_specs=row,
        out_shape=jax.ShapeDtypeStruct((t, D_MODEL), BF16), compiler_params=_cp("parallel"),
    )(x, g)


def _rms_bwd(x, g, dxn, dres, bf16_copy, name):
    t, tb = x.shape[0], 256

    def body(x_ref, g_ref, dxn_ref, dres_ref, dx_ref, dg_ref, *dxb_ref):
        xv = x_ref[...]
        r = lax.rsqrt(jnp.mean(xv * xv, axis=-1, keepdims=True) + EPS)
        nv = xv * r
        dy = dxn_ref[...]
        dn = dy * g_ref[...]
        dx = dres_ref[...] + r * (dn - nv * jnp.mean(dn * nv, axis=-1, keepdims=True))
        dx_ref[...] = dx
        for ref in dxb_ref:
            ref[...] = dx.astype(BF16)
        part = jnp.sum(dy * nv, axis=0, keepdims=True)

        @pl.when(pl.program_id(0) == 0)
        def _():
            dg_ref[...] = part

        @pl.when(pl.program_id(0) > 0)
        def _():
            dg_ref[...] += part

    row = pl.BlockSpec((tb, D_MODEL), lambda i: (i, 0))
    vec = pl.BlockSpec((1, D_MODEL), lambda i: (0, 0))
    return pl.pallas_call(
        body, grid=(t // tb,), name=name,
        in_specs=[row, vec, row, row], out_specs=[row, vec] + [row] * bf16_copy,
        out_shape=[jax.ShapeDtypeStruct((t, D_MODEL), F32), jax.ShapeDtypeStruct((1, D_MODEL), F32)]
        + [jax.ShapeDtypeStruct((t, D_MODEL), BF16)] * bf16_copy,
        compiler_params=_cp("arbitrary"),
    )(x, g, dxn, dres)


def _loss_head(x, g, target, name):
    t, tb = x.shape[0], 256

    def body(x_ref, g_ref, t_ref, loss_ref, dx_ref, dg_ref, dxb_ref):
        xv, gv = x_ref[...], g_ref[...]
        r = lax.rsqrt(jnp.mean(xv * xv, axis=-1, keepdims=True) + EPS)
        nv = xv * r
        err = nv * gv - t_ref[...]
        lpart = 0.5 * jnp.sum(jnp.mean(err * err, axis=-1, keepdims=True), axis=0, keepdims=True)
        dy = err * (1.0 / D_MODEL)
        dn = dy * gv
        dx = r * (dn - nv * jnp.mean(dn * nv, axis=-1, keepdims=True))
        dx_ref[...] = dx
        dxb_ref[...] = dx.astype(BF16)
        gpart = jnp.sum(dy * nv, axis=0, keepdims=True)

        @pl.when(pl.program_id(0) == 0)
        def _():
            dg_ref[...] = gpart
            loss_ref[...] = jnp.broadcast_to(lpart, (1, LANES))

        @pl.when(pl.program_id(0) > 0)
        def _():
            dg_ref[...] += gpart
            loss_ref[...] += jnp.broadcast_to(lpart, (1, LANES))

    row = pl.BlockSpec((tb, D_MODEL), lambda i: (i, 0))
    vec = pl.BlockSpec((1, D_MODEL), lambda i: (0, 0))
    return pl.pallas_call(
        body, grid=(t // tb,), name=name,
        in_specs=[row, vec, row], out_specs=[pl.BlockSpec((1, LANES), lambda i: (0, 0)), row, vec, row],
        out_shape=[jax.ShapeDtypeStruct((1, LANES), F32), jax.ShapeDtypeStruct((t, D_MODEL), F32),
                   jax.ShapeDtypeStruct((1, D_MODEL), F32), jax.ShapeDtypeStruct((t, D_MODEL), BF16)],
        compiler_params=_cp("arbitrary"),
    )(x, g, target)


A_GW = WIDTH // A_GROUPS


def _gmlp_common(v, lg, lb):
    xc = v - jnp.mean(v, axis=-1, keepdims=True)
    rs = lax.rsqrt(jnp.mean(xc * xc, axis=-1, keepdims=True) + EPS)
    vh = xc * rs
    return rs, vh, (vh * lg + lb).astype(BF16)


def _gmlp_fwd(h, ln_g, ln_b, ws, bs_t, name):
    t, tb = h.shape[0], 256

    def body(u_ref, v_ref, z_ref, lg_ref, lb_ref, ws_ref, bst_ref, y_ref):
        _, _, vn = _gmlp_common(v_ref[...], lg_ref[...], lb_ref[...])
        causal = _iota((CHUNK, CHUNK), 1) <= _iota((CHUNK, CHUNK), 0)
        for g in range(A_GROUPS):
            w = jnp.where(causal, ws_ref[g], 0.0).astype(BF16)
            cols = slice(g * A_GW, (g + 1) * A_GW)
            for c in range(tb // CHUNK):
                rows = slice(c * CHUNK, (c + 1) * CHUNK)
                mixed = jnp.dot(w, vn[rows, cols], preferred_element_type=F32) + bst_ref[:, g:g + 1]
                y_ref[rows, cols] = (_silu(z_ref[rows, cols]) * (u_ref[rows, cols] * mixed)).astype(BF16)

    col = lambda j: pl.BlockSpec((tb, WIDTH), lambda i: (i, j))
    full = lambda a: pl.BlockSpec(a.shape, lambda i: (0,) * a.ndim)
    return pl.pallas_call(
        body, grid=(t // tb,), name=name,
        in_specs=[col(0), col(1), col(2), full(ln_g), full(ln_b), full(ws), full(bs_t)],
        out_specs=col(0), out_shape=jax.ShapeDtypeStruct((t, 2 * WIDTH), BF16),
        compiler_params=_cp("parallel"),
    )(h, h, h, ln_g, ln_b, ws, bs_t)


def _gmlp_bwd(h, dy, ln_g, ln_b, ws, ws_t, bs_t, name):
    t, tb = h.shape[0], 256

    def body(u_ref, v_ref, z_ref, dy_ref, lg_ref, lb_ref, ws_ref, wst_ref, bst_ref,
             dh_ref, dws_ref, dbst_ref, dlg_ref, dlb_ref, dvn_ref):
        @pl.when(pl.program_id(0) == 0)
        def _():
            dws_ref[...] = jnp.zeros_like(dws_ref)
            dbst_ref[...] = jnp.zeros_like(dbst_ref)
            dlg_ref[...] = jnp.zeros_like(dlg_ref)
            dlb_ref[...] = jnp.zeros_like(dlb_ref)

        rs, vh, vn = _gmlp_common(v_ref[...], lg_ref[...], lb_ref[...])
        row, lane = _iota((CHUNK, CHUNK), 0), _iota((CHUNK, CHUNK), 1)
        for g in range(A_GROUPS):
            w = jnp.where(lane <= row, ws_ref[g], 0.0).astype(BF16)
            wt = jnp.where(row <= lane, wst_ref[g], 0.0).astype(BF16)
            cols = slice(g * A_GW, (g + 1) * A_GW)
            dws_acc = jnp.zeros((CHUNK, CHUNK), F32)
            dbs_acc = jnp.zeros((CHUNK, 1), F32)
            for c in range(tb // CHUNK):
                rows = slice(c * CHUNK, (c + 1) * CHUNK)
                vnb = vn[rows, cols]
                mixed = jnp.dot(w, vnb, preferred_element_type=F32) + bst_ref[:, g:g + 1]
                u, z, dyv = u_ref[rows, cols], z_ref[rows, cols], dy_ref[rows, cols]
                sz = _silu(z)
                dh_ref[rows, cols] = (dyv * sz * mixed).astype(BF16)
                dh_ref[rows, slice(2 * WIDTH + g * A_GW, 2 * WIDTH + (g + 1) * A_GW)] = (
                    dyv * (u * mixed) * _dsilu(z)).astype(BF16)
                dm = dyv * sz * u
                dws_acc += _dot_nt(dm, vnb)
                dbs_acc += jnp.sum(dm, axis=1, keepdims=True)
                dvn_ref[rows, cols] = jnp.dot(wt, dm.astype(BF16), preferred_element_type=F32)
            dws_ref[g] += jnp.where(lane <= row, dws_acc, 0.0)
            dbst_ref[...] += jnp.where(lane == g, dbs_acc, 0.0)
        dvn = dvn_ref[...]
        dlg_ref[...] += jnp.sum(dvn * vh, axis=0, keepdims=True)
        dlb_ref[...] += jnp.sum(dvn, axis=0, keepdims=True)
        dvh = dvn * lg_ref[...]
        dv = rs * (dvh - jnp.mean(dvh, axis=-1, keepdims=True) - vh * jnp.mean(dvh * vh, axis=-1, keepdims=True))
        dh_ref[:, WIDTH:2 * WIDTH] = dv.astype(BF16)

    col = lambda j: pl.BlockSpec((tb, WIDTH), lambda i: (i, j))
    full = lambda a: pl.BlockSpec(a.shape, lambda i: (0,) * a.ndim)
    acc = lambda shape: pl.BlockSpec(shape, lambda i: (0,) * len(shape))
    return pl.pallas_call(
        body, grid=(t // tb,), name=name,
        in_specs=[col(0), col(1), col(2), col(0), full(ln_g), full(ln_b), full(ws), full(ws_t), full(bs_t)],
        out_specs=[pl.BlockSpec((tb, 3 * WIDTH), lambda i: (i, 0)), acc((A_GROUPS, CHUNK, CHUNK)),
                   acc((CHUNK, LANES)), acc((1, WIDTH)), acc((1, WIDTH))],
        out_shape=[jax.ShapeDtypeStruct((t, F_EVEN), BF16), jax.ShapeDtypeStruct((A_GROUPS, CHUNK, CHUNK), F32),
                   jax.ShapeDtypeStruct((CHUNK, LANES), F32), jax.ShapeDtypeStruct((1, WIDTH), F32),
                   jax.ShapeDtypeStruct((1, WIDTH), F32)],
        scratch_shapes=[pltpu.VMEM((tb, WIDTH), F32)],
        compiler_params=_cp("arbitrary"),
    )(h, h, h, dy, ln_g, ln_b, ws, ws_t, bs_t)


def _halo_prev(tb, j):
    return lambda i: (jnp.maximum(i * (tb // SUBLANES) - 1, 0), j)


def _halo_next(tb, j, t):
    return lambda i: (jnp.minimum((i + 1) * (tb // SUBLANES), t // SUBLANES - 1), j)


def _row_select(parts, width):
    row = _iota((SUBLANES, width), 0)
    out = jnp.zeros((SUBLANES, width), F32)
    for k, p in enumerate(parts):
        out = jnp.where(row == k, p, out)
    return out


def _sconv_fwd(h, w, name):
    t, tb = h.shape[0], 256

    def body(bg_ref, cg_ref, hx_ref, z_ref, cgh_ref, hxh_ref, w_ref, y_ref):
        p = cg_ref[...] * hx_ref[...]
        ph = jnp.where(pl.program_id(0) > 0, cgh_ref[...] * hxh_ref[...], 0.0)
        cv = w_ref[2:3, :] * p + w_ref[1:2, :] * _shift_down(p, ph, 1) + w_ref[0:1, :] * _shift_down(p, ph, 2)
        y_ref[...] = (_silu(z_ref[...]) * (bg_ref[...] * cv)).astype(BF16)

    col = lambda j: pl.BlockSpec((tb, WIDTH), lambda i: (i, j))
    halo = lambda j: pl.BlockSpec((SUBLANES, WIDTH), _halo_prev(tb, j))
    return pl.pallas_call(
        body, grid=(t // tb,), name=name,
        in_specs=[col(0), col(1), col(2), col(3), halo(1), halo(2), pl.BlockSpec(w.shape, lambda i: (0, 0))],
        out_specs=col(0), out_shape=jax.ShapeDtypeStruct((t, 2 * WIDTH), BF16),
        compiler_params=_cp("parallel"),
    )(h, h, h, h, h, h, w)


def _sconv_bwd(h, dy, w, name):
    t, tb = h.shape[0], 256
    nb = t // tb

    def body(bg_ref, cg_ref, hx_ref, z_ref, dy_ref, cgh_ref, hxh_ref, bgn_ref, zn_ref, dyn_ref, w_ref, dh_ref, dw_ref):
        i = pl.program_id(0)
        bg, cg, hx, z, dyv = bg_ref[...], cg_ref[...], hx_ref[...], z_ref[...], dy_ref[...]
        p = cg * hx
        ph = jnp.where(i > 0, cgh_ref[...] * hxh_ref[...], 0.0)
        p1, p2 = _shift_down(p, ph, 1), _shift_down(p, ph, 2)
        cv = w_ref[2:3, :] * p + w_ref[1:2, :] * p1 + w_ref[0:1, :] * p2
        sz = _silu(z)
        dcv = dyv * sz * bg
        dcvn = jnp.where(i < nb - 1, dyn_ref[...] * _silu(zn_ref[...]) * bgn_ref[...], 0.0)
        dp = w_ref[2:3, :] * dcv + w_ref[1:2, :] * _shift_up(dcv, dcvn, 1) + w_ref[0:1, :] * _shift_up(dcv, dcvn, 2)
        dh_ref[0] = (dyv * sz * cv).astype(BF16)
        dh_ref[1] = (dp * hx).astype(BF16)
        dh_ref[2] = (dp * cg).astype(BF16)
        dh_ref[3] = (dyv * (bg * cv) * _dsilu(z)).astype(BF16)
        part = _row_select([jnp.sum(dcv * q, axis=0, keepdims=True) for q in (p2, p1, p)], WIDTH)

        @pl.when(i == 0)
        def _():
            dw_ref[...] = part

        @pl.when(i > 0)
        def _():
            dw_ref[...] += part

    col = lambda j: pl.BlockSpec((tb, WIDTH), lambda i: (i, j))
    prev = lambda j: pl.BlockSpec((SUBLANES, WIDTH), _halo_prev(tb, j))
    nxt = lambda j: pl.BlockSpec((SUBLANES, WIDTH), _halo_next(tb, j, t))
    return pl.pallas_call(
        body, grid=(nb,), name=name,
        in_specs=[col(0), col(1), col(2), col(3), col(0), prev(1), prev(2), nxt(0), nxt(3), nxt(0),
                  pl.BlockSpec(w.shape, lambda i: (0, 0))],
        out_specs=[pl.BlockSpec((4, tb, WIDTH), lambda i: (0, i, 0)), pl.BlockSpec((SUBLANES, WIDTH), lambda i: (0, 0))],
        out_shape=[jax.ShapeDtypeStruct((8, t, WIDTH), BF16), jax.ShapeDtypeStruct((SUBLANES, WIDTH), F32)],
        compiler_params=_cp("arbitrary"),
    )(h, h, h, h, dy, h, h, h, h, dy, w)


XBC_COL0 = 4 * WIDTH


def _ssd_conv_fwd(h, w, b, name):
    t, tb = h.shape[0], 256

    def body(x_ref, xh_ref, w_ref, b_ref, o_ref):
        xv = x_ref[...]
        xh = jnp.where(pl.program_id(0) > 0, xh_ref[...], 0.0)
        acc = b_ref[...] + w_ref[3:4, :] * xv
        for j in range(1, B_CONV):
            acc = acc + w_ref[B_CONV - 1 - j:B_CONV - j, :] * _shift_down(xv, xh, j)
        o_ref[...] = acc

    cb = XBC_COL0 // B_XBC
    return pl.pallas_call(
        body, grid=(t // tb,), name=name,
        in_specs=[pl.BlockSpec((tb, B_XBC), lambda i: (i, cb)), pl.BlockSpec((SUBLANES, B_XBC), _halo_prev(tb, cb)),
                  pl.BlockSpec(w.shape, lambda i: (0, 0)), pl.BlockSpec(b.shape, lambda i: (0, 0))],
        out_specs=pl.BlockSpec((tb, B_XBC), lambda i: (i, 0)), out_shape=jax.ShapeDtypeStruct((t, B_XBC), F32),
        compiler_params=_cp("parallel"),
    )(h, h, w, b)


def _ssd_conv_bwd(h, dh, dpx, dpb, dpc, w, name):
    t, tb, tc = h.shape[0], 512, 1024
    nb = t // tb
    r8 = tb // SUBLANES
    last8 = t // SUBLANES - 1

    def body(dpx_ref, dpb_ref, dpc_ref, nx_ref, nb_ref, nc_ref, x_ref, xh_ref, w_ref, dh_in, dh_ref, dw_ref, db_ref):
        j, i = pl.program_id(0), pl.program_id(1)
        pick = lambda a, b_, c: jnp.where(j < 2, a[...], jnp.where(j == 2, b_[...], c[...]))
        dp = pick(dpx_ref, dpb_ref, dpc_ref)
        dn = jnp.where(i < nb - 1, pick(nx_ref, nb_ref, nc_ref), 0.0)
        xv = x_ref[...]
        xh = jnp.where(i > 0, xh_ref[...], 0.0)
        dx = w_ref[3:4, :] * dp
        for s in range(1, B_CONV):
            dx = dx + w_ref[B_CONV - 1 - s:B_CONV - s, :] * _shift_up(dp, dn, s)
        dh_ref[...] = dx.astype(BF16)
        wpart = _row_select([jnp.sum(dp * _shift_down(xv, xh, B_CONV - 1 - k), axis=0, keepdims=True)
                             for k in range(B_CONV)], tc)
        bpart = jnp.sum(dp, axis=0, keepdims=True)

        @pl.when(i == 0)
        def _():
            dw_ref[...] = wpart
            db_ref[...] = bpart

        @pl.when(i > 0)
        def _():
            dw_ref[...] += wpart
            db_ref[...] += bpart

    def src(blk_rows, rowf, sel, colf):
        return pl.BlockSpec((blk_rows, tc), lambda j, i: (jnp.where(sel(j), rowf(i), 0), colf(j)))

    cur = lambda i: i
    nxt = lambda i: jnp.minimum((i + 1) * r8, last8)
    is_x, is_b, is_c = (lambda j: j < 2), (lambda j: j == 2), (lambda j: j == 3)
    xcol, zero = (lambda j: jnp.minimum(j, 1)), (lambda j: 0)
    c0 = XBC_COL0 // tc
    return pl.pallas_call(
        body, grid=(B_XBC // tc, nb), name=name,
        in_specs=[src(tb, cur, is_x, xcol), src(tb, cur, is_b, zero), src(tb, cur, is_c, zero),
                  src(SUBLANES, nxt, is_x, xcol), src(SUBLANES, nxt, is_b, zero), src(SUBLANES, nxt, is_c, zero),
                  pl.BlockSpec((tb, tc), lambda j, i: (i, c0 + j)),
                  pl.BlockSpec((SUBLANES, tc), lambda j, i: (jnp.maximum(i * r8 - 1, 0), c0 + j)),
                  pl.BlockSpec((B_CONV, tc), lambda j, i: (0, j)), ANY],
        out_specs=[pl.BlockSpec((tb, tc), lambda j, i: (i, c0 + j)), pl.BlockSpec((SUBLANES, tc), lambda j, i: (0, j)),
                   pl.BlockSpec((1, tc), lambda j, i: (0, j))],
        out_shape=[jax.ShapeDtypeStruct(dh.shape, dh.dtype), jax.ShapeDtypeStruct((SUBLANES, B_XBC), F32),
                   jax.ShapeDtypeStruct((1, B_XBC), F32)],
        input_output_aliases={9: 0},
        compiler_params=_cp("arbitrary", "arbitrary"),
    )(dpx, dpb, dpc, dpx, dpb, dpc, h, h, w, dh)


HEADS_PER_GROUP = B_HEADS // B_GROUPS


def _ssd_dt(hdt, alog, dtb, name):
    t = hdt.shape[0]
    nc = t // CHUNK

    def body(dtr_ref, alog_ref, dtb_ref, dt_ref, acs_ref, acst_ref):
        dt = _softplus(dtr_ref[...] + dtb_ref[...])
        tri = (_iota((CHUNK, CHUNK), 1) <= _iota((CHUNK, CHUNK), 0)).astype(BF16)
        acs = _sel_dot(tri, dt * -jnp.exp(alog_ref[...]))
        dt_ref[...] = dt
        acs_ref[...] = acs
        acst_ref[...] = acs.T

    tok = pl.BlockSpec((CHUNK, LANES), lambda c: (c, 0))
    vec = pl.BlockSpec((1, LANES), lambda c: (0, 0))
    return pl.pallas_call(
        body, grid=(nc,), name=name, in_specs=[tok, vec, vec],
        out_specs=[tok, tok, pl.BlockSpec((None, LANES, CHUNK), lambda c: (c, 0, 0))],
        out_shape=[jax.ShapeDtypeStruct((t, LANES), F32), jax.ShapeDtypeStruct((t, LANES), F32),
                   jax.ShapeDtypeStruct((nc, LANES, CHUNK), F32)],
        compiler_params=_cp("parallel"),
    )(hdt, alog, dtb)


def _to_group(m, g):
    return pltpu.roll(m, (LANES - HEADS_PER_GROUP * g) % LANES, 1)


def _from_group(m, g):
    return pltpu.roll(m, HEADS_PER_GROUP * g, 1)


def _ssd_group_terms(g, px, pb, pc, dt, acs, acst_ref):
    head = _iota((CHUNK, B_GROUP_W), 1) // B_HEAD_DIM

    def spread(m4):
        out = m4[:, HEADS_PER_GROUP - 1:HEADS_PER_GROUP]
        for j in range(HEADS_PER_GROUP - 2, -1, -1):
            out = jnp.where(head == j, m4[:, j:j + 1], out)
        return out

    dt4, a4 = _to_group(dt, g), _to_group(acs, g)
    rows = [acst_ref[pl.ds(HEADS_PER_GROUP * g + j, 1), :] for j in range(HEADS_PER_GROUP)]
    return dict(xs=_silu(px), bm=_silu(pb), cm=_silu(pc), dt4=dt4, a4=a4, rows=rows, dt_e=spread(dt4), a_e=spread(a4))


def _ssd_decay(tm, j, transposed):
    col, row = tm["a4"][:, j:j + 1], tm["rows"][j]
    lane, sub = _iota((CHUNK, CHUNK), 1), _iota((CHUNK, CHUNK), 0)
    if transposed:
        return jnp.where(sub <= lane, jnp.exp(jnp.minimum(row - col, 0.0)), 0.0)
    return jnp.where(lane <= sub, jnp.exp(jnp.minimum(col - row, 0.0)), 0.0)


GROUPS_PER_STEP = 4


def _ssd_specs(nc, rev):
    ch = (lambda c: nc - 1 - c) if rev else (lambda c: c)
    n = GROUPS_PER_STEP
    gw = lambda off: pl.BlockSpec((CHUNK, n * B_GROUP_W), lambda c, g: (ch(c), off // n + g))
    gn = lambda off: pl.BlockSpec((CHUNK, n * B_STATE), lambda c, g: (ch(c), off // n + g))
    tok = pl.BlockSpec((CHUNK, LANES), lambda c, g: (ch(c), 0))
    vec = pl.BlockSpec((1, LANES), lambda c, g: (0, 0))
    gvec = pl.BlockSpec((1, n * B_GROUP_W), lambda c, g: (0, g))
    st = pl.BlockSpec((None, B_STATE, n * B_GROUP_W), lambda c, g: (ch(c), 0, g))
    tokt = pl.BlockSpec((None, LANES, CHUNK), lambda c, g: (ch(c), 0, 0))
    return gw, gn, tok, tokt, vec, gvec, st


def _group_cols(u):
    return slice(u * B_GROUP_W, (u + 1) * B_GROUP_W), slice(u * B_STATE, (u + 1) * B_STATE)


def _ssd_scan_fwd(pre, dt, acs, acst, h, y, dfull, ng, carry, name):
    t = pre.shape[0]
    nc = t // CHUNK
    n_carried = len(carry.arrays) if carry else 0

    def body(*refs):
        px_ref, pb_ref, pc_ref, dt_ref, acs_ref, acst_ref, z_ref, df_ref, ng_ref, _ = refs[:10]
        refs = refs[10:]
        c_in, refs = refs[:n_carried], refs[n_carried:]
        y_ref, ypre_ref, st_ref = refs[:3]
        c_out, state_ref, c_sems = refs[3:3 + n_carried], refs[3 + n_carried], refs[4 + n_carried:]
        c, pair = pl.program_id(0), pl.program_id(1)
        if carry:
            pl.when((c == 0) & (pair == 0))(lambda: carry.start(c_in, c_out, c_sems))
        for u in range(GROUPS_PER_STEP):
            g = GROUPS_PER_STEP * pair + u
            wide, narrow = _group_cols(u)

            @pl.when(c == 0)
            def _():
                state_ref[g] = jnp.zeros((B_STATE, B_GROUP_W), F32)

            tm = _ssd_group_terms(g, px_ref[:, wide], pb_ref[:, narrow], pc_ref[:, narrow], dt_ref[...],
                                  acs_ref[...], acst_ref)
            xs, bm, cm, a_e = tm["xs"], tm["bm"], tm["cm"], tm["a_e"]
            xdt = xs * tm["dt_e"]
            cb = _dot_nt(cm, bm)
            head = _iota((CHUNK, B_GROUP_W), 1) // B_HEAD_DIM
            yd = jnp.zeros((CHUNK, B_GROUP_W), F32)
            for j in range(HEADS_PER_GROUP):
                yd = jnp.where(head == j, _dot(cb * _ssd_decay(tm, j, False), xdt), yd)
            st = state_ref[g]
            st_ref[:, wide] = st
            yv = yd + jnp.exp(a_e) * _dot(cm, st) + df_ref[:, wide] * xs
            a_last = a_e[CHUNK - 1:CHUNK, :]
            state_ref[g] = st * jnp.exp(a_last) + _dot_tn(bm, xdt * jnp.exp(a_last - a_e))
            ypre_ref[:, wide] = yv
            yz = yv * _silu(z_ref[:, wide])
            r = lax.rsqrt(jnp.mean(yz * yz, axis=-1, keepdims=True) + EPS)
            y_ref[:, wide] = (yz * r * ng_ref[:, wide]).astype(BF16)
        if carry:
            last = (c == nc - 1) & (pair == B_GROUPS // GROUPS_PER_STEP - 1)
            pl.when(last)(lambda: carry.finish(c_in, c_out, c_sems))

    gw, gn, tok, tokt, vec, gvec, st = _ssd_specs(nc, False)
    out = pl.pallas_call(
        body, grid=(nc, B_GROUPS // GROUPS_PER_STEP), name=name,
        in_specs=[gw(0), gn(WIDTH // B_STATE), gn((WIDTH + B_GROUPS * B_STATE) // B_STATE), tok, tok, tokt,
                  gw(3 * WIDTH // B_GROUP_W), gvec, gvec, ANY] + [ANY] * n_carried,
        out_specs=[gw(WIDTH // B_GROUP_W), gw(0), st] + [ANY] * n_carried,
        out_shape=[jax.ShapeDtypeStruct(y.shape, y.dtype), jax.ShapeDtypeStruct((t, WIDTH), F32),
                   jax.ShapeDtypeStruct((nc, B_STATE, WIDTH), F32)] + (carry.out_shape if carry else []),
        scratch_shapes=[pltpu.VMEM((B_GROUPS, B_STATE, B_GROUP_W), F32)] + (carry.scratch if carry else []),
        input_output_aliases={9: 0},
        compiler_params=_cp("arbitrary", "arbitrary"),
    )(pre, pre, pre, dt, acs, acst, h, dfull, ng, y, *(carry.arrays if carry else []))
    return out[:3], out[3:]


def _ssd_scan_bwd(pre, hdt, dt, acs, acst, h, dy, ypre, states, dh, alog, dtb, dfull, ng, carry, name):
    t = pre.shape[0]
    nc = t // CHUNK

    n_carried = len(carry.arrays) if carry else 0

    def one_group(c, g, px_ref, pb_ref, pc_ref, dtr_ref, dt_ref, acs_ref, acst_ref, z_ref, dy_ref, ypre_ref, st_ref,
                  alog_ref, dtb_ref, df_ref, ng_ref, dz_ref, dpx_ref, dpb_ref, dpc_ref, dd_ref, dng_ref, dstate_ref):
        @pl.when(c == 0)
        def _():
            dstate_ref[g] = jnp.zeros((B_STATE, B_GROUP_W), F32)
            dd_ref[g] = jnp.zeros((1, B_GROUP_W), F32)
            dng_ref[g] = jnp.zeros((1, B_GROUP_W), F32)

        px, pb, pc, dtr = px_ref[...], pb_ref[...], pc_ref[...], dtr_ref[...]
        tm = _ssd_group_terms(g, px, pb, pc, dt_ref[...], acs_ref[...], acst_ref)
        xs, bm, cm, a_e, dt_e = tm["xs"], tm["bm"], tm["cm"], tm["a_e"], tm["dt_e"]
        xdt = xs * dt_e
        head = _iota((CHUNK, B_GROUP_W), 1) // B_HEAD_DIM

        z, yv, ngv = z_ref[...], ypre_ref[...], ng_ref[...]
        sz = _silu(z)
        yz = yv * sz
        r = lax.rsqrt(jnp.mean(yz * yz, axis=-1, keepdims=True) + EPS)
        dyn = dy_ref[...]
        dng_ref[g] += jnp.sum(dyn * yz * r, axis=0, keepdims=True)
        q = dyn * ngv
        dyz = r * q - yz * (r * r * r) * jnp.mean(q * yz, axis=-1, keepdims=True)
        dyv = dyz * sz
        dz_ref[...] = (dyz * yv * _dsilu(z)).astype(BF16)
        dd_ref[g] += jnp.sum(dyv * xs, axis=0, keepdims=True)
        dxs = df_ref[...] * dyv

        st = st_ref[...]
        ea = jnp.exp(a_e)
        ead = ea * dyv
        dcm = _dot_nt(ead, st)
        da_e = dyv * (ea * _dot(cm, st))

        dsn = dstate_ref[g]
        a_last = a_e[CHUNK - 1:CHUNK, :]
        ea_last = jnp.exp(a_last)
        dstate_ref[g] = dsn * ea_last + _dot_tn(cm, ead)
        wdec = jnp.exp(a_last - a_e)
        xw = xdt * wdec
        dxw = _dot(bm, dsn)
        dxdt = dxw * wdec
        dbm = _dot_nt(xw, dsn)
        zc = dxw * xw
        da_last = jnp.sum(zc, axis=0, keepdims=True) + jnp.sum(dsn * st, axis=0, keepdims=True) * ea_last
        da_e = da_e - zc + jnp.where(_iota((CHUNK, B_GROUP_W), 0) == CHUNK - 1, da_last, 0.0)

        cb, cbt = _dot_nt(cm, bm), _dot_nt(bm, cm)
        dcb, dcbt = jnp.zeros((CHUNK, CHUNK), F32), jnp.zeros((CHUNK, CHUNK), F32)
        da4 = jnp.zeros((CHUNK, LANES), F32)
        lane = _iota((CHUNK, LANES), 1)
        for j in range(HEADS_PER_GROUP):
            mine = head == j
            gm = _dot_nt(jnp.where(mine, dyv, 0.0), xdt)
            gmt = _dot_nt(jnp.where(mine, xdt, 0.0), dyv)
            dec, dect = _ssd_decay(tm, j, False), _ssd_decay(tm, j, True)
            dcb += gm * dec
            dcbt += gmt * dect
            da_j = (jnp.sum(gm * cb * dec, axis=1, keepdims=True) - jnp.sum(gmt * cbt * dect, axis=1, keepdims=True))
            da4 = jnp.where(lane == j, da_j, da4)
            dxdt = dxdt + jnp.where(mine, _dot(cbt * dect, dyv), 0.0)
        dcm = dcm + _dot(dcb, bm)
        dbm = dbm + _dot(dcbt, cm)

        gather = (_iota((B_GROUP_W, LANES), 0) // B_HEAD_DIM == _iota((B_GROUP_W, LANES), 1)).astype(BF16)
        per_head = _dot_sel(jnp.concatenate([da_e, dxdt * xs], axis=0), gather)
        da4 = da4 + per_head[:CHUNK]
        rtri = (_iota((CHUNK, CHUNK), 1) >= _iota((CHUNK, CHUNK), 0)).astype(BF16)
        dadt4 = _sel_dot(rtri, da4)
        a_heads = -jnp.exp(alog_ref[...])
        rows8 = lambda v: jnp.broadcast_to(v, (SUBLANES, LANES))
        ddt4 = dadt4 * _to_group(rows8(a_heads), g)[0:1, :] + per_head[CHUNK:]
        dxs = dxs + dxdt * dt_e
        ddt = _from_group(ddt4, g) * _sig(dtr + dtb_ref[...])
        da_heads = _from_group(rows8(jnp.sum(dadt4 * tm["dt4"], axis=0, keepdims=True)), g)[0:1, :]

        dpx_ref[...] = dxs * _dsilu(px)
        dpb_ref[...] = dbm * _dsilu(pb)
        dpc_ref[...] = dcm * _dsilu(pc)
        return ddt, da_heads * a_heads

    def body(*refs):
        (px_ref, pb_ref, pc_ref, dtr_ref, dt_ref, acs_ref, acst_ref, z_ref, dy_ref, ypre_ref, st_ref, alog_ref,
         dtb_ref, df_ref, ng_ref, _) = refs[:16]
        refs = refs[16:]
        c_in, refs = refs[:n_carried], refs[n_carried:]
        dz_ref, dpx_ref, dpb_ref, dpc_ref, ddt_ref, dbias_ref, dalog_ref, dd_ref, dng_ref = refs[:9]
        c_out, dstate_ref, c_sems = refs[9:9 + n_carried], refs[9 + n_carried], refs[10 + n_carried:]
        c, pair = pl.program_id(0), pl.program_id(1)
        if carry:
            pl.when((c == 0) & (pair == 0))(lambda: carry.start(c_in, c_out, c_sems))
        ddt, dalog = 0.0, 0.0
        for u in range(GROUPS_PER_STEP):
            wide, narrow = _group_cols(u)
            view = lambda ref, cols: ref.at[:, cols]
            ddt_u, dalog_u = one_group(
                c, GROUPS_PER_STEP * pair + u, view(px_ref, wide), view(pb_ref, narrow), view(pc_ref, narrow), dtr_ref,
                dt_ref, acs_ref, acst_ref, view(z_ref, wide), view(dy_ref, wide), view(ypre_ref, wide), view(st_ref, wide),
                alog_ref, dtb_ref, view(df_ref, wide), view(ng_ref, wide), view(dz_ref, wide), view(dpx_ref, wide),
                view(dpb_ref, narrow), view(dpc_ref, narrow), dd_ref, dng_ref, dstate_ref)
            ddt, dalog = ddt + ddt_u, dalog + dalog_u
        first = (c == 0) & (pair == 0)
        bias_part = jnp.sum(ddt, axis=0, keepdims=True)

        @pl.when(pair == 0)
        def _():
            ddt_ref[...] = ddt

        @pl.when(pair > 0)
        def _():
            ddt_ref[...] += ddt

        @pl.when(first)
        def _():
            dbias_ref[...] = bias_part
            dalog_ref[...] = dalog

        @pl.when(jnp.logical_not(first))
        def _():
            dbias_ref[...] += bias_part
            dalog_ref[...] += dalog

        if carry:
            last = (c == nc - 1) & (pair == B_GROUPS // GROUPS_PER_STEP - 1)
            pl.when(last)(lambda: carry.finish(c_in, c_out, c_sems))

    gw, gn, tok, tokt, vec, gvec, st = _ssd_specs(nc, True)
    acc = lambda shape: pl.BlockSpec(shape, lambda c, g: (0,) * len(shape))
    rev = lambda c: nc - 1 - c
    out = pl.pallas_call(
        body, grid=(nc, B_GROUPS // GROUPS_PER_STEP), name=name,
        in_specs=[gw(0), gn(WIDTH // B_STATE), gn((WIDTH + B_GROUPS * B_STATE) // B_STATE), tok, tok, tok, tokt,
                  gw(3 * WIDTH // B_GROUP_W), gw(WIDTH // B_GROUP_W), gw(0), st, vec, vec, gvec, gvec, ANY]
        + [ANY] * n_carried,
        out_specs=[gw(3 * WIDTH // B_GROUP_W), gw(0), gn(0), gn(0), tok, vec, vec, acc((B_GROUPS, 1, B_GROUP_W)), acc((B_GROUPS, 1, B_GROUP_W))] + [ANY] * n_carried,
        out_shape=[jax.ShapeDtypeStruct(dh.shape, dh.dtype), jax.ShapeDtypeStruct((t, WIDTH), F32),
                   jax.ShapeDtypeStruct((t, B_GROUPS * B_STATE), F32), jax.ShapeDtypeStruct((t, B_GROUPS * B_STATE), F32),
                   jax.ShapeDtypeStruct((t, LANES), F32), jax.ShapeDtypeStruct((1, LANES), F32),
                   jax.ShapeDtypeStruct((1, LANES), F32), jax.ShapeDtypeStruct((B_GROUPS, 1, B_GROUP_W), F32),
                   jax.ShapeDtypeStruct((B_GROUPS, 1, B_GROUP_W), F32)] + (carry.out_shape if carry else []),
        scratch_shapes=[pltpu.VMEM((B_GROUPS, B_STATE, B_GROUP_W), F32)] + (carry.scratch if carry else []),
        input_output_aliases={15: 0},
        compiler_params=_cp("arbitrary", "arbitrary"),
    )(pre, pre, pre, hdt, dt, acs, acst, h, dy, ypre, states, alog, dtb, dfull, ng, dh,
      *(carry.arrays if carry else []))
    return out[:9], out[9:]


Q_COL, K_COL, V_COL, Z_COL = [(4 + i) * WIDTH // D_HEAD_DIM for i in range(4)]
ATT_SCALE = D_HEAD_DIM ** -0.5


SPAN = 2048


def _att_blocks():
    out = []
    for _, dil in D_PATTERNS:
        nbl = SPAN // (CHUNK * dil)
        for r in range(dil):
            for bl in range(nbl):
                st = r + dil * CHUNK * bl
                out.append((dil, st, bl > 0, st - dil * CHUNK if bl > 0 else r + dil * CHUNK * (nbl - 1)))
    return out


def _rows(start, dil):
    return pl.ds(start, CHUNK) if dil == 1 else pl.ds(start, CHUNK, stride=dil)


def _att_keys(kc_ref, kp_ref, blk):
    dil, st, inside, pst = blk
    prev = (kc_ref if inside else kp_ref)[_rows(pst, dil), :]
    return jnp.concatenate([prev, kc_ref[_rows(st, dil), :]], axis=0).astype(BF16)


def _att_band(span_index):
    lane, sub = _iota((CHUNK, 2 * CHUNK), 1), _iota((CHUNK, 2 * CHUNK), 0)
    band = (lane >= sub) & (lane <= sub + CHUNK)
    return band, band & ((lane >= CHUNK) | (span_index > 0))


def _att_specs(t):
    blk = lambda off: pl.BlockSpec((SPAN, D_HEAD_DIM), lambda hd, sb: (sb, off + hd))
    prev = lambda off: pl.BlockSpec((SPAN, D_HEAD_DIM), lambda hd, sb: (jnp.maximum(sb - 1, 0), off + hd))
    lse = pl.BlockSpec((None, SPAN // CHUNK, CHUNK), lambda hd, sb: (hd, sb, 0))
    return blk, prev, lse


def _attn2_fwd(h, y, name):
    t = h.shape[0]

    def body(q_ref, kc_ref, vc_ref, kp_ref, vp_ref, z_ref, y_in, y_ref, o_ref, lse_ref, m_ref, l_ref):
        band, band_first = _att_band(pl.program_id(1))
        for blk in _att_blocks():
            dil, st, inside, _ = blk
            rows = _rows(st, dil)
            k2, v2 = _att_keys(kc_ref, kp_ref, blk), _att_keys(vc_ref, vp_ref, blk)
            s = jnp.where(band if inside else band_first, _dot_nt(q_ref[rows, :], k2) * ATT_SCALE, NEG)
            m_b = jnp.max(s, axis=1, keepdims=True)
            p = jnp.exp(s - m_b)
            l_b = jnp.sum(p, axis=1, keepdims=True)
            o_b = _dot(p, v2)
            wide = lambda a: jnp.broadcast_to(a, (CHUNK, D_HEAD_DIM))
            if dil == 1:
                m_ref[rows, :], l_ref[rows, :], o_ref[rows, :] = wide(m_b), wide(l_b), o_b
            else:
                m_o = m_ref[rows, :]
                m_n = jnp.maximum(m_o, m_b)
                a_o, a_b = jnp.exp(m_o - m_n), jnp.exp(m_b - m_n)
                m_ref[rows, :] = m_n
                l_ref[rows, :] = a_o * l_ref[rows, :] + a_b * l_b
                o_ref[rows, :] = a_o * o_ref[rows, :] + a_b * o_b
        l = l_ref[...]
        o = o_ref[...] / l
        o_ref[...] = o
        y_ref[...] = (_silu(z_ref[...]) * o).astype(BF16)
        for i in range(SPAN // CHUNK):
            blk_rows = slice(i * CHUNK, (i + 1) * CHUNK)
            lse_ref[i:i + 1, :] = (m_ref[blk_rows, :] + jnp.log(l[blk_rows, :])).T[0:1, :]

    blk, prev, lse_spec = _att_specs(t)
    return pl.pallas_call(
        body, grid=(D_HEADS, t // SPAN), name=name,
        in_specs=[blk(Q_COL), blk(K_COL), blk(V_COL), prev(K_COL), prev(V_COL), blk(Z_COL), ANY],
        out_specs=[blk(WIDTH // D_HEAD_DIM), blk(0), lse_spec],
        out_shape=[jax.ShapeDtypeStruct(y.shape, y.dtype), jax.ShapeDtypeStruct((t, WIDTH), F32),
                   jax.ShapeDtypeStruct((D_HEADS, t // CHUNK, CHUNK), F32)],
        scratch_shapes=[pltpu.VMEM((SPAN, D_HEAD_DIM), F32)] * 2,
        input_output_aliases={6: 0},
        compiler_params=_cp("parallel", "parallel"),
    )(h, h, h, h, h, h, y)


def _attn2_bwd(h, dy, o, lse, dh, name):
    t = h.shape[0]
    ns = t // SPAN

    def body(q_ref, kc_ref, vc_ref, kp_ref, vp_ref, z_ref, dy_ref, o_ref, lse_ref, dh_in, dh_ref,
             acc_ref, dq_ref, do_ref, delta_ref, lsec_ref):
        sb = pl.program_id(1)

        @pl.when(sb == 0)
        def _():
            acc_ref[...] = jnp.zeros_like(acc_ref)

        here, before = pl.multiple_of(sb * SPAN, SPAN), jnp.maximum(sb - 1, 0) * SPAN
        z, ov, dyv = z_ref[...], o_ref[...], dy_ref[...]
        do = dyv * _silu(z)
        do_ref[...] = do
        dh_ref[3, pl.ds(here, SPAN), :] = (dyv * ov * _dsilu(z)).astype(BF16)
        delta_ref[...] = jnp.broadcast_to(jnp.sum(do * ov, axis=1, keepdims=True), (SPAN, D_HEAD_DIM))
        for i in range(SPAN // CHUNK):
            lsec_ref[i * CHUNK:(i + 1) * CHUNK, :] = jnp.broadcast_to(lse_ref[i:i + 1, :], (CHUNK, CHUNK)).T
        band, band_first = _att_band(sb)
        for blk in _att_blocks():
            dil, st, inside, pst = blk
            rows = _rows(st, dil)
            q = q_ref[rows, :].astype(BF16)
            k2, v2 = _att_keys(kc_ref, kp_ref, blk), _att_keys(vc_ref, vp_ref, blk)
            dob = do_ref[rows, :].astype(BF16)
            s = jnp.where(band if inside else band_first, _dot_nt(q, k2) * ATT_SCALE, NEG)
            p = jnp.exp(s - lsec_ref[rows, :][:, 0:1])
            ds = p * (_dot_nt(dob, v2) - delta_ref[rows, :][:, 0:1]) * ATT_SCALE
            dq_b = _dot(ds, k2)
            if dil == 1:
                dq_ref[rows, :] = dq_b
            else:
                dq_ref[rows, :] += dq_b
            dk2, dv2 = _dot_tn(ds, q), _dot_tn(p, dob)
            own = _rows(pl.multiple_of(here + st, CHUNK) if dil == 1 else here + st, dil)
            pbase = (here if inside else before) + pst
            prv = _rows(pl.multiple_of(pbase, CHUNK) if dil == 1 else pbase, dil)
            acc_ref[0, own, :] += dk2[CHUNK:]
            acc_ref[1, own, :] += dv2[CHUNK:]
            acc_ref[0, prv, :] += dk2[:CHUNK]
            acc_ref[1, prv, :] += dv2[:CHUNK]
        dh_ref[0, pl.ds(here, SPAN), :] = dq_ref[...].astype(BF16)

        @pl.when(sb == ns - 1)
        def _():
            dh_ref[1] = acc_ref[0].astype(BF16)
            dh_ref[2] = acc_ref[1].astype(BF16)

    blk, prev, lse_spec = _att_specs(t)
    span = lambda: pltpu.VMEM((SPAN, D_HEAD_DIM), F32)
    return pl.pallas_call(
        body, grid=(D_HEADS, ns), name=name,
        in_specs=[blk(Q_COL), blk(K_COL), blk(V_COL), prev(K_COL), prev(V_COL), blk(Z_COL), blk(WIDTH // D_HEAD_DIM),
                  blk(0), lse_spec, ANY],
        out_specs=pl.BlockSpec((4, t, D_HEAD_DIM), lambda hd, sb: (1, 0, hd)),
        out_shape=jax.ShapeDtypeStruct(dh.shape, dh.dtype),
        scratch_shapes=[pltpu.VMEM((2, t, D_HEAD_DIM), F32), span(), span(), span(), span()],
        input_output_aliases={9: 0},
        compiler_params=_cp("arbitrary", "arbitrary"),
    )(h, h, h, h, h, h, dy, o, lse, dh)


def _place():
    x, y, c = lax.axis_index("x"), lax.axis_index("y"), lax.axis_index("c")
    return x, y, c, 4 * x + 2 * y + c


class _Exchange:
    def __init__(self, arrays, out_shape):
        n = len(arrays)
        self.arrays, self.out_shape = list(arrays), out_shape
        self.scratch = [pltpu.SemaphoreType.DMA((n, 7)), pltpu.SemaphoreType.DMA((n, 7)), pltpu.SemaphoreType.DMA((n,))]


class _AllGather(_Exchange):
    def __init__(self, blocks):
        super().__init__(blocks, [jax.ShapeDtypeStruct((N_DEV,) + b.shape, b.dtype) for b in blocks])

    def _plan(self, ins, outs, sems):
        send_sems, recv_sems, local_sems = sems
        x, y, c, me = _place()
        chips = [(1 - x, y), (x, 1 - y), (1 - x, 1 - y)]

        def copy(a, k, block, to, src=None):
            dst = outs[a].at[block]
            return pltpu.make_async_remote_copy(
                src_ref=dst if src is None else src, dst_ref=dst, send_sem=send_sems.at[a, k],
                recv_sem=recv_sems.at[a, k], device_id=to, device_id_type=MESH)

        n = len(ins)
        index = lambda px, py, pc: 4 * px + 2 * py + pc
        mine = [pltpu.make_async_copy(ins[a], outs[a].at[me], local_sems.at[a]) for a in range(n)]
        first = [copy(a, 0, me, (x, y, 1 - c), src=ins[a]) for a in range(n)]
        first += [copy(a, 1 + j, me, (*chip, c), src=ins[a]) for j, chip in enumerate(chips) for a in range(n)]
        arrive = lambda a, k, px, py, pc: copy(a, k, index(px, py, pc), (x, y, c))
        over_ici = [[arrive(a, 1 + j, *chip, c) for a in range(n)] for j, chip in enumerate(chips)]
        passed = [[copy(a, 4 + j, index(*chip, c), (x, y, 1 - c)) for a in range(n)] for j, chip in enumerate(chips)]
        from_sibling = [arrive(a, 0, x, y, 1 - c) for a in range(n)]
        from_sibling += [arrive(a, 4 + j, *chip, 1 - c) for j, chip in enumerate(chips) for a in range(n)]
        return mine, first, over_ici, passed, from_sibling

    def start(self, ins, outs, sems):
        mine, first, _, _, _ = self._plan(ins, outs, sems)
        for cp in mine + first:
            cp.start()

    def finish(self, ins, outs, sems):
        mine, first, over_ici, passed, from_sibling = self._plan(ins, outs, sems)
        for landed, onward in zip(over_ici, passed):
            for cp, fwd in zip(landed, onward):
                cp.wait_recv()
                fwd.start()
        for cp in from_sibling:
            cp.wait_recv()
        for cp in first + [fwd for onward in passed for fwd in onward]:
            cp.wait_send()
        for cp in mine:
            cp.wait()


class _AllToAll(_Exchange):
    def __init__(self, parts):
        super().__init__(parts, [jax.ShapeDtypeStruct(p.shape, p.dtype) for p in parts])

    def _plan(self, ins, outs, sems):
        send_sems, recv_sems, local_sems = sems
        x, y, c, me = _place()
        flip = lambda v, f: 1 - v if f else v
        peers = [(flip(x, fx), flip(y, fy), flip(c, fc)) for fx in (0, 1) for fy in (0, 1) for fc in (0, 1)][1:]

        def copy(a, k, sending):
            px, py, pc = peers[k]
            there = 4 * px + 2 * py + pc
            return pltpu.make_async_remote_copy(
                src_ref=ins[a].at[there], dst_ref=outs[a].at[me if sending else there], send_sem=send_sems.at[a, k],
                recv_sem=recv_sems.at[a, k], device_id=peers[k], device_id_type=MESH)

        n = len(ins)
        local = [pltpu.make_async_copy(ins[a].at[me], outs[a].at[me], local_sems.at[a]) for a in range(n)]
        return local, [[copy(a, k, sending) for k in range(7) for a in range(n)] for sending in (True, False)]

    def start(self, ins, outs, sems):
        local, (sends, _) = self._plan(ins, outs, sems)
        for cp in local + sends:
            cp.start()

    def finish(self, ins, outs, sems):
        local, (_, both_ways) = self._plan(ins, outs, sems)
        for cp in both_ways + local:
            cp.wait()


def _exchange(ex, name):
    n = len(ex.arrays)

    def body(*refs):
        ins, outs, sems = refs[:n], refs[n:2 * n], refs[2 * n:]
        ex.start(ins, outs, sems)
        ex.finish(ins, outs, sems)

    return pl.pallas_call(body, name=name, in_specs=[ANY] * n, out_specs=[ANY] * n, out_shape=ex.out_shape,
                          scratch_shapes=ex.scratch)(*ex.arrays)


def _sum_parts(parts, tc, name):
    _, r, c = parts.shape

    def body(p_ref, o_ref):
        acc = p_ref[0].astype(F32)
        for d in range(1, N_DEV):
            acc = acc + p_ref[d].astype(F32)
        o_ref[...] = acc

    return pl.pallas_call(
        body, grid=(c // tc,), name=name,
        in_specs=[pl.BlockSpec((N_DEV, r, tc), lambda j: (0, 0, j))], out_specs=pl.BlockSpec((r, tc), lambda j: (0, j)),
        out_shape=jax.ShapeDtypeStruct((r, c), F32), compiler_params=_cp("parallel"),
    )(parts)


def _adamw(w, g, m, v, tr, name):
    r, c = w.shape

    def body(w_ref, g_ref, m_ref, v_ref, d_ref, m2_ref, v2_ref):
        gv = g_ref[...]
        m2 = ADAM_B1 * m_ref[...] + (1.0 - ADAM_B1) * gv
        v2 = ADAM_B2 * v_ref[...] + (1.0 - ADAM_B2) * (gv * gv)
        m_hat = m2 / (1.0 - ADAM_B1 ** ADAM_STEP)
        v_hat = v2 / (1.0 - ADAM_B2 ** ADAM_STEP)
        d_ref[...] = -ADAM_LR * (m_hat / (jnp.sqrt(v_hat) + ADAM_EPS) + ADAM_WD * w_ref[...])
        m2_ref[...] = m2
        v2_ref[...] = v2

    spec = pl.BlockSpec((tr, c), lambda i: (i, 0))
    return pl.pallas_call(
        body, grid=(r // tr,), name=name, in_specs=[spec] * 4, out_specs=[spec] * 3,
        out_shape=[jax.ShapeDtypeStruct((r, c), F32)] * 3, compiler_params=_cp("parallel"),
    )(w, g, m, v)


PACK_ROWS = SUBLANES * LANES


def _pack(arrays):
    flat = [jnp.pad(a.reshape(-1), (0, -a.size % PACK_ROWS)) for a in arrays]
    return jnp.concatenate(flat).reshape(-1, LANES)


def _unpack(packed, shapes):
    flat, out, pos = packed.reshape(-1), [], 0
    for s in shapes:
        size = 1
        for d in s:
            size *= d
        out.append(flat[pos:pos + size].reshape(s))
        pos += size + (-size % PACK_ROWS)
    return out


SMALL = ["even_norm_g", "gmlp_ln_g", "gmlp_ln_b", "gmlp_ws", "gmlp_bs", "ssd_conv_w", "ssd_conv_b", "ssd_dt_bias",
         "ssd_a_log", "ssd_d", "ssd_norm_g", "odd_norm_g", "sconv_w", "final_norm_g"]
ORDER = ["even_norm_g", "even_w_in", "gmlp_ln_g", "gmlp_ln_b", "gmlp_ws", "gmlp_bs", "ssd_conv_w", "ssd_conv_b",
         "ssd_dt_bias", "ssd_a_log", "ssd_d", "ssd_norm_g", "even_w_out", "odd_norm_g", "odd_w_in", "sconv_w",
         "odd_w_out", "final_norm_g"]


def kernel(x, even_norm_g, even_w_in, gmlp_ln_g, gmlp_ln_b, gmlp_ws, gmlp_bs, ssd_conv_w, ssd_conv_b, ssd_dt_bias, ssd_a_log, ssd_d, ssd_norm_g, even_w_out, odd_norm_g, odd_w_in, sconv_w, odd_w_out, final_norm_g, loss_target, m_even_norm_g, m_even_w_in, m_gmlp_ln_g, m_gmlp_ln_b, m_gmlp_ws, m_gmlp_bs, m_ssd_conv_w, m_ssd_conv_b, m_ssd_dt_bias, m_ssd_a_log, m_ssd_d, m_ssd_norm_g, m_even_w_out, m_odd_norm_g, m_odd_w_in, m_sconv_w, m_odd_w_out, m_final_norm_g, v_even_norm_g, v_even_w_in, v_gmlp_ln_g, v_gmlp_ln_b, v_gmlp_ws, v_gmlp_bs, v_ssd_conv_w, v_ssd_conv_b, v_ssd_dt_bias, v_ssd_a_log, v_ssd_d, v_ssd_norm_g, v_even_w_out, v_odd_norm_g, v_odd_w_in, v_sconv_w, v_odd_w_out, v_final_norm_g):
    w = dict(even_norm_g=even_norm_g, even_w_in=even_w_in, gmlp_ln_g=gmlp_ln_g, gmlp_ln_b=gmlp_ln_b, gmlp_ws=gmlp_ws,
             gmlp_bs=gmlp_bs, ssd_conv_w=ssd_conv_w, ssd_conv_b=ssd_conv_b, ssd_dt_bias=ssd_dt_bias,
             ssd_a_log=ssd_a_log, ssd_d=ssd_d, ssd_norm_g=ssd_norm_g, even_w_out=even_w_out, odd_norm_g=odd_norm_g,
             odd_w_in=odd_w_in, sconv_w=sconv_w, odd_w_out=odd_w_out, final_norm_g=final_norm_g)
    m1 = dict(even_norm_g=m_even_norm_g, even_w_in=m_even_w_in, gmlp_ln_g=m_gmlp_ln_g, gmlp_ln_b=m_gmlp_ln_b,
              gmlp_ws=m_gmlp_ws, gmlp_bs=m_gmlp_bs, ssd_conv_w=m_ssd_conv_w, ssd_conv_b=m_ssd_conv_b,
              ssd_dt_bias=m_ssd_dt_bias, ssd_a_log=m_ssd_a_log, ssd_d=m_ssd_d, ssd_norm_g=m_ssd_norm_g,
              even_w_out=m_even_w_out, odd_norm_g=m_odd_norm_g, odd_w_in=m_odd_w_in, sconv_w=m_sconv_w,
              odd_w_out=m_odd_w_out, final_norm_g=m_final_norm_g)
    m2 = dict(even_norm_g=v_even_norm_g, even_w_in=v_even_w_in, gmlp_ln_g=v_gmlp_ln_g, gmlp_ln_b=v_gmlp_ln_b,
              gmlp_ws=v_gmlp_ws, gmlp_bs=v_gmlp_bs, ssd_conv_w=v_ssd_conv_w, ssd_conv_b=v_ssd_conv_b,
              ssd_dt_bias=v_ssd_dt_bias, ssd_a_log=v_ssd_a_log, ssd_d=v_ssd_d, ssd_norm_g=v_ssd_norm_g,
              even_w_out=v_even_w_out, odd_norm_g=v_odd_norm_g, odd_w_in=v_odd_w_in, sconv_w=v_sconv_w,
              odd_w_out=v_odd_w_out, final_norm_g=v_final_norm_g)
    _, _, _, me = _place()
    xs = x[0]
    shard = WIDTH // N_DEV

    small_blk = jnp.concatenate([
        ssd_conv_w[0], jnp.pad(sconv_w[0], ((0, 0), (0, shard))), jnp.pad(odd_norm_g, ((0, 0), (0, shard)))], axis=0)
    g_wte, g_small = _exchange(_AllGather([even_w_in[0].T.astype(BF16), small_blk]), "gather_even_w_in")
    out_weights = _AllGather([even_w_out[0].astype(BF16), odd_w_out[0].astype(BF16)])
    wte = g_wte.reshape(F_EVEN_ALL, D_MODEL)
    wte_dt = jnp.pad(wte[F_EVEN:], ((0, LANES - B_HEADS), (0, 0)))
    conv_w = g_small[:, 0:B_CONV, :].transpose(1, 0, 2).reshape(B_CONV, B_XBC)
    sconv_full = g_small[:, B_CONV:B_CONV + C_CONV, :shard].transpose(1, 0, 2).reshape(C_CONV, WIDTH)
    odd_g = g_small[:, B_CONV + C_CONV, :shard].reshape(1, WIDTH)

    pad_heads = lambda a: jnp.pad(a, ((0, 0), (0, LANES - B_HEADS)))
    alog, dtb = pad_heads(ssd_a_log), pad_heads(ssd_dt_bias)
    d_full = jnp.repeat(ssd_d, B_HEAD_DIM, axis=1)
    ws, bs_t = gmlp_ws[0], gmlp_bs[0].T
    ws_t = jnp.swapaxes(ws, 1, 2)
    proj = dict(tb=True, tm=1024, tn=1024, tk=D_MODEL, out_dtype=F32)
    out_proj = dict(n=D_MODEL, tm=1024, tn=1024, tk=2 * WIDTH, out_dtype=F32)
    dw_out = dict(ta=True, n=D_MODEL, tm=1024, tn=D_MODEL, tk=2048, out_dtype=BF16)
    dx_in = dict(n=D_MODEL, tm=1024, tn=1024, tk=2048, out_dtype=F32)
    dw_in = dict(ta=True, n=D_MODEL, tm=1024, tn=D_MODEL, tk=2048, out_dtype=BF16)
    slabs = lambda g, rows: g.reshape(N_DEV, rows // N_DEV, D_MODEL)

    xn0 = _rms_fwd(xs, even_norm_g, "norm_even")
    h0, (g_woe, g_woo) = _mm(xn0, wte, n=F_EVEN, carry=out_weights, name="proj_even", **proj)
    woe, woo = g_woe.reshape(2 * WIDTH, D_MODEL), g_woo.reshape(2 * WIDTH, D_MODEL)
    hdt = _mm(xn0, wte_dt, tb=True, n=LANES, tm=1024, tn=LANES, tk=D_MODEL, out_dtype=F32, name="proj_dt")
    y0 = _gmlp_fwd(h0, gmlp_ln_g, gmlp_ln_b, ws, bs_t, "gmlp_fwd")
    pre = _ssd_conv_fwd(h0, conv_w, ssd_conv_b, "ssd_conv_fwd")
    dt, acs, acst = _ssd_dt(hdt, alog, dtb, "ssd_dt")
    (y0, ypre, states), (g_wto,) = _ssd_scan_fwd(
        pre, dt, acs, acst, h0, y0, d_full, ssd_norm_g, _AllGather([odd_w_in[0].T.astype(BF16)]), "ssd_scan_fwd")
    wto = g_wto.reshape(F_ODD, D_MODEL)
    x1 = _mm(y0, woe, add=xs, name="out_even", **out_proj)
    xn1 = _rms_fwd(x1, odd_g, "norm_odd")
    h1 = _mm(xn1, wto, n=F_ODD, name="proj_odd", **proj)
    y1 = _sconv_fwd(h1, sconv_full, "sconv_fwd")
    y1, att_o, att_lse = _attn2_fwd(h1, y1, "attn_fwd")
    x2 = _mm(y1, woo, add=x1, name="out_odd", **out_proj)
    loss_part, dx2, g_final, dx2_b = _loss_head(x2, final_norm_g.reshape(1, D_MODEL), loss_target[0], "loss_head")

    dy1 = _mm(dx2_b, woo, n=2 * WIDTH, name="dy_odd", **proj)
    gw_woo = _mm(y1, dx2_b, name="dw_out_odd", **dw_out)
    dh1, g_sconv = _sconv_bwd(h1, dy1, sconv_full, "sconv_bwd")
    dh1 = _attn2_bwd(h1, dy1, att_o, att_lse, dh1, "attn_bwd")
    dxn1 = _mm(dh1, wto, name="dx_odd", **dx_in)
    gw_wto = _mm(dh1, xn1, name="dw_in_odd", **dw_in)
    dx1, g_odd, dx1_b = _rms_bwd(x1, odd_g, dxn1, dx2, True, "norm_odd_bwd")

    dy0 = _mm(dx1_b, woe, n=2 * WIDTH, name="dy_even", **proj)
    gw_woe = _mm(y0, dx1_b, name="dw_out_even", **dw_out)
    dh0, g_ws, g_bs_t, g_ln_g, g_ln_b = _gmlp_bwd(h0, dy0, gmlp_ln_g, gmlp_ln_b, ws, ws_t, bs_t, "gmlp_bwd")
    odd_grads = _AllToAll([slabs(gw_wto, F_ODD), slabs(gw_woo, 2 * WIDTH)])
    (dh0, dpx, dpb, dpc, ddt, g_dtb, g_alog, g_dd, g_ng), (r_wto, r_woo) = _ssd_scan_bwd(
        pre, hdt, dt, acs, acst, h0, dy0, ypre, states, dh0, alog, dtb, d_full, ssd_norm_g, odd_grads, "ssd_scan_bwd")
    dh0, g_conv_w, g_conv_b = _ssd_conv_bwd(h0, dh0, dpx, dpb, dpc, conv_w, "ssd_conv_bwd")
    gw_main, (r_woe,) = _mm(dh0, xn0, out_rows=F_EVEN_ALL, carry=_AllToAll([slabs(gw_woe, 2 * WIDTH)]),
                            name="dw_in_even", **dw_in)
    gw_dt = _mm(ddt, xn0, ta=True, n=D_MODEL, tm=LANES, tn=D_MODEL, tk=1024, out_dtype=BF16, name="dw_dt")
    gw_wte = lax.dynamic_update_slice(gw_main, gw_dt[:B_HEADS], (F_EVEN, 0))
    dxn0_dt = _mm(ddt, wte_dt, n=D_MODEL, tm=1024, tn=D_MODEL, tk=LANES, out_dtype=F32, name="dx_dt")
    dxn0, (r_wte,) = _mm(dh0, wte, add=dxn0_dt, carry=_AllToAll([slabs(gw_wte, F_EVEN_ALL)]), name="dx_even", **dx_in)
    grad_x, g_even = _rms_bwd(xs, even_norm_g, dxn0, dx1, False, "norm_even_bwd")

    small_parts = dict(
        even_norm_g=g_even, gmlp_ln_g=g_ln_g, gmlp_ln_b=g_ln_b, gmlp_ws=g_ws, gmlp_bs=g_bs_t[:, :A_GROUPS].T,
        ssd_conv_w=g_conv_w[:B_CONV], ssd_conv_b=g_conv_b, ssd_dt_bias=g_dtb[:, :B_HEADS], ssd_a_log=g_alog[:, :B_HEADS],
        ssd_d=g_dd.reshape(B_HEADS, B_HEAD_DIM).sum(axis=1), ssd_norm_g=g_ng, odd_norm_g=g_odd,
        sconv_w=g_sconv[:C_CONV], final_norm_g=g_final)
    full_shapes = dict(
        even_norm_g=(1, D_MODEL), gmlp_ln_g=(1, WIDTH), gmlp_ln_b=(1, WIDTH), gmlp_ws=(1, A_GROUPS, CHUNK, CHUNK),
        gmlp_bs=(1, A_GROUPS, CHUNK), ssd_conv_w=(1, B_CONV, B_XBC), ssd_conv_b=(1, B_XBC), ssd_dt_bias=(1, B_HEADS),
        ssd_a_log=(1, B_HEADS), ssd_d=(1, B_HEADS), ssd_norm_g=(1, WIDTH), odd_norm_g=(1, D_MODEL),
        sconv_w=(1, C_CONV, WIDTH), final_norm_g=(D_MODEL,))
    (gathered_small,) = _exchange(_AllGather([_pack([small_parts[k] for k in SMALL])]), "gather_small_grads")
    small_sum = _sum_parts(gathered_small, LANES, "sum_small_grads")
    grads = dict(zip(SMALL, _unpack(small_sum, [full_shapes[k] for k in SMALL])))
    grads["ssd_conv_w"] = lax.dynamic_slice_in_dim(grads["ssd_conv_w"], me * 2 * shard, 2 * shard, axis=2)
    grads["odd_norm_g"] = lax.dynamic_slice_in_dim(grads["odd_norm_g"], me * shard, shard, axis=1)
    grads["sconv_w"] = lax.dynamic_slice_in_dim(grads["sconv_w"], me * shard, shard, axis=2)

    grads["even_w_in"] = _sum_parts(r_wte, 256, "sum_even_w_in").T[None]
    grads["odd_w_in"] = _sum_parts(r_wto, 256, "sum_odd_w_in").T[None]
    grads["even_w_out"] = _sum_parts(r_woe, 512, "sum_even_w_out")[None]
    grads["odd_w_out"] = _sum_parts(r_woo, 512, "sum_odd_w_out")[None]

    delta, new_m, new_v = {}, {}, {}
    for k in ("even_w_in", "odd_w_in", "even_w_out", "odd_w_out"):
        d_k, m_k, v_k = _adamw(w[k][0], grads[k][0], m1[k][0], m2[k][0], 128, "adamw_" + k)
        delta[k], new_m[k], new_v[k] = d_k[None], m_k[None], v_k[None]
    packed = [_pack([src[k] for k in SMALL]) for src in (w, grads, m1, m2)]
    small_out = _adamw(*packed, packed[0].shape[0], "adamw_small")
    shapes = [w[k].shape for k in SMALL]
    for dst, arr in zip((delta, new_m, new_v), small_out):
        dst.update(zip(SMALL, _unpack(arr, shapes)))

    loss = lax.psum(loss_part[0, 0], ("x", "y", "c"))
    return (loss, grad_x[None], *[grads[k] for k in ORDER], *[delta[k] for k in ORDER],
            *[new_m[k] for k in ORDER], *[new_v[k] for k in ORDER])
```

```python
import functools

import jax
import jax.numpy as jnp
from jax import lax
from jax.experimental import pallas as pl
from jax.experimental.pallas import tpu as pltpu

F32, BF16 = jnp.float32, jnp.bfloat16
MESH = pl.DeviceIdType.MESH
ANY = pl.BlockSpec(memory_space=pl.ANY)

N_DEV = 8
D_MODEL = 2048
WIDTH = 2048
CHUNK = 128
A_GROUPS = 8
B_HEADS, B_HEAD_DIM, B_GROUPS, B_STATE, B_CONV = 32, 64, 8, 128, 4
B_GROUP_W = WIDTH // B_GROUPS
B_XBC = WIDTH + 2 * B_GROUPS * B_STATE
C_CONV = 3
D_HEADS, D_HEAD_DIM = 16, 128
D_PATTERNS = ((128, 1), (512, 4), (2048, 16))
F_EVEN = 3 * WIDTH + WIDTH + B_XBC
F_EVEN_ALL = F_EVEN + B_HEADS
F_ODD = 8 * WIDTH
EPS = 1e-5
NEG = -1e30

ADAM_LR, ADAM_B1, ADAM_B2, ADAM_EPS, ADAM_WD, ADAM_STEP = 0.001, 0.9, 0.999, 1e-08, 0.01, 10

VMEM_LIMIT_V7X = 56 * 1024 * 1024
SUBLANES, LANES = 8, 128


def _cp(*sem):
    return pltpu.CompilerParams(dimension_semantics=sem, vmem_limit_bytes=VMEM_LIMIT_V7X)


def _sig(x):
    return 0.5 * jnp.tanh(0.5 * x) + 0.5


def _silu(x):
    return x * _sig(x)


def _dsilu(x):
    s = _sig(x)
    return s * (1.0 + x * (1.0 - s))


def _softplus(x):
    return jnp.maximum(x, 0.0) + jnp.log(1.0 + jnp.exp(-jnp.abs(x)))


def _dot(a, b):
    return jnp.dot(a.astype(BF16), b.astype(BF16), preferred_element_type=F32)


def _dot_nt(a, b):
    return lax.dot_general(a.astype(BF16), b.astype(BF16), (((1,), (1,)), ((), ())), preferred_element_type=F32)


def _dot_tn(a, b):
    return lax.dot_general(a.astype(BF16), b.astype(BF16), (((0,), (0,)), ((), ())), preferred_element_type=F32)


def _split3(x):
    hi = x.astype(BF16)
    r1 = x - hi.astype(F32)
    mid = r1.astype(BF16)
    lo = (r1 - mid.astype(F32)).astype(BF16)
    return hi, mid, lo


def _dot_sel(x, sel):
    return sum(jnp.dot(p, sel, preferred_element_type=F32) for p in _split3(x))


def _sel_dot(sel, x):
    return sum(jnp.dot(sel, p, preferred_element_type=F32) for p in _split3(x))


def _iota(shape, axis):
    return lax.broadcasted_iota(jnp.int32, shape, axis)


def _shift_down(cur, halo, j):
    if j == 0:
        return cur
    r = pltpu.roll(cur, j, 0)
    top = jnp.where(_iota(halo.shape, 0) < j, pltpu.roll(halo, j, 0), r[0:SUBLANES])
    return jnp.concatenate([top, r[SUBLANES:]], axis=0)


def _shift_up(cur, halo, j):
    if j == 0:
        return cur
    n = cur.shape[0]
    r = pltpu.roll(cur, n - j, 0)
    bot = jnp.where(_iota(halo.shape, 0) >= SUBLANES - j, pltpu.roll(halo, SUBLANES - j, 0), r[n - SUBLANES:])
    return jnp.concatenate([r[:n - SUBLANES], bot], axis=0)


def _mm(a, b, *, ta=False, tb=False, n, tm, tn, tk, out_dtype, out_rows=None, add=None, carry=None, name):
    width, rows = a.shape[-1], a.shape[-2]
    feat = width * (a.shape[0] if a.ndim == 3 else 1)
    m, k_len = (feat, rows) if ta else (rows, feat)
    per_part = width // (tm if ta else tk)
    grid = (m // tm, n // tn, k_len // tk)
    nk = grid[2]

    def a_index(i, j, k):
        f = i if ta else k
        pos = (k,) if ta else (i,)
        return pos + (f,) if a.ndim == 2 else (f // per_part,) + pos + (f % per_part,)

    a_block = (tk, tm) if ta else (tm, tk)
    a_spec = pl.BlockSpec(a_block if a.ndim == 2 else (None,) + a_block, a_index)
    b_spec = pl.BlockSpec((tn, tk), lambda i, j, k: (j, k)) if tb else pl.BlockSpec((tk, tn), lambda i, j, k: (k, j))
    io_spec = pl.BlockSpec((tm, tn), lambda i, j, k: (i, j))
    dims = (((0 if ta else 1,), (1 if tb else 0,)), ((), ()))
    has_add, nc = add is not None, len(carry.arrays) if carry else 0

    def body(*refs):
        a_ref, b_ref = refs[0], refs[1]
        add_ref = refs[2] if has_add else None
        pos = 2 + has_add
        c_in, o_ref, c_out = refs[pos:pos + nc], refs[pos + nc], refs[pos + nc + 1:pos + 2 * nc + 1]
        pos += 2 * nc + 1
        acc_ref = refs[pos] if nk > 1 else None
        c_sems = refs[pos + (nk > 1):]
        i, j, k = pl.program_id(0), pl.program_id(1), pl.program_id(2)
        if carry:
            pl.when((i == 0) & (j == 0) & (k == 0))(lambda: carry.start(c_in, c_out, c_sems))

        def finish(r):
            if add_ref is not None:
                r = r + add_ref[...]
            o_ref[...] = r.astype(out_dtype)

        p = lax.dot_general(a_ref[...].astype(BF16), b_ref[...].astype(BF16), dims, preferred_element_type=F32)
        if nk == 1:
            finish(p)
        else:
            @pl.when(k == 0)
            def _():
                acc_ref[...] = p

            @pl.when((k > 0) & (k < nk - 1))
            def _():
                acc_ref[...] += p

            @pl.when(k == nk - 1)
            def _():
                finish(acc_ref[...] + p)

        if carry:
            pl.when((i == grid[0] - 1) & (j == grid[1] - 1) & (k == nk - 1))(lambda: carry.finish(c_in, c_out, c_sems))

    out = pl.pallas_call(
        body, grid=grid, name=name,
        in_specs=[a_spec, b_spec] + [io_spec] * has_add + [ANY] * nc,
        out_specs=[io_spec] + [ANY] * nc,
        out_shape=[jax.ShapeDtypeStruct((out_rows or m, n), out_dtype)] + (carry.out_shape if carry else []),
        scratch_shapes=([pltpu.VMEM((tm, tn), F32)] if nk > 1 else []) + (carry.scratch if carry else []),
        compiler_params=_cp(*(("arbitrary",) * 3 if carry else ("parallel", "parallel", "arbitrary"))),
    )(a, b, *([add] if has_add else []), *(carry.arrays if carry else []))
    return (out[0], out[1:]) if carry else out[0]


def _rms_fwd(x, g, name):
    t, tb = x.shape[0], 512

    def body(x_ref, g_ref, o_ref):
        xv = x_ref[...]
        r = lax.rsqrt(jnp.mean(xv * xv, axis=-1, keepdims=True) + EPS)
        o_ref[...] = (xv * r * g_ref[...]).astype(BF16)

    row = pl.BlockSpec((tb, D_MODEL), lambda i: (i, 0))
    return pl.pallas_call(
        body, grid=(t // tb,), name=name,
        in_specs=[row, pl.BlockSpec((1, D_MODEL), lambda i: (0, 0))], out_specs=row,
        out_shape=jax.ShapeDtypeStruct((t, D_MODEL), BF16), compiler_params=_cp("parallel"),
    )(x, g)


def _rms_bwd(x, g, dxn, dres, bf16_copy, name):
    t, tb = x.shape[0], 256

    def body(x_ref, g_ref, dxn_ref, dres_ref, dx_ref, dg_ref, *dxb_ref):
        xv = x_ref[...]
        r = lax.rsqrt(jnp.mean(xv * xv, axis=-1, keepdims=True) + EPS)
        nv = xv * r
        dy = dxn_ref[...]
        dn = dy * g_ref[...]
        dx = dres_ref[...] + r * (dn - nv * jnp.mean(dn * nv, axis=-1, keepdims=True))
        dx_ref[...] = dx
        for ref in dxb_ref:
            ref[...] = dx.astype(BF16)
        part = jnp.sum(dy * nv, axis=0, keepdims=True)

        @pl.when(pl.program_id(0) == 0)
        def _():
            dg_ref[...] = part

        @pl.when(pl.program_id(0) > 0)
        def _():
            dg_ref[...] += part

    row = pl.BlockSpec((tb, D_MODEL), lambda i: (i, 0))
    vec = pl.BlockSpec((1, D_MODEL), lambda i: (0, 0))
    return pl.pallas_call(
        body, grid=(t // tb,), name=name,
        in_specs=[row, vec, row, row], out_specs=[row, vec] + [row] * bf16_copy,
        out_shape=[jax.ShapeDtypeStruct((t, D_MODEL), F32), jax.ShapeDtypeStruct((1, D_MODEL), F32)]
        + [jax.ShapeDtypeStruct((t, D_MODEL), BF16)] * bf16_copy,
        compiler_params=_cp("arbitrary"),
    )(x, g, dxn, dres)


def _loss_head(x, g, target, name):
    t, tb = x.shape[0], 256

    def body(x_ref, g_ref, t_ref, loss_ref, dx_ref, dg_ref, dxb_ref):
        xv, gv = x_ref[...], g_ref[...]
        r = lax.rsqrt(jnp.mean(xv * xv, axis=-1, keepdims=True) + EPS)
        nv = xv * r
        err = nv * gv - t_ref[...]
        lpart = 0.5 * jnp.sum(jnp.mean(err * err, axis=-1, keepdims=True), axis=0, keepdims=True)
        dy = err * (1.0 / D_MODEL)
        dn = dy * gv
        dx = r * (dn - nv * jnp.mean(dn * nv, axis=-1, keepdims=True))
        dx_ref[...] = dx
        dxb_ref[...] = dx.astype(BF16)
        gpart = jnp.sum(dy * nv, axis=0, keepdims=True)

        @pl.when(pl.program_id(0) == 0)
        def _():
            dg_ref[...] = gpart
            loss_ref[...] = jnp.broadcast_to(lpart, (1, LANES))

        @pl.when(pl.program_id(0) > 0)
        def _():
            dg_ref[...] += gpart
            loss_ref[...] += jnp.broadcast_to(lpart, (1, LANES))

    row = pl.BlockSpec((tb, D_MODEL), lambda i: (i, 0))
    vec = pl.BlockSpec((1, D_MODEL), lambda i: (0, 0))
    return pl.pallas_call(
        body, grid=(t // tb,), name=name,
        in_specs=[row, vec, row], out_specs=[pl.BlockSpec((1, LANES), lambda i: (0, 0)), row, vec, row],
        out_shape=[jax.ShapeDtypeStruct((1, LANES), F32), jax.ShapeDtypeStruct((t, D_MODEL), F32),
                   jax.ShapeDtypeStruct((1, D_MODEL), F32), jax.ShapeDtypeStruct((t, D_MODEL), BF16)],
        compiler_params=_cp("arbitrary"),
    )(x, g, target)


A_GW = WIDTH // A_GROUPS


def _gmlp_common(v, lg, lb):
    xc = v - jnp.mean(v, axis=-1, keepdims=True)
    rs = lax.rsqrt(jnp.mean(xc * xc, axis=-1, keepdims=True) + EPS)
    vh = xc * rs
    return rs, vh, (vh * lg + lb).astype(BF16)


def _gmlp_fwd(h, ln_g, ln_b, ws, bs_t, name):
    t, tb = h.shape[0], 256

    def body(u_ref, v_ref, z_ref, lg_ref, lb_ref, ws_ref, bst_ref, y_ref):
        _, _, vn = _gmlp_common(v_ref[...], lg_ref[...], lb_ref[...])
        causal = _iota((CHUNK, CHUNK), 1) <= _iota((CHUNK, CHUNK), 0)
        for g in range(A_GROUPS):
            w = jnp.where(causal, ws_ref[g], 0.0).astype(BF16)
            cols = slice(g * A_GW, (g + 1) * A_GW)
            for c in range(tb // CHUNK):
                rows = slice(c * CHUNK, (c + 1) * CHUNK)
                mixed = jnp.dot(w, vn[rows, cols], preferred_element_type=F32) + bst_ref[:, g:g + 1]
                y_ref[rows, cols] = (_silu(z_ref[rows, cols]) * (u_ref[rows, cols] * mixed)).astype(BF16)

    col = lambda j: pl.BlockSpec((tb, WIDTH), lambda i: (i, j))
    full = lambda a: pl.BlockSpec(a.shape, lambda i: (0,) * a.ndim)
    return pl.pallas_call(
        body, grid=(t // tb,), name=name,
        in_specs=[col(0), col(1), col(2), full(ln_g), full(ln_b), full(ws), full(bs_t)],
        out_specs=col(0), out_shape=jax.ShapeDtypeStruct((t, 2 * WIDTH), BF16),
        compiler_params=_cp("parallel"),
    )(h, h, h, ln_g, ln_b, ws, bs_t)


def _gmlp_bwd(h, dy, ln_g, ln_b, ws, ws_t, bs_t, name):
    t, tb = h.shape[0], 256

    def body(u_ref, v_ref, z_ref, dy_ref, lg_ref, lb_ref, ws_ref, wst_ref, bst_ref,
             dh_ref, dws_ref, dbst_ref, dlg_ref, dlb_ref, dvn_ref):
        @pl.when(pl.program_id(0) == 0)
        def _():
            dws_ref[...] = jnp.zeros_like(dws_ref)
            dbst_ref[...] = jnp.zeros_like(dbst_ref)
            dlg_ref[...] = jnp.zeros_like(dlg_ref)
            dlb_ref[...] = jnp.zeros_like(dlb_ref)

        rs, vh, vn = _gmlp_common(v_ref[...], lg_ref[...], lb_ref[...])
        row, lane = _iota((CHUNK, CHUNK), 0), _iota((CHUNK, CHUNK), 1)
        for g in range(A_GROUPS):
            w = jnp.where(lane <= row, ws_ref[g], 0.0).astype(BF16)
            wt = jnp.where(row <= lane, wst_ref[g], 0.0).astype(BF16)
            cols = slice(g * A_GW, (g + 1) * A_GW)
            dws_acc = jnp.zeros((CHUNK, CHUNK), F32)
            dbs_acc = jnp.zeros((CHUNK, 1), F32)
            for c in range(tb // CHUNK):
                rows = slice(c * CHUNK, (c + 1) * CHUNK)
                vnb = vn[rows, cols]
                mixed = jnp.dot(w, vnb, preferred_element_type=F32) + bst_ref[:, g:g + 1]
                u, z, dyv = u_ref[rows, cols], z_ref[rows, cols], dy_ref[rows, cols]
                sz = _silu(z)
                dh_ref[rows, cols] = (dyv * sz * mixed).astype(BF16)
                dh_ref[rows, slice(2 * WIDTH + g * A_GW, 2 * WIDTH + (g + 1) * A_GW)] = (
                    dyv * (u * mixed) * _dsilu(z)).astype(BF16)
                dm = dyv * sz * u
                dws_acc += _dot_nt(dm, vnb)
                dbs_acc += jnp.sum(dm, axis=1, keepdims=True)
                dvn_ref[rows, cols] = jnp.dot(wt, dm.astype(BF16), preferred_element_type=F32)
            dws_ref[g] += jnp.where(lane <= row, dws_acc, 0.0)
            dbst_ref[...] += jnp.where(lane == g, dbs_acc, 0.0)
        dvn = dvn_ref[...]
        dlg_ref[...] += jnp.sum(dvn * vh, axis=0, keepdims=True)
        dlb_ref[...] += jnp.sum(dvn, axis=0, keepdims=True)
        dvh = dvn * lg_ref[...]
        dv = rs * (dvh - jnp.mean(dvh, axis=-1, keepdims=True) - vh * jnp.mean(dvh * vh, axis=-1, keepdims=True))
        dh_ref[:, WIDTH:2 * WIDTH] = dv.astype(BF16)

    col = lambda j: pl.BlockSpec((tb, WIDTH), lambda i: (i, j))
    full = lambda a: pl.BlockSpec(a.shape, lambda i: (0,) * a.ndim)
    acc = lambda shape: pl.BlockSpec(shape, lambda i: (0,) * len(shape))
    return pl.pallas_call(
        body, grid=(t // tb,), name=name,
        in_specs=[col(0), col(1), col(2), col(0), full(ln_g), full(ln_b), full(ws), full(ws_t), full(bs_t)],
        out_specs=[pl.BlockSpec((tb, 3 * WIDTH), lambda i: (i, 0)), acc((A_GROUPS, CHUNK, CHUNK)),
                   acc((CHUNK, LANES)), acc((1, WIDTH)), acc((1, WIDTH))],
        out_shape=[jax.ShapeDtypeStruct((t, F_EVEN), BF16), jax.ShapeDtypeStruct((A_GROUPS, CHUNK, CHUNK), F32),
                   jax.ShapeDtypeStruct((CHUNK, LANES), F32), jax.ShapeDtypeStruct((1, WIDTH), F32),
                   jax.ShapeDtypeStruct((1, WIDTH), F32)],
        scratch_shapes=[pltpu.VMEM((tb, WIDTH), F32)],
        compiler_params=_cp("arbitrary"),
    )(h, h, h, dy, ln_g, ln_b, ws, ws_t, bs_t)


def _halo_prev(tb, j):
    return lambda i: (jnp.maximum(i * (tb // SUBLANES) - 1, 0), j)


def _halo_next(tb, j, t):
    return lambda i: (jnp.minimum((i + 1) * (tb // SUBLANES), t // SUBLANES - 1), j)


def _row_select(parts, width):
    row = _iota((SUBLANES, width), 0)
    out = jnp.zeros((SUBLANES, width), F32)
    for k, p in enumerate(parts):
        out = jnp.where(row == k, p, out)
    return out


def _sconv_fwd(h, w, name):
    t, tb = h.shape[0], 256

    def body(bg_ref, cg_ref, hx_ref, z_ref, cgh_ref, hxh_ref, w_ref, y_ref):
        p = cg_ref[...] * hx_ref[...]
        ph = jnp.where(pl.program_id(0) > 0, cgh_ref[...] * hxh_ref[...], 0.0)
        cv = w_ref[2:3, :] * p + w_ref[1:2, :] * _shift_down(p, ph, 1) + w_ref[0:1, :] * _shift_down(p, ph, 2)
        y_ref[...] = (_silu(z_ref[...]) * (bg_ref[...] * cv)).astype(BF16)

    col = lambda j: pl.BlockSpec((tb, WIDTH), lambda i: (i, j))
    halo = lambda j: pl.BlockSpec((SUBLANES, WIDTH), _halo_prev(tb, j))
    return pl.pallas_call(
        body, grid=(t // tb,), name=name,
        in_specs=[col(0), col(1), col(2), col(3), halo(1), halo(2), pl.BlockSpec(w.shape, lambda i: (0, 0))],
        out_specs=col(0), out_shape=jax.ShapeDtypeStruct((t, 2 * WIDTH), BF16),
        compiler_params=_cp("parallel"),
    )(h, h, h, h, h, h, w)


def _sconv_bwd(h, dy, w, name):
    t, tb = h.shape[0], 256
    nb = t // tb

    def body(bg_ref, cg_ref, hx_ref, z_ref, dy_ref, cgh_ref, hxh_ref, bgn_ref, zn_ref, dyn_ref, w_ref, dh_ref, dw_ref):
        i = pl.program_id(0)
        bg, cg, hx, z, dyv = bg_ref[...], cg_ref[...], hx_ref[...], z_ref[...], dy_ref[...]
        p = cg * hx
        ph = jnp.where(i > 0, cgh_ref[...] * hxh_ref[...], 0.0)
        p1, p2 = _shift_down(p, ph, 1), _shift_down(p, ph, 2)
        cv = w_ref[2:3, :] * p + w_ref[1:2, :] * p1 + w_ref[0:1, :] * p2
        sz = _silu(z)
        dcv = dyv * sz * bg
        dcvn = jnp.where(i < nb - 1, dyn_ref[...] * _silu(zn_ref[...]) * bgn_ref[...], 0.0)
        dp = w_ref[2:3, :] * dcv + w_ref[1:2, :] * _shift_up(dcv, dcvn, 1) + w_ref[0:1, :] * _shift_up(dcv, dcvn, 2)
        dh_ref[0] = (dyv * sz * cv).astype(BF16)
        dh_ref[1] = (dp * hx).astype(BF16)
        dh_ref[2] = (dp * cg).astype(BF16)
        dh_ref[3] = (dyv * (bg * cv) * _dsilu(z)).astype(BF16)
        part = _row_select([jnp.sum(dcv * q, axis=0, keepdims=True) for q in (p2, p1, p)], WIDTH)

        @pl.when(i == 0)
        def _():
            dw_ref[...] = part

        @pl.when(i > 0)
        def _():
            dw_ref[...] += part

    col = lambda j: pl.BlockSpec((tb, WIDTH), lambda i: (i, j))
    prev = lambda j: pl.BlockSpec((SUBLANES, WIDTH), _halo_prev(tb, j))
    nxt = lambda j: pl.BlockSpec((SUBLANES, WIDTH), _halo_next(tb, j, t))
    return pl.pallas_call(
        body, grid=(nb,), name=name,
        in_specs=[col(0), col(1), col(2), col(3), col(0), prev(1), prev(2), nxt(0), nxt(3), nxt(0),
                  pl.BlockSpec(w.shape, lambda i: (0, 0))],
        out_specs=[pl.BlockSpec((4, tb, WIDTH), lambda i: (0, i, 0)), pl.BlockSpec((SUBLANES, WIDTH), lambda i: (0, 0))],
        out_shape=[jax.ShapeDtypeStruct((8, t, WIDTH), BF16), jax.ShapeDtypeStruct((SUBLANES, WIDTH), F32)],
        compiler_params=_cp("arbitrary"),
    )(h, h, h, h, dy, h, h, h, h, dy, w)


XBC_COL0 = 4 * WIDTH


def _ssd_conv_fwd(h, w, b, name):
    t, tb = h.shape[0], 256

    def body(x_ref, xh_ref, w_ref, b_ref, o_ref):
        xv = x_ref[...]
        xh = jnp.where(pl.program_id(0) > 0, xh_ref[...], 0.0)
        acc = b_ref[...] + w_ref[3:4, :] * xv
        for j in range(1, B_CONV):
            acc = acc + w_ref[B_CONV - 1 - j:B_CONV - j, :] * _shift_down(xv, xh, j)
        o_ref[...] = acc

    cb = XBC_COL0 // B_XBC
    return pl.pallas_call(
        body, grid=(t // tb,), name=name,
        in_specs=[pl.BlockSpec((tb, B_XBC), lambda i: (i, cb)), pl.BlockSpec((SUBLANES, B_XBC), _halo_prev(tb, cb)),
                  pl.BlockSpec(w.shape, lambda i: (0, 0)), pl.BlockSpec(b.shape, lambda i: (0, 0))],
        out_specs=pl.BlockSpec((tb, B_XBC), lambda i: (i, 0)), out_shape=jax.ShapeDtypeStruct((t, B_XBC), F32),
        compiler_params=_cp("parallel"),
    )(h, h, w, b)


def _ssd_conv_bwd(h, dh, dpx, dpb, dpc, w, name):
    t, tb, tc = h.shape[0], 512, 1024
    nb = t // tb
    r8 = tb // SUBLANES
    last8 = t // SUBLANES - 1

    def body(dpx_ref, dpb_ref, dpc_ref, nx_ref, nb_ref, nc_ref, x_ref, xh_ref, w_ref, dh_in, dh_ref, dw_ref, db_ref):
        j, i = pl.program_id(0), pl.program_id(1)
        pick = lambda a, b_, c: jnp.where(j < 2, a[...], jnp.where(j == 2, b_[...], c[...]))
        dp = pick(dpx_ref, dpb_ref, dpc_ref)
        dn = jnp.where(i < nb - 1, pick(nx_ref, nb_ref, nc_ref), 0.0)
        xv = x_ref[...]
        xh = jnp.where(i > 0, xh_ref[...], 0.0)
        dx = w_ref[3:4, :] * dp
        for s in range(1, B_CONV):
            dx = dx + w_ref[B_CONV - 1 - s:B_CONV - s, :] * _shift_up(dp, dn, s)
        dh_ref[...] = dx.astype(BF16)
        wpart = _row_select([jnp.sum(dp * _shift_down(xv, xh, B_CONV - 1 - k), axis=0, keepdims=True)
                             for k in range(B_CONV)], tc)
        bpart = jnp.sum(dp, axis=0, keepdims=True)

        @pl.when(i == 0)
        def _():
            dw_ref[...] = wpart
            db_ref[...] = bpart

        @pl.when(i > 0)
        def _():
            dw_ref[...] += wpart
            db_ref[...] += bpart

    def src(blk_rows, rowf, sel, colf):
        return pl.BlockSpec((blk_rows, tc), lambda j, i: (jnp.where(sel(j), rowf(i), 0), colf(j)))

    cur = lambda i: i
    nxt = lambda i: jnp.minimum((i + 1) * r8, last8)
    is_x, is_b, is_c = (lambda j: j < 2), (lambda j: j == 2), (lambda j: j == 3)
    xcol, zero = (lambda j: jnp.minimum(j, 1)), (lambda j: 0)
    c0 = XBC_COL0 // tc
    return pl.pallas_call(
        body, grid=(B_XBC // tc, nb), name=name,
        in_specs=[src(tb, cur, is_x, xcol), src(tb, cur, is_b, zero), src(tb, cur, is_c, zero),
                  src(SUBLANES, nxt, is_x, xcol), src(SUBLANES, nxt, is_b, zero), src(SUBLANES, nxt, is_c, zero),
                  pl.BlockSpec((tb, tc), lambda j, i: (i, c0 + j)),
                  pl.BlockSpec((SUBLANES, tc), lambda j, i: (jnp.maximum(i * r8 - 1, 0), c0 + j)),
                  pl.BlockSpec((B_CONV, tc), lambda j, i: (0, j)), ANY],
        out_specs=[pl.BlockSpec((tb, tc), lambda j, i: (i, c0 + j)), pl.BlockSpec((SUBLANES, tc), lambda j, i: (0, j)),
                   pl.BlockSpec((1, tc), lambda j, i: (0, j))],
        out_shape=[jax.ShapeDtypeStruct(dh.shape, dh.dtype), jax.ShapeDtypeStruct((SUBLANES, B_XBC), F32),
                   jax.ShapeDtypeStruct((1, B_XBC), F32)],
        input_output_aliases={9: 0},
        compiler_params=_cp("arbitrary", "arbitrary"),
    )(dpx, dpb, dpc, dpx, dpb, dpc, h, h, w, dh)


HEADS_PER_GROUP = B_HEADS // B_GROUPS


def _ssd_dt(hdt, alog, dtb, name):
    t = hdt.shape[0]
    nc = t // CHUNK

    def body(dtr_ref, alog_ref, dtb_ref, dt_ref, acs_ref, acst_ref):
        dt = _softplus(dtr_ref[...] + dtb_ref[...])
        tri = (_iota((CHUNK, CHUNK), 1) <= _iota((CHUNK, CHUNK), 0)).astype(BF16)
        acs = _sel_dot(tri, dt * -jnp.exp(alog_ref[...]))
        dt_ref[...] = dt
        acs_ref[...] = acs
        acst_ref[...] = acs.T

    tok = pl.BlockSpec((CHUNK, LANES), lambda c: (c, 0))
    vec = pl.BlockSpec((1, LANES), lambda c: (0, 0))
    return pl.pallas_call(
        body, grid=(nc,), name=name, in_specs=[tok, vec, vec],
        out_specs=[tok, tok, pl.BlockSpec((None, LANES, CHUNK), lambda c: (c, 0, 0))],
        out_shape=[jax.ShapeDtypeStruct((t, LANES), F32), jax.ShapeDtypeStruct((t, LANES), F32),
                   jax.ShapeDtypeStruct((nc, LANES, CHUNK), F32)],
        compiler_params=_cp("parallel"),
    )(hdt, alog, dtb)


def _to_group(m, g):
    return pltpu.roll(m, (LANES - HEADS_PER_GROUP * g) % LANES, 1)


def _from_group(m, g):
    return pltpu.roll(m, HEADS_PER_GROUP * g, 1)


def _ssd_group_terms(g, px, pb, pc, dt, acs, acst_ref):
    head = _iota((CHUNK, B_GROUP_W), 1) // B_HEAD_DIM

    def spread(m4):
        out = m4[:, HEADS_PER_GROUP - 1:HEADS_PER_GROUP]
        for j in range(HEADS_PER_GROUP - 2, -1, -1):
            out = jnp.where(head == j, m4[:, j:j + 1], out)
        return out

    dt4, a4 = _to_group(dt, g), _to_group(acs, g)
    rows = [acst_ref[pl.ds(HEADS_PER_GROUP * g + j, 1), :] for j in range(HEADS_PER_GROUP)]
    return dict(xs=_silu(px), bm=_silu(pb), cm=_silu(pc), dt4=dt4, a4=a4, rows=rows, dt_e=spread(dt4), a_e=spread(a4))


def _ssd_decay(tm, j, transposed):
    col, row = tm["a4"][:, j:j + 1], tm["rows"][j]
    lane, sub = _iota((CHUNK, CHUNK), 1), _iota((CHUNK, CHUNK), 0)
    if transposed:
        return jnp.where(sub <= lane, jnp.exp(jnp.minimum(row - col, 0.0)), 0.0)
    return jnp.where(lane <= sub, jnp.exp(jnp.minimum(col - row, 0.0)), 0.0)


GROUPS_PER_STEP = 4


def _ssd_specs(nc, rev):
    ch = (lambda c: nc - 1 - c) if rev else (lambda c: c)
    n = GROUPS_PER_STEP
    gw = lambda off: pl.BlockSpec((CHUNK, n * B_GROUP_W), lambda c, g: (ch(c), off // n + g))
    gn = lambda off: pl.BlockSpec((CHUNK, n * B_STATE), lambda c, g: (ch(c), off // n + g))
    tok = pl.BlockSpec((CHUNK, LANES), lambda c, g: (ch(c), 0))
    vec = pl.BlockSpec((1, LANES), lambda c, g: (0, 0))
    gvec = pl.BlockSpec((1, n * B_GROUP_W), lambda c, g: (0, g))
    st = pl.BlockSpec((None, B_STATE, n * B_GROUP_W), lambda c, g: (ch(c), 0, g))
    tokt = pl.BlockSpec((None, LANES, CHUNK), lambda c, g: (ch(c), 0, 0))
    return gw, gn, tok, tokt, vec, gvec, st


def _group_cols(u):
    return slice(u * B_GROUP_W, (u + 1) * B_GROUP_W), slice(u * B_STATE, (u + 1) * B_STATE)


def _ssd_scan_fwd(pre, dt, acs, acst, h, y, dfull, ng, carry, name):
    t = pre.shape[0]
    nc = t // CHUNK
    n_carried = len(carry.arrays) if carry else 0

    def body(*refs):
        px_ref, pb_ref, pc_ref, dt_ref, acs_ref, acst_ref, z_ref, df_ref, ng_ref, _ = refs[:10]
        refs = refs[10:]
        c_in, refs = refs[:n_carried], refs[n_carried:]
        y_ref, ypre_ref, st_ref = refs[:3]
        c_out, state_ref, c_sems = refs[3:3 + n_carried], refs[3 + n_carried], refs[4 + n_carried:]
        c, pair = pl.program_id(0), pl.program_id(1)
        if carry:
            pl.when((c == 0) & (pair == 0))(lambda: carry.start(c_in, c_out, c_sems))
        groups = [GROUPS_PER_STEP * pair + u for u in range(GROUPS_PER_STEP)]

        @pl.when(c == 0)
        def _():
            for g in groups:
                state_ref[g] = jnp.zeros((B_STATE, B_GROUP_W), F32)

        entering, leaving = [state_ref[g] for g in groups], []
        for u, g in enumerate(groups):
            wide, narrow = _group_cols(u)
            tm = _ssd_group_terms(g, px_ref[:, wide], pb_ref[:, narrow], pc_ref[:, narrow], dt_ref[...],
                                  acs_ref[...], acst_ref)
            xs, bm, cm, a_e = tm["xs"], tm["bm"], tm["cm"], tm["a_e"]
            xdt = xs * tm["dt_e"]
            cb = _dot_nt(cm, bm)
            head = _iota((CHUNK, B_GROUP_W), 1) // B_HEAD_DIM
            yd = jnp.zeros((CHUNK, B_GROUP_W), F32)
            for j in range(HEADS_PER_GROUP):
                yd = jnp.where(head == j, _dot(cb * _ssd_decay(tm, j, False), xdt), yd)
            st = entering[u]
            st_ref[:, wide] = st
            yv = yd + jnp.exp(a_e) * _dot(cm, st) + df_ref[:, wide] * xs
            a_last = a_e[CHUNK - 1:CHUNK, :]
            leaving.append(st * jnp.exp(a_last) + _dot_tn(bm, xdt * jnp.exp(a_last - a_e)))
            ypre_ref[:, wide] = yv
            yz = yv * _silu(z_ref[:, wide])
            r = lax.rsqrt(jnp.mean(yz * yz, axis=-1, keepdims=True) + EPS)
            y_ref[:, wide] = (yz * r * ng_ref[:, wide]).astype(BF16)
        for g, st in zip(groups, leaving):
            state_ref[g] = st
        if carry:
            last = (c == nc - 1) & (pair == B_GROUPS // GROUPS_PER_STEP - 1)
            pl.when(last)(lambda: carry.finish(c_in, c_out, c_sems))

    gw, gn, tok, tokt, vec, gvec, st = _ssd_specs(nc, False)
    out = pl.pallas_call(
        body, grid=(nc, B_GROUPS // GROUPS_PER_STEP), name=name,
        in_specs=[gw(0), gn(WIDTH // B_STATE), gn((WIDTH + B_GROUPS * B_STATE) // B_STATE), tok, tok, tokt,
                  gw(3 * WIDTH // B_GROUP_W), gvec, gvec, ANY] + [ANY] * n_carried,
        out_specs=[gw(WIDTH // B_GROUP_W), gw(0), st] + [ANY] * n_carried,
        out_shape=[jax.ShapeDtypeStruct(y.shape, y.dtype), jax.ShapeDtypeStruct((t, WIDTH), F32),
                   jax.ShapeDtypeStruct((nc, B_STATE, WIDTH), F32)] + (carry.out_shape if carry else []),
        scratch_shapes=[pltpu.VMEM((B_GROUPS, B_STATE, B_GROUP_W), F32)] + (carry.scratch if carry else []),
        input_output_aliases={9: 0},
        compiler_params=_cp("arbitrary", "arbitrary"),
    )(pre, pre, pre, dt, acs, acst, h, dfull, ng, y, *(carry.arrays if carry else []))
    return out[:3], out[3:]


def _ssd_scan_bwd(pre, hdt, dt, acs, acst, h, dy, ypre, states, dh, alog, dtb, dfull, ng, carry, name):
    t = pre.shape[0]
    nc = t // CHUNK

    n_carried = len(carry.arrays) if carry else 0

    def one_group(g, dsn, px_ref, pb_ref, pc_ref, dtr_ref, dt_ref, acs_ref, acst_ref, z_ref, dy_ref, ypre_ref, st_ref,
                  alog_ref, dtb_ref, df_ref, ng_ref, dz_ref, dpx_ref, dpb_ref, dpc_ref):
        px, pb, pc, dtr = px_ref[...], pb_ref[...], pc_ref[...], dtr_ref[...]
        tm = _ssd_group_terms(g, px, pb, pc, dt_ref[...], acs_ref[...], acst_ref)
        xs, bm, cm, a_e, dt_e = tm["xs"], tm["bm"], tm["cm"], tm["a_e"], tm["dt_e"]
        xdt = xs * dt_e
        head = _iota((CHUNK, B_GROUP_W), 1) // B_HEAD_DIM

        z, yv, ngv = z_ref[...], ypre_ref[...], ng_ref[...]
        sz = _silu(z)
        yz = yv * sz
        r = lax.rsqrt(jnp.mean(yz * yz, axis=-1, keepdims=True) + EPS)
        dyn = dy_ref[...]
        dng_part = jnp.sum(dyn * yz * r, axis=0, keepdims=True)
        q = dyn * ngv
        dyz = r * q - yz * (r * r * r) * jnp.mean(q * yz, axis=-1, keepdims=True)
        dyv = dyz * sz
        dz_ref[...] = (dyz * yv * _dsilu(z)).astype(BF16)
        dd_part = jnp.sum(dyv * xs, axis=0, keepdims=True)
        dxs = df_ref[...] * dyv

        st = st_ref[...]
        ea = jnp.exp(a_e)
        ead = ea * dyv
        dcm = _dot_nt(ead, st)
        da_e = dyv * (ea * _dot(cm, st))

        a_last = a_e[CHUNK - 1:CHUNK, :]
        ea_last = jnp.exp(a_last)
        dstate = dsn * ea_last + _dot_tn(cm, ead)
        wdec = jnp.exp(a_last - a_e)
        xw = xdt * wdec
        dxw = _dot(bm, dsn)
        dxdt = dxw * wdec
        dbm = _dot_nt(xw, dsn)
        zc = dxw * xw
        da_last = jnp.sum(zc, axis=0, keepdims=True) + jnp.sum(dsn * st, axis=0, keepdims=True) * ea_last
        da_e = da_e - zc + jnp.where(_iota((CHUNK, B_GROUP_W), 0) == CHUNK - 1, da_last, 0.0)

        cb, cbt = _dot_nt(cm, bm), _dot_nt(bm, cm)
        dcb, dcbt = jnp.zeros((CHUNK, CHUNK), F32), jnp.zeros((CHUNK, CHUNK), F32)
        da4 = jnp.zeros((CHUNK, LANES), F32)
        lane = _iota((CHUNK, LANES), 1)
        for j in range(HEADS_PER_GROUP):
            mine = head == j
            gm = _dot_nt(jnp.where(mine, dyv, 0.0), xdt)
            gmt = _dot_nt(jnp.where(mine, xdt, 0.0), dyv)
            dec, dect = _ssd_decay(tm, j, False), _ssd_decay(tm, j, True)
            dcb += gm * dec
            dcbt += gmt * dect
            da_j = (jnp.sum(gm * cb * dec, axis=1, keepdims=True) - jnp.sum(gmt * cbt * dect, axis=1, keepdims=True))
            da4 = jnp.where(lane == j, da_j, da4)
            dxdt = dxdt + jnp.where(mine, _dot(cbt * dect, dyv), 0.0)
        dcm = dcm + _dot(dcb, bm)
        dbm = dbm + _dot(dcbt, cm)

        gather = (_iota((B_GROUP_W, LANES), 0) // B_HEAD_DIM == _iota((B_GROUP_W, LANES), 1)).astype(BF16)
        per_head = _dot_sel(jnp.concatenate([da_e, dxdt * xs], axis=0), gather)
        da4 = da4 + per_head[:CHUNK]
        rtri = (_iota((CHUNK, CHUNK), 1) >= _iota((CHUNK, CHUNK), 0)).astype(BF16)
        dadt4 = _sel_dot(rtri, da4)
        a_heads = -jnp.exp(alog_ref[...])
        rows8 = lambda v: jnp.broadcast_to(v, (SUBLANES, LANES))
        ddt4 = dadt4 * _to_group(rows8(a_heads), g)[0:1, :] + per_head[CHUNK:]
        dxs = dxs + dxdt * dt_e
        ddt = _from_group(ddt4, g) * _sig(dtr + dtb_ref[...])
        da_heads = _from_group(rows8(jnp.sum(dadt4 * tm["dt4"], axis=0, keepdims=True)), g)[0:1, :]

        dpx_ref[...] = dxs * _dsilu(px)
        dpb_ref[...] = dbm * _dsilu(pb)
        dpc_ref[...] = dcm * _dsilu(pc)
        return ddt, da_heads * a_heads, dstate, dd_part, dng_part

    def body(*refs):
        (px_ref, pb_ref, pc_ref, dtr_ref, dt_ref, acs_ref, acst_ref, z_ref, dy_ref, ypre_ref, st_ref, alog_ref,
         dtb_ref, df_ref, ng_ref, _) = refs[:16]
        refs = refs[16:]
        c_in, refs = refs[:n_carried], refs[n_carried:]
        dz_ref, dpx_ref, dpb_ref, dpc_ref, ddt_ref, dbias_ref, dalog_ref, dd_ref, dng_ref = refs[:9]
        c_out, dstate_ref, c_sems = refs[9:9 + n_carried], refs[9 + n_carried], refs[10 + n_carried:]
        c, pair = pl.program_id(0), pl.program_id(1)
        if carry:
            pl.when((c == 0) & (pair == 0))(lambda: carry.start(c_in, c_out, c_sems))
        groups = [GROUPS_PER_STEP * pair + u for u in range(GROUPS_PER_STEP)]

        @pl.when(c == 0)
        def _():
            for g in groups:
                dstate_ref[g] = jnp.zeros((B_STATE, B_GROUP_W), F32)
                dd_ref[g] = jnp.zeros((1, B_GROUP_W), F32)
                dng_ref[g] = jnp.zeros((1, B_GROUP_W), F32)

        leaving = [dstate_ref[g] for g in groups]
        ddt, dalog, done = 0.0, 0.0, []
        for u, g in enumerate(groups):
            wide, narrow = _group_cols(u)
            view = lambda ref, cols: ref.at[:, cols]
            ddt_u, dalog_u, *carried = one_group(
                g, leaving[u], view(px_ref, wide), view(pb_ref, narrow), view(pc_ref, narrow), dtr_ref,
                dt_ref, acs_ref, acst_ref, view(z_ref, wide), view(dy_ref, wide), view(ypre_ref, wide), view(st_ref, wide),
                alog_ref, dtb_ref, view(df_ref, wide), view(ng_ref, wide), view(dz_ref, wide), view(dpx_ref, wide),
                view(dpb_ref, narrow), view(dpc_ref, narrow))
            ddt, dalog = ddt + ddt_u, dalog + dalog_u
            done.append(carried)
        for g, (dstate, dd_part, dng_part) in zip(groups, done):
            dstate_ref[g] = dstate
            dd_ref[g] += dd_part
            dng_ref[g] += dng_part
        first = (c == 0) & (pair == 0)
        bias_part = jnp.sum(ddt, axis=0, keepdims=True)

        @pl.when(pair == 0)
        def _():
            ddt_ref[...] = ddt

        @pl.when(pair > 0)
        def _():
            ddt_ref[...] += ddt

        @pl.when(first)
        def _():
            dbias_ref[...] = bias_part
            dalog_ref[...] = dalog

        @pl.when(jnp.logical_not(first))
        def _():
            dbias_ref[...] += bias_part
            dalog_ref[...] += dalog

        if carry:
            last = (c == nc - 1) & (pair == B_GROUPS // GROUPS_PER_STEP - 1)
            pl.when(last)(lambda: carry.finish(c_in, c_out, c_sems))

    gw, gn, tok, tokt, vec, gvec, st = _ssd_specs(nc, True)
    acc = lambda shape: pl.BlockSpec(shape, lambda c, g: (0,) * len(shape))
    rev = lambda c: nc - 1 - c
    out = pl.pallas_call(
        body, grid=(nc, B_GROUPS // GROUPS_PER_STEP), name=name,
        in_specs=[gw(0), gn(WIDTH // B_STATE), gn((WIDTH + B_GROUPS * B_STATE) // B_STATE), tok, tok, tok, tokt,
                  gw(3 * WIDTH // B_GROUP_W), gw(WIDTH // B_GROUP_W), gw(0), st, vec, vec, gvec, gvec, ANY]
        + [ANY] * n_carried,
        out_specs=[gw(3 * WIDTH // B_GROUP_W), gw(0), gn(0), gn(0), tok, vec, vec, acc((B_GROUPS, 1, B_GROUP_W)), acc((B_GROUPS, 1, B_GROUP_W))] + [ANY] * n_carried,
        out_shape=[jax.ShapeDtypeStruct(dh.shape, dh.dtype), jax.ShapeDtypeStruct((t, WIDTH), F32),
                   jax.ShapeDtypeStruct((t, B_GROUPS * B_STATE), F32), jax.ShapeDtypeStruct((t, B_GROUPS * B_STATE), F32),
                   jax.ShapeDtypeStruct((t, LANES), F32), jax.ShapeDtypeStruct((1, LANES), F32),
                   jax.ShapeDtypeStruct((1, LANES), F32), jax.ShapeDtypeStruct((B_GROUPS, 1, B_GROUP_W), F32),
                   jax.ShapeDtypeStruct((B_GROUPS, 1, B_GROUP_W), F32)] + (carry.out_shape if carry else []),
        scratch_shapes=[pltpu.VMEM((B_GROUPS, B_STATE, B_GROUP_W), F32)] + (carry.scratch if carry else []),
        input_output_aliases={15: 0},
        compiler_params=_cp("arbitrary", "arbitrary"),
    )(pre, pre, pre, hdt, dt, acs, acst, h, dy, ypre, states, alog, dtb, dfull, ng, dh,
      *(carry.arrays if carry else []))
    return out[:9], out[9:]


Q_COL, K_COL, V_COL, Z_COL = [(4 + i) * WIDTH // D_HEAD_DIM for i in range(4)]
ATT_SCALE = D_HEAD_DIM ** -0.5


SPAN = 2048


def _att_blocks():
    out = []
    for _, dil in D_PATTERNS:
        nbl = SPAN // (CHUNK * dil)
        for r in range(dil):
            for bl in range(nbl):
                st = r + dil * CHUNK * bl
                out.append((dil, st, bl > 0, st - dil * CHUNK if bl > 0 else r + dil * CHUNK * (nbl - 1)))
    return out


def _rows(start, dil):
    return pl.ds(start, CHUNK) if dil == 1 else pl.ds(start, CHUNK, stride=dil)


def _att_keys(kc_ref, kp_ref, blk):
    dil, st, inside, pst = blk
    prev = (kc_ref if inside else kp_ref)[_rows(pst, dil), :]
    return jnp.concatenate([prev, kc_ref[_rows(st, dil), :]], axis=0).astype(BF16)


def _att_band(span_index):
    lane, sub = _iota((CHUNK, 2 * CHUNK), 1), _iota((CHUNK, 2 * CHUNK), 0)
    band = (lane >= sub) & (lane <= sub + CHUNK)
    return band, band & ((lane >= CHUNK) | (span_index > 0))


def _att_specs(t):
    blk = lambda off: pl.BlockSpec((SPAN, D_HEAD_DIM), lambda hd, sb: (sb, off + hd))
    prev = lambda off: pl.BlockSpec((SPAN, D_HEAD_DIM), lambda hd, sb: (jnp.maximum(sb - 1, 0), off + hd))
    lse = pl.BlockSpec((None, SPAN // CHUNK, CHUNK), lambda hd, sb: (hd, sb, 0))
    return blk, prev, lse


def _attn2_fwd(h, y, name):
    t = h.shape[0]

    def body(q_ref, kc_ref, vc_ref, kp_ref, vp_ref, z_ref, y_in, y_ref, o_ref, lse_ref, m_ref, l_ref):
        band, band_first = _att_band(pl.program_id(1))
        for blk in _att_blocks():
            dil, st, inside, _ = blk
            rows = _rows(st, dil)
            k2, v2 = _att_keys(kc_ref, kp_ref, blk), _att_keys(vc_ref, vp_ref, blk)
            s = jnp.where(band if inside else band_first, _dot_nt(q_ref[rows, :], k2) * ATT_SCALE, NEG)
            m_b = jnp.max(s, axis=1, keepdims=True)
            p = jnp.exp(s - m_b)
            l_b = jnp.sum(p, axis=1, keepdims=True)
            o_b = _dot(p, v2)
            wide = lambda a: jnp.broadcast_to(a, (CHUNK, D_HEAD_DIM))
            if dil == 1:
                m_ref[rows, :], l_ref[rows, :], o_ref[rows, :] = wide(m_b), wide(l_b), o_b
            else:
                m_o = m_ref[rows, :]
                m_n = jnp.maximum(m_o, m_b)
                a_o, a_b = jnp.exp(m_o - m_n), jnp.exp(m_b - m_n)
                m_ref[rows, :] = m_n
                l_ref[rows, :] = a_o * l_ref[rows, :] + a_b * l_b
                o_ref[rows, :] = a_o * o_ref[rows, :] + a_b * o_b
        l = l_ref[...]
        o = o_ref[...] / l
        o_ref[...] = o
        y_ref[...] = (_silu(z_ref[...]) * o).astype(BF16)
        for i in range(SPAN // CHUNK):
            blk_rows = slice(i * CHUNK, (i + 1) * CHUNK)
            lse_ref[i:i + 1, :] = (m_ref[blk_rows, :] + jnp.log(l[blk_rows, :])).T[0:1, :]

    blk, prev, lse_spec = _att_specs(t)
    return pl.pallas_call(
        body, grid=(D_HEADS, t // SPAN), name=name,
        in_specs=[blk(Q_COL), blk(K_COL), blk(V_COL), prev(K_COL), prev(V_COL), blk(Z_COL), ANY],
        out_specs=[blk(WIDTH // D_HEAD_DIM), blk(0), lse_spec],
        out_shape=[jax.ShapeDtypeStruct(y.shape, y.dtype), jax.ShapeDtypeStruct((t, WIDTH), F32),
                   jax.ShapeDtypeStruct((D_HEADS, t // CHUNK, CHUNK), F32)],
        scratch_shapes=[pltpu.VMEM((SPAN, D_HEAD_DIM), F32)] * 2,
        input_output_aliases={6: 0},
        compiler_params=_cp("parallel", "parallel"),
    )(h, h, h, h, h, h, y)


def _attn2_bwd(h, dy, o, lse, dh, name):
    t = h.shape[0]
    ns = t // SPAN

    def body(q_ref, kc_ref, vc_ref, kp_ref, vp_ref, z_ref, dy_ref, o_ref, lse_ref, dh_in, dh_ref,
             acc_ref, dq_ref, do_ref, delta_ref, lsec_ref):
        sb = pl.program_id(1)

        @pl.when(sb == 0)
        def _():
            acc_ref[...] = jnp.zeros_like(acc_ref)

        here, before = pl.multiple_of(sb * SPAN, SPAN), jnp.maximum(sb - 1, 0) * SPAN
        z, ov, dyv = z_ref[...], o_ref[...], dy_ref[...]
        do = dyv * _silu(z)
        do_ref[...] = do
        dh_ref[3, pl.ds(here, SPAN), :] = (dyv * ov * _dsilu(z)).astype(BF16)
        delta_ref[...] = jnp.broadcast_to(jnp.sum(do * ov, axis=1, keepdims=True), (SPAN, D_HEAD_DIM))
        for i in range(SPAN // CHUNK):
            lsec_ref[i * CHUNK:(i + 1) * CHUNK, :] = jnp.broadcast_to(lse_ref[i:i + 1, :], (CHUNK, CHUNK)).T
        band, band_first = _att_band(sb)
        for blk in _att_blocks():
            dil, st, inside, pst = blk
            rows = _rows(st, dil)
            q = q_ref[rows, :].astype(BF16)
            k2, v2 = _att_keys(kc_ref, kp_ref, blk), _att_keys(vc_ref, vp_ref, blk)
            dob = do_ref[rows, :].astype(BF16)
            s = jnp.where(band if inside else band_first, _dot_nt(q, k2) * ATT_SCALE, NEG)
            p = jnp.exp(s - lsec_ref[rows, :][:, 0:1])
            ds = p * (_dot_nt(dob, v2) - delta_ref[rows, :][:, 0:1]) * ATT_SCALE
            dq_b = _dot(ds, k2)
            if dil == 1:
                dq_ref[rows, :] = dq_b
            else:
                dq_ref[rows, :] += dq_b
            dk2, dv2 = _dot_tn(ds, q), _dot_tn(p, dob)
            own = _rows(pl.multiple_of(here + st, CHUNK) if dil == 1 else here + st, dil)
            pbase = (here if inside else before) + pst
            prv = _rows(pl.multiple_of(pbase, CHUNK) if dil == 1 else pbase, dil)
            acc_ref[0, own, :] += dk2[CHUNK:]
            acc_ref[1, own, :] += dv2[CHUNK:]
            acc_ref[0, prv, :] += dk2[:CHUNK]
            acc_ref[1, prv, :] += dv2[:CHUNK]
        dh_ref[0, pl.ds(here, SPAN), :] = dq_ref[...].astype(BF16)

        @pl.when(sb == ns - 1)
        def _():
            dh_ref[1] = acc_ref[0].astype(BF16)
            dh_ref[2] = acc_ref[1].astype(BF16)

    blk, prev, lse_spec = _att_specs(t)
    span = lambda: pltpu.VMEM((SPAN, D_HEAD_DIM), F32)
    return pl.pallas_call(
        body, grid=(D_HEADS, ns), name=name,
        in_specs=[blk(Q_COL), blk(K_COL), blk(V_COL), prev(K_COL), prev(V_COL), blk(Z_COL), blk(WIDTH // D_HEAD_DIM),
                  blk(0), lse_spec, ANY],
        out_specs=pl.BlockSpec((4, t, D_HEAD_DIM), lambda hd, sb: (1, 0, hd)),
        out_shape=jax.ShapeDtypeStruct(dh.shape, dh.dtype),
        scratch_shapes=[pltpu.VMEM((2, t, D_HEAD_DIM), F32), span(), span(), span(), span()],
        input_output_aliases={9: 0},
        compiler_params=_cp("arbitrary", "arbitrary"),
    )(h, h, h, h, h, h, dy, o, lse, dh)


def _place():
    x, y, c = lax.axis_index("x"), lax.axis_index("y"), lax.axis_index("c")
    return x, y, c, 4 * x + 2 * y + c


class _Exchange:
    def __init__(self, arrays, out_shape):
        n = len(arrays)
        self.arrays, self.out_shape = list(arrays), out_shape
        self.scratch = [pltpu.SemaphoreType.DMA((n, 7)), pltpu.SemaphoreType.DMA((n, 7)), pltpu.SemaphoreType.DMA((n,))]


class _AllGather(_Exchange):
    def __init__(self, blocks):
        super().__init__(blocks, [jax.ShapeDtypeStruct((N_DEV,) + b.shape, b.dtype) for b in blocks])

    def _plan(self, ins, outs, sems):
        send_sems, recv_sems, local_sems = sems
        x, y, c, me = _place()
        chips = [(1 - x, y), (x, 1 - y), (1 - x, 1 - y)]

        def copy(a, k, block, to, src=None):
            dst = outs[a].at[block]
            return pltpu.make_async_remote_copy(
                src_ref=dst if src is None else src, dst_ref=dst, send_sem=send_sems.at[a, k],
                recv_sem=recv_sems.at[a, k], device_id=to, device_id_type=MESH)

        n = len(ins)
        index = lambda px, py, pc: 4 * px + 2 * py + pc
        mine = [pltpu.make_async_copy(ins[a], outs[a].at[me], local_sems.at[a]) for a in range(n)]
        first = [copy(a, 0, me, (x, y, 1 - c), src=ins[a]) for a in range(n)]
        first += [copy(a, 1 + j, me, (*chip, c), src=ins[a]) for j, chip in enumerate(chips) for a in range(n)]
        arrive = lambda a, k, px, py, pc: copy(a, k, index(px, py, pc), (x, y, c))
        over_ici = [[arrive(a, 1 + j, *chip, c) for a in range(n)] for j, chip in enumerate(chips)]
        passed = [[copy(a, 4 + j, index(*chip, c), (x, y, 1 - c)) for a in range(n)] for j, chip in enumerate(chips)]
        from_sibling = [arrive(a, 0, x, y, 1 - c) for a in range(n)]
        from_sibling += [arrive(a, 4 + j, *chip, 1 - c) for j, chip in enumerate(chips) for a in range(n)]
        return mine, first, over_ici, passed, from_sibling

    def start(self, ins, outs, sems):
        mine, first, _, _, _ = self._plan(ins, outs, sems)
        for cp in mine + first:
            cp.start()

    def finish(self, ins, outs, sems):
        mine, first, over_ici, passed, from_sibling = self._plan(ins, outs, sems)
        for landed, onward in zip(over_ici, passed):
            for cp, fwd in zip(landed, onward):
                cp.wait_recv()
                fwd.start()
        for cp in from_sibling:
            cp.wait_recv()
        for cp in first + [fwd for onward in passed for fwd in onward]:
            cp.wait_send()
        for cp in mine:
            cp.wait()


class _AllToAll(_Exchange):
    def __init__(self, parts):
        super().__init__(parts, [jax.ShapeDtypeStruct(p.shape, p.dtype) for p in parts])

    def _plan(self, ins, outs, sems):
        send_sems, recv_sems, local_sems = sems
        x, y, c, me = _place()
        flip = lambda v, f: 1 - v if f else v
        peers = [(flip(x, fx), flip(y, fy), flip(c, fc)) for fx in (0, 1) for fy in (0, 1) for fc in (0, 1)][1:]

        def copy(a, k, sending):
            px, py, pc = peers[k]
            there = 4 * px + 2 * py + pc
            return pltpu.make_async_remote_copy(
                src_ref=ins[a].at[there], dst_ref=outs[a].at[me if sending else there], send_sem=send_sems.at[a, k],
                recv_sem=recv_sems.at[a, k], device_id=peers[k], device_id_type=MESH)

        n = len(ins)
        local = [pltpu.make_async_copy(ins[a].at[me], outs[a].at[me], local_sems.at[a]) for a in range(n)]
        return local, [[copy(a, k, sending) for k in range(7) for a in range(n)] for sending in (True, False)]

    def start(self, ins, outs, sems):
        local, (sends, _) = self._plan(ins, outs, sems)
        for cp in local + sends:
            cp.start()

    def finish(self, ins, outs, sems):
        local, (_, both_ways) = self._plan(ins, outs, sems)
        for cp in both_ways + local:
            cp.wait()


def _exchange(ex, name):
    n = len(ex.arrays)

    def body(*refs):
        ins, outs, sems = refs[:n], refs[n:2 * n], refs[2 * n:]
        ex.start(ins, outs, sems)
        ex.finish(ins, outs, sems)

    return pl.pallas_call(body, name=name, in_specs=[ANY] * n, out_specs=[ANY] * n, out_shape=ex.out_shape,
                          scratch_shapes=ex.scratch)(*ex.arrays)


def _sum_parts(parts, tc, name):
    _, r, c = parts.shape

    def body(p_ref, o_ref):
        acc = p_ref[0].astype(F32)
        for d in range(1, N_DEV):
            acc = acc + p_ref[d].astype(F32)
        o_ref[...] = acc

    return pl.pallas_call(
        body, grid=(c // tc,), name=name,
        in_specs=[pl.BlockSpec((N_DEV, r, tc), lambda j: (0, 0, j))], out_specs=pl.BlockSpec((r, tc), lambda j: (0, j)),
        out_shape=jax.ShapeDtypeStruct((r, c), F32), compiler_params=_cp("parallel"),
    )(parts)


def _adamw(w, g, m, v, tr, name):
    r, c = w.shape

    def body(w_ref, g_ref, m_ref, v_ref, d_ref, m2_ref, v2_ref):
        gv = g_ref[...]
        m2 = ADAM_B1 * m_ref[...] + (1.0 - ADAM_B1) * gv
        v2 = ADAM_B2 * v_ref[...] + (1.0 - ADAM_B2) * (gv * gv)
        m_hat = m2 / (1.0 - ADAM_B1 ** ADAM_STEP)
        v_hat = v2 / (1.0 - ADAM_B2 ** ADAM_STEP)
        d_ref[...] = -ADAM_LR * (m_hat / (jnp.sqrt(v_hat) + ADAM_EPS) + ADAM_WD * w_ref[...])
        m2_ref[...] = m2
        v2_ref[...] = v2

    spec = pl.BlockSpec((tr, c), lambda i: (i, 0))
    return pl.pallas_call(
        body, grid=(r // tr,), name=name, in_specs=[spec] * 4, out_specs=[spec] * 3,
        out_shape=[jax.ShapeDtypeStruct((r, c), F32)] * 3, compiler_params=_cp("parallel"),
    )(w, g, m, v)


PACK_ROWS = SUBLANES * LANES


def _pack(arrays):
    flat = [jnp.pad(a.reshape(-1), (0, -a.size % PACK_ROWS)) for a in arrays]
    return jnp.concatenate(flat).reshape(-1, LANES)


def _unpack(packed, shapes):
    flat, out, pos = packed.reshape(-1), [], 0
    for s in shapes:
        size = 1
        for d in s:
            size *= d
        out.append(flat[pos:pos + size].reshape(s))
        pos += size + (-size % PACK_ROWS)
    return out


SMALL = ["even_norm_g", "gmlp_ln_g", "gmlp_ln_b", "gmlp_ws", "gmlp_bs", "ssd_conv_w", "ssd_conv_b", "ssd_dt_bias",
         "ssd_a_log", "ssd_d", "ssd_norm_g", "odd_norm_g", "sconv_w", "final_norm_g"]
ORDER = ["even_norm_g", "even_w_in", "gmlp_ln_g", "gmlp_ln_b", "gmlp_ws", "gmlp_bs", "ssd_conv_w", "ssd_conv_b",
         "ssd_dt_bias", "ssd_a_log", "ssd_d", "ssd_norm_g", "even_w_out", "odd_norm_g", "odd_w_in", "sconv_w",
         "odd_w_out", "final_norm_g"]


def kernel(x, even_norm_g, even_w_in, gmlp_ln_g, gmlp_ln_b, gmlp_ws, gmlp_bs, ssd_conv_w, ssd_conv_b, ssd_dt_bias, ssd_a_log, ssd_d, ssd_norm_g, even_w_out, odd_norm_g, odd_w_in, sconv_w, odd_w_out, final_norm_g, loss_target, m_even_norm_g, m_even_w_in, m_gmlp_ln_g, m_gmlp_ln_b, m_gmlp_ws, m_gmlp_bs, m_ssd_conv_w, m_ssd_conv_b, m_ssd_dt_bias, m_ssd_a_log, m_ssd_d, m_ssd_norm_g, m_even_w_out, m_odd_norm_g, m_odd_w_in, m_sconv_w, m_odd_w_out, m_final_norm_g, v_even_norm_g, v_even_w_in, v_gmlp_ln_g, v_gmlp_ln_b, v_gmlp_ws, v_gmlp_bs, v_ssd_conv_w, v_ssd_conv_b, v_ssd_dt_bias, v_ssd_a_log, v_ssd_d, v_ssd_norm_g, v_even_w_out, v_odd_norm_g, v_odd_w_in, v_sconv_w, v_odd_w_out, v_final_norm_g):
    w = dict(even_norm_g=even_norm_g, even_w_in=even_w_in, gmlp_ln_g=gmlp_ln_g, gmlp_ln_b=gmlp_ln_b, gmlp_ws=gmlp_ws,
             gmlp_bs=gmlp_bs, ssd_conv_w=ssd_conv_w, ssd_conv_b=ssd_conv_b, ssd_dt_bias=ssd_dt_bias,
             ssd_a_log=ssd_a_log, ssd_d=ssd_d, ssd_norm_g=ssd_norm_g, even_w_out=even_w_out, odd_norm_g=odd_norm_g,
             odd_w_in=odd_w_in, sconv_w=sconv_w, odd_w_out=odd_w_out, final_norm_g=final_norm_g)
    m1 = dict(even_norm_g=m_even_norm_g, even_w_in=m_even_w_in, gmlp_ln_g=m_gmlp_ln_g, gmlp_ln_b=m_gmlp_ln_b,
              gmlp_ws=m_gmlp_ws, gmlp_bs=m_gmlp_bs, ssd_conv_w=m_ssd_conv_w, ssd_conv_b=m_ssd_conv_b,
              ssd_dt_bias=m_ssd_dt_bias, ssd_a_log=m_ssd_a_log, ssd_d=m_ssd_d, ssd_norm_g=m_ssd_norm_g,
              even_w_out=m_even_w_out, odd_norm_g=m_odd_norm_g, odd_w_in=m_odd_w_in, sconv_w=m_sconv_w,
              odd_w_out=m_odd_w_out, final_norm_g=m_final_norm_g)
    m2 = dict(even_norm_g=v_even_norm_g, even_w_in=v_even_w_in, gmlp_ln_g=v_gmlp_ln_g, gmlp_ln_b=v_gmlp_ln_b,
              gmlp_ws=v_gmlp_ws, gmlp_bs=v_gmlp_bs, ssd_conv_w=v_ssd_conv_w, ssd_conv_b=v_ssd_conv_b,
              ssd_dt_bias=v_ssd_dt_bias, ssd_a_log=v_ssd_a_log, ssd_d=v_ssd_d, ssd_norm_g=v_ssd_norm_g,
              even_w_out=v_even_w_out, odd_norm_g=v_odd_norm_g, odd_w_in=v_odd_w_in, sconv_w=v_sconv_w,
              odd_w_out=v_odd_w_out, final_norm_g=v_final_norm_g)
    _, _, _, me = _place()
    xs = x[0]
    shard = WIDTH // N_DEV

    small_blk = jnp.concatenate([
        ssd_conv_w[0], jnp.pad(sconv_w[0], ((0, 0), (0, shard))), jnp.pad(odd_norm_g, ((0, 0), (0, shard)))], axis=0)
    g_wte, g_small = _exchange(_AllGather([even_w_in[0].T.astype(BF16), small_blk]), "gather_even_w_in")
    out_weights = _AllGather([even_w_out[0].astype(BF16), odd_w_out[0].astype(BF16)])
    wte = g_wte.reshape(F_EVEN_ALL, D_MODEL)
    wte_dt = jnp.pad(wte[F_EVEN:], ((0, LANES - B_HEADS), (0, 0)))
    conv_w = g_small[:, 0:B_CONV, :].transpose(1, 0, 2).reshape(B_CONV, B_XBC)
    sconv_full = g_small[:, B_CONV:B_CONV + C_CONV, :shard].transpose(1, 0, 2).reshape(C_CONV, WIDTH)
    odd_g = g_small[:, B_CONV + C_CONV, :shard].reshape(1, WIDTH)

    pad_heads = lambda a: jnp.pad(a, ((0, 0), (0, LANES - B_HEADS)))
    alog, dtb = pad_heads(ssd_a_log), pad_heads(ssd_dt_bias)
    d_full = jnp.repeat(ssd_d, B_HEAD_DIM, axis=1)
    ws, bs_t = gmlp_ws[0], gmlp_bs[0].T
    ws_t = jnp.swapaxes(ws, 1, 2)
    proj = dict(tb=True, tm=1024, tn=1024, tk=D_MODEL, out_dtype=F32)
    out_proj = dict(n=D_MODEL, tm=1024, tn=1024, tk=2 * WIDTH, out_dtype=F32)
    dw_out = dict(ta=True, n=D_MODEL, tm=1024, tn=D_MODEL, tk=2048, out_dtype=BF16)
    dx_in = dict(n=D_MODEL, tm=1024, tn=1024, tk=2048, out_dtype=F32)
    dw_in = dict(ta=True, n=D_MODEL, tm=1024, tn=D_MODEL, tk=2048, out_dtype=BF16)
    slabs = lambda g, rows: g.reshape(N_DEV, rows // N_DEV, D_MODEL)

    xn0 = _rms_fwd(xs, even_norm_g, "norm_even")
    h0, (g_woe, g_woo) = _mm(xn0, wte, n=F_EVEN, carry=out_weights, name="proj_even", **proj)
    woe, woo = g_woe.reshape(2 * WIDTH, D_MODEL), g_woo.reshape(2 * WIDTH, D_MODEL)
    hdt = _mm(xn0, wte_dt, tb=True, n=LANES, tm=1024, tn=LANES, tk=D_MODEL, out_dtype=F32, name="proj_dt")
    y0 = _gmlp_fwd(h0, gmlp_ln_g, gmlp_ln_b, ws, bs_t, "gmlp_fwd")
    pre = _ssd_conv_fwd(h0, conv_w, ssd_conv_b, "ssd_conv_fwd")
    dt, acs, acst = _ssd_dt(hdt, alog, dtb, "ssd_dt")
    (y0, ypre, states), (g_wto,) = _ssd_scan_fwd(
        pre, dt, acs, acst, h0, y0, d_full, ssd_norm_g, _AllGather([odd_w_in[0].T.astype(BF16)]), "ssd_scan_fwd")
    wto = g_wto.reshape(F_ODD, D_MODEL)
    x1 = _mm(y0, woe, add=xs, name="out_even", **out_proj)
    xn1 = _rms_fwd(x1, odd_g, "norm_odd")
    h1 = _mm(xn1, wto, n=F_ODD, name="proj_odd", **proj)
    y1 = _sconv_fwd(h1, sconv_full, "sconv_fwd")
    y1, att_o, att_lse = _attn2_fwd(h1, y1, "attn_fwd")
    x2 = _mm(y1, woo, add=x1, name="out_odd", **out_proj)
    loss_part, dx2, g_final, dx2_b = _loss_head(x2, final_norm_g.reshape(1, D_MODEL), loss_target[0], "loss_head")

    dy1 = _mm(dx2_b, woo, n=2 * WIDTH, name="dy_odd", **proj)
    gw_woo = _mm(y1, dx2_b, name="dw_out_odd", **dw_out)
    dh1, g_sconv = _sconv_bwd(h1, dy1, sconv_full, "sconv_bwd")
    dh1 = _attn2_bwd(h1, dy1, att_o, att_lse, dh1, "attn_bwd")
    dxn1 = _mm(dh1, wto, name="dx_odd", **dx_in)
    gw_wto = _mm(dh1, xn1, name="dw_in_odd", **dw_in)
    dx1, g_odd, dx1_b = _rms_bwd(x1, odd_g, dxn1, dx2, True, "norm_odd_bwd")

    dy0 = _mm(dx1_b, woe, n=2 * WIDTH, name="dy_even", **proj)
    gw_woe = _mm(y0, dx1_b, name="dw_out_even", **dw_out)
    dh0, g_ws, g_bs_t, g_ln_g, g_ln_b = _gmlp_bwd(h0, dy0, gmlp_ln_g, gmlp_ln_b, ws, ws_t, bs_t, "gmlp_bwd")
    odd_grads = _AllToAll([slabs(gw_wto, F_ODD), slabs(gw_woo, 2 * WIDTH)])
    (dh0, dpx, dpb, dpc, ddt, g_dtb, g_alog, g_dd, g_ng), (r_wto, r_woo) = _ssd_scan_bwd(
        pre, hdt, dt, acs, acst, h0, dy0, ypre, states, dh0, alog, dtb, d_full, ssd_norm_g, odd_grads, "ssd_scan_bwd")
    dh0, g_conv_w, g_conv_b = _ssd_conv_bwd(h0, dh0, dpx, dpb, dpc, conv_w, "ssd_conv_bwd")
    gw_main, (r_woe,) = _mm(dh0, xn0, out_rows=F_EVEN_ALL, carry=_AllToAll([slabs(gw_woe, 2 * WIDTH)]),
                            name="dw_in_even", **dw_in)
    gw_dt = _mm(ddt, xn0, ta=True, n=D_MODEL, tm=LANES, tn=D_MODEL, tk=1024, out_dtype=BF16, name="dw_dt")
    gw_wte = lax.dynamic_update_slice(gw_main, gw_dt[:B_HEADS], (F_EVEN, 0))
    dxn0_dt = _mm(ddt, wte_dt, n=D_MODEL, tm=1024, tn=D_MODEL, tk=LANES, out_dtype=F32, name="dx_dt")
    dxn0, (r_wte,) = _mm(dh0, wte, add=dxn0_dt, carry=_AllToAll([slabs(gw_wte, F_EVEN_ALL)]), name="dx_even", **dx_in)
    grad_x, g_even = _rms_bwd(xs, even_norm_g, dxn0, dx1, False, "norm_even_bwd")

    small_parts = dict(
        even_norm_g=g_even, gmlp_ln_g=g_ln_g, gmlp_ln_b=g_ln_b, gmlp_ws=g_ws, gmlp_bs=g_bs_t[:, :A_GROUPS].T,
        ssd_conv_w=g_conv_w[:B_CONV], ssd_conv_b=g_conv_b, ssd_dt_bias=g_dtb[:, :B_HEADS], ssd_a_log=g_alog[:, :B_HEADS],
        ssd_d=g_dd.reshape(B_HEADS, B_HEAD_DIM).sum(axis=1), ssd_norm_g=g_ng, odd_norm_g=g_odd,
        sconv_w=g_sconv[:C_CONV], final_norm_g=g_final)
    full_shapes = dict(
        even_norm_g=(1, D_MODEL), gmlp_ln_g=(1, WIDTH), gmlp_ln_b=(1, WIDTH), gmlp_ws=(1, A_GROUPS, CHUNK, CHUNK),
        gmlp_bs=(1, A_GROUPS, CHUNK), ssd_conv_w=(1, B_CONV, B_XBC), ssd_conv_b=(1, B_XBC), ssd_dt_bias=(1, B_HEADS),
        ssd_a_log=(1, B_HEADS), ssd_d=(1, B_HEADS), ssd_norm_g=(1, WIDTH), odd_norm_g=(1, D_MODEL),
        sconv_w=(1, C_CONV, WIDTH), final_norm_g=(D_MODEL,))
    (gathered_small,) = _exchange(_AllGather([_pack([small_parts[k] for k in SMALL])]), "gather_small_grads")
    small_sum = _sum_parts(gathered_small, LANES, "sum_small_grads")
    grads = dict(zip(SMALL, _unpack(small_sum, [full_shapes[k] for k in SMALL])))
    grads["ssd_conv_w"] = lax.dynamic_slice_in_dim(grads["ssd_conv_w"], me * 2 * shard, 2 * shard, axis=2)
    grads["odd_norm_g"] = lax.dynamic_slice_in_dim(grads["odd_norm_g"], me * shard, shard, axis=1)
    grads["sconv_w"] = lax.dynamic_slice_in_dim(grads["sconv_w"], me * shard, shard, axis=2)

    grads["even_w_in"] = _sum_parts(r_wte, 256, "sum_even_w_in").T[None]
    grads["odd_w_in"] = _sum_parts(r_wto, 256, "sum_odd_w_in").T[None]
    grads["even_w_out"] = _sum_parts(r_woe, 512, "sum_even_w_out")[None]
    grads["odd_w_out"] = _sum_parts(r_woo, 512, "sum_odd_w_out")[None]

    delta, new_m, new_v = {}, {}, {}
    for k in ("even_w_in", "odd_w_in", "even_w_out", "odd_w_out"):
        d_k, m_k, v_k = _adamw(w[k][0], grads[k][0], m1[k][0], m2[k][0], 128, "adamw_" + k)
        delta[k], new_m[k], new_v[k] = d_k[None], m_k[None], v_k[None]
    packed = [_pack([src[k] for k in SMALL]) for src in (w, grads, m1, m2)]
    small_out = _adamw(*packed, packed[0].shape[0], "adamw_small")
    shapes = [w[k].shape for k in SMALL]
    for dst, arr in zip((delta, new_m, new_v), small_out):
        dst.update(zip(SMALL, _unpack(arr, shapes)))

    loss = lax.psum(loss_part[0, 0], ("x", "y", "c"))
    return (loss, grad_x[None], *[grads[k] for k in ORDER], *[delta[k] for k in ORDER],
            *[new_m[k] for k in ORDER], *[new_v[k] for k in ORDER])
```

```python
import functools

import jax
import jax.numpy as jnp
from jax import lax
from jax.experimental import pallas as pl
from jax.experimental.pallas import tpu as pltpu

F32, BF16 = jnp.float32, jnp.bfloat16
MESH = pl.DeviceIdType.MESH
ANY = pl.BlockSpec(memory_space=pl.ANY)

N_DEV = 8
D_MODEL = 2048
WIDTH = 2048
CHUNK = 128
A_GROUPS = 8
B_HEADS, B_HEAD_DIM, B_GROUPS, B_STATE, B_CONV = 32, 64, 8, 128, 4
B_GROUP_W = WIDTH // B_GROUPS
B_XBC = WIDTH + 2 * B_GROUPS * B_STATE
C_CONV = 3
D_HEADS, D_HEAD_DIM = 16, 128
D_PATTERNS = ((128, 1), (512, 4), (2048, 16))
F_EVEN = 3 * WIDTH + WIDTH + B_XBC
F_EVEN_ALL = F_EVEN + B_HEADS
F_ODD = 8 * WIDTH
EPS = 1e-5
NEG = -1e30

ADAM_LR, ADAM_B1, ADAM_B2, ADAM_EPS, ADAM_WD, ADAM_STEP = 0.001, 0.9, 0.999, 1e-08, 0.01, 10

VMEM_LIMIT_V7X = 56 * 1024 * 1024
SUBLANES, LANES = 8, 128


def _cp(*sem):
    return pltpu.CompilerParams(dimension_semantics=sem, vmem_limit_bytes=VMEM_LIMIT_V7X)


def _sig(x):
    return 0.5 * jnp.tanh(0.5 * x) + 0.5


def _silu(x):
    return x * _sig(x)


def _dsilu(x):
    s = _sig(x)
    return s * (1.0 + x * (1.0 - s))


def _softplus(x):
    return jnp.maximum(x, 0.0) + jnp.log(1.0 + jnp.exp(-jnp.abs(x)))


def _dot(a, b):
    return jnp.dot(a.astype(BF16), b.astype(BF16), preferred_element_type=F32)


def _dot_nt(a, b):
    return lax.dot_general(a.astype(BF16), b.astype(BF16), (((1,), (1,)), ((), ())), preferred_element_type=F32)


def _dot_tn(a, b):
    return lax.dot_general(a.astype(BF16), b.astype(BF16), (((0,), (0,)), ((), ())), preferred_element_type=F32)


def _split3(x):
    hi = x.astype(BF16)
    r1 = x - hi.astype(F32)
    mid = r1.astype(BF16)
    lo = (r1 - mid.astype(F32)).astype(BF16)
    return hi, mid, lo


def _dot_sel(x, sel):
    return sum(jnp.dot(p, sel, preferred_element_type=F32) for p in _split3(x))


def _sel_dot(sel, x):
    return sum(jnp.dot(sel, p, preferred_element_type=F32) for p in _split3(x))


def _iota(shape, axis):
    return lax.broadcasted_iota(jnp.int32, shape, axis)


def _shift_down(cur, halo, j):
    if j == 0:
        return cur
    r = pltpu.roll(cur, j, 0)
    top = jnp.where(_iota(halo.shape, 0) < j, pltpu.roll(halo, j, 0), r[0:SUBLANES])
    return jnp.concatenate([top, r[SUBLANES:]], axis=0)


def _shift_up(cur, halo, j):
    if j == 0:
        return cur
    n = cur.shape[0]
    r = pltpu.roll(cur, n - j, 0)
    bot = jnp.where(_iota(halo.shape, 0) >= SUBLANES - j, pltpu.roll(halo, SUBLANES - j, 0), r[n - SUBLANES:])
    return jnp.concatenate([r[:n - SUBLANES], bot], axis=0)


def _mm(a, b, *, ta=False, tb=False, n, tm, tn, tk, out_dtype, out_rows=None, add=None, carry=None, name):
    width, rows = a.shape[-1], a.shape[-2]
    feat = width * (a.shape[0] if a.ndim == 3 else 1)
    m, k_len = (feat, rows) if ta else (rows, feat)
    per_part = width // (tm if ta else tk)
    grid = (m // tm, n // tn, k_len // tk)
    nk = grid[2]

    def a_index(i, j, k):
        f = i if ta else k
        pos = (k,) if ta else (i,)
        return pos + (f,) if a.ndim == 2 else (f // per_part,) + pos + (f % per_part,)

    a_block = (tk, tm) if ta else (tm, tk)
    a_spec = pl.BlockSpec(a_block if a.ndim == 2 else (None,) + a_block, a_index)
    b_spec = pl.BlockSpec((tn, tk), lambda i, j, k: (j, k)) if tb else pl.BlockSpec((tk, tn), lambda i, j, k: (k, j))
    io_spec = pl.BlockSpec((tm, tn), lambda i, j, k: (i, j))
    dims = (((0 if ta else 1,), (1 if tb else 0,)), ((), ()))
    has_add, nc = add is not None, len(carry.arrays) if carry else 0

    def body(*refs):
        a_ref, b_ref = refs[0], refs[1]
        add_ref = refs[2] if has_add else None
        pos = 2 + has_add
        c_in, o_ref, c_out = refs[pos:pos + nc], refs[pos + nc], refs[pos + nc + 1:pos + 2 * nc + 1]
        pos += 2 * nc + 1
        acc_ref = refs[pos] if nk > 1 else None
        c_sems = refs[pos + (nk > 1):]
        i, j, k = pl.program_id(0), pl.program_id(1), pl.program_id(2)
        if carry:
            pl.when((i == 0) & (j == 0) & (k == 0))(lambda: carry.start(c_in, c_out, c_sems))

        def finish(r):
            if add_ref is not None:
                r = r + add_ref[...]
            o_ref[...] = r.astype(out_dtype)

        p = lax.dot_general(a_ref[...].astype(BF16), b_ref[...].astype(BF16), dims, preferred_element_type=F32)
        if nk == 1:
            finish(p)
        else:
            @pl.when(k == 0)
            def _():
                acc_ref[...] = p

            @pl.when((k > 0) & (k < nk - 1))
            def _():
                acc_ref[...] += p

            @pl.when(k == nk - 1)
            def _():
                finish(acc_ref[...] + p)

        if carry:
            pl.when((i == grid[0] - 1) & (j == grid[1] - 1) & (k == nk - 1))(lambda: carry.finish(c_in, c_out, c_sems))

    out = pl.pallas_call(
        body, grid=grid, name=name,
        in_specs=[a_spec, b_spec] + [io_spec] * has_add + [ANY] * nc,
        out_specs=[io_spec] + [ANY] * nc,
        out_shape=[jax.ShapeDtypeStruct((out_rows or m, n), out_dtype)] + (carry.out_shape if carry else []),
        scratch_shapes=([pltpu.VMEM((tm, tn), F32)] if nk > 1 else []) + (carry.scratch if carry else []),
        compiler_params=_cp(*(("arbitrary",) * 3 if carry else ("parallel", "parallel", "arbitrary"))),
    )(a, b, *([add] if has_add else []), *(carry.arrays if carry else []))
    return (out[0], out[1:]) if carry else out[0]


def _rms_fwd(x, g, name):
    t, tb = x.shape[0], 512

    def body(x_ref, g_ref, o_ref):
        xv = x_ref[...]
        r = lax.rsqrt(jnp.mean(xv * xv, axis=-1, keepdims=True) + EPS)
        o_ref[...] = (xv * r * g_ref[...]).astype(BF16)

    row = pl.BlockSpec((tb, D_MODEL), lambda i: (i, 0))
    return pl.pallas_call(
        body, grid=(t // tb,), name=name,
        in_specs=[row, pl.BlockSpec((1, D_MODEL), lambda i: (0, 0))], out_specs=row,
        out_shape=jax.ShapeDtypeStruct((t, D_MODEL), BF16), compiler_params=_cp("parallel"),
    )(x, g)


def _rms_bwd(x, g, dxn, dres, bf16_copy, name):
    t, tb = x.shape[0], 256

    def body(x_ref, g_ref, dxn_ref, dres_ref, dx_ref, dg_ref, *dxb_ref):
        xv = x_ref[...]
        r = lax.rsqrt(jnp.mean(xv * xv, axis=-1, keepdims=True) + EPS)
        nv = xv * r
        dy = dxn_ref[...]
        dn = dy * g_ref[...]
        dx = dres_ref[...] + r * (dn - nv * jnp.mean(dn * nv, axis=-1, keepdims=True))
        dx_ref[...] = dx
        for ref in dxb_ref:
            ref[...] = dx.astype(BF16)
        part = jnp.sum(dy * nv, axis=0, keepdims=True)

        @pl.when(pl.program_id(0) == 0)
        def _():
            dg_ref[...] = part

        @pl.when(pl.program_id(0) > 0)
        def _():
            dg_ref[...] += part

    row = pl.BlockSpec((tb, D_MODEL), lambda i: (i, 0))
    vec = pl.BlockSpec((1, D_MODEL), lambda i: (0, 0))
    return pl.pallas_call(
        body, grid=(t // tb,), name=name,
        in_specs=[row, vec, row, row], out_specs=[row, vec] + [row] * bf16_copy,
        out_shape=[jax.ShapeDtypeStruct((t, D_MODEL), F32), jax.ShapeDtypeStruct((1, D_MODEL), F32)]
        + [jax.ShapeDtypeStruct((t, D_MODEL), BF16)] * bf16_copy,
        compiler_params=_cp("arbitrary"),
    )(x, g, dxn, dres)


def _loss_head(x, g, target, name):
    t, tb = x.shape[0], 256

    def body(x_ref, g_ref, t_ref, loss_ref, dx_ref, dg_ref, dxb_ref):
        xv, gv = x_ref[...], g_ref[...]
        r = lax.rsqrt(jnp.mean(xv * xv, axis=-1, keepdims=True) + EPS)
        nv = xv * r
        err = nv * gv - t_ref[...]
        lpart = 0.5 * jnp.sum(jnp.mean(err * err, axis=-1, keepdims=True), axis=0, keepdims=True)
        dy = err * (1.0 / D_MODEL)
        dn = dy * gv
        dx = r * (dn - nv * jnp.mean(dn * nv, axis=-1, keepdims=True))
        dx_ref[...] = dx
        dxb_ref[...] = dx.astype(BF16)
        gpart = jnp.sum(dy * nv, axis=0, keepdims=True)

        @pl.when(pl.program_id(0) == 0)
        def _():
            dg_ref[...] = gpart
            loss_ref[...] = jnp.broadcast_to(lpart, (1, LANES))

        @pl.when(pl.program_id(0) > 0)
        def _():
            dg_ref[...] += gpart
            loss_ref[...] += jnp.broadcast_to(lpart, (1, LANES))

    row = pl.BlockSpec((tb, D_MODEL), lambda i: (i, 0))
    vec = pl.BlockSpec((1, D_MODEL), lambda i: (0, 0))
    return pl.pallas_call(
        body, grid=(t // tb,), name=name,
        in_specs=[row, vec, row], out_specs=[pl.BlockSpec((1, LANES), lambda i: (0, 0)), row, vec, row],
        out_shape=[jax.ShapeDtypeStruct((1, LANES), F32), jax.ShapeDtypeStruct((t, D_MODEL), F32),
                   jax.ShapeDtypeStruct((1, D_MODEL), F32), jax.ShapeDtypeStruct((t, D_MODEL), BF16)],
        compiler_params=_cp("arbitrary"),
    )(x, g, target)


A_GW = WIDTH // A_GROUPS


def _gmlp_common(v, lg, lb):
    xc = v - jnp.mean(v, axis=-1, keepdims=True)
    rs = lax.rsqrt(jnp.mean(xc * xc, axis=-1, keepdims=True) + EPS)
    vh = xc * rs
    return rs, vh, (vh * lg + lb).astype(BF16)


def _gmlp_fwd(h, ln_g, ln_b, ws, bs_t, name):
    t, tb = h.shape[0], 256

    def body(u_ref, v_ref, z_ref, lg_ref, lb_ref, ws_ref, bst_ref, y_ref):
        _, _, vn = _gmlp_common(v_ref[...], lg_ref[...], lb_ref[...])
        causal = _iota((CHUNK, CHUNK), 1) <= _iota((CHUNK, CHUNK), 0)
        for g in range(A_GROUPS):
            w = jnp.where(causal, ws_ref[g], 0.0).astype(BF16)
            cols = slice(g * A_GW, (g + 1) * A_GW)
            for c in range(tb // CHUNK):
                rows = slice(c * CHUNK, (c + 1) * CHUNK)
                mixed = jnp.dot(w, vn[rows, cols], preferred_element_type=F32) + bst_ref[:, g:g + 1]
                y_ref[rows, cols] = (_silu(z_ref[rows, cols]) * (u_ref[rows, cols] * mixed)).astype(BF16)

    col = lambda j: pl.BlockSpec((tb, WIDTH), lambda i: (i, j))
    full = lambda a: pl.BlockSpec(a.shape, lambda i: (0,) * a.ndim)
    return pl.pallas_call(
        body, grid=(t // tb,), name=name,
        in_specs=[col(0), col(1), col(2), full(ln_g), full(ln_b), full(ws), full(bs_t)],
        out_specs=col(0), out_shape=jax.ShapeDtypeStruct((t, 2 * WIDTH), BF16),
        compiler_params=_cp("parallel"),
    )(h, h, h, ln_g, ln_b, ws, bs_t)


def _gmlp_bwd(h, dy, ln_g, ln_b, ws, ws_t, bs_t, name):
    t, tb = h.shape[0], 256

    def body(u_ref, v_ref, z_ref, dy_ref, lg_ref, lb_ref, ws_ref, wst_ref, bst_ref,
             dh_ref, dws_ref, dbst_ref, dlg_ref, dlb_ref, dvn_ref):
        @pl.when(pl.program_id(0) == 0)
        def _():
            dws_ref[...] = jnp.zeros_like(dws_ref)
            dbst_ref[...] = jnp.zeros_like(dbst_ref)
            dlg_ref[...] = jnp.zeros_like(dlg_ref)
            dlb_ref[...] = jnp.zeros_like(dlb_ref)

        rs, vh, vn = _gmlp_common(v_ref[...], lg_ref[...], lb_ref[...])
        row, lane = _iota((CHUNK, CHUNK), 0), _iota((CHUNK, CHUNK), 1)
        for g in range(A_GROUPS):
            w = jnp.where(lane <= row, ws_ref[g], 0.0).astype(BF16)
            wt = jnp.where(row <= lane, wst_ref[g], 0.0).astype(BF16)
            cols = slice(g * A_GW, (g + 1) * A_GW)
            dws_acc = jnp.zeros((CHUNK, CHUNK), F32)
            dbs_acc = jnp.zeros((CHUNK, 1), F32)
            for c in range(tb // CHUNK):
                rows = slice(c * CHUNK, (c + 1) * CHUNK)
                vnb = vn[rows, cols]
                mixed = jnp.dot(w, vnb, preferred_element_type=F32) + bst_ref[:, g:g + 1]
                u, z, dyv = u_ref[rows, cols], z_ref[rows, cols], dy_ref[rows, cols]
                sz = _silu(z)
                dh_ref[rows, cols] = (dyv * sz * mixed).astype(BF16)
                dh_ref[rows, slice(2 * WIDTH + g * A_GW, 2 * WIDTH + (g + 1) * A_GW)] = (
                    dyv * (u * mixed) * _dsilu(z)).astype(BF16)
                dm = dyv * sz * u
                dws_acc += _dot_nt(dm, vnb)
                dbs_acc += jnp.sum(dm, axis=1, keepdims=True)
                dvn_ref[rows, cols] = jnp.dot(wt, dm.astype(BF16), preferred_element_type=F32)
            dws_ref[g] += jnp.where(lane <= row, dws_acc, 0.0)
            dbst_ref[...] += jnp.where(lane == g, dbs_acc, 0.0)
        dvn = dvn_ref[...]
        dlg_ref[...] += jnp.sum(dvn * vh, axis=0, keepdims=True)
        dlb_ref[...] += jnp.sum(dvn, axis=0, keepdims=True)
        dvh = dvn * lg_ref[...]
        dv = rs * (dvh - jnp.mean(dvh, axis=-1, keepdims=True) - vh * jnp.mean(dvh * vh, axis=-1, keepdims=True))
        dh_ref[:, WIDTH:2 * WIDTH] = dv.astype(BF16)

    col = lambda j: pl.BlockSpec((tb, WIDTH), lambda i: (i, j))
    full = lambda a: pl.BlockSpec(a.shape, lambda i: (0,) * a.ndim)
    acc = lambda shape: pl.BlockSpec(shape, lambda i: (0,) * len(shape))
    return pl.pallas_call(
        body, grid=(t // tb,), name=name,
        in_specs=[col(0), col(1), col(2), col(0), full(ln_g), full(ln_b), full(ws), full(ws_t), full(bs_t)],
        out_specs=[pl.BlockSpec((tb, 3 * WIDTH), lambda i: (i, 0)), acc((A_GROUPS, CHUNK, CHUNK)),
                   acc((CHUNK, LANES)), acc((1, WIDTH)), acc((1, WIDTH))],
        out_shape=[jax.ShapeDtypeStruct((t, F_EVEN), BF16), jax.ShapeDtypeStruct((A_GROUPS, CHUNK, CHUNK), F32),
                   jax.ShapeDtypeStruct((CHUNK, LANES), F32), jax.ShapeDtypeStruct((1, WIDTH), F32),
                   jax.ShapeDtypeStruct((1, WIDTH), F32)],
        scratch_shapes=[pltpu.VMEM((tb, WIDTH), F32)],
        compiler_params=_cp("arbitrary"),
    )(h, h, h, dy, ln_g, ln_b, ws, ws_t, bs_t)


def _halo_prev(tb, j):
    return lambda i: (jnp.maximum(i * (tb // SUBLANES) - 1, 0), j)


def _halo_next(tb, j, t):
    return lambda i: (jnp.minimum((i + 1) * (tb // SUBLANES), t // SUBLANES - 1), j)


def _row_select(parts, width):
    row = _iota((SUBLANES, width), 0)
    out = jnp.zeros((SUBLANES, width), F32)
    for k, p in enumerate(parts):
        out = jnp.where(row == k, p, out)
    return out


def _sconv_fwd(h, w, name):
    t, tb = h.shape[0], 256

    def body(bg_ref, cg_ref, hx_ref, z_ref, cgh_ref, hxh_ref, w_ref, y_ref):
        p = cg_ref[...] * hx_ref[...]
        ph = jnp.where(pl.program_id(0) > 0, cgh_ref[...] * hxh_ref[...], 0.0)
        cv = w_ref[2:3, :] * p + w_ref[1:2, :] * _shift_down(p, ph, 1) + w_ref[0:1, :] * _shift_down(p, ph, 2)
        y_ref[...] = (_silu(z_ref[...]) * (bg_ref[...] * cv)).astype(BF16)

    col = lambda j: pl.BlockSpec((tb, WIDTH), lambda i: (i, j))
    halo = lambda j: pl.BlockSpec((SUBLANES, WIDTH), _halo_prev(tb, j))
    return pl.pallas_call(
        body, grid=(t // tb,), name=name,
        in_specs=[col(0), col(1), col(2), col(3), halo(1), halo(2), pl.BlockSpec(w.shape, lambda i: (0, 0))],
        out_specs=col(0), out_shape=jax.ShapeDtypeStruct((t, 2 * WIDTH), BF16),
        compiler_params=_cp("parallel"),
    )(h, h, h, h, h, h, w)


def _sconv_bwd(h, dy, w, name):
    t, tb = h.shape[0], 256
    nb = t // tb

    def body(bg_ref, cg_ref, hx_ref, z_ref, dy_ref, cgh_ref, hxh_ref, bgn_ref, zn_ref, dyn_ref, w_ref, dh_ref, dw_ref):
        i = pl.program_id(0)
        bg, cg, hx, z, dyv = bg_ref[...], cg_ref[...], hx_ref[...], z_ref[...], dy_ref[...]
        p = cg * hx
        ph = jnp.where(i > 0, cgh_ref[...] * hxh_ref[...], 0.0)
        p1, p2 = _shift_down(p, ph, 1), _shift_down(p, ph, 2)
        cv = w_ref[2:3, :] * p + w_ref[1:2, :] * p1 + w_ref[0:1, :] * p2
        sz = _silu(z)
        dcv = dyv * sz * bg
        dcvn = jnp.where(i < nb - 1, dyn_ref[...] * _silu(zn_ref[...]) * bgn_ref[...], 0.0)
        dp = w_ref[2:3, :] * dcv + w_ref[1:2, :] * _shift_up(dcv, dcvn, 1) + w_ref[0:1, :] * _shift_up(dcv, dcvn, 2)
        dh_ref[0] = (dyv * sz * cv).astype(BF16)
        dh_ref[1] = (dp * hx).astype(BF16)
        dh_ref[2] = (dp * cg).astype(BF16)
        dh_ref[3] = (dyv * (bg * cv) * _dsilu(z)).astype(BF16)
        part = _row_select([jnp.sum(dcv * q, axis=0, keepdims=True) for q in (p2, p1, p)], WIDTH)

        @pl.when(i == 0)
        def _():
            dw_ref[...] = part

        @pl.when(i > 0)
        def _():
            dw_ref[...] += part

    col = lambda j: pl.BlockSpec((tb, WIDTH), lambda i: (i, j))
    prev = lambda j: pl.BlockSpec((SUBLANES, WIDTH), _halo_prev(tb, j))
    nxt = lambda j: pl.BlockSpec((SUBLANES, WIDTH), _halo_next(tb, j, t))
    return pl.pallas_call(
        body, grid=(nb,), name=name,
        in_specs=[col(0), col(1), col(2), col(3), col(0), prev(1), prev(2), nxt(0), nxt(3), nxt(0),
                  pl.BlockSpec(w.shape, lambda i: (0, 0))],
        out_specs=[pl.BlockSpec((4, tb, WIDTH), lambda i: (0, i, 0)), pl.BlockSpec((SUBLANES, WIDTH), lambda i: (0, 0))],
        out_shape=[jax.ShapeDtypeStruct((8, t, WIDTH), BF16), jax.ShapeDtypeStruct((SUBLANES, WIDTH), F32)],
        compiler_params=_cp("arbitrary"),
    )(h, h, h, h, dy, h, h, h, h, dy, w)


XBC_COL0 = 4 * WIDTH


def _ssd_conv_fwd(h, w, b, name):
    t, tb = h.shape[0], 256

    def body(x_ref, xh_ref, w_ref, b_ref, o_ref):
        xv = x_ref[...]
        xh = jnp.where(pl.program_id(0) > 0, xh_ref[...], 0.0)
        acc = b_ref[...] + w_ref[3:4, :] * xv
        for j in range(1, B_CONV):
            acc = acc + w_ref[B_CONV - 1 - j:B_CONV - j, :] * _shift_down(xv, xh, j)
        o_ref[...] = acc

    cb = XBC_COL0 // B_XBC
    return pl.pallas_call(
        body, grid=(t // tb,), name=name,
        in_specs=[pl.BlockSpec((tb, B_XBC), lambda i: (i, cb)), pl.BlockSpec((SUBLANES, B_XBC), _halo_prev(tb, cb)),
                  pl.BlockSpec(w.shape, lambda i: (0, 0)), pl.BlockSpec(b.shape, lambda i: (0, 0))],
        out_specs=pl.BlockSpec((tb, B_XBC), lambda i: (i, 0)), out_shape=jax.ShapeDtypeStruct((t, B_XBC), F32),
        compiler_params=_cp("parallel"),
    )(h, h, w, b)


def _ssd_conv_bwd(h, dh, dpx, dpb, dpc, w, name):
    t, tb, tc = h.shape[0], 512, 1024
    nb = t // tb
    r8 = tb // SUBLANES
    last8 = t // SUBLANES - 1

    def body(dpx_ref, dpb_ref, dpc_ref, nx_ref, nb_ref, nc_ref, x_ref, xh_ref, w_ref, dh_in, dh_ref, dw_ref, db_ref):
        j, i = pl.program_id(0), pl.program_id(1)
        pick = lambda a, b_, c: jnp.where(j < 2, a[...], jnp.where(j == 2, b_[...], c[...]))
        dp = pick(dpx_ref, dpb_ref, dpc_ref)
        dn = jnp.where(i < nb - 1, pick(nx_ref, nb_ref, nc_ref), 0.0)
        xv = x_ref[...]
        xh = jnp.where(i > 0, xh_ref[...], 0.0)
        dx = w_ref[3:4, :] * dp
        for s in range(1, B_CONV):
            dx = dx + w_ref[B_CONV - 1 - s:B_CONV - s, :] * _shift_up(dp, dn, s)
        dh_ref[...] = dx.astype(BF16)
        wpart = _row_select([jnp.sum(dp * _shift_down(xv, xh, B_CONV - 1 - k), axis=0, keepdims=True)
                             for k in range(B_CONV)], tc)
        bpart = jnp.sum(dp, axis=0, keepdims=True)

        @pl.when(i == 0)
        def _():
            dw_ref[...] = wpart
            db_ref[...] = bpart

        @pl.when(i > 0)
        def _():
            dw_ref[...] += wpart
            db_ref[...] += bpart

    def src(blk_rows, rowf, sel, colf):
        return pl.BlockSpec((blk_rows, tc), lambda j, i: (jnp.where(sel(j), rowf(i), 0), colf(j)))

    cur = lambda i: i
    nxt = lambda i: jnp.minimum((i + 1) * r8, last8)
    is_x, is_b, is_c = (lambda j: j < 2), (lambda j: j == 2), (lambda j: j == 3)
    xcol, zero = (lambda j: jnp.minimum(j, 1)), (lambda j: 0)
    c0 = XBC_COL0 // tc
    return pl.pallas_call(
        body, grid=(B_XBC // tc, nb), name=name,
        in_specs=[src(tb, cur, is_x, xcol), src(tb, cur, is_b, zero), src(tb, cur, is_c, zero),
                  src(SUBLANES, nxt, is_x, xcol), src(SUBLANES, nxt, is_b, zero), src(SUBLANES, nxt, is_c, zero),
                  pl.BlockSpec((tb, tc), lambda j, i: (i, c0 + j)),
                  pl.BlockSpec((SUBLANES, tc), lambda j, i: (jnp.maximum(i * r8 - 1, 0), c0 + j)),
                  pl.BlockSpec((B_CONV, tc), lambda j, i: (0, j)), ANY],
        out_specs=[pl.BlockSpec((tb, tc), lambda j, i: (i, c0 + j)), pl.BlockSpec((SUBLANES, tc), lambda j, i: (0, j)),
                   pl.BlockSpec((1, tc), lambda j, i: (0, j))],
        out_shape=[jax.ShapeDtypeStruct(dh.shape, dh.dtype), jax.ShapeDtypeStruct((SUBLANES, B_XBC), F32),
                   jax.ShapeDtypeStruct((1, B_XBC), F32)],
        input_output_aliases={9: 0},
        compiler_params=_cp("arbitrary", "arbitrary"),
    )(dpx, dpb, dpc, dpx, dpb, dpc, h, h, w, dh)


HEADS_PER_GROUP = B_HEADS // B_GROUPS


def _ssd_dt(hdt, alog, dtb, name):
    t = hdt.shape[0]
    nc = t // CHUNK

    def body(dtr_ref, alog_ref, dtb_ref, dt_ref, acs_ref, acst_ref):
        dt = _softplus(dtr_ref[...] + dtb_ref[...])
        tri = (_iota((CHUNK, CHUNK), 1) <= _iota((CHUNK, CHUNK), 0)).astype(BF16)
        acs = _sel_dot(tri, dt * -jnp.exp(alog_ref[...]))
        dt_ref[...] = dt
        acs_ref[...] = acs
        acst_ref[...] = acs.T

    tok = pl.BlockSpec((CHUNK, LANES), lambda c: (c, 0))
    vec = pl.BlockSpec((1, LANES), lambda c: (0, 0))
    return pl.pallas_call(
        body, grid=(nc,), name=name, in_specs=[tok, vec, vec],
        out_specs=[tok, tok, pl.BlockSpec((None, LANES, CHUNK), lambda c: (c, 0, 0))],
        out_shape=[jax.ShapeDtypeStruct((t, LANES), F32), jax.ShapeDtypeStruct((t, LANES), F32),
                   jax.ShapeDtypeStruct((nc, LANES, CHUNK), F32)],
        compiler_params=_cp("parallel"),
    )(hdt, alog, dtb)


def _to_group(m, g):
    return pltpu.roll(m, (LANES - HEADS_PER_GROUP * g) % LANES, 1)


def _from_group(m, g):
    return pltpu.roll(m, HEADS_PER_GROUP * g, 1)


def _ssd_group_terms(g, px, pb, pc, dt, acs, acst_ref):
    head = _iota((CHUNK, B_GROUP_W), 1) // B_HEAD_DIM

    def spread(m4):
        out = m4[:, HEADS_PER_GROUP - 1:HEADS_PER_GROUP]
        for j in range(HEADS_PER_GROUP - 2, -1, -1):
            out = jnp.where(head == j, m4[:, j:j + 1], out)
        return out

    dt4, a4 = _to_group(dt, g), _to_group(acs, g)
    rows = [acst_ref[pl.ds(HEADS_PER_GROUP * g + j, 1), :] for j in range(HEADS_PER_GROUP)]
    return dict(xs=_silu(px), bm=_silu(pb), cm=_silu(pc), dt4=dt4, a4=a4, rows=rows, dt_e=spread(dt4), a_e=spread(a4))


def _ssd_decay(tm, j, transposed):
    col, row = tm["a4"][:, j:j + 1], tm["rows"][j]
    lane, sub = _iota((CHUNK, CHUNK), 1), _iota((CHUNK, CHUNK), 0)
    if transposed:
        return jnp.where(sub <= lane, jnp.exp(jnp.minimum(row - col, 0.0)), 0.0)
    return jnp.where(lane <= sub, jnp.exp(jnp.minimum(col - row, 0.0)), 0.0)


GROUPS_PER_STEP = 4


def _ssd_specs(nc, rev):
    ch = (lambda c: nc - 1 - c) if rev else (lambda c: c)
    n = GROUPS_PER_STEP
    gw = lambda off: pl.BlockSpec((CHUNK, n * B_GROUP_W), lambda c, g: (ch(c), off // n + g))
    gn = lambda off: pl.BlockSpec((CHUNK, n * B_STATE), lambda c, g: (ch(c), off // n + g))
    tok = pl.BlockSpec((CHUNK, LANES), lambda c, g: (ch(c), 0))
    vec = pl.BlockSpec((1, LANES), lambda c, g: (0, 0))
    gvec = pl.BlockSpec((1, n * B_GROUP_W), lambda c, g: (0, g))
    st = pl.BlockSpec((None, B_STATE, n * B_GROUP_W), lambda c, g: (ch(c), 0, g))
    tokt = pl.BlockSpec((None, LANES, CHUNK), lambda c, g: (ch(c), 0, 0))
    return gw, gn, tok, tokt, vec, gvec, st


def _group_cols(u):
    return slice(u * B_GROUP_W, (u + 1) * B_GROUP_W), slice(u * B_STATE, (u + 1) * B_STATE)


def _ssd_scan_fwd(pre, dt, acs, acst, h, y, dfull, ng, carry, name):
    t = pre.shape[0]
    nc = t // CHUNK
    n_carried = len(carry.arrays) if carry else 0

    def body(*refs):
        px_ref, pb_ref, pc_ref, dt_ref, acs_ref, acst_ref, z_ref, df_ref, ng_ref, _ = refs[:10]
        refs = refs[10:]
        c_in, refs = refs[:n_carried], refs[n_carried:]
        y_ref, ypre_ref, st_ref = refs[:3]
        c_out, state_ref, c_sems = refs[3:3 + n_carried], refs[3 + n_carried], refs[4 + n_carried:]
        c, pair = pl.program_id(0), pl.program_id(1)
        if carry:
            pl.when((c == 0) & (pair == 0))(lambda: carry.start(c_in, c_out, c_sems))
        groups = [GROUPS_PER_STEP * pair + u for u in range(GROUPS_PER_STEP)]

        @pl.when(c == 0)
        def _():
            for g in groups:
                state_ref[g] = jnp.zeros((B_STATE, B_GROUP_W), F32)

        entering, leaving = [state_ref[g] for g in groups], []
        for u, g in enumerate(groups):
            wide, narrow = _group_cols(u)
            tm = _ssd_group_terms(g, px_ref[:, wide], pb_ref[:, narrow], pc_ref[:, narrow], dt_ref[...],
                                  acs_ref[...], acst_ref)
            xs, bm, cm, a_e = tm["xs"], tm["bm"], tm["cm"], tm["a_e"]
            xdt = xs * tm["dt_e"]
            cb = _dot_nt(cm, bm)
            head = _iota((CHUNK, B_GROUP_W), 1) // B_HEAD_DIM
            yd = jnp.zeros((CHUNK, B_GROUP_W), F32)
            for j in range(HEADS_PER_GROUP):
                yd = jnp.where(head == j, _dot(cb * _ssd_decay(tm, j, False), xdt), yd)
            st = entering[u]
            st_ref[:, wide] = st
            yv = yd + jnp.exp(a_e) * _dot(cm, st) + df_ref[:, wide] * xs
            a_last = a_e[CHUNK - 1:CHUNK, :]
            leaving.append(st * jnp.exp(a_last) + _dot_tn(bm, xdt * jnp.exp(a_last - a_e)))
            ypre_ref[:, wide] = yv
            yz = yv * _silu(z_ref[:, wide])
            r = lax.rsqrt(jnp.mean(yz * yz, axis=-1, keepdims=True) + EPS)
            y_ref[:, wide] = (yz * r * ng_ref[:, wide]).astype(BF16)
        for g, st in zip(groups, leaving):
            state_ref[g] = st
        if carry:
            last = (c == nc - 1) & (pair == B_GROUPS // GROUPS_PER_STEP - 1)
            pl.when(last)(lambda: carry.finish(c_in, c_out, c_sems))

    gw, gn, tok, tokt, vec, gvec, st = _ssd_specs(nc, False)
    out = pl.pallas_call(
        body, grid=(nc, B_GROUPS // GROUPS_PER_STEP), name=name,
        in_specs=[gw(0), gn(WIDTH // B_STATE), gn((WIDTH + B_GROUPS * B_STATE) // B_STATE), tok, tok, tokt,
                  gw(3 * WIDTH // B_GROUP_W), gvec, gvec, ANY] + [ANY] * n_carried,
        out_specs=[gw(WIDTH // B_GROUP_W), gw(0), st] + [ANY] * n_carried,
        out_shape=[jax.ShapeDtypeStruct(y.shape, y.dtype), jax.ShapeDtypeStruct((t, WIDTH), F32),
                   jax.ShapeDtypeStruct((nc, B_STATE, WIDTH), F32)] + (carry.out_shape if carry else []),
        scratch_shapes=[pltpu.VMEM((B_GROUPS, B_STATE, B_GROUP_W), F32)] + (carry.scratch if carry else []),
        input_output_aliases={9: 0},
        compiler_params=_cp("arbitrary", "arbitrary"),
    )(pre, pre, pre, dt, acs, acst, h, dfull, ng, y, *(carry.arrays if carry else []))
    return out[:3], out[3:]


def _ssd_scan_bwd(pre, hdt, dt, acs, acst, h, dy, ypre, states, dh, alog, dtb, dfull, ng, carry, name):
    t = pre.shape[0]
    nc = t // CHUNK

    n_carried = len(carry.arrays) if carry else 0

    def one_group(g, dsn, px_ref, pb_ref, pc_ref, dtr_ref, dt_ref, acs_ref, acst_ref, z_ref, dy_ref, ypre_ref, st_ref,
                  alog_ref, dtb_ref, df_ref, ng_ref, dz_ref, dpx_ref, dpb_ref, dpc_ref):
        px, pb, pc, dtr = px_ref[...], pb_ref[...], pc_ref[...], dtr_ref[...]
        tm = _ssd_group_terms(g, px, pb, pc, dt_ref[...], acs_ref[...], acst_ref)
        xs, bm, cm, a_e, dt_e = tm["xs"], tm["bm"], tm["cm"], tm["a_e"], tm["dt_e"]
        xdt = xs * dt_e
        head = _iota((CHUNK, B_GROUP_W), 1) // B_HEAD_DIM

        z, yv, ngv = z_ref[...], ypre_ref[...], ng_ref[...]
        sz = _silu(z)
        yz = yv * sz
        r = lax.rsqrt(jnp.mean(yz * yz, axis=-1, keepdims=True) + EPS)
        dyn = dy_ref[...]
        dng_part = jnp.sum(dyn * yz * r, axis=0, keepdims=True)
        q = dyn * ngv
        dyz = r * q - yz * (r * r * r) * jnp.mean(q * yz, axis=-1, keepdims=True)
        dyv = dyz * sz
        dz_ref[...] = (dyz * yv * _dsilu(z)).astype(BF16)
        dd_part = jnp.sum(dyv * xs, axis=0, keepdims=True)
        dxs = df_ref[...] * dyv

        st = st_ref[...]
        ea = jnp.exp(a_e)
        ead = ea * dyv
        dcm = _dot_nt(ead, st)
        da_e = dyv * (ea * _dot(cm, st))

        a_last = a_e[CHUNK - 1:CHUNK, :]
        ea_last = jnp.exp(a_last)
        dstate = dsn * ea_last + _dot_tn(cm, ead)
        wdec = jnp.exp(a_last - a_e)
        xw = xdt * wdec
        dxw = _dot(bm, dsn)
        dxdt = dxw * wdec
        dbm = _dot_nt(xw, dsn)
        zc = dxw * xw
        da_last = jnp.sum(zc, axis=0, keepdims=True) + jnp.sum(dsn * st, axis=0, keepdims=True) * ea_last
        da_e = da_e - zc + jnp.where(_iota((CHUNK, B_GROUP_W), 0) == CHUNK - 1, da_last, 0.0)

        cb, cbt = _dot_nt(cm, bm), _dot_nt(bm, cm)
        dcb, dcbt = jnp.zeros((CHUNK, CHUNK), F32), jnp.zeros((CHUNK, CHUNK), F32)
        da4 = jnp.zeros((CHUNK, LANES), F32)
        lane = _iota((CHUNK, LANES), 1)
        for j in range(HEADS_PER_GROUP):
            mine = head == j
            gm = _dot_nt(jnp.where(mine, dyv, 0.0), xdt)
            gmt = _dot_nt(jnp.where(mine, xdt, 0.0), dyv)
            dec, dect = _ssd_decay(tm, j, False), _ssd_decay(tm, j, True)
            dcb += gm * dec
            dcbt += gmt * dect
            da_j = (jnp.sum(gm * cb * dec, axis=1, keepdims=True) - jnp.sum(gmt * cbt * dect, axis=1, keepdims=True))
            da4 = jnp.where(lane == j, da_j, da4)
            dxdt = dxdt + jnp.where(mine, _dot(cbt * dect, dyv), 0.0)
        dcm = dcm + _dot(dcb, bm)
        dbm = dbm + _dot(dcbt, cm)

        gather = (_iota((B_GROUP_W, LANES), 0) // B_HEAD_DIM == _iota((B_GROUP_W, LANES), 1)).astype(BF16)
        per_head = _dot_sel(jnp.concatenate([da_e, dxdt * xs], axis=0), gather)
        da4 = da4 + per_head[:CHUNK]
        rtri = (_iota((CHUNK, CHUNK), 1) >= _iota((CHUNK, CHUNK), 0)).astype(BF16)
        dadt4 = _sel_dot(rtri, da4)
        a_heads = -jnp.exp(alog_ref[...])
        rows8 = lambda v: jnp.broadcast_to(v, (SUBLANES, LANES))
        ddt4 = dadt4 * _to_group(rows8(a_heads), g)[0:1, :] + per_head[CHUNK:]
        dxs = dxs + dxdt * dt_e
        ddt = _from_group(ddt4, g) * _sig(dtr + dtb_ref[...])
        da_heads = _from_group(rows8(jnp.sum(dadt4 * tm["dt4"], axis=0, keepdims=True)), g)[0:1, :]

        dpx_ref[...] = dxs * _dsilu(px)
        dpb_ref[...] = dbm * _dsilu(pb)
        dpc_ref[...] = dcm * _dsilu(pc)
        return ddt, da_heads * a_heads, dstate, dd_part, dng_part

    def body(*refs):
        (px_ref, pb_ref, pc_ref, dtr_ref, dt_ref, acs_ref, acst_ref, z_ref, dy_ref, ypre_ref, st_ref, alog_ref,
         dtb_ref, df_ref, ng_ref, _) = refs[:16]
        refs = refs[16:]
        c_in, refs = refs[:n_carried], refs[n_carried:]
        dz_ref, dpx_ref, dpb_ref, dpc_ref, ddt_ref, dbias_ref, dalog_ref, dd_ref, dng_ref = refs[:9]
        c_out, dstate_ref, c_sems = refs[9:9 + n_carried], refs[9 + n_carried], refs[10 + n_carried:]
        c, pair = pl.program_id(0), pl.program_id(1)
        if carry:
            pl.when((c == 0) & (pair == 0))(lambda: carry.start(c_in, c_out, c_sems))
        groups = [GROUPS_PER_STEP * pair + u for u in range(GROUPS_PER_STEP)]

        @pl.when(c == 0)
        def _():
            for g in groups:
                dstate_ref[g] = jnp.zeros((B_STATE, B_GROUP_W), F32)
                dd_ref[g] = jnp.zeros((1, B_GROUP_W), F32)
                dng_ref[g] = jnp.zeros((1, B_GROUP_W), F32)

        leaving = [dstate_ref[g] for g in groups]
        ddt, dalog, done = 0.0, 0.0, []
        for u, g in enumerate(groups):
            wide, narrow = _group_cols(u)
            view = lambda ref, cols: ref.at[:, cols]
            ddt_u, dalog_u, *carried = one_group(
                g, leaving[u], view(px_ref, wide), view(pb_ref, narrow), view(pc_ref, narrow), dtr_ref,
                dt_ref, acs_ref, acst_ref, view(z_ref, wide), view(dy_ref, wide), view(ypre_ref, wide), view(st_ref, wide),
                alog_ref, dtb_ref, view(df_ref, wide), view(ng_ref, wide), view(dz_ref, wide), view(dpx_ref, wide),
                view(dpb_ref, narrow), view(dpc_ref, narrow))
            ddt, dalog = ddt + ddt_u, dalog + dalog_u
            done.append(carried)
        for g, (dstate, dd_part, dng_part) in zip(groups, done):
            dstate_ref[g] = dstate
            dd_ref[g] += dd_part
            dng_ref[g] += dng_part
        first = (c == 0) & (pair == 0)
        bias_part = jnp.sum(ddt, axis=0, keepdims=True)

        @pl.when(pair == 0)
        def _():
            ddt_ref[...] = ddt

        @pl.when(pair > 0)
        def _():
            ddt_ref[...] += ddt

        @pl.when(first)
        def _():
            dbias_ref[...] = bias_part
            dalog_ref[...] = dalog

        @pl.when(jnp.logical_not(first))
        def _():
            dbias_ref[...] += bias_part
            dalog_ref[...] += dalog

        if carry:
            last = (c == nc - 1) & (pair == B_GROUPS // GROUPS_PER_STEP - 1)
            pl.when(last)(lambda: carry.finish(c_in, c_out, c_sems))

    gw, gn, tok, tokt, vec, gvec, st = _ssd_specs(nc, True)
    acc = lambda shape: pl.BlockSpec(shape, lambda c, g: (0,) * len(shape))
    rev = lambda c: nc - 1 - c
    out = pl.pallas_call(
        body, grid=(nc, B_GROUPS // GROUPS_PER_STEP), name=name,
        in_specs=[gw(0), gn(WIDTH // B_STATE), gn((WIDTH + B_GROUPS * B_STATE) // B_STATE), tok, tok, tok, tokt,
                  gw(3 * WIDTH // B_GROUP_W), gw(WIDTH // B_GROUP_W), gw(0), st, vec, vec, gvec, gvec, ANY]
        + [ANY] * n_carried,
        out_specs=[gw(3 * WIDTH // B_GROUP_W), gw(0), gn(0), gn(0), tok, vec, vec, acc((B_GROUPS, 1, B_GROUP_W)), acc((B_GROUPS, 1, B_GROUP_W))] + [ANY] * n_carried,
        out_shape=[jax.ShapeDtypeStruct(dh.shape, dh.dtype), jax.ShapeDtypeStruct((t, WIDTH), F32),
                   jax.ShapeDtypeStruct((t, B_GROUPS * B_STATE), F32), jax.ShapeDtypeStruct((t, B_GROUPS * B_STATE), F32),
                   jax.ShapeDtypeStruct((t, LANES), F32), jax.ShapeDtypeStruct((1, LANES), F32),
                   jax.ShapeDtypeStruct((1, LANES), F32), jax.ShapeDtypeStruct((B_GROUPS, 1, B_GROUP_W), F32),
                   jax.ShapeDtypeStruct((B_GROUPS, 1, B_GROUP_W), F32)] + (carry.out_shape if carry else []),
        scratch_shapes=[pltpu.VMEM((B_GROUPS, B_STATE, B_GROUP_W), F32)] + (carry.scratch if carry else []),
        input_output_aliases={15: 0},
        compiler_params=_cp("arbitrary", "arbitrary"),
    )(pre, pre, pre, hdt, dt, acs, acst, h, dy, ypre, states, alog, dtb, dfull, ng, dh,
      *(carry.arrays if carry else []))
    return out[:9], out[9:]


Q_COL, K_COL, V_COL, Z_COL = [(4 + i) * WIDTH // D_HEAD_DIM for i in range(4)]
ATT_SCALE = D_HEAD_DIM ** -0.5


SPAN = 2048


def _att_blocks():
    out = []
    for _, dil in D_PATTERNS:
        nbl = SPAN // (CHUNK * dil)
        for r in range(dil):
            for bl in range(nbl):
                st = r + dil * CHUNK * bl
                out.append((dil, st, bl > 0, st - dil * CHUNK if bl > 0 else r + dil * CHUNK * (nbl - 1)))
    return out


def _rows(start, dil):
    return pl.ds(start, CHUNK) if dil == 1 else pl.ds(start, CHUNK, stride=dil)


def _att_keys(kc_ref, kp_ref, blk):
    dil, st, inside, pst = blk
    prev = (kc_ref if inside else kp_ref)[_rows(pst, dil), :]
    return jnp.concatenate([prev, kc_ref[_rows(st, dil), :]], axis=0).astype(BF16)


def _att_band(span_index):
    lane, sub = _iota((CHUNK, 2 * CHUNK), 1), _iota((CHUNK, 2 * CHUNK), 0)
    band = (lane >= sub) & (lane <= sub + CHUNK)
    return band, band & ((lane >= CHUNK) | (span_index > 0))


def _att_specs(t):
    blk = lambda off: pl.BlockSpec((SPAN, D_HEAD_DIM), lambda hd, sb: (sb, off + hd))
    prev = lambda off: pl.BlockSpec((SPAN, D_HEAD_DIM), lambda hd, sb: (jnp.maximum(sb - 1, 0), off + hd))
    lse = pl.BlockSpec((None, SPAN // CHUNK, CHUNK), lambda hd, sb: (hd, sb, 0))
    return blk, prev, lse


def _attn2_fwd(h, y, name):
    t = h.shape[0]

    def body(q_ref, kc_ref, vc_ref, kp_ref, vp_ref, z_ref, y_in, y_ref, o_ref, lse_ref, m_ref, l_ref):
        band, band_first = _att_band(pl.program_id(1))
        for blk in _att_blocks():
            dil, st, inside, _ = blk
            rows = _rows(st, dil)
            k2, v2 = _att_keys(kc_ref, kp_ref, blk), _att_keys(vc_ref, vp_ref, blk)
            s = jnp.where(band if inside else band_first, _dot_nt(q_ref[rows, :], k2) * ATT_SCALE, NEG)
            m_b = jnp.max(s, axis=1, keepdims=True)
            p = jnp.exp(s - m_b)
            l_b = jnp.sum(p, axis=1, keepdims=True)
            o_b = _dot(p, v2)
            wide = lambda a: jnp.broadcast_to(a, (CHUNK, D_HEAD_DIM))
            if dil == 1:
                m_ref[rows, :], l_ref[rows, :], o_ref[rows, :] = wide(m_b), wide(l_b), o_b
            else:
                m_o = m_ref[rows, :]
                m_n = jnp.maximum(m_o, m_b)
                a_o, a_b = jnp.exp(m_o - m_n), jnp.exp(m_b - m_n)
                m_ref[rows, :] = m_n
                l_ref[rows, :] = a_o * l_ref[rows, :] + a_b * l_b
                o_ref[rows, :] = a_o * o_ref[rows, :] + a_b * o_b
        l = l_ref[...]
        o = o_ref[...] / l
        o_ref[...] = o
        y_ref[...] = (_silu(z_ref[...]) * o).astype(BF16)
        for i in range(SPAN // CHUNK):
            blk_rows = slice(i * CHUNK, (i + 1) * CHUNK)
            lse_ref[i:i + 1, :] = (m_ref[blk_rows, :] + jnp.log(l[blk_rows, :])).T[0:1, :]

    blk, prev, lse_spec = _att_specs(t)
    return pl.pallas_call(
        body, grid=(D_HEADS, t // SPAN), name=name,
        in_specs=[blk(Q_COL), blk(K_COL), blk(V_COL), prev(K_COL), prev(V_COL), blk(Z_COL), ANY],
        out_specs=[blk(WIDTH // D_HEAD_DIM), blk(0), lse_spec],
        out_shape=[jax.ShapeDtypeStruct(y.shape, y.dtype), jax.ShapeDtypeStruct((t, WIDTH), F32),
                   jax.ShapeDtypeStruct((D_HEADS, t // CHUNK, CHUNK), F32)],
        scratch_shapes=[pltpu.VMEM((SPAN, D_HEAD_DIM), F32)] * 2,
        input_output_aliases={6: 0},
        compiler_params=_cp("parallel", "parallel"),
    )(h, h, h, h, h, h, y)


def _attn2_bwd(h, dy, o, lse, dh, name):
    t = h.shape[0]
    ns = t // SPAN

    def body(q_ref, kc_ref, vc_ref, kp_ref, vp_ref, z_ref, dy_ref, o_ref, lse_ref, dh_in, dh_ref,
             acc_ref, dq_ref, do_ref, delta_ref, lsec_ref):
        sb = pl.program_id(1)

        @pl.when(sb == 0)
        def _():
            acc_ref[...] = jnp.zeros_like(acc_ref)

        here, before = pl.multiple_of(sb * SPAN, SPAN), jnp.maximum(sb - 1, 0) * SPAN
        z, ov, dyv = z_ref[...], o_ref[...], dy_ref[...]
        do = dyv * _silu(z)
        do_ref[...] = do
        dh_ref[3, pl.ds(here, SPAN), :] = (dyv * ov * _dsilu(z)).astype(BF16)
        delta_ref[...] = jnp.broadcast_to(jnp.sum(do * ov, axis=1, keepdims=True), (SPAN, D_HEAD_DIM))
        for i in range(SPAN // CHUNK):
            lsec_ref[i * CHUNK:(i + 1) * CHUNK, :] = jnp.broadcast_to(lse_ref[i:i + 1, :], (CHUNK, CHUNK)).T
        band, band_first = _att_band(sb)
        for blk in _att_blocks():
            dil, st, inside, pst = blk
            rows = _rows(st, dil)
            q = q_ref[rows, :].astype(BF16)
            k2, v2 = _att_keys(kc_ref, kp_ref, blk), _att_keys(vc_ref, vp_ref, blk)
            dob = do_ref[rows, :].astype(BF16)
            s = jnp.where(band if inside else band_first, _dot_nt(q, k2) * ATT_SCALE, NEG)
            p = jnp.exp(s - lsec_ref[rows, :][:, 0:1])
            ds = p * (_dot_nt(dob, v2) - delta_ref[rows, :][:, 0:1]) * ATT_SCALE
            dq_b = _dot(ds, k2)
            if dil == 1:
                dq_ref[rows, :] = dq_b
            else:
                dq_ref[rows, :] += dq_b
            dk2, dv2 = _dot_tn(ds, q), _dot_tn(p, dob)
            own = _rows(pl.multiple_of(here + st, CHUNK) if dil == 1 else here + st, dil)
            pbase = (here if inside else before) + pst
            prv = _rows(pl.multiple_of(pbase, CHUNK) if dil == 1 else pbase, dil)
            acc_ref[0, own, :] += dk2[CHUNK:]
            acc_ref[1, own, :] += dv2[CHUNK:]
            acc_ref[0, prv, :] += dk2[:CHUNK]
            acc_ref[1, prv, :] += dv2[:CHUNK]
        dh_ref[0, pl.ds(here, SPAN), :] = dq_ref[...].astype(BF16)

        @pl.when(sb == ns - 1)
        def _():
            dh_ref[1] = acc_ref[0].astype(BF16)
            dh_ref[2] = acc_ref[1].astype(BF16)

    blk, prev, lse_spec = _att_specs(t)
    span = lambda: pltpu.VMEM((SPAN, D_HEAD_DIM), F32)
    return pl.pallas_call(
        body, grid=(D_HEADS, ns), name=name,
        in_specs=[blk(Q_COL), blk(K_COL), blk(V_COL), prev(K_COL), prev(V_COL), blk(Z_COL), blk(WIDTH // D_HEAD_DIM),
                  blk(0), lse_spec, ANY],
        out_specs=pl.BlockSpec((4, t, D_HEAD_DIM), lambda hd, sb: (1, 0, hd)),
        out_shape=jax.ShapeDtypeStruct(dh.shape, dh.dtype),
        scratch_shapes=[pltpu.VMEM((2, t, D_HEAD_DIM), F32), span(), span(), span(), span()],
        input_output_aliases={9: 0},
        compiler_params=_cp("arbitrary", "arbitrary"),
    )(h, h, h, h, h, h, dy, o, lse, dh)


def _place():
    x, y, c = lax.axis_index("x"), lax.axis_index("y"), lax.axis_index("c")
    return x, y, c, 4 * x + 2 * y + c


class _Exchange:
    def __init__(self, arrays, out_shape):
        n = len(arrays)
        self.arrays, self.out_shape = list(arrays), out_shape
        self.scratch = [pltpu.SemaphoreType.DMA((n, 7)), pltpu.SemaphoreType.DMA((n, 7)), pltpu.SemaphoreType.DMA((n,))]


class _AllGather(_Exchange):
    def __init__(self, blocks):
        super().__init__(blocks, [jax.ShapeDtypeStruct((N_DEV,) + b.shape, b.dtype) for b in blocks])

    def _plan(self, ins, outs, sems):
        send_sems, recv_sems, local_sems = sems
        x, y, c, me = _place()
        chips = [(1 - x, y), (x, 1 - y), (1 - x, 1 - y)]

        def copy(a, k, block, to, src=None):
            dst = outs[a].at[block]
            return pltpu.make_async_remote_copy(
                src_ref=dst if src is None else src, dst_ref=dst, send_sem=send_sems.at[a, k],
                recv_sem=recv_sems.at[a, k], device_id=to, device_id_type=MESH)

        n = len(ins)
        index = lambda px, py, pc: 4 * px + 2 * py + pc
        mine = [pltpu.make_async_copy(ins[a], outs[a].at[me], local_sems.at[a]) for a in range(n)]
        first = [copy(a, 0, me, (x, y, 1 - c), src=ins[a]) for a in range(n)]
        first += [copy(a, 1 + j, me, (*chip, c), src=ins[a]) for j, chip in enumerate(chips) for a in range(n)]
        arrive = lambda a, k, px, py, pc: copy(a, k, index(px, py, pc), (x, y, c))
        over_ici = [[arrive(a, 1 + j, *chip, c) for a in range(n)] for j, chip in enumerate(chips)]
        passed = [[copy(a, 4 + j, index(*chip, c), (x, y, 1 - c)) for a in range(n)] for j, chip in enumerate(chips)]
        from_sibling = [arrive(a, 0, x, y, 1 - c) for a in range(n)]
        from_sibling += [arrive(a, 4 + j, *chip, 1 - c) for j, chip in enumerate(chips) for a in range(n)]
        return mine, first, over_ici, passed, from_sibling

    def start(self, ins, outs, sems):
        mine, first, _, _, _ = self._plan(ins, outs, sems)
        for cp in mine + first:
            cp.start()

    def finish(self, ins, outs, sems):
        mine, first, over_ici, passed, from_sibling = self._plan(ins, outs, sems)
        for landed, onward in zip(over_ici, passed):
            for cp, fwd in zip(landed, onward):
                cp.wait_recv()
                fwd.start()
        for cp in from_sibling:
            cp.wait_recv()
        for cp in first + [fwd for onward in passed for fwd in onward]:
            cp.wait_send()
        for cp in mine:
            cp.wait()


class _AllToAll(_Exchange):
    def __init__(self, parts):
        super().__init__(parts, [jax.ShapeDtypeStruct(p.shape, p.dtype) for p in parts])

    def _plan(self, ins, outs, sems):
        send_sems, recv_sems, local_sems = sems
        x, y, c, me = _place()
        flip = lambda v, f: 1 - v if f else v
        peers = [(flip(x, fx), flip(y, fy), flip(c, fc)) for fx in (0, 1) for fy in (0, 1) for fc in (0, 1)][1:]

        def copy(a, k, sending):
            px, py, pc = peers[k]
            there = 4 * px + 2 * py + pc
            return pltpu.make_async_remote_copy(
                src_ref=ins[a].at[there], dst_ref=outs[a].at[me if sending else there], send_sem=send_sems.at[a, k],
                recv_sem=recv_sems.at[a, k], device_id=peers[k], device_id_type=MESH)

        n = len(ins)
        local = [pltpu.make_async_copy(ins[a].at[me], outs[a].at[me], local_sems.at[a]) for a in range(n)]
        return local, [[copy(a, k, sending) for k in range(7) for a in range(n)] for sending in (True, False)]

    def start(self, ins, outs, sems):
        local, (sends, _) = self._plan(ins, outs, sems)
        for cp in local + sends:
            cp.start()

    def finish(self, ins, outs, sems):
        local, (_, both_ways) = self._plan(ins, outs, sems)
        for cp in both_ways + local:
            cp.wait()


def _exchange(ex, name):
    n = len(ex.arrays)

    def body(*refs):
        ins, outs, sems = refs[:n], refs[n:2 * n], refs[2 * n:]
        ex.start(ins, outs, sems)
        ex.finish(ins, outs, sems)

    return pl.pallas_call(body, name=name, in_specs=[ANY] * n, out_specs=[ANY] * n, out_shape=ex.out_shape,
                          scratch_shapes=ex.scratch)(*ex.arrays)


def _sum_parts(parts, tc, name):
    _, r, c = parts.shape

    def body(p_ref, o_ref):
        acc = p_ref[0].astype(F32)
        for d in range(1, N_DEV):
            acc = acc + p_ref[d].astype(F32)
        o_ref[...] = acc

    return pl.pallas_call(
        body, grid=(c // tc,), name=name,
        in_specs=[pl.BlockSpec((N_DEV, r, tc), lambda j: (0, 0, j))], out_specs=pl.BlockSpec((r, tc), lambda j: (0, j)),
        out_shape=jax.ShapeDtypeStruct((r, c), F32), compiler_params=_cp("parallel"),
    )(parts)


def _adamw(w, g, m, v, tr, name):
    r, c = w.shape

    def body(w_ref, g_ref, m_ref, v_ref, d_ref, m2_ref, v2_ref):
        gv = g_ref[...]
        m2 = ADAM_B1 * m_ref[...] + (1.0 - ADAM_B1) * gv
        v2 = ADAM_B2 * v_ref[...] + (1.0 - ADAM_B2) * (gv * gv)
        m_hat = m2 / (1.0 - ADAM_B1 ** ADAM_STEP)
        v_hat = v2 / (1.0 - ADAM_B2 ** ADAM_STEP)
        d_ref[...] = -ADAM_LR * (m_hat / (jnp.sqrt(v_hat) + ADAM_EPS) + ADAM_WD * w_ref[...])
        m2_ref[...] = m2
        v2_ref[...] = v2

    spec = pl.BlockSpec((tr, c), lambda i: (i, 0))
    return pl.pallas_call(
        body, grid=(r // tr,), name=name, in_specs=[spec] * 4, out_specs=[spec] * 3,
        out_shape=[jax.ShapeDtypeStruct((r, c), F32)] * 3, compiler_params=_cp("parallel"),
    )(w, g, m, v)


PACK_ROWS = SUBLANES * LANES


def _pack(arrays):
    flat = [jnp.pad(a.reshape(-1), (0, -a.size % PACK_ROWS)) for a in arrays]
    return jnp.concatenate(flat).reshape(-1, LANES)


def _unpack(packed, shapes):
    flat, out, pos = packed.reshape(-1), [], 0
    for s in shapes:
        size = 1
        for d in s:
            size *= d
        out.append(flat[pos:pos + size].reshape(s))
        pos += size + (-size % PACK_ROWS)
    return out


SMALL = ["even_norm_g", "gmlp_ln_g", "gmlp_ln_b", "gmlp_ws", "gmlp_bs", "ssd_conv_w", "ssd_conv_b", "ssd_dt_bias",
         "ssd_a_log", "ssd_d", "ssd_norm_g", "odd_norm_g", "sconv_w", "final_norm_g"]
ORDER = ["even_norm_g", "even_w_in", "gmlp_ln_g", "gmlp_ln_b", "gmlp_ws", "gmlp_bs", "ssd_conv_w", "ssd_conv_b",
         "ssd_dt_bias", "ssd_a_log", "ssd_d", "ssd_norm_g", "even_w_out", "odd_norm_g", "odd_w_in", "sconv_w",
         "odd_w_out", "final_norm_g"]


def kernel(x, even_norm_g, even_w_in, gmlp_ln_g, gmlp_ln_b, gmlp_ws, gmlp_bs, ssd_conv_w, ssd_conv_b, ssd_dt_bias, ssd_a_log, ssd_d, ssd_norm_g, even_w_out, odd_norm_g, odd_w_in, sconv_w, odd_w_out, final_norm_g, loss_target, m_even_norm_g, m_even_w_in, m_gmlp_ln_g, m_gmlp_ln_b, m_gmlp_ws, m_gmlp_bs, m_ssd_conv_w, m_ssd_conv_b, m_ssd_dt_bias, m_ssd_a_log, m_ssd_d, m_ssd_norm_g, m_even_w_out, m_odd_norm_g, m_odd_w_in, m_sconv_w, m_odd_w_out, m_final_norm_g, v_even_norm_g, v_even_w_in, v_gmlp_ln_g, v_gmlp_ln_b, v_gmlp_ws, v_gmlp_bs, v_ssd_conv_w, v_ssd_conv_b, v_ssd_dt_bias, v_ssd_a_log, v_ssd_d, v_ssd_norm_g, v_even_w_out, v_odd_norm_g, v_odd_w_in, v_sconv_w, v_odd_w_out, v_final_norm_g):
    w = dict(even_norm_g=even_norm_g, even_w_in=even_w_in, gmlp_ln_g=gmlp_ln_g, gmlp_ln_b=gmlp_ln_b, gmlp_ws=gmlp_ws,
             gmlp_bs=gmlp_bs, ssd_conv_w=ssd_conv_w, ssd_conv_b=ssd_conv_b, ssd_dt_bias=ssd_dt_bias,
             ssd_a_log=ssd_a_log, ssd_d=ssd_d, ssd_norm_g=ssd_norm_g, even_w_out=even_w_out, odd_norm_g=odd_norm_g,
             odd_w_in=odd_w_in, sconv_w=sconv_w, odd_w_out=odd_w_out, final_norm_g=final_norm_g)
    m1 = dict(even_norm_g=m_even_norm_g, even_w_in=m_even_w_in, gmlp_ln_g=m_gmlp_ln_g, gmlp_ln_b=m_gmlp_ln_b,
              gmlp_ws=m_gmlp_ws, gmlp_bs=m_gmlp_bs, ssd_conv_w=m_ssd_conv_w, ssd_conv_b=m_ssd_conv_b,
              ssd_dt_bias=m_ssd_dt_bias, ssd_a_log=m_ssd_a_log, ssd_d=m_ssd_d, ssd_norm_g=m_ssd_norm_g,
              even_w_out=m_even_w_out, odd_norm_g=m_odd_norm_g, odd_w_in=m_odd_w_in, sconv_w=m_sconv_w,
              odd_w_out=m_odd_w_out, final_norm_g=m_final_norm_g)
    m2 = dict(even_norm_g=v_even_norm_g, even_w_in=v_even_w_in, gmlp_ln_g=v_gmlp_ln_g, gmlp_ln_b=v_gmlp_ln_b,
              gmlp_ws=v_gmlp_ws, gmlp_bs=v_gmlp_bs, ssd_conv_w=v_ssd_conv_w, ssd_conv_b=v_ssd_conv_b,
              ssd_dt_bias=v_ssd_dt_bias, ssd_a_log=v_ssd_a_log, ssd_d=v_ssd_d, ssd_norm_g=v_ssd_norm_g,
              even_w_out=v_even_w_out, odd_norm_g=v_odd_norm_g, odd_w_in=v_odd_w_in, sconv_w=v_sconv_w,
              odd_w_out=v_odd_w_out, final_norm_g=v_final_norm_g)
    _, _, _, me = _place()
    xs = x[0]
    shard = WIDTH // N_DEV

    small_blk = jnp.concatenate([
        ssd_conv_w[0], jnp.pad(sconv_w[0], ((0, 0), (0, shard))), jnp.pad(odd_norm_g, ((0, 0), (0, shard)))], axis=0)
    g_wte, g_small = _exchange(_AllGather([even_w_in[0].T.astype(BF16), small_blk]), "gather_even_w_in")
    next_weights = _AllGather([odd_w_in[0].T.astype(BF16), even_w_out[0].astype(BF16)])
    wte = g_wte.reshape(F_EVEN_ALL, D_MODEL)
    wte_dt = jnp.pad(wte[F_EVEN:], ((0, LANES - B_HEADS), (0, 0)))
    conv_w = g_small[:, 0:B_CONV, :].transpose(1, 0, 2).reshape(B_CONV, B_XBC)
    sconv_full = g_small[:, B_CONV:B_CONV + C_CONV, :shard].transpose(1, 0, 2).reshape(C_CONV, WIDTH)
    odd_g = g_small[:, B_CONV + C_CONV, :shard].reshape(1, WIDTH)

    pad_heads = lambda a: jnp.pad(a, ((0, 0), (0, LANES - B_HEADS)))
    alog, dtb = pad_heads(ssd_a_log), pad_heads(ssd_dt_bias)
    d_full = jnp.repeat(ssd_d, B_HEAD_DIM, axis=1)
    ws, bs_t = gmlp_ws[0], gmlp_bs[0].T
    ws_t = jnp.swapaxes(ws, 1, 2)
    proj = dict(tb=True, tm=1024, tn=1024, tk=D_MODEL, out_dtype=F32)
    out_proj = dict(n=D_MODEL, tm=1024, tn=1024, tk=2 * WIDTH, out_dtype=F32)
    dw_out = dict(ta=True, n=D_MODEL, tm=1024, tn=D_MODEL, tk=2048, out_dtype=BF16)
    dx_in = dict(n=D_MODEL, tm=1024, tn=1024, tk=2048, out_dtype=F32)
    dw_in = dict(ta=True, n=D_MODEL, tm=1024, tn=D_MODEL, tk=2048, out_dtype=BF16)
    slabs = lambda g, rows: g.reshape(N_DEV, rows // N_DEV, D_MODEL)

    xn0 = _rms_fwd(xs, even_norm_g, "norm_even")
    h0, (g_wto, g_woe) = _mm(xn0, wte, n=F_EVEN, carry=next_weights, name="proj_even", **proj)
    wto, woe = g_wto.reshape(F_ODD, D_MODEL), g_woe.reshape(2 * WIDTH, D_MODEL)
    hdt = _mm(xn0, wte_dt, tb=True, n=LANES, tm=1024, tn=LANES, tk=D_MODEL, out_dtype=F32, name="proj_dt")
    y0 = _gmlp_fwd(h0, gmlp_ln_g, gmlp_ln_b, ws, bs_t, "gmlp_fwd")
    pre = _ssd_conv_fwd(h0, conv_w, ssd_conv_b, "ssd_conv_fwd")
    dt, acs, acst = _ssd_dt(hdt, alog, dtb, "ssd_dt")
    (y0, ypre, states), (g_woo,) = _ssd_scan_fwd(
        pre, dt, acs, acst, h0, y0, d_full, ssd_norm_g, _AllGather([odd_w_out[0].astype(BF16)]), "ssd_scan_fwd")
    woo = g_woo.reshape(2 * WIDTH, D_MODEL)
    x1 = _mm(y0, woe, add=xs, name="out_even", **out_proj)
    xn1 = _rms_fwd(x1, odd_g, "norm_odd")
    h1 = _mm(xn1, wto, n=F_ODD, name="proj_odd", **proj)
    y1 = _sconv_fwd(h1, sconv_full, "sconv_fwd")
    y1, att_o, att_lse = _attn2_fwd(h1, y1, "attn_fwd")
    x2 = _mm(y1, woo, add=x1, name="out_odd", **out_proj)
    loss_part, dx2, g_final, dx2_b = _loss_head(x2, final_norm_g.reshape(1, D_MODEL), loss_target[0], "loss_head")

    dy1 = _mm(dx2_b, woo, n=2 * WIDTH, name="dy_odd", **proj)
    gw_woo = _mm(y1, dx2_b, name="dw_out_odd", **dw_out)
    dh1, g_sconv = _sconv_bwd(h1, dy1, sconv_full, "sconv_bwd")
    dh1 = _attn2_bwd(h1, dy1, att_o, att_lse, dh1, "attn_bwd")
    dxn1, (r_woo,) = _mm(dh1, wto, carry=_AllToAll([slabs(gw_woo, 2 * WIDTH)]), name="dx_odd", **dx_in)
    gw_wto = _mm(dh1, xn1, name="dw_in_odd", **dw_in)
    dx1, g_odd, dx1_b = _rms_bwd(x1, odd_g, dxn1, dx2, True, "norm_odd_bwd")

    dy0 = _mm(dx1_b, woe, n=2 * WIDTH, name="dy_even", **proj)
    gw_woe = _mm(y0, dx1_b, name="dw_out_even", **dw_out)
    dh0, g_ws, g_bs_t, g_ln_g, g_ln_b = _gmlp_bwd(h0, dy0, gmlp_ln_g, gmlp_ln_b, ws, ws_t, bs_t, "gmlp_bwd")
    (dh0, dpx, dpb, dpc, ddt, g_dtb, g_alog, g_dd, g_ng), (r_wto,) = _ssd_scan_bwd(
        pre, hdt, dt, acs, acst, h0, dy0, ypre, states, dh0, alog, dtb, d_full, ssd_norm_g,
        _AllToAll([slabs(gw_wto, F_ODD)]), "ssd_scan_bwd")
    dh0, g_conv_w, g_conv_b = _ssd_conv_bwd(h0, dh0, dpx, dpb, dpc, conv_w, "ssd_conv_bwd")
    gw_main, (r_woe,) = _mm(dh0, xn0, out_rows=F_EVEN_ALL, carry=_AllToAll([slabs(gw_woe, 2 * WIDTH)]),
                            name="dw_in_even", **dw_in)
    gw_dt = _mm(ddt, xn0, ta=True, n=D_MODEL, tm=LANES, tn=D_MODEL, tk=1024, out_dtype=BF16, name="dw_dt")
    gw_wte = lax.dynamic_update_slice(gw_main, gw_dt[:B_HEADS], (F_EVEN, 0))
    dxn0_dt = _mm(ddt, wte_dt, n=D_MODEL, tm=1024, tn=D_MODEL, tk=LANES, out_dtype=F32, name="dx_dt")
    dxn0, (r_wte,) = _mm(dh0, wte, add=dxn0_dt, carry=_AllToAll([slabs(gw_wte, F_EVEN_ALL)]), name="dx_even", **dx_in)
    grad_x, g_even = _rms_bwd(xs, even_norm_g, dxn0, dx1, False, "norm_even_bwd")

    small_parts = dict(
        even_norm_g=g_even, gmlp_ln_g=g_ln_g, gmlp_ln_b=g_ln_b, gmlp_ws=g_ws, gmlp_bs=g_bs_t[:, :A_GROUPS].T,
        ssd_conv_w=g_conv_w[:B_CONV], ssd_conv_b=g_conv_b, ssd_dt_bias=g_dtb[:, :B_HEADS], ssd_a_log=g_alog[:, :B_HEADS],
        ssd_d=g_dd.reshape(B_HEADS, B_HEAD_DIM).sum(axis=1), ssd_norm_g=g_ng, odd_norm_g=g_odd,
        sconv_w=g_sconv[:C_CONV], final_norm_g=g_final)
    full_shapes = dict(
        even_norm_g=(1, D_MODEL), gmlp_ln_g=(1, WIDTH), gmlp_ln_b=(1, WIDTH), gmlp_ws=(1, A_GROUPS, CHUNK, CHUNK),
        gmlp_bs=(1, A_GROUPS, CHUNK), ssd_conv_w=(1, B_CONV, B_XBC), ssd_conv_b=(1, B_XBC), ssd_dt_bias=(1, B_HEADS),
        ssd_a_log=(1, B_HEADS), ssd_d=(1, B_HEADS), ssd_norm_g=(1, WIDTH), odd_norm_g=(1, D_MODEL),
        sconv_w=(1, C_CONV, WIDTH), final_norm_g=(D_MODEL,))
    (gathered_small,) = _exchange(_AllGather([_pack([small_parts[k] for k in SMALL])]), "gather_small_grads")
    small_sum = _sum_parts(gathered_small, LANES, "sum_small_grads")
    grads = dict(zip(SMALL, _unpack(small_sum, [full_shapes[k] for k in SMALL])))
    grads["ssd_conv_w"] = lax.dynamic_slice_in_dim(grads["ssd_conv_w"], me * 2 * shard, 2 * shard, axis=2)
    grads["odd_norm_g"] = lax.dynamic_slice_in_dim(grads["odd_norm_g"], me * shard, shard, axis=1)
    grads["sconv_w"] = lax.dynamic_slice_in_dim(grads["sconv_w"], me * shard, shard, axis=2)

    grads["even_w_in"] = _sum_parts(r_wte, 256, "sum_even_w_in").T[None]
    grads["odd_w_in"] = _sum_parts(r_wto, 256, "sum_odd_w_in").T[None]
    grads["even_w_out"] = _sum_parts(r_woe, 512, "sum_even_w_out")[None]
    grads["odd_w_out"] = _sum_parts(r_woo, 512, "sum_odd_w_out")[None]

    delta, new_m, new_v = {}, {}, {}
    for k in ("even_w_in", "odd_w_in", "even_w_out", "odd_w_out"):
        d_k, m_k, v_k = _adamw(w[k][0], grads[k][0], m1[k][0], m2[k][0], 128, "adamw_" + k)
        delta[k], new_m[k], new_v[k] = d_k[None], m_k[None], v_k[None]
    packed = [_pack([src[k] for k in SMALL]) for src in (w, grads, m1, m2)]
    small_out = _adamw(*packed, packed[0].shape[0], "adamw_small")
    shapes = [w[k].shape for k in SMALL]
    for dst, arr in zip((delta, new_m, new_v), small_out):
        dst.update(zip(SMALL, _unpack(arr, shapes)))

    loss = lax.psum(loss_part[0, 0], ("x", "y", "c"))
    return (loss, grad_x[None], *[grads[k] for k in ORDER], *[delta[k] for k in ORDER],
            *[new_m[k] for k in ORDER], *[new_v[k] for k in ORDER])
```

```python
import functools

import jax
import jax.numpy as jnp
from jax import lax
from jax.experimental import pallas as pl
from jax.experimental.pallas import tpu as pltpu

F32, BF16 = jnp.float32, jnp.bfloat16
MESH = pl.DeviceIdType.MESH
ANY = pl.BlockSpec(memory_space=pl.ANY)

N_DEV = 8
D_MODEL = 2048
WIDTH = 2048
CHUNK = 128
A_GROUPS = 8
B_HEADS, B_HEAD_DIM, B_GROUPS, B_STATE, B_CONV = 32, 64, 8, 128, 4
B_GROUP_W = WIDTH // B_GROUPS
B_XBC = WIDTH + 2 * B_GROUPS * B_STATE
C_CONV = 3
D_HEADS, D_HEAD_DIM = 16, 128
D_PATTERNS = ((128, 1), (512, 4), (2048, 16))
F_EVEN = 3 * WIDTH + WIDTH + B_XBC
F_EVEN_ALL = F_EVEN + B_HEADS
F_ODD = 8 * WIDTH
EPS = 1e-5
NEG = -1e30

ADAM_LR, ADAM_B1, ADAM_B2, ADAM_EPS, ADAM_WD, ADAM_STEP = 0.001, 0.9, 0.999, 1e-08, 0.01, 10

VMEM_LIMIT_V7X = 56 * 1024 * 1024
SUBLANES, LANES = 8, 128


def _cp(*sem):
    return pltpu.CompilerParams(dimension_semantics=sem, vmem_limit_bytes=VMEM_LIMIT_V7X)


def _sig(x):
    return 0.5 * jnp.tanh(0.5 * x) + 0.5


def _silu(x):
    return x * _sig(x)


def _dsilu(x):
    s = _sig(x)
    return s * (1.0 + x * (1.0 - s))


def _softplus(x):
    return jnp.maximum(x, 0.0) + jnp.log(1.0 + jnp.exp(-jnp.abs(x)))


def _dot(a, b):
    return jnp.dot(a.astype(BF16), b.astype(BF16), preferred_element_type=F32)


def _dot_nt(a, b):
    return lax.dot_general(a.astype(BF16), b.astype(BF16), (((1,), (1,)), ((), ())), preferred_element_type=F32)


def _dot_tn(a, b):
    return lax.dot_general(a.astype(BF16), b.astype(BF16), (((0,), (0,)), ((), ())), preferred_element_type=F32)


def _split3(x):
    hi = x.astype(BF16)
    r1 = x - hi.astype(F32)
    mid = r1.astype(BF16)
    lo = (r1 - mid.astype(F32)).astype(BF16)
    return hi, mid, lo


def _dot_sel(x, sel):
    return sum(jnp.dot(p, sel, preferred_element_type=F32) for p in _split3(x))


def _sel_dot(sel, x):
    return sum(jnp.dot(sel, p, preferred_element_type=F32) for p in _split3(x))


def _iota(shape, axis):
    return lax.broadcasted_iota(jnp.int32, shape, axis)


def _shift_down(cur, halo, j):
    if j == 0:
        return cur
    r = pltpu.roll(cur, j, 0)
    top = jnp.where(_iota(halo.shape, 0) < j, pltpu.roll(halo, j, 0), r[0:SUBLANES])
    return jnp.concatenate([top, r[SUBLANES:]], axis=0)


def _shift_up(cur, halo, j):
    if j == 0:
        return cur
    n = cur.shape[0]
    r = pltpu.roll(cur, n - j, 0)
    bot = jnp.where(_iota(halo.shape, 0) >= SUBLANES - j, pltpu.roll(halo, SUBLANES - j, 0), r[n - SUBLANES:])
    return jnp.concatenate([r[:n - SUBLANES], bot], axis=0)


def _mm(a, b, *, ta=False, tb=False, n, tm, tn, tk, out_dtype, out_rows=None, add=None, carry=None, name):
    width, rows = a.shape[-1], a.shape[-2]
    feat = width * (a.shape[0] if a.ndim == 3 else 1)
    m, k_len = (feat, rows) if ta else (rows, feat)
    per_part = width // (tm if ta else tk)
    grid = (m // tm, n // tn, k_len // tk)
    nk = grid[2]

    def a_index(i, j, k):
        f = i if ta else k
        pos = (k,) if ta else (i,)
        return pos + (f,) if a.ndim == 2 else (f // per_part,) + pos + (f % per_part,)

    a_block = (tk, tm) if ta else (tm, tk)
    a_spec = pl.BlockSpec(a_block if a.ndim == 2 else (None,) + a_block, a_index)
    b_spec = pl.BlockSpec((tn, tk), lambda i, j, k: (j, k)) if tb else pl.BlockSpec((tk, tn), lambda i, j, k: (k, j))
    io_spec = pl.BlockSpec((tm, tn), lambda i, j, k: (i, j))
    dims = (((0 if ta else 1,), (1 if tb else 0,)), ((), ()))
    has_add, nc = add is not None, len(carry.arrays) if carry else 0

    def body(*refs):
        a_ref, b_ref = refs[0], refs[1]
        add_ref = refs[2] if has_add else None
        pos = 2 + has_add
        c_in, o_ref, c_out = refs[pos:pos + nc], refs[pos + nc], refs[pos + nc + 1:pos + 2 * nc + 1]
        pos += 2 * nc + 1
        acc_ref = refs[pos] if nk > 1 else None
        c_sems = refs[pos + (nk > 1):]
        i, j, k = pl.program_id(0), pl.program_id(1), pl.program_id(2)
        if carry:
            pl.when((i == 0) & (j == 0) & (k == 0))(lambda: carry.start(c_in, c_out, c_sems))

        def finish(r):
            if add_ref is not None:
                r = r + add_ref[...]
            o_ref[...] = r.astype(out_dtype)

        p = lax.dot_general(a_ref[...].astype(BF16), b_ref[...].astype(BF16), dims, preferred_element_type=F32)
        if nk == 1:
            finish(p)
        else:
            @pl.when(k == 0)
            def _():
                acc_ref[...] = p

            @pl.when((k > 0) & (k < nk - 1))
            def _():
                acc_ref[...] += p

            @pl.when(k == nk - 1)
            def _():
                finish(acc_ref[...] + p)

        if carry:
            pl.when((i == grid[0] - 1) & (j == grid[1] - 1) & (k == nk - 1))(lambda: carry.finish(c_in, c_out, c_sems))

    out = pl.pallas_call(
        body, grid=grid, name=name,
        in_specs=[a_spec, b_spec] + [io_spec] * has_add + [ANY] * nc,
        out_specs=[io_spec] + [ANY] * nc,
        out_shape=[jax.ShapeDtypeStruct((out_rows or m, n), out_dtype)] + (carry.out_shape if carry else []),
        scratch_shapes=([pltpu.VMEM((tm, tn), F32)] if nk > 1 else []) + (carry.scratch if carry else []),
        compiler_params=_cp(*(("arbitrary",) * 3 if carry else ("parallel", "parallel", "arbitrary"))),
    )(a, b, *([add] if has_add else []), *(carry.arrays if carry else []))
    return (out[0], out[1:]) if carry else out[0]


def _rms_fwd(x, g, name):
    t, tb = x.shape[0], 512

    def body(x_ref, g_ref, o_ref):
        xv = x_ref[...]
        r = lax.rsqrt(jnp.mean(xv * xv, axis=-1, keepdims=True) + EPS)
        o_ref[...] = (xv * r * g_ref[...]).astype(BF16)

    row = pl.BlockSpec((tb, D_MODEL), lambda i: (i, 0))
    return pl.pallas_call(
        body, grid=(t // tb,), name=name,
        in_specs=[row, pl.BlockSpec((1, D_MODEL), lambda i: (0, 0))], out_specs=row,
        out_shape=jax.ShapeDtypeStruct((t, D_MODEL), BF16), compiler_params=_cp("parallel"),
    )(x, g)


def _rms_bwd(x, g, dxn, dres, bf16_copy, name):
    t, tb = x.shape[0], 256

    def body(x_ref, g_ref, dxn_ref, dres_ref, dx_ref, dg_ref, *dxb_ref):
        xv = x_ref[...]
        r = lax.rsqrt(jnp.mean(xv * xv, axis=-1, keepdims=True) + EPS)
        nv = xv * r
        dy = dxn_ref[...]
        dn = dy * g_ref[...]
        dx = dres_ref[...] + r * (dn - nv * jnp.mean(dn * nv, axis=-1, keepdims=True))
        dx_ref[...] = dx
        for ref in dxb_ref:
            ref[...] = dx.astype(BF16)
        part = jnp.sum(dy * nv, axis=0, keepdims=True)

        @pl.when(pl.program_id(0) == 0)
        def _():
            dg_ref[...] = part

        @pl.when(pl.program_id(0) > 0)
        def _():
            dg_ref[...] += part

    row = pl.BlockSpec((tb, D_MODEL), lambda i: (i, 0))
    vec = pl.BlockSpec((1, D_MODEL), lambda i: (0, 0))
    return pl.pallas_call(
        body, grid=(t // tb,), name=name,
        in_specs=[row, vec, row, row], out_specs=[row, vec] + [row] * bf16_copy,
        out_shape=[jax.ShapeDtypeStruct((t, D_MODEL), F32), jax.ShapeDtypeStruct((1, D_MODEL), F32)]
        + [jax.ShapeDtypeStruct((t, D_MODEL), BF16)] * bf16_copy,
        compiler_params=_cp("arbitrary"),
    )(x, g, dxn, dres)


def _loss_head(x, g, target, name):
    t, tb = x.shape[0], 256

    def body(x_ref, g_ref, t_ref, loss_ref, dx_ref, dg_ref, dxb_ref):
        xv, gv = x_ref[...], g_ref[...]
        r = lax.rsqrt(jnp.mean(xv * xv, axis=-1, keepdims=True) + EPS)
        nv = xv * r
        err = nv * gv - t_ref[...]
        lpart = 0.5 * jnp.sum(jnp.mean(err * err, axis=-1, keepdims=True), axis=0, keepdims=True)
        dy = err * (1.0 / D_MODEL)
        dn = dy * gv
        dx = r * (dn - nv * jnp.mean(dn * nv, axis=-1, keepdims=True))
        dx_ref[...] = dx
        dxb_ref[...] = dx.astype(BF16)
        gpart = jnp.sum(dy * nv, axis=0, keepdims=True)

        @pl.when(pl.program_id(0) == 0)
        def _():
            dg_ref[...] = gpart
            loss_ref[...] = jnp.broadcast_to(lpart, (1, LANES))

        @pl.when(pl.program_id(0) > 0)
        def _():
            dg_ref[...] += gpart
            loss_ref[...] += jnp.broadcast_to(lpart, (1, LANES))

    row = pl.BlockSpec((tb, D_MODEL), lambda i: (i, 0))
    vec = pl.BlockSpec((1, D_MODEL), lambda i: (0, 0))
    return pl.pallas_call(
        body, grid=(t // tb,), name=name,
        in_specs=[row, vec, row], out_specs=[pl.BlockSpec((1, LANES), lambda i: (0, 0)), row, vec, row],
        out_shape=[jax.ShapeDtypeStruct((1, LANES), F32), jax.ShapeDtypeStruct((t, D_MODEL), F32),
                   jax.ShapeDtypeStruct((1, D_MODEL), F32), jax.ShapeDtypeStruct((t, D_MODEL), BF16)],
        compiler_params=_cp("arbitrary"),
    )(x, g, target)


A_GW = WIDTH // A_GROUPS


def _gmlp_common(v, lg, lb):
    xc = v - jnp.mean(v, axis=-1, keepdims=True)
    rs = lax.rsqrt(jnp.mean(xc * xc, axis=-1, keepdims=True) + EPS)
    vh = xc * rs
    return rs, vh, (vh * lg + lb).astype(BF16)


def _gmlp_fwd(h, ln_g, ln_b, ws, bs_t, name):
    t, tb = h.shape[0], 256

    def body(u_ref, v_ref, z_ref, lg_ref, lb_ref, ws_ref, bst_ref, y_ref):
        _, _, vn = _gmlp_common(v_ref[...], lg_ref[...], lb_ref[...])
        causal = _iota((CHUNK, CHUNK), 1) <= _iota((CHUNK, CHUNK), 0)
        for g in range(A_GROUPS):
            w = jnp.where(causal, ws_ref[g], 0.0).astype(BF16)
            cols = slice(g * A_GW, (g + 1) * A_GW)
            for c in range(tb // CHUNK):
                rows = slice(c * CHUNK, (c + 1) * CHUNK)
                mixed = jnp.dot(w, vn[rows, cols], preferred_element_type=F32) + bst_ref[:, g:g + 1]
                y_ref[rows, cols] = (_silu(z_ref[rows, cols]) * (u_ref[rows, cols] * mixed)).astype(BF16)

    col = lambda j: pl.BlockSpec((tb, WIDTH), lambda i: (i, j))
    full = lambda a: pl.BlockSpec(a.shape, lambda i: (0,) * a.ndim)
    return pl.pallas_call(
        body, grid=(t // tb,), name=name,
        in_specs=[col(0), col(1), col(2), full(ln_g), full(ln_b), full(ws), full(bs_t)],
        out_specs=col(0), out_shape=jax.ShapeDtypeStruct((t, 2 * WIDTH), BF16),
        compiler_params=_cp("parallel"),
    )(h, h, h, ln_g, ln_b, ws, bs_t)


def _gmlp_bwd(h, dy, ln_g, ln_b, ws, ws_t, bs_t, name):
    t, tb = h.shape[0], 256

    def body(u_ref, v_ref, z_ref, dy_ref, lg_ref, lb_ref, ws_ref, wst_ref, bst_ref,
             dh_ref, dws_ref, dbst_ref, dlg_ref, dlb_ref, dvn_ref):
        @pl.when(pl.program_id(0) == 0)
        def _():
            dws_ref[...] = jnp.zeros_like(dws_ref)
            dbst_ref[...] = jnp.zeros_like(dbst_ref)
            dlg_ref[...] = jnp.zeros_like(dlg_ref)
            dlb_ref[...] = jnp.zeros_like(dlb_ref)

        rs, vh, vn = _gmlp_common(v_ref[...], lg_ref[...], lb_ref[...])
        row, lane = _iota((CHUNK, CHUNK), 0), _iota((CHUNK, CHUNK), 1)
        for g in range(A_GROUPS):
            w = jnp.where(lane <= row, ws_ref[g], 0.0).astype(BF16)
            wt = jnp.where(row <= lane, wst_ref[g], 0.0).astype(BF16)
            cols = slice(g * A_GW, (g + 1) * A_GW)
            dws_acc = jnp.zeros((CHUNK, CHUNK), F32)
            dbs_acc = jnp.zeros((CHUNK, 1), F32)
            for c in range(tb // CHUNK):
                rows = slice(c * CHUNK, (c + 1) * CHUNK)
                vnb = vn[rows, cols]
                mixed = jnp.dot(w, vnb, preferred_element_type=F32) + bst_ref[:, g:g + 1]
                u, z, dyv = u_ref[rows, cols], z_ref[rows, cols], dy_ref[rows, cols]
                sz = _silu(z)
                dh_ref[rows, cols] = (dyv * sz * mixed).astype(BF16)
                dh_ref[rows, slice(2 * WIDTH + g * A_GW, 2 * WIDTH + (g + 1) * A_GW)] = (
                    dyv * (u * mixed) * _dsilu(z)).astype(BF16)
                dm = dyv * sz * u
                dws_acc += _dot_nt(dm, vnb)
                dbs_acc += jnp.sum(dm, axis=1, keepdims=True)
                dvn_ref[rows, cols] = jnp.dot(wt, dm.astype(BF16), preferred_element_type=F32)
            dws_ref[g] += jnp.where(lane <= row, dws_acc, 0.0)
            dbst_ref[...] += jnp.where(lane == g, dbs_acc, 0.0)
        dvn = dvn_ref[...]
        dlg_ref[...] += jnp.sum(dvn * vh, axis=0, keepdims=True)
        dlb_ref[...] += jnp.sum(dvn, axis=0, keepdims=True)
        dvh = dvn * lg_ref[...]
        dv = rs * (dvh - jnp.mean(dvh, axis=-1, keepdims=True) - vh * jnp.mean(dvh * vh, axis=-1, keepdims=True))
        dh_ref[:, WIDTH:2 * WIDTH] = dv.astype(BF16)

    col = lambda j: pl.BlockSpec((tb, WIDTH), lambda i: (i, j))
    full = lambda a: pl.BlockSpec(a.shape, lambda i: (0,) * a.ndim)
    acc = lambda shape: pl.BlockSpec(shape, lambda i: (0,) * len(shape))
    return pl.pallas_call(
        body, grid=(t // tb,), name=name,
        in_specs=[col(0), col(1), col(2), col(0), full(ln_g), full(ln_b), full(ws), full(ws_t), full(bs_t)],
        out_specs=[pl.BlockSpec((tb, 3 * WIDTH), lambda i: (i, 0)), acc((A_GROUPS, CHUNK, CHUNK)),
                   acc((CHUNK, LANES)), acc((1, WIDTH)), acc((1, WIDTH))],
        out_shape=[jax.ShapeDtypeStruct((t, F_EVEN), BF16), jax.ShapeDtypeStruct((A_GROUPS, CHUNK, CHUNK), F32),
                   jax.ShapeDtypeStruct((CHUNK, LANES), F32), jax.ShapeDtypeStruct((1, WIDTH), F32),
                   jax.ShapeDtypeStruct((1, WIDTH), F32)],
        scratch_shapes=[pltpu.VMEM((tb, WIDTH), F32)],
        compiler_params=_cp("arbitrary"),
    )(h, h, h, dy, ln_g, ln_b, ws, ws_t, bs_t)


def _halo_prev(tb, j):
    return lambda i: (jnp.maximum(i * (tb // SUBLANES) - 1, 0), j)


def _halo_next(tb, j, t):
    return lambda i: (jnp.minimum((i + 1) * (tb // SUBLANES), t // SUBLANES - 1), j)


def _row_select(parts, width):
    row = _iota((SUBLANES, width), 0)
    out = jnp.zeros((SUBLANES, width), F32)
    for k, p in enumerate(parts):
        out = jnp.where(row == k, p, out)
    return out


def _sconv_fwd(h, w, name):
    t, tb = h.shape[0], 256

    def body(bg_ref, cg_ref, hx_ref, z_ref, cgh_ref, hxh_ref, w_ref, y_ref):
        p = cg_ref[...] * hx_ref[...]
        ph = jnp.where(pl.program_id(0) > 0, cgh_ref[...] * hxh_ref[...], 0.0)
        cv = w_ref[2:3, :] * p + w_ref[1:2, :] * _shift_down(p, ph, 1) + w_ref[0:1, :] * _shift_down(p, ph, 2)
        y_ref[...] = (_silu(z_ref[...]) * (bg_ref[...] * cv)).astype(BF16)

    col = lambda j: pl.BlockSpec((tb, WIDTH), lambda i: (i, j))
    halo = lambda j: pl.BlockSpec((SUBLANES, WIDTH), _halo_prev(tb, j))
    return pl.pallas_call(
        body, grid=(t // tb,), name=name,
        in_specs=[col(0), col(1), col(2), col(3), halo(1), halo(2), pl.BlockSpec(w.shape, lambda i: (0, 0))],
        out_specs=col(0), out_shape=jax.ShapeDtypeStruct((t, 2 * WIDTH), BF16),
        compiler_params=_cp("parallel"),
    )(h, h, h, h, h, h, w)


def _sconv_bwd(h, dy, w, name):
    t, tb = h.shape[0], 256
    nb = t // tb

    def body(bg_ref, cg_ref, hx_ref, z_ref, dy_ref, cgh_ref, hxh_ref, bgn_ref, zn_ref, dyn_ref, w_ref, dh_ref, dw_ref):
        i = pl.program_id(0)
        bg, cg, hx, z, dyv = bg_ref[...], cg_ref[...], hx_ref[...], z_ref[...], dy_ref[...]
        p = cg * hx
        ph = jnp.where(i > 0, cgh_ref[...] * hxh_ref[...], 0.0)
        p1, p2 = _shift_down(p, ph, 1), _shift_down(p, ph, 2)
        cv = w_ref[2:3, :] * p + w_ref[1:2, :] * p1 + w_ref[0:1, :] * p2
        sz = _silu(z)
        dcv = dyv * sz * bg
        dcvn = jnp.where(i < nb - 1, dyn_ref[...] * _silu(zn_ref[...]) * bgn_ref[...], 0.0)
        dp = w_ref[2:3, :] * dcv + w_ref[1:2, :] * _shift_up(dcv, dcvn, 1) + w_ref[0:1, :] * _shift_up(dcv, dcvn, 2)
        dh_ref[0] = (dyv * sz * cv).astype(BF16)
        dh_ref[1] = (dp * hx).astype(BF16)
        dh_ref[2] = (dp * cg).astype(BF16)
        dh_ref[3] = (dyv * (bg * cv) * _dsilu(z)).astype(BF16)
        part = _row_select([jnp.sum(dcv * q, axis=0, keepdims=True) for q in (p2, p1, p)], WIDTH)

        @pl.when(i == 0)
        def _():
            dw_ref[...] = part

        @pl.when(i > 0)
        def _():
            dw_ref[...] += part

    col = lambda j: pl.BlockSpec((tb, WIDTH), lambda i: (i, j))
    prev = lambda j: pl.BlockSpec((SUBLANES, WIDTH), _halo_prev(tb, j))
    nxt = lambda j: pl.BlockSpec((SUBLANES, WIDTH), _halo_next(tb, j, t))
    return pl.pallas_call(
        body, grid=(nb,), name=name,
        in_specs=[col(0), col(1), col(2), col(3), col(0), prev(1), prev(2), nxt(0), nxt(3), nxt(0),
                  pl.BlockSpec(w.shape, lambda i: (0, 0))],
        out_specs=[pl.BlockSpec((4, tb, WIDTH), lambda i: (0, i, 0)), pl.BlockSpec((SUBLANES, WIDTH), lambda i: (0, 0))],
        out_shape=[jax.ShapeDtypeStruct((8, t, WIDTH), BF16), jax.ShapeDtypeStruct((SUBLANES, WIDTH), F32)],
        compiler_params=_cp("arbitrary"),
    )(h, h, h, h, dy, h, h, h, h, dy, w)


XBC_COL0 = 4 * WIDTH


def _ssd_conv_fwd(h, w, b, name):
    t, tb = h.shape[0], 256

    def body(x_ref, xh_ref, w_ref, b_ref, o_ref):
        xv = x_ref[...]
        xh = jnp.where(pl.program_id(0) > 0, xh_ref[...], 0.0)
        acc = b_ref[...] + w_ref[3:4, :] * xv
        for j in range(1, B_CONV):
            acc = acc + w_ref[B_CONV - 1 - j:B_CONV - j, :] * _shift_down(xv, xh, j)
        o_ref[...] = acc

    cb = XBC_COL0 // B_XBC
    return pl.pallas_call(
        body, grid=(t // tb,), name=name,
        in_specs=[pl.BlockSpec((tb, B_XBC), lambda i: (i, cb)), pl.BlockSpec((SUBLANES, B_XBC), _halo_prev(tb, cb)),
                  pl.BlockSpec(w.shape, lambda i: (0, 0)), pl.BlockSpec(b.shape, lambda i: (0, 0))],
        out_specs=pl.BlockSpec((tb, B_XBC), lambda i: (i, 0)), out_shape=jax.ShapeDtypeStruct((t, B_XBC), F32),
        compiler_params=_cp("parallel"),
    )(h, h, w, b)


def _ssd_conv_bwd(h, dh, dpx, dpb, dpc, w, name):
    t, tb, tc = h.shape[0], 512, 1024
    nb = t // tb
    r8 = tb // SUBLANES
    last8 = t // SUBLANES - 1

    def body(dpx_ref, dpb_ref, dpc_ref, nx_ref, nb_ref, nc_ref, x_ref, xh_ref, w_ref, dh_in, dh_ref, dw_ref, db_ref):
        j, i = pl.program_id(0), pl.program_id(1)
        pick = lambda a, b_, c: jnp.where(j < 2, a[...], jnp.where(j == 2, b_[...], c[...]))
        dp = pick(dpx_ref, dpb_ref, dpc_ref)
        dn = jnp.where(i < nb - 1, pick(nx_ref, nb_ref, nc_ref), 0.0)
        xv = x_ref[...]
        xh = jnp.where(i > 0, xh_ref[...], 0.0)
        dx = w_ref[3:4, :] * dp
        for s in range(1, B_CONV):
            dx = dx + w_ref[B_CONV - 1 - s:B_CONV - s, :] * _shift_up(dp, dn, s)
        dh_ref[...] = dx.astype(BF16)
        wpart = _row_select([jnp.sum(dp * _shift_down(xv, xh, B_CONV - 1 - k), axis=0, keepdims=True)
                             for k in range(B_CONV)], tc)
        bpart = jnp.sum(dp, axis=0, keepdims=True)

        @pl.when(i == 0)
        def _():
            dw_ref[...] = wpart
            db_ref[...] = bpart

        @pl.when(i > 0)
        def _():
            dw_ref[...] += wpart
            db_ref[...] += bpart

    def src(blk_rows, rowf, sel, colf):
        return pl.BlockSpec((blk_rows, tc), lambda j, i: (jnp.where(sel(j), rowf(i), 0), colf(j)))

    cur = lambda i: i
    nxt = lambda i: jnp.minimum((i + 1) * r8, last8)
    is_x, is_b, is_c = (lambda j: j < 2), (lambda j: j == 2), (lambda j: j == 3)
    xcol, zero = (lambda j: jnp.minimum(j, 1)), (lambda j: 0)
    c0 = XBC_COL0 // tc
    return pl.pallas_call(
        body, grid=(B_XBC // tc, nb), name=name,
        in_specs=[src(tb, cur, is_x, xcol), src(tb, cur, is_b, zero), src(tb, cur, is_c, zero),
                  src(SUBLANES, nxt, is_x, xcol), src(SUBLANES, nxt, is_b, zero), src(SUBLANES, nxt, is_c, zero),
                  pl.BlockSpec((tb, tc), lambda j, i: (i, c0 + j)),
                  pl.BlockSpec((SUBLANES, tc), lambda j, i: (jnp.maximum(i * r8 - 1, 0), c0 + j)),
                  pl.BlockSpec((B_CONV, tc), lambda j, i: (0, j)), ANY],
        out_specs=[pl.BlockSpec((tb, tc), lambda j, i: (i, c0 + j)), pl.BlockSpec((SUBLANES, tc), lambda j, i: (0, j)),
                   pl.BlockSpec((1, tc), lambda j, i: (0, j))],
        out_shape=[jax.ShapeDtypeStruct(dh.shape, dh.dtype), jax.ShapeDtypeStruct((SUBLANES, B_XBC), F32),
                   jax.ShapeDtypeStruct((1, B_XBC), F32)],
        input_output_aliases={9: 0},
        compiler_params=_cp("arbitrary", "arbitrary"),
    )(dpx, dpb, dpc, dpx, dpb, dpc, h, h, w, dh)


HEADS_PER_GROUP = B_HEADS // B_GROUPS


def _ssd_dt(hdt, alog, dtb, name):
    t = hdt.shape[0]
    nc = t // CHUNK

    def body(dtr_ref, alog_ref, dtb_ref, dt_ref, acs_ref, acst_ref):
        dt = _softplus(dtr_ref[...] + dtb_ref[...])
        tri = (_iota((CHUNK, CHUNK), 1) <= _iota((CHUNK, CHUNK), 0)).astype(BF16)
        acs = _sel_dot(tri, dt * -jnp.exp(alog_ref[...]))
        dt_ref[...] = dt
        acs_ref[...] = acs
        acst_ref[...] = acs.T

    tok = pl.BlockSpec((CHUNK, LANES), lambda c: (c, 0))
    vec = pl.BlockSpec((1, LANES), lambda c: (0, 0))
    return pl.pallas_call(
        body, grid=(nc,), name=name, in_specs=[tok, vec, vec],
        out_specs=[tok, tok, pl.BlockSpec((None, LANES, CHUNK), lambda c: (c, 0, 0))],
        out_shape=[jax.ShapeDtypeStruct((t, LANES), F32), jax.ShapeDtypeStruct((t, LANES), F32),
                   jax.ShapeDtypeStruct((nc, LANES, CHUNK), F32)],
        compiler_params=_cp("parallel"),
    )(hdt, alog, dtb)


def _to_group(m, g):
    return pltpu.roll(m, (LANES - HEADS_PER_GROUP * g) % LANES, 1)


def _from_group(m, g):
    return pltpu.roll(m, HEADS_PER_GROUP * g, 1)


def _ssd_group_terms(g, px, pb, pc, dt, acs, acst_ref):
    head = _iota((CHUNK, B_GROUP_W), 1) // B_HEAD_DIM

    def spread(m4):
        out = m4[:, HEADS_PER_GROUP - 1:HEADS_PER_GROUP]
        for j in range(HEADS_PER_GROUP - 2, -1, -1):
            out = jnp.where(head == j, m4[:, j:j + 1], out)
        return out

    dt4, a4 = _to_group(dt, g), _to_group(acs, g)
    rows = [acst_ref[pl.ds(HEADS_PER_GROUP * g + j, 1), :] for j in range(HEADS_PER_GROUP)]
    return dict(xs=_silu(px), bm=_silu(pb), cm=_silu(pc), dt4=dt4, a4=a4, rows=rows, dt_e=spread(dt4), a_e=spread(a4))


def _ssd_decay(tm, j, transposed):
    col, row = tm["a4"][:, j:j + 1], tm["rows"][j]
    lane, sub = _iota((CHUNK, CHUNK), 1), _iota((CHUNK, CHUNK), 0)
    if transposed:
        return jnp.where(sub <= lane, jnp.exp(jnp.minimum(row - col, 0.0)), 0.0)
    return jnp.where(lane <= sub, jnp.exp(jnp.minimum(col - row, 0.0)), 0.0)


GROUPS_PER_STEP = 4


def _ssd_specs(nc, rev):
    ch = (lambda c: nc - 1 - c) if rev else (lambda c: c)
    n = GROUPS_PER_STEP
    gw = lambda off: pl.BlockSpec((CHUNK, n * B_GROUP_W), lambda c, g: (ch(c), off // n + g))
    gn = lambda off: pl.BlockSpec((CHUNK, n * B_STATE), lambda c, g: (ch(c), off // n + g))
    tok = pl.BlockSpec((CHUNK, LANES), lambda c, g: (ch(c), 0))
    vec = pl.BlockSpec((1, LANES), lambda c, g: (0, 0))
    gvec = pl.BlockSpec((1, n * B_GROUP_W), lambda c, g: (0, g))
    st = pl.BlockSpec((None, B_STATE, n * B_GROUP_W), lambda c, g: (ch(c), 0, g))
    tokt = pl.BlockSpec((None, LANES, CHUNK), lambda c, g: (ch(c), 0, 0))
    return gw, gn, tok, tokt, vec, gvec, st


def _group_cols(u):
    return slice(u * B_GROUP_W, (u + 1) * B_GROUP_W), slice(u * B_STATE, (u + 1) * B_STATE)


def _ssd_scan_fwd(pre, dt, acs, acst, h, y, dfull, ng, carry, name):
    t = pre.shape[0]
    nc = t // CHUNK
    n_carried = len(carry.arrays) if carry else 0

    def body(*refs):
        px_ref, pb_ref, pc_ref, dt_ref, acs_ref, acst_ref, z_ref, df_ref, ng_ref, _ = refs[:10]
        refs = refs[10:]
        c_in, refs = refs[:n_carried], refs[n_carried:]
        y_ref, ypre_ref, st_ref = refs[:3]
        c_out, state_ref, c_sems = refs[3:3 + n_carried], refs[3 + n_carried], refs[4 + n_carried:]
        c, pair = pl.program_id(0), pl.program_id(1)
        if carry:
            pl.when((c == 0) & (pair == 0))(lambda: carry.start(c_in, c_out, c_sems))
        groups = [GROUPS_PER_STEP * pair + u for u in range(GROUPS_PER_STEP)]

        @pl.when(c == 0)
        def _():
            for g in groups:
                state_ref[g] = jnp.zeros((B_STATE, B_GROUP_W), F32)

        entering, leaving = [state_ref[g] for g in groups], []
        for u, g in enumerate(groups):
            wide, narrow = _group_cols(u)
            tm = _ssd_group_terms(g, px_ref[:, wide], pb_ref[:, narrow], pc_ref[:, narrow], dt_ref[...],
                                  acs_ref[...], acst_ref)
            xs, bm, cm, a_e = tm["xs"], tm["bm"], tm["cm"], tm["a_e"]
            xdt = xs * tm["dt_e"]
            cb = _dot_nt(cm, bm)
            head = _iota((CHUNK, B_GROUP_W), 1) // B_HEAD_DIM
            yd = jnp.zeros((CHUNK, B_GROUP_W), F32)
            for j in range(HEADS_PER_GROUP):
                yd = jnp.where(head == j, _dot(cb * _ssd_decay(tm, j, False), xdt), yd)
            st = entering[u]
            st_ref[:, wide] = st
            yv = yd + jnp.exp(a_e) * _dot(cm, st) + df_ref[:, wide] * xs
            a_last = a_e[CHUNK - 1:CHUNK, :]
            leaving.append(st * jnp.exp(a_last) + _dot_tn(bm, xdt * jnp.exp(a_last - a_e)))
            ypre_ref[:, wide] = yv
            yz = yv * _silu(z_ref[:, wide])
            r = lax.rsqrt(jnp.mean(yz * yz, axis=-1, keepdims=True) + EPS)
            y_ref[:, wide] = (yz * r * ng_ref[:, wide]).astype(BF16)
        for g, st in zip(groups, leaving):
            state_ref[g] = st
        if carry:
            last = (c == nc - 1) & (pair == B_GROUPS // GROUPS_PER_STEP - 1)
            pl.when(last)(lambda: carry.finish(c_in, c_out, c_sems))

    gw, gn, tok, tokt, vec, gvec, st = _ssd_specs(nc, False)
    out = pl.pallas_call(
        body, grid=(nc, B_GROUPS // GROUPS_PER_STEP), name=name,
        in_specs=[gw(0), gn(WIDTH // B_STATE), gn((WIDTH + B_GROUPS * B_STATE) // B_STATE), tok, tok, tokt,
                  gw(3 * WIDTH // B_GROUP_W), gvec, gvec, ANY] + [ANY] * n_carried,
        out_specs=[gw(WIDTH // B_GROUP_W), gw(0), st] + [ANY] * n_carried,
        out_shape=[jax.ShapeDtypeStruct(y.shape, y.dtype), jax.ShapeDtypeStruct((t, WIDTH), F32),
                   jax.ShapeDtypeStruct((nc, B_STATE, WIDTH), F32)] + (carry.out_shape if carry else []),
        scratch_shapes=[pltpu.VMEM((B_GROUPS, B_STATE, B_GROUP_W), F32)] + (carry.scratch if carry else []),
        input_output_aliases={9: 0},
        compiler_params=_cp("arbitrary", "arbitrary"),
    )(pre, pre, pre, dt, acs, acst, h, dfull, ng, y, *(carry.arrays if carry else []))
    return out[:3], out[3:]


def _ssd_scan_bwd(pre, hdt, dt, acs, acst, h, dy, ypre, states, dh, alog, dtb, dfull, ng, carry, name):
    t = pre.shape[0]
    nc = t // CHUNK

    n_carried = len(carry.arrays) if carry else 0

    def one_group(g, dsn, px_ref, pb_ref, pc_ref, dtr_ref, dt_ref, acs_ref, acst_ref, z_ref, dy_ref, ypre_ref, st_ref,
                  alog_ref, dtb_ref, df_ref, ng_ref, dz_ref, dpx_ref, dpb_ref, dpc_ref):
        px, pb, pc, dtr = px_ref[...], pb_ref[...], pc_ref[...], dtr_ref[...]
        tm = _ssd_group_terms(g, px, pb, pc, dt_ref[...], acs_ref[...], acst_ref)
        xs, bm, cm, a_e, dt_e = tm["xs"], tm["bm"], tm["cm"], tm["a_e"], tm["dt_e"]
        xdt = xs * dt_e
        head = _iota((CHUNK, B_GROUP_W), 1) // B_HEAD_DIM

        z, yv, ngv = z_ref[...], ypre_ref[...], ng_ref[...]
        sz = _silu(z)
        yz = yv * sz
        r = lax.rsqrt(jnp.mean(yz * yz, axis=-1, keepdims=True) + EPS)
        dyn = dy_ref[...]
        dng_part = jnp.sum(dyn * yz * r, axis=0, keepdims=True)
        q = dyn * ngv
        dyz = r * q - yz * (r * r * r) * jnp.mean(q * yz, axis=-1, keepdims=True)
        dyv = dyz * sz
        dz_ref[...] = (dyz * yv * _dsilu(z)).astype(BF16)
        dd_part = jnp.sum(dyv * xs, axis=0, keepdims=True)
        dxs = df_ref[...] * dyv

        st = st_ref[...]
        ea = jnp.exp(a_e)
        ead = ea * dyv
        dcm = _dot_nt(ead, st)
        da_e = dyv * (ea * _dot(cm, st))

        a_last = a_e[CHUNK - 1:CHUNK, :]
        ea_last = jnp.exp(a_last)
        dstate = dsn * ea_last + _dot_tn(cm, ead)
        wdec = jnp.exp(a_last - a_e)
        xw = xdt * wdec
        dxw = _dot(bm, dsn)
        dxdt = dxw * wdec
        dbm = _dot_nt(xw, dsn)
        zc = dxw * xw
        da_last = jnp.sum(zc, axis=0, keepdims=True) + jnp.sum(dsn * st, axis=0, keepdims=True) * ea_last
        da_e = da_e - zc + jnp.where(_iota((CHUNK, B_GROUP_W), 0) == CHUNK - 1, da_last, 0.0)

        cb, cbt = _dot_nt(cm, bm), _dot_nt(bm, cm)
        dcb, dcbt = jnp.zeros((CHUNK, CHUNK), F32), jnp.zeros((CHUNK, CHUNK), F32)
        da4 = jnp.zeros((CHUNK, LANES), F32)
        lane = _iota((CHUNK, LANES), 1)
        for j in range(HEADS_PER_GROUP):
            mine = head == j
            gm = _dot_nt(jnp.where(mine, dyv, 0.0), xdt)
            gmt = _dot_nt(jnp.where(mine, xdt, 0.0), dyv)
            dec, dect = _ssd_decay(tm, j, False), _ssd_decay(tm, j, True)
            dcb += gm * dec
            dcbt += gmt * dect
            da_j = (jnp.sum(gm * cb * dec, axis=1, keepdims=True) - jnp.sum(gmt * cbt * dect, axis=1, keepdims=True))
            da4 = jnp.where(lane == j, da_j, da4)
            dxdt = dxdt + jnp.where(mine, _dot(cbt * dect, dyv), 0.0)
        dcm = dcm + _dot(dcb, bm)
        dbm = dbm + _dot(dcbt, cm)

        gather = (_iota((B_GROUP_W, LANES), 0) // B_HEAD_DIM == _iota((B_GROUP_W, LANES), 1)).astype(BF16)
        per_head = _dot_sel(jnp.concatenate([da_e, dxdt * xs], axis=0), gather)
        da4 = da4 + per_head[:CHUNK]
        rtri = (_iota((CHUNK, CHUNK), 1) >= _iota((CHUNK, CHUNK), 0)).astype(BF16)
        dadt4 = _sel_dot(rtri, da4)
        a_heads = -jnp.exp(alog_ref[...])
        rows8 = lambda v: jnp.broadcast_to(v, (SUBLANES, LANES))
        ddt4 = dadt4 * _to_group(rows8(a_heads), g)[0:1, :] + per_head[CHUNK:]
        dxs = dxs + dxdt * dt_e
        ddt = _from_group(ddt4, g) * _sig(dtr + dtb_ref[...])
        da_heads = _from_group(rows8(jnp.sum(dadt4 * tm["dt4"], axis=0, keepdims=True)), g)[0:1, :]

        dpx_ref[...] = dxs * _dsilu(px)
        dpb_ref[...] = dbm * _dsilu(pb)
        dpc_ref[...] = dcm * _dsilu(pc)
        return ddt, da_heads * a_heads, dstate, dd_part, dng_part

    def body(*refs):
        (px_ref, pb_ref, pc_ref, dtr_ref, dt_ref, acs_ref, acst_ref, z_ref, dy_ref, ypre_ref, st_ref, alog_ref,
         dtb_ref, df_ref, ng_ref, _) = refs[:16]
        refs = refs[16:]
        c_in, refs = refs[:n_carried], refs[n_carried:]
        dz_ref, dpx_ref, dpb_ref, dpc_ref, ddt_ref, dbias_ref, dalog_ref, dd_ref, dng_ref = refs[:9]
        c_out, dstate_ref, c_sems = refs[9:9 + n_carried], refs[9 + n_carried], refs[10 + n_carried:]
        c, pair = pl.program_id(0), pl.program_id(1)
        if carry:
            pl.when((c == 0) & (pair == 0))(lambda: carry.start(c_in, c_out, c_sems))
        groups = [GROUPS_PER_STEP * pair + u for u in range(GROUPS_PER_STEP)]

        @pl.when(c == 0)
        def _():
            for g in groups:
                dstate_ref[g] = jnp.zeros((B_STATE, B_GROUP_W), F32)
                dd_ref[g] = jnp.zeros((1, B_GROUP_W), F32)
                dng_ref[g] = jnp.zeros((1, B_GROUP_W), F32)

        leaving = [dstate_ref[g] for g in groups]
        ddt, dalog, done = 0.0, 0.0, []
        for u, g in enumerate(groups):
            wide, narrow = _group_cols(u)
            view = lambda ref, cols: ref.at[:, cols]
            ddt_u, dalog_u, *carried = one_group(
                g, leaving[u], view(px_ref, wide), view(pb_ref, narrow), view(pc_ref, narrow), dtr_ref,
                dt_ref, acs_ref, acst_ref, view(z_ref, wide), view(dy_ref, wide), view(ypre_ref, wide), view(st_ref, wide),
                alog_ref, dtb_ref, view(df_ref, wide), view(ng_ref, wide), view(dz_ref, wide), view(dpx_ref, wide),
                view(dpb_ref, narrow), view(dpc_ref, narrow))
            ddt, dalog = ddt + ddt_u, dalog + dalog_u
            done.append(carried)
        for g, (dstate, dd_part, dng_part) in zip(groups, done):
            dstate_ref[g] = dstate
            dd_ref[g] += dd_part
            dng_ref[g] += dng_part
        first = (c == 0) & (pair == 0)
        bias_part = jnp.sum(ddt, axis=0, keepdims=True)

        @pl.when(pair == 0)
        def _():
            ddt_ref[...] = ddt

        @pl.when(pair > 0)
        def _():
            ddt_ref[...] += ddt

        @pl.when(first)
        def _():
            dbias_ref[...] = bias_part
            dalog_ref[...] = dalog

        @pl.when(jnp.logical_not(first))
        def _():
            dbias_ref[...] += bias_part
            dalog_ref[...] += dalog

        if carry:
            last = (c == nc - 1) & (pair == B_GROUPS // GROUPS_PER_STEP - 1)
            pl.when(last)(lambda: carry.finish(c_in, c_out, c_sems))

    gw, gn, tok, tokt, vec, gvec, st = _ssd_specs(nc, True)
    acc = lambda shape: pl.BlockSpec(shape, lambda c, g: (0,) * len(shape))
    rev = lambda c: nc - 1 - c
    out = pl.pallas_call(
        body, grid=(nc, B_GROUPS // GROUPS_PER_STEP), name=name,
        in_specs=[gw(0), gn(WIDTH // B_STATE), gn((WIDTH + B_GROUPS * B_STATE) // B_STATE), tok, tok, tok, tokt,
                  gw(3 * WIDTH // B_GROUP_W), gw(WIDTH // B_GROUP_W), gw(0), st, vec, vec, gvec, gvec, ANY]
        + [ANY] * n_carried,
        out_specs=[gw(3 * WIDTH // B_GROUP_W), gw(0), gn(0), gn(0), tok, vec, vec, acc((B_GROUPS, 1, B_GROUP_W)), acc((B_GROUPS, 1, B_GROUP_W))] + [ANY] * n_carried,
        out_shape=[jax.ShapeDtypeStruct(dh.shape, dh.dtype), jax.ShapeDtypeStruct((t, WIDTH), F32),
                   jax.ShapeDtypeStruct((t, B_GROUPS * B_STATE), F32), jax.ShapeDtypeStruct((t, B_GROUPS * B_STATE), F32),
                   jax.ShapeDtypeStruct((t, LANES), F32), jax.ShapeDtypeStruct((1, LANES), F32),
                   jax.ShapeDtypeStruct((1, LANES), F32), jax.ShapeDtypeStruct((B_GROUPS, 1, B_GROUP_W), F32),
                   jax.ShapeDtypeStruct((B_GROUPS, 1, B_GROUP_W), F32)] + (carry.out_shape if carry else []),
        scratch_shapes=[pltpu.VMEM((B_GROUPS, B_STATE, B_GROUP_W), F32)] + (carry.scratch if carry else []),
        input_output_aliases={15: 0},
        compiler_params=_cp("arbitrary", "arbitrary"),
    )(pre, pre, pre, hdt, dt, acs, acst, h, dy, ypre, states, alog, dtb, dfull, ng, dh,
      *(carry.arrays if carry else []))
    return out[:9], out[9:]


Q_COL, K_COL, V_COL, Z_COL = [(4 + i) * WIDTH // D_HEAD_DIM for i in range(4)]
ATT_SCALE = D_HEAD_DIM ** -0.5


SPAN = 2048


def _att_blocks():
    out = []
    for _, dil in D_PATTERNS:
        nbl = SPAN // (CHUNK * dil)
        for r in range(dil):
            for bl in range(nbl):
                st = r + dil * CHUNK * bl
                out.append((dil, st, bl > 0, st - dil * CHUNK if bl > 0 else r + dil * CHUNK * (nbl - 1)))
    return out


def _rows(start, dil):
    return pl.ds(start, CHUNK) if dil == 1 else pl.ds(start, CHUNK, stride=dil)


def _att_keys(kc_ref, kp_ref, blk):
    dil, st, inside, pst = blk
    prev = (kc_ref if inside else kp_ref)[_rows(pst, dil), :]
    return jnp.concatenate([prev, kc_ref[_rows(st, dil), :]], axis=0).astype(BF16)


def _att_band(span_index):
    lane, sub = _iota((CHUNK, 2 * CHUNK), 1), _iota((CHUNK, 2 * CHUNK), 0)
    band = (lane >= sub) & (lane <= sub + CHUNK)
    return band, band & ((lane >= CHUNK) | (span_index > 0))


def _att_specs(t):
    blk = lambda off: pl.BlockSpec((SPAN, D_HEAD_DIM), lambda hd, sb: (sb, off + hd))
    prev = lambda off: pl.BlockSpec((SPAN, D_HEAD_DIM), lambda hd, sb: (jnp.maximum(sb - 1, 0), off + hd))
    lse = pl.BlockSpec((None, SPAN // CHUNK, CHUNK), lambda hd, sb: (hd, sb, 0))
    return blk, prev, lse


def _attn2_fwd(h, y, name):
    t = h.shape[0]

    def body(q_ref, kc_ref, vc_ref, kp_ref, vp_ref, z_ref, y_in, y_ref, o_ref, lse_ref, m_ref, l_ref):
        band, band_first = _att_band(pl.program_id(1))
        for blk in _att_blocks():
            dil, st, inside, _ = blk
            rows = _rows(st, dil)
            k2, v2 = _att_keys(kc_ref, kp_ref, blk), _att_keys(vc_ref, vp_ref, blk)
            s = jnp.where(band if inside else band_first, _dot_nt(q_ref[rows, :], k2) * ATT_SCALE, NEG)
            m_b = jnp.max(s, axis=1, keepdims=True)
            p = jnp.exp(s - m_b)
            l_b = jnp.sum(p, axis=1, keepdims=True)
            o_b = _dot(p, v2)
            wide = lambda a: jnp.broadcast_to(a, (CHUNK, D_HEAD_DIM))
            if dil == 1:
                m_ref[rows, :], l_ref[rows, :], o_ref[rows, :] = wide(m_b), wide(l_b), o_b
            else:
                m_o = m_ref[rows, :]
                m_n = jnp.maximum(m_o, m_b)
                a_o, a_b = jnp.exp(m_o - m_n), jnp.exp(m_b - m_n)
                m_ref[rows, :] = m_n
                l_ref[rows, :] = a_o * l_ref[rows, :] + a_b * l_b
                o_ref[rows, :] = a_o * o_ref[rows, :] + a_b * o_b
        l = l_ref[...]
        o = o_ref[...] / l
        o_ref[...] = o
        y_ref[...] = (_silu(z_ref[...]) * o).astype(BF16)
        for i in range(SPAN // CHUNK):
            blk_rows = slice(i * CHUNK, (i + 1) * CHUNK)
            lse_ref[i:i + 1, :] = (m_ref[blk_rows, :] + jnp.log(l[blk_rows, :])).T[0:1, :]

    blk, prev, lse_spec = _att_specs(t)
    return pl.pallas_call(
        body, grid=(D_HEADS, t // SPAN), name=name,
        in_specs=[blk(Q_COL), blk(K_COL), blk(V_COL), prev(K_COL), prev(V_COL), blk(Z_COL), ANY],
        out_specs=[blk(WIDTH // D_HEAD_DIM), blk(0), lse_spec],
        out_shape=[jax.ShapeDtypeStruct(y.shape, y.dtype), jax.ShapeDtypeStruct((t, WIDTH), F32),
                   jax.ShapeDtypeStruct((D_HEADS, t // CHUNK, CHUNK), F32)],
        scratch_shapes=[pltpu.VMEM((SPAN, D_HEAD_DIM), F32)] * 2,
        input_output_aliases={6: 0},
        compiler_params=_cp("parallel", "parallel"),
    )(h, h, h, h, h, h, y)


def _attn2_bwd(h, dy, o, lse, dh, name):
    t = h.shape[0]
    ns = t // SPAN

    def body(q_ref, kc_ref, vc_ref, kp_ref, vp_ref, z_ref, dy_ref, o_ref, lse_ref, dh_in, dh_ref,
             acc_ref, dq_ref, do_ref, delta_ref, lsec_ref):
        sb = pl.program_id(1)

        @pl.when(sb == 0)
        def _():
            acc_ref[...] = jnp.zeros_like(acc_ref)

        here, before = pl.multiple_of(sb * SPAN, SPAN), jnp.maximum(sb - 1, 0) * SPAN
        z, ov, dyv = z_ref[...], o_ref[...], dy_ref[...]
        do = dyv * _silu(z)
        do_ref[...] = do
        dh_ref[3, pl.ds(here, SPAN), :] = (dyv * ov * _dsilu(z)).astype(BF16)
        delta_ref[...] = jnp.broadcast_to(jnp.sum(do * ov, axis=1, keepdims=True), (SPAN, D_HEAD_DIM))
        for i in range(SPAN // CHUNK):
            lsec_ref[i * CHUNK:(i + 1) * CHUNK, :] = jnp.broadcast_to(lse_ref[i:i + 1, :], (CHUNK, CHUNK)).T
        band, band_first = _att_band(sb)
        for blk in _att_blocks():
            dil, st, inside, pst = blk
            rows = _rows(st, dil)
            q = q_ref[rows, :].astype(BF16)
            k2, v2 = _att_keys(kc_ref, kp_ref, blk), _att_keys(vc_ref, vp_ref, blk)
            dob = do_ref[rows, :].astype(BF16)
            s = jnp.where(band if inside else band_first, _dot_nt(q, k2) * ATT_SCALE, NEG)
            p = jnp.exp(s - lsec_ref[rows, :][:, 0:1])
            ds = p * (_dot_nt(dob, v2) - delta_ref[rows, :][:, 0:1]) * ATT_SCALE
            dq_b = _dot(ds, k2)
            if dil == 1:
                dq_ref[rows, :] = dq_b
            else:
                dq_ref[rows, :] += dq_b
            dk2, dv2 = _dot_tn(ds, q), _dot_tn(p, dob)
            own = _rows(pl.multiple_of(here + st, CHUNK) if dil == 1 else here + st, dil)
            pbase = (here if inside else before) + pst
            prv = _rows(pl.multiple_of(pbase, CHUNK) if dil == 1 else pbase, dil)
            acc_ref[0, own, :] += dk2[CHUNK:]
            acc_ref[1, own, :] += dv2[CHUNK:]
            acc_ref[0, prv, :] += dk2[:CHUNK]
            acc_ref[1, prv, :] += dv2[:CHUNK]
        dh_ref[0, pl.ds(here, SPAN), :] = dq_ref[...].astype(BF16)

        @pl.when(sb == ns - 1)
        def _():
            dh_ref[1] = acc_ref[0].astype(BF16)
            dh_ref[2] = acc_ref[1].astype(BF16)

    blk, prev, lse_spec = _att_specs(t)
    span = lambda: pltpu.VMEM((SPAN, D_HEAD_DIM), F32)
    return pl.pallas_call(
        body, grid=(D_HEADS, ns), name=name,
        in_specs=[blk(Q_COL), blk(K_COL), blk(V_COL), prev(K_COL), prev(V_COL), blk(Z_COL), blk(WIDTH // D_HEAD_DIM),
                  blk(0), lse_spec, ANY],
        out_specs=pl.BlockSpec((4, t, D_HEAD_DIM), lambda hd, sb: (1, 0, hd)),
        out_shape=jax.ShapeDtypeStruct(dh.shape, dh.dtype),
        scratch_shapes=[pltpu.VMEM((2, t, D_HEAD_DIM), F32), span(), span(), span(), span()],
        input_output_aliases={9: 0},
        compiler_params=_cp("arbitrary", "arbitrary"),
    )(h, h, h, h, h, h, dy, o, lse, dh)


def _place():
    x, y, c = lax.axis_index("x"), lax.axis_index("y"), lax.axis_index("c")
    return x, y, c, 4 * x + 2 * y + c


class _Exchange:
    def __init__(self, arrays, out_shape):
        n = len(arrays)
        self.arrays, self.out_shape = list(arrays), out_shape
        self.scratch = [pltpu.SemaphoreType.DMA((n, 7)), pltpu.SemaphoreType.DMA((n, 7)), pltpu.SemaphoreType.DMA((n,))]


class _AllGather(_Exchange):
    def __init__(self, blocks):
        super().__init__(blocks, [jax.ShapeDtypeStruct((N_DEV,) + b.shape, b.dtype) for b in blocks])

    def _plan(self, ins, outs, sems):
        send_sems, recv_sems, local_sems = sems
        x, y, c, me = _place()
        chips = [(1 - x, y), (x, 1 - y), (1 - x, 1 - y)]

        def copy(a, k, block, to, src=None):
            dst = outs[a].at[block]
            return pltpu.make_async_remote_copy(
                src_ref=dst if src is None else src, dst_ref=dst, send_sem=send_sems.at[a, k],
                recv_sem=recv_sems.at[a, k], device_id=to, device_id_type=MESH)

        n = len(ins)
        index = lambda px, py, pc: 4 * px + 2 * py + pc
        mine = [pltpu.make_async_copy(ins[a], outs[a].at[me], local_sems.at[a]) for a in range(n)]
        first = [copy(a, 0, me, (x, y, 1 - c), src=ins[a]) for a in range(n)]
        first += [copy(a, 1 + j, me, (*chip, c), src=ins[a]) for j, chip in enumerate(chips) for a in range(n)]
        arrive = lambda a, k, px, py, pc: copy(a, k, index(px, py, pc), (x, y, c))
        over_ici = [[arrive(a, 1 + j, *chip, c) for a in range(n)] for j, chip in enumerate(chips)]
        passed = [[copy(a, 4 + j, index(*chip, c), (x, y, 1 - c)) for a in range(n)] for j, chip in enumerate(chips)]
        from_sibling = [arrive(a, 0, x, y, 1 - c) for a in range(n)]
        from_sibling += [arrive(a, 4 + j, *chip, 1 - c) for j, chip in enumerate(chips) for a in range(n)]
        return mine, first, over_ici, passed, from_sibling

    def start(self, ins, outs, sems):
        mine, first, _, _, _ = self._plan(ins, outs, sems)
        for cp in mine + first:
            cp.start()

    def finish(self, ins, outs, sems):
        mine, first, over_ici, passed, from_sibling = self._plan(ins, outs, sems)
        for landed, onward in zip(over_ici, passed):
            for cp, fwd in zip(landed, onward):
                cp.wait_recv()
                fwd.start()
        for cp in from_sibling:
            cp.wait_recv()
        for cp in first + [fwd for onward in passed for fwd in onward]:
            cp.wait_send()
        for cp in mine:
            cp.wait()


class _AllToAll(_Exchange):
    def __init__(self, parts):
        super().__init__(parts, [jax.ShapeDtypeStruct(p.shape, p.dtype) for p in parts])

    def _plan(self, ins, outs, sems):
        send_sems, recv_sems, local_sems = sems
        x, y, c, me = _place()
        flip = lambda v, f: 1 - v if f else v
        peers = [(flip(x, fx), flip(y, fy), flip(c, fc)) for fx in (0, 1) for fy in (0, 1) for fc in (0, 1)][1:]

        def copy(a, k, sending):
            px, py, pc = peers[k]
            there = 4 * px + 2 * py + pc
            return pltpu.make_async_remote_copy(
                src_ref=ins[a].at[there], dst_ref=outs[a].at[me if sending else there], send_sem=send_sems.at[a, k],
                recv_sem=recv_sems.at[a, k], device_id=peers[k], device_id_type=MESH)

        n = len(ins)
        local = [pltpu.make_async_copy(ins[a].at[me], outs[a].at[me], local_sems.at[a]) for a in range(n)]
        return local, [[copy(a, k, sending) for k in range(7) for a in range(n)] for sending in (True, False)]

    def start(self, ins, outs, sems):
        local, (sends, _) = self._plan(ins, outs, sems)
        for cp in local + sends:
            cp.start()

    def finish(self, ins, outs, sems):
        local, (_, both_ways) = self._plan(ins, outs, sems)
        for cp in both_ways + local:
            cp.wait()


def _exchange(ex, name):
    n = len(ex.arrays)

    def body(*refs):
        ins, outs, sems = refs[:n], refs[n:2 * n], refs[2 * n:]
        ex.start(ins, outs, sems)
        ex.finish(ins, outs, sems)

    return pl.pallas_call(body, name=name, in_specs=[ANY] * n, out_specs=[ANY] * n, out_shape=ex.out_shape,
                          scratch_shapes=ex.scratch)(*ex.arrays)


def _sum_parts(parts, tc, name):
    _, r, c = parts.shape

    def body(p_ref, o_ref):
        acc = p_ref[0].astype(F32)
        for d in range(1, N_DEV):
            acc = acc + p_ref[d].astype(F32)
        o_ref[...] = acc

    return pl.pallas_call(
        body, grid=(c // tc,), name=name,
        in_specs=[pl.BlockSpec((N_DEV, r, tc), lambda j: (0, 0, j))], out_specs=pl.BlockSpec((r, tc), lambda j: (0, j)),
        out_shape=jax.ShapeDtypeStruct((r, c), F32), compiler_params=_cp("parallel"),
    )(parts)


def _adamw(w, g, m, v, tr, name):
    r, c = w.shape

    def body(w_ref, g_ref, m_ref, v_ref, d_ref, m2_ref, v2_ref):
        gv = g_ref[...]
        m2 = ADAM_B1 * m_ref[...] + (1.0 - ADAM_B1) * gv
        v2 = ADAM_B2 * v_ref[...] + (1.0 - ADAM_B2) * (gv * gv)
        m_hat = m2 / (1.0 - ADAM_B1 ** ADAM_STEP)
        v_hat = v2 / (1.0 - ADAM_B2 ** ADAM_STEP)
        d_ref[...] = -ADAM_LR * (m_hat / (jnp.sqrt(v_hat) + ADAM_EPS) + ADAM_WD * w_ref[...])
        m2_ref[...] = m2
        v2_ref[...] = v2

    spec = pl.BlockSpec((tr, c), lambda i: (i, 0))
    return pl.pallas_call(
        body, grid=(r // tr,), name=name, in_specs=[spec] * 4, out_specs=[spec] * 3,
        out_shape=[jax.ShapeDtypeStruct((r, c), F32)] * 3, compiler_params=_cp("parallel"),
    )(w, g, m, v)


PACK_ROWS = SUBLANES * LANES


def _pack(arrays):
    flat = [jnp.pad(a.reshape(-1), (0, -a.size % PACK_ROWS)) for a in arrays]
    return jnp.concatenate(flat).reshape(-1, LANES)


def _unpack(packed, shapes):
    flat, out, pos = packed.reshape(-1), [], 0
    for s in shapes:
        size = 1
        for d in s:
            size *= d
        out.append(flat[pos:pos + size].reshape(s))
        pos += size + (-size % PACK_ROWS)
    return out


SMALL = ["even_norm_g", "gmlp_ln_g", "gmlp_ln_b", "gmlp_ws", "gmlp_bs", "ssd_conv_w", "ssd_conv_b", "ssd_dt_bias",
         "ssd_a_log", "ssd_d", "ssd_norm_g", "odd_norm_g", "sconv_w", "final_norm_g"]
ORDER = ["even_norm_g", "even_w_in", "gmlp_ln_g", "gmlp_ln_b", "gmlp_ws", "gmlp_bs", "ssd_conv_w", "ssd_conv_b",
         "ssd_dt_bias", "ssd_a_log", "ssd_d", "ssd_norm_g", "even_w_out", "odd_norm_g", "odd_w_in", "sconv_w",
         "odd_w_out", "final_norm_g"]


def kernel(x, even_norm_g, even_w_in, gmlp_ln_g, gmlp_ln_b, gmlp_ws, gmlp_bs, ssd_conv_w, ssd_conv_b, ssd_dt_bias, ssd_a_log, ssd_d, ssd_norm_g, even_w_out, odd_norm_g, odd_w_in, sconv_w, odd_w_out, final_norm_g, loss_target, m_even_norm_g, m_even_w_in, m_gmlp_ln_g, m_gmlp_ln_b, m_gmlp_ws, m_gmlp_bs, m_ssd_conv_w, m_ssd_conv_b, m_ssd_dt_bias, m_ssd_a_log, m_ssd_d, m_ssd_norm_g, m_even_w_out, m_odd_norm_g, m_odd_w_in, m_sconv_w, m_odd_w_out, m_final_norm_g, v_even_norm_g, v_even_w_in, v_gmlp_ln_g, v_gmlp_ln_b, v_gmlp_ws, v_gmlp_bs, v_ssd_conv_w, v_ssd_conv_b, v_ssd_dt_bias, v_ssd_a_log, v_ssd_d, v_ssd_norm_g, v_even_w_out, v_odd_norm_g, v_odd_w_in, v_sconv_w, v_odd_w_out, v_final_norm_g):
    w = dict(even_norm_g=even_norm_g, even_w_in=even_w_in, gmlp_ln_g=gmlp_ln_g, gmlp_ln_b=gmlp_ln_b, gmlp_ws=gmlp_ws,
             gmlp_bs=gmlp_bs, ssd_conv_w=ssd_conv_w, ssd_conv_b=ssd_conv_b, ssd_dt_bias=ssd_dt_bias,
             ssd_a_log=ssd_a_log, ssd_d=ssd_d, ssd_norm_g=ssd_norm_g, even_w_out=even_w_out, odd_norm_g=odd_norm_g,
             odd_w_in=odd_w_in, sconv_w=sconv_w, odd_w_out=odd_w_out, final_norm_g=final_norm_g)
    m1 = dict(even_norm_g=m_even_norm_g, even_w_in=m_even_w_in, gmlp_ln_g=m_gmlp_ln_g, gmlp_ln_b=m_gmlp_ln_b,
              gmlp_ws=m_gmlp_ws, gmlp_bs=m_gmlp_bs, ssd_conv_w=m_ssd_conv_w, ssd_conv_b=m_ssd_conv_b,
              ssd_dt_bias=m_ssd_dt_bias, ssd_a_log=m_ssd_a_log, ssd_d=m_ssd_d, ssd_norm_g=m_ssd_norm_g,
              even_w_out=m_even_w_out, odd_norm_g=m_odd_norm_g, odd_w_in=m_odd_w_in, sconv_w=m_sconv_w,
              odd_w_out=m_odd_w_out, final_norm_g=m_final_norm_g)
    m2 = dict(even_norm_g=v_even_norm_g, even_w_in=v_even_w_in, gmlp_ln_g=v_gmlp_ln_g, gmlp_ln_b=v_gmlp_ln_b,
              gmlp_ws=v_gmlp_ws, gmlp_bs=v_gmlp_bs, ssd_conv_w=v_ssd_conv_w, ssd_conv_b=v_ssd_conv_b,
              ssd_dt_bias=v_ssd_dt_bias, ssd_a_log=v_ssd_a_log, ssd_d=v_ssd_d, ssd_norm_g=v_ssd_norm_g,
              even_w_out=v_even_w_out, odd_norm_g=v_odd_norm_g, odd_w_in=v_odd_w_in, sconv_w=v_sconv_w,
              odd_w_out=v_odd_w_out, final_norm_g=v_final_norm_g)
    _, _, _, me = _place()
    xs = x[0]
    shard = WIDTH // N_DEV

    small_blk = jnp.concatenate([
        ssd_conv_w[0], jnp.pad(sconv_w[0], ((0, 0), (0, shard))), jnp.pad(odd_norm_g, ((0, 0), (0, shard)))], axis=0)
    g_wte, g_small = _exchange(_AllGather([even_w_in[0].T.astype(BF16), small_blk]), "gather_even_w_in")
    next_weights = _AllGather([odd_w_in[0].T.astype(BF16)])
    out_weights = _AllGather([even_w_out[0].astype(BF16), odd_w_out[0].astype(BF16)])
    wte = g_wte.reshape(F_EVEN_ALL, D_MODEL)
    wte_dt = jnp.pad(wte[F_EVEN:], ((0, LANES - B_HEADS), (0, 0)))
    conv_w = g_small[:, 0:B_CONV, :].transpose(1, 0, 2).reshape(B_CONV, B_XBC)
    sconv_full = g_small[:, B_CONV:B_CONV + C_CONV, :shard].transpose(1, 0, 2).reshape(C_CONV, WIDTH)
    odd_g = g_small[:, B_CONV + C_CONV, :shard].reshape(1, WIDTH)

    pad_heads = lambda a: jnp.pad(a, ((0, 0), (0, LANES - B_HEADS)))
    alog, dtb = pad_heads(ssd_a_log), pad_heads(ssd_dt_bias)
    d_full = jnp.repeat(ssd_d, B_HEAD_DIM, axis=1)
    ws, bs_t = gmlp_ws[0], gmlp_bs[0].T
    ws_t = jnp.swapaxes(ws, 1, 2)
    proj = dict(tb=True, tm=1024, tn=1024, tk=D_MODEL, out_dtype=F32)
    out_proj = dict(n=D_MODEL, tm=1024, tn=1024, tk=2 * WIDTH, out_dtype=F32)
    dw_out = dict(ta=True, n=D_MODEL, tm=1024, tn=D_MODEL, tk=2048, out_dtype=BF16)
    dx_in = dict(n=D_MODEL, tm=1024, tn=1024, tk=2048, out_dtype=F32)
    dw_in = dict(ta=True, n=D_MODEL, tm=1024, tn=D_MODEL, tk=2048, out_dtype=BF16)
    slabs = lambda g, rows: g.reshape(N_DEV, rows // N_DEV, D_MODEL)

    xn0 = _rms_fwd(xs, even_norm_g, "norm_even")
    h0, (g_wto,) = _mm(xn0, wte, n=F_EVEN, carry=next_weights, name="proj_even", **proj)
    wto = g_wto.reshape(F_ODD, D_MODEL)
    hdt = _mm(xn0, wte_dt, tb=True, n=LANES, tm=1024, tn=LANES, tk=D_MODEL, out_dtype=F32, name="proj_dt")
    y0 = _gmlp_fwd(h0, gmlp_ln_g, gmlp_ln_b, ws, bs_t, "gmlp_fwd")
    pre = _ssd_conv_fwd(h0, conv_w, ssd_conv_b, "ssd_conv_fwd")
    dt, acs, acst = _ssd_dt(hdt, alog, dtb, "ssd_dt")
    (y0, ypre, states), (g_woe, g_woo) = _ssd_scan_fwd(
        pre, dt, acs, acst, h0, y0, d_full, ssd_norm_g, out_weights, "ssd_scan_fwd")
    woe, woo = g_woe.reshape(2 * WIDTH, D_MODEL), g_woo.reshape(2 * WIDTH, D_MODEL)
    x1 = _mm(y0, woe, add=xs, name="out_even", **out_proj)
    xn1 = _rms_fwd(x1, odd_g, "norm_odd")
    h1 = _mm(xn1, wto, n=F_ODD, name="proj_odd", **proj)
    y1 = _sconv_fwd(h1, sconv_full, "sconv_fwd")
    y1, att_o, att_lse = _attn2_fwd(h1, y1, "attn_fwd")
    x2 = _mm(y1, woo, add=x1, name="out_odd", **out_proj)
    loss_part, dx2, g_final, dx2_b = _loss_head(x2, final_norm_g.reshape(1, D_MODEL), loss_target[0], "loss_head")

    dy1 = _mm(dx2_b, woo, n=2 * WIDTH, name="dy_odd", **proj)
    gw_woo = _mm(y1, dx2_b, name="dw_out_odd", **dw_out)
    dh1, g_sconv = _sconv_bwd(h1, dy1, sconv_full, "sconv_bwd")
    dh1 = _attn2_bwd(h1, dy1, att_o, att_lse, dh1, "attn_bwd")
    dxn1, (r_woo,) = _mm(dh1, wto, carry=_AllToAll([slabs(gw_woo, 2 * WIDTH)]), name="dx_odd", **dx_in)
    gw_wto = _mm(dh1, xn1, name="dw_in_odd", **dw_in)
    dx1, g_odd, dx1_b = _rms_bwd(x1, odd_g, dxn1, dx2, True, "norm_odd_bwd")

    dy0 = _mm(dx1_b, woe, n=2 * WIDTH, name="dy_even", **proj)
    gw_woe = _mm(y0, dx1_b, name="dw_out_even", **dw_out)
    dh0, g_ws, g_bs_t, g_ln_g, g_ln_b = _gmlp_bwd(h0, dy0, gmlp_ln_g, gmlp_ln_b, ws, ws_t, bs_t, "gmlp_bwd")
    (dh0, dpx, dpb, dpc, ddt, g_dtb, g_alog, g_dd, g_ng), (r_wto,) = _ssd_scan_bwd(
        pre, hdt, dt, acs, acst, h0, dy0, ypre, states, dh0, alog, dtb, d_full, ssd_norm_g,
        _AllToAll([slabs(gw_wto, F_ODD)]), "ssd_scan_bwd")
    dh0, g_conv_w, g_conv_b = _ssd_conv_bwd(h0, dh0, dpx, dpb, dpc, conv_w, "ssd_conv_bwd")
    gw_main, (r_woe,) = _mm(dh0, xn0, out_rows=F_EVEN_ALL, carry=_AllToAll([slabs(gw_woe, 2 * WIDTH)]),
                            name="dw_in_even", **dw_in)
    gw_dt = _mm(ddt, xn0, ta=True, n=D_MODEL, tm=LANES, tn=D_MODEL, tk=1024, out_dtype=BF16, name="dw_dt")
    gw_wte = lax.dynamic_update_slice(gw_main, gw_dt[:B_HEADS], (F_EVEN, 0))
    dxn0_dt = _mm(ddt, wte_dt, n=D_MODEL, tm=1024, tn=D_MODEL, tk=LANES, out_dtype=F32, name="dx_dt")
    dxn0, (r_wte,) = _mm(dh0, wte, add=dxn0_dt, carry=_AllToAll([slabs(gw_wte, F_EVEN_ALL)]), name="dx_even", **dx_in)
    grad_x, g_even = _rms_bwd(xs, even_norm_g, dxn0, dx1, False, "norm_even_bwd")

    small_parts = dict(
        even_norm_g=g_even, gmlp_ln_g=g_ln_g, gmlp_ln_b=g_ln_b, gmlp_ws=g_ws, gmlp_bs=g_bs_t[:, :A_GROUPS].T,
        ssd_conv_w=g_conv_w[:B_CONV], ssd_conv_b=g_conv_b, ssd_dt_bias=g_dtb[:, :B_HEADS], ssd_a_log=g_alog[:, :B_HEADS],
        ssd_d=g_dd.reshape(B_HEADS, B_HEAD_DIM).sum(axis=1), ssd_norm_g=g_ng, odd_norm_g=g_odd,
        sconv_w=g_sconv[:C_CONV], final_norm_g=g_final)
    full_shapes = dict(
        even_norm_g=(1, D_MODEL), gmlp_ln_g=(1, WIDTH), gmlp_ln_b=(1, WIDTH), gmlp_ws=(1, A_GROUPS, CHUNK, CHUNK),
        gmlp_bs=(1, A_GROUPS, CHUNK), ssd_conv_w=(1, B_CONV, B_XBC), ssd_conv_b=(1, B_XBC), ssd_dt_bias=(1, B_HEADS),
        ssd_a_log=(1, B_HEADS), ssd_d=(1, B_HEADS), ssd_norm_g=(1, WIDTH), odd_norm_g=(1, D_MODEL),
        sconv_w=(1, C_CONV, WIDTH), final_norm_g=(D_MODEL,))
    (gathered_small,) = _exchange(_AllGather([_pack([small_parts[k] for k in SMALL])]), "gather_small_grads")
    small_sum = _sum_parts(gathered_small, LANES, "sum_small_grads")
    grads = dict(zip(SMALL, _unpack(small_sum, [full_shapes[k] for k in SMALL])))
    grads["ssd_conv_w"] = lax.dynamic_slice_in_dim(grads["ssd_conv_w"], me * 2 * shard, 2 * shard, axis=2)
    grads["odd_norm_g"] = lax.dynamic_slice_in_dim(grads["odd_norm_g"], me * shard, shard, axis=1)
    grads["sconv_w"] = lax.dynamic_slice_in_dim(grads["sconv_w"], me * shard, shard, axis=2)

    grads["even_w_in"] = _sum_parts(r_wte, 256, "sum_even_w_in").T[None]
    grads["odd_w_in"] = _sum_parts(r_wto, 256, "sum_odd_w_in").T[None]
    grads["even_w_out"] = _sum_parts(r_woe, 512, "sum_even_w_out")[None]
    grads["odd_w_out"] = _sum_parts(r_woo, 512, "sum_odd_w_out")[None]

    delta, new_m, new_v = {}, {}, {}
    for k in ("even_w_in", "odd_w_in", "even_w_out", "odd_w_out"):
        d_k, m_k, v_k = _adamw(w[k][0], grads[k][0], m1[k][0], m2[k][0], 128, "adamw_" + k)
        delta[k], new_m[k], new_v[k] = d_k[None], m_k[None], v_k[None]
    packed = [_pack([src[k] for k in SMALL]) for src in (w, grads, m1, m2)]
    small_out = _adamw(*packed, packed[0].shape[0], "adamw_small")
    shapes = [w[k].shape for k in SMALL]
    for dst, arr in zip((delta, new_m, new_v), small_out):
        dst.update(zip(SMALL, _unpack(arr, shapes)))

    loss = lax.psum(loss_part[0, 0], ("x", "y", "c"))
    return (loss, grad_x[None], *[grads[k] for k in ORDER], *[delta[k] for k in ORDER],
            *[new_m[k] for k in ORDER], *[new_v[k] for k in ORDER])
```

```python
import functools

import jax
import jax.numpy as jnp
from jax import lax
from jax.experimental import pallas as pl
from jax.experimental.pallas import tpu as pltpu

F32, BF16 = jnp.float32, jnp.bfloat16
MESH = pl.DeviceIdType.MESH
ANY = pl.BlockSpec(memory_space=pl.ANY)

N_DEV = 8
D_MODEL = 2048
WIDTH = 2048
CHUNK = 128
A_GROUPS = 8
B_HEADS, B_HEAD_DIM, B_GROUPS, B_STATE, B_CONV = 32, 64, 8, 128, 4
B_GROUP_W = WIDTH // B_GROUPS
B_XBC = WIDTH + 2 * B_GROUPS * B_STATE
C_CONV = 3
D_HEADS, D_HEAD_DIM = 16, 128
D_PATTERNS = ((128, 1), (512, 4), (2048, 16))
F_EVEN = 3 * WIDTH + WIDTH + B_XBC
F_EVEN_ALL = F_EVEN + B_HEADS
F_ODD = 8 * WIDTH
EPS = 1e-5
NEG = -1e30

ADAM_LR, ADAM_B1, ADAM_B2, ADAM_EPS, ADAM_WD, ADAM_STEP = 0.001, 0.9, 0.999, 1e-08, 0.01, 10

VMEM_LIMIT_V7X = 56 * 1024 * 1024
SUBLANES, LANES = 8, 128


def _cp(*sem):
    return pltpu.CompilerParams(dimension_semantics=sem, vmem_limit_bytes=VMEM_LIMIT_V7X)


def _sig(x):
    return 0.5 * jnp.tanh(0.5 * x) + 0.5


def _silu(x):
    return x * _sig(x)


def _dsilu(x):
    s = _sig(x)
    return s * (1.0 + x * (1.0 - s))


def _softplus(x):
    return jnp.maximum(x, 0.0) + jnp.log(1.0 + jnp.exp(-jnp.abs(x)))


def _dot(a, b):
    return jnp.dot(a.astype(BF16), b.astype(BF16), preferred_element_type=F32)


def _dot_nt(a, b):
    return lax.dot_general(a.astype(BF16), b.astype(BF16), (((1,), (1,)), ((), ())), preferred_element_type=F32)


def _dot_tn(a, b):
    return lax.dot_general(a.astype(BF16), b.astype(BF16), (((0,), (0,)), ((), ())), preferred_element_type=F32)


def _split3(x):
    hi = x.astype(BF16)
    r1 = x - hi.astype(F32)
    mid = r1.astype(BF16)
    lo = (r1 - mid.astype(F32)).astype(BF16)
    return hi, mid, lo


def _dot_sel(x, sel):
    return sum(jnp.dot(p, sel, preferred_element_type=F32) for p in _split3(x))


def _sel_dot(sel, x):
    return sum(jnp.dot(sel, p, preferred_element_type=F32) for p in _split3(x))


def _iota(shape, axis):
    return lax.broadcasted_iota(jnp.int32, shape, axis)


def _shift_down(cur, halo, j):
    if j == 0:
        return cur
    r = pltpu.roll(cur, j, 0)
    top = jnp.where(_iota(halo.shape, 0) < j, pltpu.roll(halo, j, 0), r[0:SUBLANES])
    return jnp.concatenate([top, r[SUBLANES:]], axis=0)


def _shift_up(cur, halo, j):
    if j == 0:
        return cur
    n = cur.shape[0]
    r = pltpu.roll(cur, n - j, 0)
    bot = jnp.where(_iota(halo.shape, 0) >= SUBLANES - j, pltpu.roll(halo, SUBLANES - j, 0), r[n - SUBLANES:])
    return jnp.concatenate([r[:n - SUBLANES], bot], axis=0)


def _mm(a, b, *, ta=False, tb=False, n, tm, tn, tk, out_dtype, out_rows=None, add=None, carry=None, name):
    width, rows = a.shape[-1], a.shape[-2]
    feat = width * (a.shape[0] if a.ndim == 3 else 1)
    m, k_len = (feat, rows) if ta else (rows, feat)
    per_part = width // (tm if ta else tk)
    grid = (m // tm, n // tn, k_len // tk)
    nk = grid[2]

    def a_index(i, j, k):
        f = i if ta else k
        pos = (k,) if ta else (i,)
        return pos + (f,) if a.ndim == 2 else (f // per_part,) + pos + (f % per_part,)

    a_block = (tk, tm) if ta else (tm, tk)
    a_spec = pl.BlockSpec(a_block if a.ndim == 2 else (None,) + a_block, a_index)
    b_spec = pl.BlockSpec((tn, tk), lambda i, j, k: (j, k)) if tb else pl.BlockSpec((tk, tn), lambda i, j, k: (k, j))
    io_spec = pl.BlockSpec((tm, tn), lambda i, j, k: (i, j))
    dims = (((0 if ta else 1,), (1 if tb else 0,)), ((), ()))
    has_add, nc = add is not None, len(carry.arrays) if carry else 0

    def body(*refs):
        a_ref, b_ref = refs[0], refs[1]
        add_ref = refs[2] if has_add else None
        pos = 2 + has_add
        c_in, o_ref, c_out = refs[pos:pos + nc], refs[pos + nc], refs[pos + nc + 1:pos + 2 * nc + 1]
        pos += 2 * nc + 1
        acc_ref = refs[pos] if nk > 1 else None
        c_sems = refs[pos + (nk > 1):]
        i, j, k = pl.program_id(0), pl.program_id(1), pl.program_id(2)
        if carry:
            pl.when((i == 0) & (j == 0) & (k == 0))(lambda: carry.start(c_in, c_out, c_sems))

        def finish(r):
            if add_ref is not None:
                r = r + add_ref[...]
            o_ref[...] = r.astype(out_dtype)

        p = lax.dot_general(a_ref[...].astype(BF16), b_ref[...].astype(BF16), dims, preferred_element_type=F32)
        if nk == 1:
            finish(p)
        else:
            @pl.when(k == 0)
            def _():
                acc_ref[...] = p

            @pl.when((k > 0) & (k < nk - 1))
            def _():
                acc_ref[...] += p

            @pl.when(k == nk - 1)
            def _():
                finish(acc_ref[...] + p)

        if carry:
            pl.when((i == grid[0] - 1) & (j == grid[1] - 1) & (k == nk - 1))(lambda: carry.finish(c_in, c_out, c_sems))

    out = pl.pallas_call(
        body, grid=grid, name=name,
        in_specs=[a_spec, b_spec] + [io_spec] * has_add + [ANY] * nc,
        out_specs=[io_spec] + [ANY] * nc,
        out_shape=[jax.ShapeDtypeStruct((out_rows or m, n), out_dtype)] + (carry.out_shape if carry else []),
        scratch_shapes=([pltpu.VMEM((tm, tn), F32)] if nk > 1 else []) + (carry.scratch if carry else []),
        compiler_params=_cp(*(("arbitrary",) * 3 if carry else ("parallel", "parallel", "arbitrary"))),
    )(a, b, *([add] if has_add else []), *(carry.arrays if carry else []))
    return (out[0], out[1:]) if carry else out[0]


def _rms_fwd(x, g, name):
    t, tb = x.shape[0], 512

    def body(x_ref, g_ref, o_ref):
        xv = x_ref[...]
        r = lax.rsqrt(jnp.mean(xv * xv, axis=-1, keepdims=True) + EPS)
        o_ref[...] = (xv * r * g_ref[...]).astype(BF16)

    row = pl.BlockSpec((tb, D_MODEL), lambda i: (i, 0))
    return pl.pallas_call(
        body, grid=(t // tb,), name=name,
        in_specs=[row, pl.BlockSpec((1, D_MODEL), lambda i: (0, 0))], out_specs=row,
        out_shape=jax.ShapeDtypeStruct((t, D_MODEL), BF16), compiler_params=_cp("parallel"),
    )(x, g)


def _rms_bwd(x, g, dxn, dres, bf16_copy, name):
    t, tb = x.shape[0], 256

    def body(x_ref, g_ref, dxn_ref, dres_ref, dx_ref, dg_ref, *dxb_ref):
        xv = x_ref[...]
        r = lax.rsqrt(jnp.mean(xv * xv, axis=-1, keepdims=True) + EPS)
        nv = xv * r
        dy = dxn_ref[...]
        dn = dy * g_ref[...]
        dx = dres_ref[...] + r * (dn - nv * jnp.mean(dn * nv, axis=-1, keepdims=True))
        dx_ref[...] = dx
        for ref in dxb_ref:
            ref[...] = dx.astype(BF16)
        part = jnp.sum(dy * nv, axis=0, keepdims=True)

        @pl.when(pl.program_id(0) == 0)
        def _():
            dg_ref[...] = part

        @pl.when(pl.program_id(0) > 0)
        def _():
            dg_ref[...] += part

    row = pl.BlockSpec((tb, D_MODEL), lambda i: (i, 0))
    vec = pl.BlockSpec((1, D_MODEL), lambda i: (0, 0))
    return pl.pallas_call(
        body, grid=(t // tb,), name=name,
        in_specs=[row, vec, row, row], out_specs=[row, vec] + [row] * bf16_copy,
        out_shape=[jax.ShapeDtypeStruct((t, D_MODEL), F32), jax.ShapeDtypeStruct((1, D_MODEL), F32)]
        + [jax.ShapeDtypeStruct((t, D_MODEL), BF16)] * bf16_copy,
        compiler_params=_cp("arbitrary"),
    )(x, g, dxn, dres)


def _loss_head(x, g, target, name):
    t, tb = x.shape[0], 256

    def body(x_ref, g_ref, t_ref, loss_ref, dx_ref, dg_ref, dxb_ref):
        xv, gv = x_ref[...], g_ref[...]
        r = lax.rsqrt(jnp.mean(xv * xv, axis=-1, keepdims=True) + EPS)
        nv = xv * r
        err = nv * gv - t_ref[...]
        lpart = 0.5 * jnp.sum(jnp.mean(err * err, axis=-1, keepdims=True), axis=0, keepdims=True)
        dy = err * (1.0 / D_MODEL)
        dn = dy * gv
        dx = r * (dn - nv * jnp.mean(dn * nv, axis=-1, keepdims=True))
        dx_ref[...] = dx
        dxb_ref[...] = dx.astype(BF16)
        gpart = jnp.sum(dy * nv, axis=0, keepdims=True)

        @pl.when(pl.program_id(0) == 0)
        def _():
            dg_ref[...] = gpart
            loss_ref[...] = jnp.broadcast_to(lpart, (1, LANES))

        @pl.when(pl.program_id(0) > 0)
        def _():
            dg_ref[...] += gpart
            loss_ref[...] += jnp.broadcast_to(lpart, (1, LANES))

    row = pl.BlockSpec((tb, D_MODEL), lambda i: (i, 0))
    vec = pl.BlockSpec((1, D_MODEL), lambda i: (0, 0))
    return pl.pallas_call(
        body, grid=(t // tb,), name=name,
        in_specs=[row, vec, row], out_specs=[pl.BlockSpec((1, LANES), lambda i: (0, 0)), row, vec, row],
        out_shape=[jax.ShapeDtypeStruct((1, LANES), F32), jax.ShapeDtypeStruct((t, D_MODEL), F32),
                   jax.ShapeDtypeStruct((1, D_MODEL), F32), jax.ShapeDtypeStruct((t, D_MODEL), BF16)],
        compiler_params=_cp("arbitrary"),
    )(x, g, target)


A_GW = WIDTH // A_GROUPS


def _gmlp_common(v, lg, lb):
    xc = v - jnp.mean(v, axis=-1, keepdims=True)
    rs = lax.rsqrt(jnp.mean(xc * xc, axis=-1, keepdims=True) + EPS)
    vh = xc * rs
    return rs, vh, (vh * lg + lb).astype(BF16)


def _gmlp_fwd(h, ln_g, ln_b, ws, bs_t, name):
    t, tb = h.shape[0], 256

    def body(u_ref, v_ref, z_ref, lg_ref, lb_ref, ws_ref, bst_ref, y_ref):
        _, _, vn = _gmlp_common(v_ref[...], lg_ref[...], lb_ref[...])
        causal = _iota((CHUNK, CHUNK), 1) <= _iota((CHUNK, CHUNK), 0)
        for g in range(A_GROUPS):
            w = jnp.where(causal, ws_ref[g], 0.0).astype(BF16)
            cols = slice(g * A_GW, (g + 1) * A_GW)
            for c in range(tb // CHUNK):
                rows = slice(c * CHUNK, (c + 1) * CHUNK)
                mixed = jnp.dot(w, vn[rows, cols], preferred_element_type=F32) + bst_ref[:, g:g + 1]
                y_ref[rows, cols] = (_silu(z_ref[rows, cols]) * (u_ref[rows, cols] * mixed)).astype(BF16)

    col = lambda j: pl.BlockSpec((tb, WIDTH), lambda i: (i, j))
    full = lambda a: pl.BlockSpec(a.shape, lambda i: (0,) * a.ndim)
    return pl.pallas_call(
        body, grid=(t // tb,), name=name,
        in_specs=[col(0), col(1), col(2), full(ln_g), full(ln_b), full(ws), full(bs_t)],
        out_specs=col(0), out_shape=jax.ShapeDtypeStruct((t, 2 * WIDTH), BF16),
        compiler_params=_cp("parallel"),
    )(h, h, h, ln_g, ln_b, ws, bs_t)


def _gmlp_bwd(h, dy, ln_g, ln_b, ws, ws_t, bs_t, name):
    t, tb = h.shape[0], 256

    def body(u_ref, v_ref, z_ref, dy_ref, lg_ref, lb_ref, ws_ref, wst_ref, bst_ref,
             dh_ref, dws_ref, dbst_ref, dlg_ref, dlb_ref, dvn_ref):
        @pl.when(pl.program_id(0) == 0)
        def _():
            dws_ref[...] = jnp.zeros_like(dws_ref)
            dbst_ref[...] = jnp.zeros_like(dbst_ref)
            dlg_ref[...] = jnp.zeros_like(dlg_ref)
            dlb_ref[...] = jnp.zeros_like(dlb_ref)

        rs, vh, vn = _gmlp_common(v_ref[...], lg_ref[...], lb_ref[...])
        row, lane = _iota((CHUNK, CHUNK), 0), _iota((CHUNK, CHUNK), 1)
        for g in range(A_GROUPS):
            w = jnp.where(lane <= row, ws_ref[g], 0.0).astype(BF16)
            wt = jnp.where(row <= lane, wst_ref[g], 0.0).astype(BF16)
            cols = slice(g * A_GW, (g + 1) * A_GW)
            dws_acc = jnp.zeros((CHUNK, CHUNK), F32)
            dbs_acc = jnp.zeros((CHUNK, 1), F32)
            for c in range(tb // CHUNK):
                rows = slice(c * CHUNK, (c + 1) * CHUNK)
                vnb = vn[rows, cols]
                mixed = jnp.dot(w, vnb, preferred_element_type=F32) + bst_ref[:, g:g + 1]
                u, z, dyv = u_ref[rows, cols], z_ref[rows, cols], dy_ref[rows, cols]
                sz = _silu(z)
                dh_ref[rows, cols] = (dyv * sz * mixed).astype(BF16)
                dh_ref[rows, slice(2 * WIDTH + g * A_GW, 2 * WIDTH + (g + 1) * A_GW)] = (
                    dyv * (u * mixed) * _dsilu(z)).astype(BF16)
                dm = dyv * sz * u
                dws_acc += _dot_nt(dm, vnb)
                dbs_acc += jnp.sum(dm, axis=1, keepdims=True)
                dvn_ref[rows, cols] = jnp.dot(wt, dm.astype(BF16), preferred_element_type=F32)
            dws_ref[g] += jnp.where(lane <= row, dws_acc, 0.0)
            dbst_ref[...] += jnp.where(lane == g, dbs_acc, 0.0)
        dvn = dvn_ref[...]
        dlg_ref[...] += jnp.sum(dvn * vh, axis=0, keepdims=True)
        dlb_ref[...] += jnp.sum(dvn, axis=0, keepdims=True)
        dvh = dvn * lg_ref[...]
        dv = rs * (dvh - jnp.mean(dvh, axis=-1, keepdims=True) - vh * jnp.mean(dvh * vh, axis=-1, keepdims=True))
        dh_ref[:, WIDTH:2 * WIDTH] = dv.astype(BF16)

    col = lambda j: pl.BlockSpec((tb, WIDTH), lambda i: (i, j))
    full = lambda a: pl.BlockSpec(a.shape, lambda i: (0,) * a.ndim)
    acc = lambda shape: pl.BlockSpec(shape, lambda i: (0,) * len(shape))
    return pl.pallas_call(
        body, grid=(t // tb,), name=name,
        in_specs=[col(0), col(1), col(2), col(0), full(ln_g), full(ln_b), full(ws), full(ws_t), full(bs_t)],
        out_specs=[pl.BlockSpec((tb, 3 * WIDTH), lambda i: (i, 0)), acc((A_GROUPS, CHUNK, CHUNK)),
                   acc((CHUNK, LANES)), acc((1, WIDTH)), acc((1, WIDTH))],
        out_shape=[jax.ShapeDtypeStruct((t, F_EVEN), BF16), jax.ShapeDtypeStruct((A_GROUPS, CHUNK, CHUNK), F32),
                   jax.ShapeDtypeStruct((CHUNK, LANES), F32), jax.ShapeDtypeStruct((1, WIDTH), F32),
                   jax.ShapeDtypeStruct((1, WIDTH), F32)],
        scratch_shapes=[pltpu.VMEM((tb, WIDTH), F32)],
        compiler_params=_cp("arbitrary"),
    )(h, h, h, dy, ln_g, ln_b, ws, ws_t, bs_t)


def _halo_prev(tb, j):
    return lambda i: (jnp.maximum(i * (tb // SUBLANES) - 1, 0), j)


def _halo_next(tb, j, t):
    return lambda i: (jnp.minimum((i + 1) * (tb // SUBLANES), t // SUBLANES - 1), j)


def _row_select(parts, width):
    row = _iota((SUBLANES, width), 0)
    out = jnp.zeros((SUBLANES, width), F32)
    for k, p in enumerate(parts):
        out = jnp.where(row == k, p, out)
    return out


def _sconv_fwd(h, w, name):
    t, tb = h.shape[0], 256

    def body(bg_ref, cg_ref, hx_ref, z_ref, cgh_ref, hxh_ref, w_ref, y_ref):
        p = cg_ref[...] * hx_ref[...]
        ph = jnp.where(pl.program_id(0) > 0, cgh_ref[...] * hxh_ref[...], 0.0)
        cv = w_ref[2:3, :] * p + w_ref[1:2, :] * _shift_down(p, ph, 1) + w_ref[0:1, :] * _shift_down(p, ph, 2)
        y_ref[...] = (_silu(z_ref[...]) * (bg_ref[...] * cv)).astype(BF16)

    col = lambda j: pl.BlockSpec((tb, WIDTH), lambda i: (i, j))
    halo = lambda j: pl.BlockSpec((SUBLANES, WIDTH), _halo_prev(tb, j))
    return pl.pallas_call(
        body, grid=(t // tb,), name=name,
        in_specs=[col(0), col(1), col(2), col(3), halo(1), halo(2), pl.BlockSpec(w.shape, lambda i: (0, 0))],
        out_specs=col(0), out_shape=jax.ShapeDtypeStruct((t, 2 * WIDTH), BF16),
        compiler_params=_cp("parallel"),
    )(h, h, h, h, h, h, w)


def _sconv_bwd(h, dy, w, name):
    t, tb = h.shape[0], 256
    nb = t // tb

    def body(bg_ref, cg_ref, hx_ref, z_ref, dy_ref, cgh_ref, hxh_ref, bgn_ref, zn_ref, dyn_ref, w_ref, dh_ref, dw_ref):
        i = pl.program_id(0)
        bg, cg, hx, z, dyv = bg_ref[...], cg_ref[...], hx_ref[...], z_ref[...], dy_ref[...]
        p = cg * hx
        ph = jnp.where(i > 0, cgh_ref[...] * hxh_ref[...], 0.0)
        p1, p2 = _shift_down(p, ph, 1), _shift_down(p, ph, 2)
        cv = w_ref[2:3, :] * p + w_ref[1:2, :] * p1 + w_ref[0:1, :] * p2
        sz = _silu(z)
        dcv = dyv * sz * bg
        dcvn = jnp.where(i < nb - 1, dyn_ref[...] * _silu(zn_ref[...]) * bgn_ref[...], 0.0)
        dp = w_ref[2:3, :] * dcv + w_ref[1:2, :] * _shift_up(dcv, dcvn, 1) + w_ref[0:1, :] * _shift_up(dcv, dcvn, 2)
        dh_ref[0] = (dyv * sz * cv).astype(BF16)
        dh_ref[1] = (dp * hx).astype(BF16)
        dh_ref[2] = (dp * cg).astype(BF16)
        dh_ref[3] = (dyv * (bg * cv) * _dsilu(z)).astype(BF16)
        part = _row_select([jnp.sum(dcv * q, axis=0, keepdims=True) for q in (p2, p1, p)], WIDTH)

        @pl.when(i == 0)
        def _():
            dw_ref[...] = part

        @pl.when(i > 0)
        def _():
            dw_ref[...] += part

    col = lambda j: pl.BlockSpec((tb, WIDTH), lambda i: (i, j))
    prev = lambda j: pl.BlockSpec((SUBLANES, WIDTH), _halo_prev(tb, j))
    nxt = lambda j: pl.BlockSpec((SUBLANES, WIDTH), _halo_next(tb, j, t))
    return pl.pallas_call(
        body, grid=(nb,), name=name,
        in_specs=[col(0), col(1), col(2), col(3), col(0), prev(1), prev(2), nxt(0), nxt(3), nxt(0),
                  pl.BlockSpec(w.shape, lambda i: (0, 0))],
        out_specs=[pl.BlockSpec((4, tb, WIDTH), lambda i: (0, i, 0)), pl.BlockSpec((SUBLANES, WIDTH), lambda i: (0, 0))],
        out_shape=[jax.ShapeDtypeStruct((8, t, WIDTH), BF16), jax.ShapeDtypeStruct((SUBLANES, WIDTH), F32)],
        compiler_params=_cp("arbitrary"),
    )(h, h, h, h, dy, h, h, h, h, dy, w)


XBC_COL0 = 4 * WIDTH


def _ssd_conv_fwd(h, w, b, name):
    t, tb = h.shape[0], 256

    def body(x_ref, xh_ref, w_ref, b_ref, o_ref):
        xv = x_ref[...]
        xh = jnp.where(pl.program_id(0) > 0, xh_ref[...], 0.0)
        acc = b_ref[...] + w_ref[3:4, :] * xv
        for j in range(1, B_CONV):
            acc = acc + w_ref[B_CONV - 1 - j:B_CONV - j, :] * _shift_down(xv, xh, j)
        o_ref[...] = acc

    cb = XBC_COL0 // B_XBC
    return pl.pallas_call(
        body, grid=(t // tb,), name=name,
        in_specs=[pl.BlockSpec((tb, B_XBC), lambda i: (i, cb)), pl.BlockSpec((SUBLANES, B_XBC), _halo_prev(tb, cb)),
                  pl.BlockSpec(w.shape, lambda i: (0, 0)), pl.BlockSpec(b.shape, lambda i: (0, 0))],
        out_specs=pl.BlockSpec((tb, B_XBC), lambda i: (i, 0)), out_shape=jax.ShapeDtypeStruct((t, B_XBC), F32),
        compiler_params=_cp("parallel"),
    )(h, h, w, b)


def _ssd_conv_bwd(h, dh, dpx, dpb, dpc, w, name):
    t, tb, tc = h.shape[0], 512, 1024
    nb = t // tb
    r8 = tb // SUBLANES
    last8 = t // SUBLANES - 1

    def body(dpx_ref, dpb_ref, dpc_ref, nx_ref, nb_ref, nc_ref, x_ref, xh_ref, w_ref, dh_in, dh_ref, dw_ref, db_ref):
        j, i = pl.program_id(0), pl.program_id(1)
        pick = lambda a, b_, c: jnp.where(j < 2, a[...], jnp.where(j == 2, b_[...], c[...]))
        dp = pick(dpx_ref, dpb_ref, dpc_ref)
        dn = jnp.where(i < nb - 1, pick(nx_ref, nb_ref, nc_ref), 0.0)
        xv = x_ref[...]
        xh = jnp.where(i > 0, xh_ref[...], 0.0)
        dx = w_ref[3:4, :] * dp
        for s in range(1, B_CONV):
            dx = dx + w_ref[B_CONV - 1 - s:B_CONV - s, :] * _shift_up(dp, dn, s)
        dh_ref[...] = dx.astype(BF16)
        wpart = _row_select([jnp.sum(dp * _shift_down(xv, xh, B_CONV - 1 - k), axis=0, keepdims=True)
                             for k in range(B_CONV)], tc)
        bpart = jnp.sum(dp, axis=0, keepdims=True)

        @pl.when(i == 0)
        def _():
            dw_ref[...] = wpart
            db_ref[...] = bpart

        @pl.when(i > 0)
        def _():
            dw_ref[...] += wpart
            db_ref[...] += bpart

    def src(blk_rows, rowf, sel, colf):
        return pl.BlockSpec((blk_rows, tc), lambda j, i: (jnp.where(sel(j), rowf(i), 0), colf(j)))

    cur = lambda i: i
    nxt = lambda i: jnp.minimum((i + 1) * r8, last8)
    is_x, is_b, is_c = (lambda j: j < 2), (lambda j: j == 2), (lambda j: j == 3)
    xcol, zero = (lambda j: jnp.minimum(j, 1)), (lambda j: 0)
    c0 = XBC_COL0 // tc
    return pl.pallas_call(
        body, grid=(B_XBC // tc, nb), name=name,
        in_specs=[src(tb, cur, is_x, xcol), src(tb, cur, is_b, zero), src(tb, cur, is_c, zero),
                  src(SUBLANES, nxt, is_x, xcol), src(SUBLANES, nxt, is_b, zero), src(SUBLANES, nxt, is_c, zero),
                  pl.BlockSpec((tb, tc), lambda j, i: (i, c0 + j)),
                  pl.BlockSpec((SUBLANES, tc), lambda j, i: (jnp.maximum(i * r8 - 1, 0), c0 + j)),
                  pl.BlockSpec((B_CONV, tc), lambda j, i: (0, j)), ANY],
        out_specs=[pl.BlockSpec((tb, tc), lambda j, i: (i, c0 + j)), pl.BlockSpec((SUBLANES, tc), lambda j, i: (0, j)),
                   pl.BlockSpec((1, tc), lambda j, i: (0, j))],
        out_shape=[jax.ShapeDtypeStruct(dh.shape, dh.dtype), jax.ShapeDtypeStruct((SUBLANES, B_XBC), F32),
                   jax.ShapeDtypeStruct((1, B_XBC), F32)],
        input_output_aliases={9: 0},
        compiler_params=_cp("arbitrary", "arbitrary"),
    )(dpx, dpb, dpc, dpx, dpb, dpc, h, h, w, dh)


HEADS_PER_GROUP = B_HEADS // B_GROUPS


def _ssd_dt(hdt, alog, dtb, name):
    t = hdt.shape[0]
    nc = t // CHUNK

    def body(dtr_ref, alog_ref, dtb_ref, dt_ref, acs_ref, acst_ref):
        dt = _softplus(dtr_ref[...] + dtb_ref[...])
        tri = (_iota((CHUNK, CHUNK), 1) <= _iota((CHUNK, CHUNK), 0)).astype(BF16)
        acs = _sel_dot(tri, dt * -jnp.exp(alog_ref[...]))
        dt_ref[...] = dt
        acs_ref[...] = acs
        acst_ref[...] = acs.T

    tok = pl.BlockSpec((CHUNK, LANES), lambda c: (c, 0))
    vec = pl.BlockSpec((1, LANES), lambda c: (0, 0))
    return pl.pallas_call(
        body, grid=(nc,), name=name, in_specs=[tok, vec, vec],
        out_specs=[tok, tok, pl.BlockSpec((None, LANES, CHUNK), lambda c: (c, 0, 0))],
        out_shape=[jax.ShapeDtypeStruct((t, LANES), F32), jax.ShapeDtypeStruct((t, LANES), F32),
                   jax.ShapeDtypeStruct((nc, LANES, CHUNK), F32)],
        compiler_params=_cp("parallel"),
    )(hdt, alog, dtb)


def _to_group(m, g):
    return pltpu.roll(m, (LANES - HEADS_PER_GROUP * g) % LANES, 1)


def _from_group(m, g):
    return pltpu.roll(m, HEADS_PER_GROUP * g, 1)


def _ssd_group_terms(g, px, pb, pc, dt, acs, acst_ref):
    head = _iota((CHUNK, B_GROUP_W), 1) // B_HEAD_DIM

    def spread(m4):
        out = m4[:, HEADS_PER_GROUP - 1:HEADS_PER_GROUP]
        for j in range(HEADS_PER_GROUP - 2, -1, -1):
            out = jnp.where(head == j, m4[:, j:j + 1], out)
        return out

    dt4, a4 = _to_group(dt, g), _to_group(acs, g)
    rows = [acst_ref[pl.ds(HEADS_PER_GROUP * g + j, 1), :] for j in range(HEADS_PER_GROUP)]
    return dict(xs=_silu(px), bm=_silu(pb), cm=_silu(pc), dt4=dt4, a4=a4, rows=rows, dt_e=spread(dt4), a_e=spread(a4))


def _ssd_decay(tm, j, transposed):
    col, row = tm["a4"][:, j:j + 1], tm["rows"][j]
    lane, sub = _iota((CHUNK, CHUNK), 1), _iota((CHUNK, CHUNK), 0)
    if transposed:
        return jnp.where(sub <= lane, jnp.exp(jnp.minimum(row - col, 0.0)), 0.0)
    return jnp.where(lane <= sub, jnp.exp(jnp.minimum(col - row, 0.0)), 0.0)


GROUPS_PER_STEP = 4


def _ssd_specs(nc, rev):
    ch = (lambda c: nc - 1 - c) if rev else (lambda c: c)
    n = GROUPS_PER_STEP
    gw = lambda off: pl.BlockSpec((CHUNK, n * B_GROUP_W), lambda c, g: (ch(c), off // n + g))
    gn = lambda off: pl.BlockSpec((CHUNK, n * B_STATE), lambda c, g: (ch(c), off // n + g))
    tok = pl.BlockSpec((CHUNK, LANES), lambda c, g: (ch(c), 0))
    vec = pl.BlockSpec((1, LANES), lambda c, g: (0, 0))
    gvec = pl.BlockSpec((1, n * B_GROUP_W), lambda c, g: (0, g))
    st = pl.BlockSpec((None, B_STATE, n * B_GROUP_W), lambda c, g: (ch(c), 0, g))
    tokt = pl.BlockSpec((None, LANES, CHUNK), lambda c, g: (ch(c), 0, 0))
    return gw, gn, tok, tokt, vec, gvec, st


def _group_cols(u):
    return slice(u * B_GROUP_W, (u + 1) * B_GROUP_W), slice(u * B_STATE, (u + 1) * B_STATE)


def _ssd_scan_fwd(pre, dt, acs, acst, h, y, dfull, ng, carry, name):
    t = pre.shape[0]
    nc = t // CHUNK
    n_carried = len(carry.arrays) if carry else 0

    def body(*refs):
        px_ref, pb_ref, pc_ref, dt_ref, acs_ref, acst_ref, z_ref, df_ref, ng_ref, _ = refs[:10]
        refs = refs[10:]
        c_in, refs = refs[:n_carried], refs[n_carried:]
        y_ref, ypre_ref, st_ref = refs[:3]
        c_out, state_ref, c_sems = refs[3:3 + n_carried], refs[3 + n_carried], refs[4 + n_carried:]
        c, pair = pl.program_id(0), pl.program_id(1)
        if carry:
            pl.when((c == 0) & (pair == 0))(lambda: carry.start(c_in, c_out, c_sems))
        groups = [GROUPS_PER_STEP * pair + u for u in range(GROUPS_PER_STEP)]

        @pl.when(c == 0)
        def _():
            for g in groups:
                state_ref[g] = jnp.zeros((B_STATE, B_GROUP_W), F32)

        entering, leaving = [state_ref[g] for g in groups], []
        for u, g in enumerate(groups):
            wide, narrow = _group_cols(u)
            tm = _ssd_group_terms(g, px_ref[:, wide], pb_ref[:, narrow], pc_ref[:, narrow], dt_ref[...],
                                  acs_ref[...], acst_ref)
            xs, bm, cm, a_e = tm["xs"], tm["bm"], tm["cm"], tm["a_e"]
            xdt = xs * tm["dt_e"]
            cb = _dot_nt(cm, bm)
            head = _iota((CHUNK, B_GROUP_W), 1) // B_HEAD_DIM
            yd = jnp.zeros((CHUNK, B_GROUP_W), F32)
            for j in range(HEADS_PER_GROUP):
                yd = jnp.where(head == j, _dot(cb * _ssd_decay(tm, j, False), xdt), yd)
            st = entering[u]
            st_ref[:, wide] = st
            yv = yd + jnp.exp(a_e) * _dot(cm, st) + df_ref[:, wide] * xs
            a_last = a_e[CHUNK - 1:CHUNK, :]
            leaving.append(st * jnp.exp(a_last) + _dot_tn(bm, xdt * jnp.exp(a_last - a_e)))
            ypre_ref[:, wide] = yv
            yz = yv * _silu(z_ref[:, wide])
            r = lax.rsqrt(jnp.mean(yz * yz, axis=-1, keepdims=True) + EPS)
            y_ref[:, wide] = (yz * r * ng_ref[:, wide]).astype(BF16)
        for g, st in zip(groups, leaving):
            state_ref[g] = st
        if carry:
            last = (c == nc - 1) & (pair == B_GROUPS // GROUPS_PER_STEP - 1)
            pl.when(last)(lambda: carry.finish(c_in, c_out, c_sems))

    gw, gn, tok, tokt, vec, gvec, st = _ssd_specs(nc, False)
    out = pl.pallas_call(
        body, grid=(nc, B_GROUPS // GROUPS_PER_STEP), name=name,
        in_specs=[gw(0), gn(WIDTH // B_STATE), gn((WIDTH + B_GROUPS * B_STATE) // B_STATE), tok, tok, tokt,
                  gw(3 * WIDTH // B_GROUP_W), gvec, gvec, ANY] + [ANY] * n_carried,
        out_specs=[gw(WIDTH // B_GROUP_W), gw(0), st] + [ANY] * n_carried,
        out_shape=[jax.ShapeDtypeStruct(y.shape, y.dtype), jax.ShapeDtypeStruct((t, WIDTH), F32),
                   jax.ShapeDtypeStruct((nc, B_STATE, WIDTH), F32)] + (carry.out_shape if carry else []),
        scratch_shapes=[pltpu.VMEM((B_GROUPS, B_STATE, B_GROUP_W), F32)] + (carry.scratch if carry else []),
        input_output_aliases={9: 0},
        compiler_params=_cp("arbitrary", "arbitrary"),
    )(pre, pre, pre, dt, acs, acst, h, dfull, ng, y, *(carry.arrays if carry else []))
    return out[:3], out[3:]


def _ssd_scan_bwd(pre, hdt, dt, acs, acst, h, dy, ypre, states, dh, alog, dtb, dfull, ng, carry, name):
    t = pre.shape[0]
    nc = t // CHUNK

    n_carried = len(carry.arrays) if carry else 0

    def one_group(g, dsn, px_ref, pb_ref, pc_ref, dtr_ref, dt_ref, acs_ref, acst_ref, z_ref, dy_ref, ypre_ref, st_ref,
                  alog_ref, dtb_ref, df_ref, ng_ref, dz_ref, dpx_ref, dpb_ref, dpc_ref):
        px, pb, pc, dtr = px_ref[...], pb_ref[...], pc_ref[...], dtr_ref[...]
        tm = _ssd_group_terms(g, px, pb, pc, dt_ref[...], acs_ref[...], acst_ref)
        xs, bm, cm, a_e, dt_e = tm["xs"], tm["bm"], tm["cm"], tm["a_e"], tm["dt_e"]
        xdt = xs * dt_e
        head = _iota((CHUNK, B_GROUP_W), 1) // B_HEAD_DIM

        z, yv, ngv = z_ref[...], ypre_ref[...], ng_ref[...]
        sz = _silu(z)
        yz = yv * sz
        r = lax.rsqrt(jnp.mean(yz * yz, axis=-1, keepdims=True) + EPS)
        dyn = dy_ref[...]
        dng_part = jnp.sum(dyn * yz * r, axis=0, keepdims=True)
        q = dyn * ngv
        dyz = r * q - yz * (r * r * r) * jnp.mean(q * yz, axis=-1, keepdims=True)
        dyv = dyz * sz
        dz_ref[...] = (dyz * yv * _dsilu(z)).astype(BF16)
        dd_part = jnp.sum(dyv * xs, axis=0, keepdims=True)
        dxs = df_ref[...] * dyv

        st = st_ref[...]
        ea = jnp.exp(a_e)
        ead = ea * dyv
        dcm = _dot_nt(ead, st)
        da_e = dyv * (ea * _dot(cm, st))

        a_last = a_e[CHUNK - 1:CHUNK, :]
        ea_last = jnp.exp(a_last)
        dstate = dsn * ea_last + _dot_tn(cm, ead)
        wdec = jnp.exp(a_last - a_e)
        xw = xdt * wdec
        dxw = _dot(bm, dsn)
        dxdt = dxw * wdec
        dbm = _dot_nt(xw, dsn)
        zc = dxw * xw
        da_last = jnp.sum(zc, axis=0, keepdims=True) + jnp.sum(dsn * st, axis=0, keepdims=True) * ea_last
        da_e = da_e - zc + jnp.where(_iota((CHUNK, B_GROUP_W), 0) == CHUNK - 1, da_last, 0.0)

        cb, cbt = _dot_nt(cm, bm), _dot_nt(bm, cm)
        dcb, dcbt = jnp.zeros((CHUNK, CHUNK), F32), jnp.zeros((CHUNK, CHUNK), F32)
        da4 = jnp.zeros((CHUNK, LANES), F32)
        lane = _iota((CHUNK, LANES), 1)
        for j in range(HEADS_PER_GROUP):
            mine = head == j
            gm = _dot_nt(jnp.where(mine, dyv, 0.0), xdt)
            gmt = _dot_nt(jnp.where(mine, xdt, 0.0), dyv)
            dec, dect = _ssd_decay(tm, j, False), _ssd_decay(tm, j, True)
            dcb += gm * dec
            dcbt += gmt * dect
            da_j = (jnp.sum(gm * cb * dec, axis=1, keepdims=True) - jnp.sum(gmt * cbt * dect, axis=1, keepdims=True))
            da4 = jnp.where(lane == j, da_j, da4)
            dxdt = dxdt + jnp.where(mine, _dot(cbt * dect, dyv), 0.0)
        dcm = dcm + _dot(dcb, bm)
        dbm = dbm + _dot(dcbt, cm)

        gather = (_iota((B_GROUP_W, LANES), 0) // B_HEAD_DIM == _iota((B_GROUP_W, LANES), 1)).astype(BF16)
        per_head = _dot_sel(jnp.concatenate([da_e, dxdt * xs], axis=0), gather)
        da4 = da4 + per_head[:CHUNK]
        rtri = (_iota((CHUNK, CHUNK), 1) >= _iota((CHUNK, CHUNK), 0)).astype(BF16)
        dadt4 = _sel_dot(rtri, da4)
        a_heads = -jnp.exp(alog_ref[...])
        rows8 = lambda v: jnp.broadcast_to(v, (SUBLANES, LANES))
        ddt4 = dadt4 * _to_group(rows8(a_heads), g)[0:1, :] + per_head[CHUNK:]
        dxs = dxs + dxdt * dt_e
        ddt = _from_group(ddt4, g) * _sig(dtr + dtb_ref[...])
        da_heads = _from_group(rows8(jnp.sum(dadt4 * tm["dt4"], axis=0, keepdims=True)), g)[0:1, :]

        dpx_ref[...] = dxs * _dsilu(px)
        dpb_ref[...] = dbm * _dsilu(pb)
        dpc_ref[...] = dcm * _dsilu(pc)
        return ddt, da_heads * a_heads, dstate, dd_part, dng_part

    def body(*refs):
        (px_ref, pb_ref, pc_ref, dtr_ref, dt_ref, acs_ref, acst_ref, z_ref, dy_ref, ypre_ref, st_ref, alog_ref,
         dtb_ref, df_ref, ng_ref, _) = refs[:16]
        refs = refs[16:]
        c_in, refs = refs[:n_carried], refs[n_carried:]
        dz_ref, dpx_ref, dpb_ref, dpc_ref, ddt_ref, dbias_ref, dalog_ref, dd_ref, dng_ref = refs[:9]
        c_out, dstate_ref, c_sems = refs[9:9 + n_carried], refs[9 + n_carried], refs[10 + n_carried:]
        c, pair = pl.program_id(0), pl.program_id(1)
        if carry:
            pl.when((c == 0) & (pair == 0))(lambda: carry.start(c_in, c_out, c_sems))
        groups = [GROUPS_PER_STEP * pair + u for u in range(GROUPS_PER_STEP)]

        @pl.when(c == 0)
        def _():
            for g in groups:
                dstate_ref[g] = jnp.zeros((B_STATE, B_GROUP_W), F32)
                dd_ref[g] = jnp.zeros((1, B_GROUP_W), F32)
                dng_ref[g] = jnp.zeros((1, B_GROUP_W), F32)

        leaving = [dstate_ref[g] for g in groups]
        ddt, dalog, done = 0.0, 0.0, []
        for u, g in enumerate(groups):
            wide, narrow = _group_cols(u)
            view = lambda ref, cols: ref.at[:, cols]
            ddt_u, dalog_u, *carried = one_group(
                g, leaving[u], view(px_ref, wide), view(pb_ref, narrow), view(pc_ref, narrow), dtr_ref,
                dt_ref, acs_ref, acst_ref, view(z_ref, wide), view(dy_ref, wide), view(ypre_ref, wide), view(st_ref, wide),
                alog_ref, dtb_ref, view(df_ref, wide), view(ng_ref, wide), view(dz_ref, wide), view(dpx_ref, wide),
                view(dpb_ref, narrow), view(dpc_ref, narrow))
            ddt, dalog = ddt + ddt_u, dalog + dalog_u
            done.append(carried)
        for g, (dstate, dd_part, dng_part) in zip(groups, done):
            dstate_ref[g] = dstate
            dd_ref[g] += dd_part
            dng_ref[g] += dng_part
        first = (c == 0) & (pair == 0)
        bias_part = jnp.sum(ddt, axis=0, keepdims=True)

        @pl.when(pair == 0)
        def _():
            ddt_ref[...] = ddt

        @pl.when(pair > 0)
        def _():
            ddt_ref[...] += ddt

        @pl.when(first)
        def _():
            dbias_ref[...] = bias_part
            dalog_ref[...] = dalog

        @pl.when(jnp.logical_not(first))
        def _():
            dbias_ref[...] += bias_part
            dalog_ref[...] += dalog

        if carry:
            last = (c == nc - 1) & (pair == B_GROUPS // GROUPS_PER_STEP - 1)
            pl.when(last)(lambda: carry.finish(c_in, c_out, c_sems))

    gw, gn, tok, tokt, vec, gvec, st = _ssd_specs(nc, True)
    acc = lambda shape: pl.BlockSpec(shape, lambda c, g: (0,) * len(shape))
    rev = lambda c: nc - 1 - c
    out = pl.pallas_call(
        body, grid=(nc, B_GROUPS // GROUPS_PER_STEP), name=name,
        in_specs=[gw(0), gn(WIDTH // B_STATE), gn((WIDTH + B_GROUPS * B_STATE) // B_STATE), tok, tok, tok, tokt,
                  gw(3 * WIDTH // B_GROUP_W), gw(WIDTH // B_GROUP_W), gw(0), st, vec, vec, gvec, gvec, ANY]
        + [ANY] * n_carried,
        out_specs=[gw(3 * WIDTH // B_GROUP_W), gw(0), gn(0), gn(0), tok, vec, vec, acc((B_GROUPS, 1, B_GROUP_W)), acc((B_GROUPS, 1, B_GROUP_W))] + [ANY] * n_carried,
        out_shape=[jax.ShapeDtypeStruct(dh.shape, dh.dtype), jax.ShapeDtypeStruct((t, WIDTH), F32),
                   jax.ShapeDtypeStruct((t, B_GROUPS * B_STATE), F32), jax.ShapeDtypeStruct((t, B_GROUPS * B_STATE), F32),
                   jax.ShapeDtypeStruct((t, LANES), F32), jax.ShapeDtypeStruct((1, LANES), F32),
                   jax.ShapeDtypeStruct((1, LANES), F32), jax.ShapeDtypeStruct((B_GROUPS, 1, B_GROUP_W), F32),
                   jax.ShapeDtypeStruct((B_GROUPS, 1, B_GROUP_W), F32)] + (carry.out_shape if carry else []),
        scratch_shapes=[pltpu.VMEM((B_GROUPS, B_STATE, B_GROUP_W), F32)] + (carry.scratch if carry else []),
        input_output_aliases={15: 0},
        compiler_params=_cp("arbitrary", "arbitrary"),
    )(pre, pre, pre, hdt, dt, acs, acst, h, dy, ypre, states, alog, dtb, dfull, ng, dh,
      *(carry.arrays if carry else []))
    return out[:9], out[9:]


Q_COL, K_COL, V_COL, Z_COL = [(4 + i) * WIDTH // D_HEAD_DIM for i in range(4)]
ATT_SCALE = D_HEAD_DIM ** -0.5


SPAN = 2048


def _att_blocks():
    out = []
    for _, dil in D_PATTERNS:
        nbl = SPAN // (CHUNK * dil)
        for r in range(dil):
            for bl in range(nbl):
                st = r + dil * CHUNK * bl
                out.append((dil, st, bl > 0, st - dil * CHUNK if bl > 0 else r + dil * CHUNK * (nbl - 1)))
    return out


def _rows(start, dil):
    return pl.ds(start, CHUNK) if dil == 1 else pl.ds(start, CHUNK, stride=dil)


def _att_keys(kc_ref, kp_ref, blk):
    dil, st, inside, pst = blk
    prev = (kc_ref if inside else kp_ref)[_rows(pst, dil), :]
    return jnp.concatenate([prev, kc_ref[_rows(st, dil), :]], axis=0).astype(BF16)


def _att_band(span_index):
    lane, sub = _iota((CHUNK, 2 * CHUNK), 1), _iota((CHUNK, 2 * CHUNK), 0)
    band = (lane >= sub) & (lane <= sub + CHUNK)
    return band, band & ((lane >= CHUNK) | (span_index > 0))


def _att_specs(t):
    blk = lambda off: pl.BlockSpec((SPAN, D_HEAD_DIM), lambda hd, sb: (sb, off + hd))
    prev = lambda off: pl.BlockSpec((SPAN, D_HEAD_DIM), lambda hd, sb: (jnp.maximum(sb - 1, 0), off + hd))
    lse = pl.BlockSpec((None, SPAN // CHUNK, CHUNK), lambda hd, sb: (hd, sb, 0))
    return blk, prev, lse


def _attn2_fwd(h, y, name):
    t = h.shape[0]

    def body(q_ref, kc_ref, vc_ref, kp_ref, vp_ref, z_ref, y_in, y_ref, o_ref, lse_ref, m_ref, l_ref):
        band, band_first = _att_band(pl.program_id(1))
        for blk in _att_blocks():
            dil, st, inside, _ = blk
            rows = _rows(st, dil)
            k2, v2 = _att_keys(kc_ref, kp_ref, blk), _att_keys(vc_ref, vp_ref, blk)
            s = jnp.where(band if inside else band_first, _dot_nt(q_ref[rows, :], k2) * ATT_SCALE, NEG)
            m_b = jnp.max(s, axis=1, keepdims=True)
            p = jnp.exp(s - m_b)
            l_b = jnp.sum(p, axis=1, keepdims=True)
            o_b = _dot(p, v2)
            wide = lambda a: jnp.broadcast_to(a, (CHUNK, D_HEAD_DIM))
            if dil == 1:
                m_ref[rows, :], l_ref[rows, :], o_ref[rows, :] = wide(m_b), wide(l_b), o_b
            else:
                m_o = m_ref[rows, :]
                m_n = jnp.maximum(m_o, m_b)
                a_o, a_b = jnp.exp(m_o - m_n), jnp.exp(m_b - m_n)
                m_ref[rows, :] = m_n
                l_ref[rows, :] = a_o * l_ref[rows, :] + a_b * l_b
                o_ref[rows, :] = a_o * o_ref[rows, :] + a_b * o_b
        l = l_ref[...]
        o = o_ref[...] / l
        o_ref[...] = o
        y_ref[...] = (_silu(z_ref[...]) * o).astype(BF16)
        for i in range(SPAN // CHUNK):
            blk_rows = slice(i * CHUNK, (i + 1) * CHUNK)
            lse_ref[i:i + 1, :] = (m_ref[blk_rows, :] + jnp.log(l[blk_rows, :])).T[0:1, :]

    blk, prev, lse_spec = _att_specs(t)
    return pl.pallas_call(
        body, grid=(D_HEADS, t // SPAN), name=name,
        in_specs=[blk(Q_COL), blk(K_COL), blk(V_COL), prev(K_COL), prev(V_COL), blk(Z_COL), ANY],
        out_specs=[blk(WIDTH // D_HEAD_DIM), blk(0), lse_spec],
        out_shape=[jax.ShapeDtypeStruct(y.shape, y.dtype), jax.ShapeDtypeStruct((t, WIDTH), F32),
                   jax.ShapeDtypeStruct((D_HEADS, t // CHUNK, CHUNK), F32)],
        scratch_shapes=[pltpu.VMEM((SPAN, D_HEAD_DIM), F32)] * 2,
        input_output_aliases={6: 0},
        compiler_params=_cp("parallel", "parallel"),
    )(h, h, h, h, h, h, y)


def _attn2_bwd(h, dy, o, lse, dh, name):
    t = h.shape[0]
    ns = t // SPAN

    def body(q_ref, kc_ref, vc_ref, kp_ref, vp_ref, z_ref, dy_ref, o_ref, lse_ref, dh_in, dh_ref,
             acc_ref, dq_ref, do_ref, delta_ref, lsec_ref):
        sb = pl.program_id(1)

        @pl.when(sb == 0)
        def _():
            acc_ref[...] = jnp.zeros_like(acc_ref)

        here, before = pl.multiple_of(sb * SPAN, SPAN), jnp.maximum(sb - 1, 0) * SPAN
        z, ov, dyv = z_ref[...], o_ref[...], dy_ref[...]
        do = dyv * _silu(z)
        do_ref[...] = do
        dh_ref[3, pl.ds(here, SPAN), :] = (dyv * ov * _dsilu(z)).astype(BF16)
        delta_ref[...] = jnp.broadcast_to(jnp.sum(do * ov, axis=1, keepdims=True), (SPAN, D_HEAD_DIM))
        for i in range(SPAN // CHUNK):
            lsec_ref[i * CHUNK:(i + 1) * CHUNK, :] = jnp.broadcast_to(lse_ref[i:i + 1, :], (CHUNK, CHUNK)).T
        band, band_first = _att_band(sb)
        for blk in _att_blocks():
            dil, st, inside, pst = blk
            rows = _rows(st, dil)
            q = q_ref[rows, :].astype(BF16)
            k2, v2 = _att_keys(kc_ref, kp_ref, blk), _att_keys(vc_ref, vp_ref, blk)
            dob = do_ref[rows, :].astype(BF16)
            s = jnp.where(band if inside else band_first, _dot_nt(q, k2) * ATT_SCALE, NEG)
            p = jnp.exp(s - lsec_ref[rows, :][:, 0:1])
            ds = p * (_dot_nt(dob, v2) - delta_ref[rows, :][:, 0:1]) * ATT_SCALE
            dq_b = _dot(ds, k2)
            if dil == 1:
                dq_ref[rows, :] = dq_b
            else:
                dq_ref[rows, :] += dq_b
            dk2, dv2 = _dot_tn(ds, q), _dot_tn(p, dob)
            own = _rows(pl.multiple_of(here + st, CHUNK) if dil == 1 else here + st, dil)
            pbase = (here if inside else before) + pst
            prv = _rows(pl.multiple_of(pbase, CHUNK) if dil == 1 else pbase, dil)
            acc_ref[0, own, :] += dk2[CHUNK:]
            acc_ref[1, own, :] += dv2[CHUNK:]
            acc_ref[0, prv, :] += dk2[:CHUNK]
            acc_ref[1, prv, :] += dv2[:CHUNK]
        dh_ref[0, pl.ds(here, SPAN), :] = dq_ref[...].astype(BF16)

        @pl.when(sb == ns - 1)
        def _():
            dh_ref[1] = acc_ref[0].astype(BF16)
            dh_ref[2] = acc_ref[1].astype(BF16)

    blk, prev, lse_spec = _att_specs(t)
    span = lambda: pltpu.VMEM((SPAN, D_HEAD_DIM), F32)
    return pl.pallas_call(
        body, grid=(D_HEADS, ns), name=name,
        in_specs=[blk(Q_COL), blk(K_COL), blk(V_COL), prev(K_COL), prev(V_COL), blk(Z_COL), blk(WIDTH // D_HEAD_DIM),
                  blk(0), lse_spec, ANY],
        out_specs=pl.BlockSpec((4, t, D_HEAD_DIM), lambda hd, sb: (1, 0, hd)),
        out_shape=jax.ShapeDtypeStruct(dh.shape, dh.dtype),
        scratch_shapes=[pltpu.VMEM((2, t, D_HEAD_DIM), F32), span(), span(), span(), span()],
        input_output_aliases={9: 0},
        compiler_params=_cp("arbitrary", "arbitrary"),
    )(h, h, h, h, h, h, dy, o, lse, dh)


def _place():
    x, y, c = lax.axis_index("x"), lax.axis_index("y"), lax.axis_index("c")
    return x, y, c, 4 * x + 2 * y + c


class _Exchange:
    def __init__(self, arrays, out_shape):
        n = len(arrays)
        self.arrays, self.out_shape = list(arrays), out_shape
        self.scratch = [pltpu.SemaphoreType.DMA((n, 7)), pltpu.SemaphoreType.DMA((n, 7)), pltpu.SemaphoreType.DMA((n,))]


class _AllGather(_Exchange):
    def __init__(self, blocks):
        super().__init__(blocks, [jax.ShapeDtypeStruct((N_DEV,) + b.shape, b.dtype) for b in blocks])

    def _plan(self, ins, outs, sems):
        send_sems, recv_sems, local_sems = sems
        x, y, c, me = _place()
        chips = [(1 - x, y), (x, 1 - y), (1 - x, 1 - y)]

        def copy(a, k, block, to, src=None):
            dst = outs[a].at[block]
            return pltpu.make_async_remote_copy(
                src_ref=dst if src is None else src, dst_ref=dst, send_sem=send_sems.at[a, k],
                recv_sem=recv_sems.at[a, k], device_id=to, device_id_type=MESH)

        n = len(ins)
        index = lambda px, py, pc: 4 * px + 2 * py + pc
        mine = [pltpu.make_async_copy(ins[a], outs[a].at[me], local_sems.at[a]) for a in range(n)]
        first = [copy(a, 0, me, (x, y, 1 - c), src=ins[a]) for a in range(n)]
        first += [copy(a, 1 + j, me, (*chip, c), src=ins[a]) for j, chip in enumerate(chips) for a in range(n)]
        arrive = lambda a, k, px, py, pc: copy(a, k, index(px, py, pc), (x, y, c))
        over_ici = [[arrive(a, 1 + j, *chip, c) for a in range(n)] for j, chip in enumerate(chips)]
        passed = [[copy(a, 4 + j, index(*chip, c), (x, y, 1 - c)) for a in range(n)] for j, chip in enumerate(chips)]
        from_sibling = [arrive(a, 0, x, y, 1 - c) for a in range(n)]
        from_sibling += [arrive(a, 4 + j, *chip, 1 - c) for j, chip in enumerate(chips) for a in range(n)]
        return mine, first, over_ici, passed, from_sibling

    def start(self, ins, outs, sems):
        mine, first, _, _, _ = self._plan(ins, outs, sems)
        for cp in mine + first:
            cp.start()

    def finish(self, ins, outs, sems):
        mine, first, over_ici, passed, from_sibling = self._plan(ins, outs, sems)
        for landed, onward in zip(over_ici, passed):
            for cp, fwd in zip(landed, onward):
                cp.wait_recv()
                fwd.start()
        for cp in from_sibling:
            cp.wait_recv()
        for cp in first + [fwd for onward in passed for fwd in onward]:
            cp.wait_send()
        for cp in mine:
            cp.wait()


class _AllToAll(_Exchange):
    def __init__(self, parts):
        super().__init__(parts, [jax.ShapeDtypeStruct(p.shape, p.dtype) for p in parts])

    def _plan(self, ins, outs, sems):
        send_sems, recv_sems, local_sems = sems
        x, y, c, me = _place()
        flip = lambda v, f: 1 - v if f else v
        peers = [(flip(x, fx), flip(y, fy), flip(c, fc)) for fx in (0, 1) for fy in (0, 1) for fc in (0, 1)][1:]

        def copy(a, k, sending):
            px, py, pc = peers[k]
            there = 4 * px + 2 * py + pc
            return pltpu.make_async_remote_copy(
                src_ref=ins[a].at[there], dst_ref=outs[a].at[me if sending else there], send_sem=send_sems.at[a, k],
                recv_sem=recv_sems.at[a, k], device_id=peers[k], device_id_type=MESH)

        n = len(ins)
        local = [pltpu.make_async_copy(ins[a].at[me], outs[a].at[me], local_sems.at[a]) for a in range(n)]
        return local, [[copy(a, k, sending) for k in range(7) for a in range(n)] for sending in (True, False)]

    def start(self, ins, outs, sems):
        local, (sends, _) = self._plan(ins, outs, sems)
        for cp in local + sends:
            cp.start()

    def finish(self, ins, outs, sems):
        local, (_, both_ways) = self._plan(ins, outs, sems)
        for cp in both_ways + local:
            cp.wait()


def _exchange(ex, name):
    n = len(ex.arrays)

    def body(*refs):
        ins, outs, sems = refs[:n], refs[n:2 * n], refs[2 * n:]
        ex.start(ins, outs, sems)
        ex.finish(ins, outs, sems)

    return pl.pallas_call(body, name=name, in_specs=[ANY] * n, out_specs=[ANY] * n, out_shape=ex.out_shape,
                          scratch_shapes=ex.scratch)(*ex.arrays)


def _sum_parts(parts, tc, name):
    _, r, c = parts.shape

    def body(p_ref, o_ref):
        acc = p_ref[0].astype(F32)
        for d in range(1, N_DEV):
            acc = acc + p_ref[d].astype(F32)
        o_ref[...] = acc

    return pl.pallas_call(
        body, grid=(c // tc,), name=name,
        in_specs=[pl.BlockSpec((N_DEV, r, tc), lambda j: (0, 0, j))], out_specs=pl.BlockSpec((r, tc), lambda j: (0, j)),
        out_shape=jax.ShapeDtypeStruct((r, c), F32), compiler_params=_cp("parallel"),
    )(parts)


def _adamw(w, g, m, v, tr, name):
    r, c = w.shape

    def body(w_ref, g_ref, m_ref, v_ref, d_ref, m2_ref, v2_ref):
        gv = g_ref[...]
        m2 = ADAM_B1 * m_ref[...] + (1.0 - ADAM_B1) * gv
        v2 = ADAM_B2 * v_ref[...] + (1.0 - ADAM_B2) * (gv * gv)
        m_hat = m2 / (1.0 - ADAM_B1 ** ADAM_STEP)
        v_hat = v2 / (1.0 - ADAM_B2 ** ADAM_STEP)
        d_ref[...] = -ADAM_LR * (m_hat / (jnp.sqrt(v_hat) + ADAM_EPS) + ADAM_WD * w_ref[...])
        m2_ref[...] = m2
        v2_ref[...] = v2

    spec = pl.BlockSpec((tr, c), lambda i: (i, 0))
    return pl.pallas_call(
        body, grid=(r // tr,), name=name, in_specs=[spec] * 4, out_specs=[spec] * 3,
        out_shape=[jax.ShapeDtypeStruct((r, c), F32)] * 3, compiler_params=_cp("parallel"),
    )(w, g, m, v)


def _sum_adamw(parts, w, m, v, tr, name):
    r, c = w.shape

    def body(p_ref, w_ref, m_ref, v_ref, g_ref, d_ref, m2_ref, v2_ref):
        gv = p_ref[0].astype(F32)
        for d in range(1, N_DEV):
            gv = gv + p_ref[d].astype(F32)
        m2 = ADAM_B1 * m_ref[...] + (1.0 - ADAM_B1) * gv
        v2 = ADAM_B2 * v_ref[...] + (1.0 - ADAM_B2) * (gv * gv)
        m_hat = m2 / (1.0 - ADAM_B1 ** ADAM_STEP)
        v_hat = v2 / (1.0 - ADAM_B2 ** ADAM_STEP)
        g_ref[...] = gv
        d_ref[...] = -ADAM_LR * (m_hat / (jnp.sqrt(v_hat) + ADAM_EPS) + ADAM_WD * w_ref[...])
        m2_ref[...] = m2
        v2_ref[...] = v2

    spec = pl.BlockSpec((tr, c), lambda i: (i, 0))
    return pl.pallas_call(
        body, grid=(r // tr,), name=name,
        in_specs=[pl.BlockSpec((N_DEV, tr, c), lambda i: (0, i, 0))] + [spec] * 3, out_specs=[spec] * 4,
        out_shape=[jax.ShapeDtypeStruct((r, c), F32)] * 4, compiler_params=_cp("parallel"),
    )(parts, w, m, v)


PACK_ROWS = SUBLANES * LANES


def _pack(arrays):
    flat = [jnp.pad(a.reshape(-1), (0, -a.size % PACK_ROWS)) for a in arrays]
    return jnp.concatenate(flat).reshape(-1, LANES)


def _unpack(packed, shapes):
    flat, out, pos = packed.reshape(-1), [], 0
    for s in shapes:
        size = 1
        for d in s:
            size *= d
        out.append(flat[pos:pos + size].reshape(s))
        pos += size + (-size % PACK_ROWS)
    return out


SMALL = ["even_norm_g", "gmlp_ln_g", "gmlp_ln_b", "gmlp_ws", "gmlp_bs", "ssd_conv_w", "ssd_conv_b", "ssd_dt_bias",
         "ssd_a_log", "ssd_d", "ssd_norm_g", "odd_norm_g", "sconv_w", "final_norm_g"]
ORDER = ["even_norm_g", "even_w_in", "gmlp_ln_g", "gmlp_ln_b", "gmlp_ws", "gmlp_bs", "ssd_conv_w", "ssd_conv_b",
         "ssd_dt_bias", "ssd_a_log", "ssd_d", "ssd_norm_g", "even_w_out", "odd_norm_g", "odd_w_in", "sconv_w",
         "odd_w_out", "final_norm_g"]


def kernel(x, even_norm_g, even_w_in, gmlp_ln_g, gmlp_ln_b, gmlp_ws, gmlp_bs, ssd_conv_w, ssd_conv_b, ssd_dt_bias, ssd_a_log, ssd_d, ssd_norm_g, even_w_out, odd_norm_g, odd_w_in, sconv_w, odd_w_out, final_norm_g, loss_target, m_even_norm_g, m_even_w_in, m_gmlp_ln_g, m_gmlp_ln_b, m_gmlp_ws, m_gmlp_bs, m_ssd_conv_w, m_ssd_conv_b, m_ssd_dt_bias, m_ssd_a_log, m_ssd_d, m_ssd_norm_g, m_even_w_out, m_odd_norm_g, m_odd_w_in, m_sconv_w, m_odd_w_out, m_final_norm_g, v_even_norm_g, v_even_w_in, v_gmlp_ln_g, v_gmlp_ln_b, v_gmlp_ws, v_gmlp_bs, v_ssd_conv_w, v_ssd_conv_b, v_ssd_dt_bias, v_ssd_a_log, v_ssd_d, v_ssd_norm_g, v_even_w_out, v_odd_norm_g, v_odd_w_in, v_sconv_w, v_odd_w_out, v_final_norm_g):
    w = dict(even_norm_g=even_norm_g, even_w_in=even_w_in, gmlp_ln_g=gmlp_ln_g, gmlp_ln_b=gmlp_ln_b, gmlp_ws=gmlp_ws,
             gmlp_bs=gmlp_bs, ssd_conv_w=ssd_conv_w, ssd_conv_b=ssd_conv_b, ssd_dt_bias=ssd_dt_bias,
             ssd_a_log=ssd_a_log, ssd_d=ssd_d, ssd_norm_g=ssd_norm_g, even_w_out=even_w_out, odd_norm_g=odd_norm_g,
             odd_w_in=odd_w_in, sconv_w=sconv_w, odd_w_out=odd_w_out, final_norm_g=final_norm_g)
    m1 = dict(even_norm_g=m_even_norm_g, even_w_in=m_even_w_in, gmlp_ln_g=m_gmlp_ln_g, gmlp_ln_b=m_gmlp_ln_b,
              gmlp_ws=m_gmlp_ws, gmlp_bs=m_gmlp_bs, ssd_conv_w=m_ssd_conv_w, ssd_conv_b=m_ssd_conv_b,
              ssd_dt_bias=m_ssd_dt_bias, ssd_a_log=m_ssd_a_log, ssd_d=m_ssd_d, ssd_norm_g=m_ssd_norm_g,
              even_w_out=m_even_w_out, odd_norm_g=m_odd_norm_g, odd_w_in=m_odd_w_in, sconv_w=m_sconv_w,
              odd_w_out=m_odd_w_out, final_norm_g=m_final_norm_g)
    m2 = dict(even_norm_g=v_even_norm_g, even_w_in=v_even_w_in, gmlp_ln_g=v_gmlp_ln_g, gmlp_ln_b=v_gmlp_ln_b,
              gmlp_ws=v_gmlp_ws, gmlp_bs=v_gmlp_bs, ssd_conv_w=v_ssd_conv_w, ssd_conv_b=v_ssd_conv_b,
              ssd_dt_bias=v_ssd_dt_bias, ssd_a_log=v_ssd_a_log, ssd_d=v_ssd_d, ssd_norm_g=v_ssd_norm_g,
              even_w_out=v_even_w_out, odd_norm_g=v_odd_norm_g, odd_w_in=v_odd_w_in, sconv_w=v_sconv_w,
              odd_w_out=v_odd_w_out, final_norm_g=v_final_norm_g)
    _, _, _, me = _place()
    xs = x[0]
    shard = WIDTH // N_DEV

    small_blk = jnp.concatenate([
        ssd_conv_w[0], jnp.pad(sconv_w[0], ((0, 0), (0, shard))), jnp.pad(odd_norm_g, ((0, 0), (0, shard)))], axis=0)
    g_wte, g_small = _exchange(_AllGather([even_w_in[0].T.astype(BF16), small_blk]), "gather_even_w_in")
    next_weights = _AllGather([odd_w_in[0].T.astype(BF16)])
    out_weights = _AllGather([even_w_out[0].astype(BF16), odd_w_out[0].astype(BF16)])
    wte = g_wte.reshape(F_EVEN_ALL, D_MODEL)
    wte_dt = jnp.pad(wte[F_EVEN:], ((0, LANES - B_HEADS), (0, 0)))
    conv_w = g_small[:, 0:B_CONV, :].transpose(1, 0, 2).reshape(B_CONV, B_XBC)
    sconv_full = g_small[:, B_CONV:B_CONV + C_CONV, :shard].transpose(1, 0, 2).reshape(C_CONV, WIDTH)
    odd_g = g_small[:, B_CONV + C_CONV, :shard].reshape(1, WIDTH)

    pad_heads = lambda a: jnp.pad(a, ((0, 0), (0, LANES - B_HEADS)))
    alog, dtb = pad_heads(ssd_a_log), pad_heads(ssd_dt_bias)
    d_full = jnp.repeat(ssd_d, B_HEAD_DIM, axis=1)
    ws, bs_t = gmlp_ws[0], gmlp_bs[0].T
    ws_t = jnp.swapaxes(ws, 1, 2)
    proj = dict(tb=True, tm=1024, tn=1024, tk=D_MODEL, out_dtype=F32)
    out_proj = dict(n=D_MODEL, tm=1024, tn=1024, tk=2 * WIDTH, out_dtype=F32)
    dw_out = dict(ta=True, n=D_MODEL, tm=1024, tn=D_MODEL, tk=2048, out_dtype=BF16)
    dx_in = dict(n=D_MODEL, tm=1024, tn=1024, tk=2048, out_dtype=F32)
    dw_in = dict(ta=True, n=D_MODEL, tm=1024, tn=D_MODEL, tk=2048, out_dtype=BF16)
    slabs = lambda g, rows: g.reshape(N_DEV, rows // N_DEV, D_MODEL)

    xn0 = _rms_fwd(xs, even_norm_g, "norm_even")
    h0, (g_wto,) = _mm(xn0, wte, n=F_EVEN, carry=next_weights, name="proj_even", **proj)
    wto = g_wto.reshape(F_ODD, D_MODEL)
    hdt = _mm(xn0, wte_dt, tb=True, n=LANES, tm=1024, tn=LANES, tk=D_MODEL, out_dtype=F32, name="proj_dt")
    y0 = _gmlp_fwd(h0, gmlp_ln_g, gmlp_ln_b, ws, bs_t, "gmlp_fwd")
    pre = _ssd_conv_fwd(h0, conv_w, ssd_conv_b, "ssd_conv_fwd")
    dt, acs, acst = _ssd_dt(hdt, alog, dtb, "ssd_dt")
    (y0, ypre, states), (g_woe, g_woo) = _ssd_scan_fwd(
        pre, dt, acs, acst, h0, y0, d_full, ssd_norm_g, out_weights, "ssd_scan_fwd")
    woe, woo = g_woe.reshape(2 * WIDTH, D_MODEL), g_woo.reshape(2 * WIDTH, D_MODEL)
    x1 = _mm(y0, woe, add=xs, name="out_even", **out_proj)
    xn1 = _rms_fwd(x1, odd_g, "norm_odd")
    h1 = _mm(xn1, wto, n=F_ODD, name="proj_odd", **proj)
    y1 = _sconv_fwd(h1, sconv_full, "sconv_fwd")
    y1, att_o, att_lse = _attn2_fwd(h1, y1, "attn_fwd")
    x2 = _mm(y1, woo, add=x1, name="out_odd", **out_proj)
    loss_part, dx2, g_final, dx2_b = _loss_head(x2, final_norm_g.reshape(1, D_MODEL), loss_target[0], "loss_head")

    dy1 = _mm(dx2_b, woo, n=2 * WIDTH, name="dy_odd", **proj)
    gw_woo = _mm(y1, dx2_b, name="dw_out_odd", **dw_out)
    dh1, g_sconv = _sconv_bwd(h1, dy1, sconv_full, "sconv_bwd")
    dh1 = _attn2_bwd(h1, dy1, att_o, att_lse, dh1, "attn_bwd")
    dxn1, (r_woo,) = _mm(dh1, wto, carry=_AllToAll([slabs(gw_woo, 2 * WIDTH)]), name="dx_odd", **dx_in)
    gw_wto = _mm(dh1, xn1, name="dw_in_odd", **dw_in)
    dx1, g_odd, dx1_b = _rms_bwd(x1, odd_g, dxn1, dx2, True, "norm_odd_bwd")

    dy0 = _mm(dx1_b, woe, n=2 * WIDTH, name="dy_even", **proj)
    gw_woe = _mm(y0, dx1_b, name="dw_out_even", **dw_out)
    dh0, g_ws, g_bs_t, g_ln_g, g_ln_b = _gmlp_bwd(h0, dy0, gmlp_ln_g, gmlp_ln_b, ws, ws_t, bs_t, "gmlp_bwd")
    (dh0, dpx, dpb, dpc, ddt, g_dtb, g_alog, g_dd, g_ng), (r_wto,) = _ssd_scan_bwd(
        pre, hdt, dt, acs, acst, h0, dy0, ypre, states, dh0, alog, dtb, d_full, ssd_norm_g,
        _AllToAll([slabs(gw_wto, F_ODD)]), "ssd_scan_bwd")
    dh0, g_conv_w, g_conv_b = _ssd_conv_bwd(h0, dh0, dpx, dpb, dpc, conv_w, "ssd_conv_bwd")
    gw_main, (r_woe,) = _mm(dh0, xn0, out_rows=F_EVEN_ALL, carry=_AllToAll([slabs(gw_woe, 2 * WIDTH)]),
                            name="dw_in_even", **dw_in)
    gw_dt = _mm(ddt, xn0, ta=True, n=D_MODEL, tm=LANES, tn=D_MODEL, tk=1024, out_dtype=BF16, name="dw_dt")
    gw_wte = lax.dynamic_update_slice(gw_main, gw_dt[:B_HEADS], (F_EVEN, 0))
    dxn0_dt = _mm(ddt, wte_dt, n=D_MODEL, tm=1024, tn=D_MODEL, tk=LANES, out_dtype=F32, name="dx_dt")
    dxn0, (r_wte,) = _mm(dh0, wte, add=dxn0_dt, carry=_AllToAll([slabs(gw_wte, F_EVEN_ALL)]), name="dx_even", **dx_in)
    grad_x, g_even = _rms_bwd(xs, even_norm_g, dxn0, dx1, False, "norm_even_bwd")

    small_parts = dict(
        even_norm_g=g_even, gmlp_ln_g=g_ln_g, gmlp_ln_b=g_ln_b, gmlp_ws=g_ws, gmlp_bs=g_bs_t[:, :A_GROUPS].T,
        ssd_conv_w=g_conv_w[:B_CONV], ssd_conv_b=g_conv_b, ssd_dt_bias=g_dtb[:, :B_HEADS], ssd_a_log=g_alog[:, :B_HEADS],
        ssd_d=g_dd.reshape(B_HEADS, B_HEAD_DIM).sum(axis=1), ssd_norm_g=g_ng, odd_norm_g=g_odd,
        sconv_w=g_sconv[:C_CONV], final_norm_g=g_final)
    full_shapes = dict(
        even_norm_g=(1, D_MODEL), gmlp_ln_g=(1, WIDTH), gmlp_ln_b=(1, WIDTH), gmlp_ws=(1, A_GROUPS, CHUNK, CHUNK),
        gmlp_bs=(1, A_GROUPS, CHUNK), ssd_conv_w=(1, B_CONV, B_XBC), ssd_conv_b=(1, B_XBC), ssd_dt_bias=(1, B_HEADS),
        ssd_a_log=(1, B_HEADS), ssd_d=(1, B_HEADS), ssd_norm_g=(1, WIDTH), odd_norm_g=(1, D_MODEL),
        sconv_w=(1, C_CONV, WIDTH), final_norm_g=(D_MODEL,))
    (gathered_small,) = _exchange(_AllGather([_pack([small_parts[k] for k in SMALL])]), "gather_small_grads")
    small_sum = _sum_parts(gathered_small, LANES, "sum_small_grads")
    grads = dict(zip(SMALL, _unpack(small_sum, [full_shapes[k] for k in SMALL])))
    grads["ssd_conv_w"] = lax.dynamic_slice_in_dim(grads["ssd_conv_w"], me * 2 * shard, 2 * shard, axis=2)
    grads["odd_norm_g"] = lax.dynamic_slice_in_dim(grads["odd_norm_g"], me * shard, shard, axis=1)
    grads["sconv_w"] = lax.dynamic_slice_in_dim(grads["sconv_w"], me * shard, shard, axis=2)

    grads["even_w_in"] = _sum_parts(r_wte, 256, "sum_even_w_in").T[None]
    grads["odd_w_in"] = _sum_parts(r_wto, 256, "sum_odd_w_in").T[None]

    delta, new_m, new_v = {}, {}, {}
    for k, received in (("even_w_out", r_woe), ("odd_w_out", r_woo)):
        g_k, d_k, m_k, v_k = _sum_adamw(received, w[k][0], m1[k][0], m2[k][0], 128, "sum_adamw_" + k)
        grads[k], delta[k], new_m[k], new_v[k] = g_k[None], d_k[None], m_k[None], v_k[None]
    for k in ("even_w_in", "odd_w_in"):
        d_k, m_k, v_k = _adamw(w[k][0], grads[k][0], m1[k][0], m2[k][0], 128, "adamw_" + k)
        delta[k], new_m[k], new_v[k] = d_k[None], m_k[None], v_k[None]
    packed = [_pack([src[k] for k in SMALL]) for src in (w, grads, m1, m2)]
    small_out = _adamw(*packed, packed[0].shape[0], "adamw_small")
    shapes = [w[k].shape for k in SMALL]
    for dst, arr in zip((delta, new_m, new_v), small_out):
        dst.update(zip(SMALL, _unpack(arr, shapes)))

    loss = lax.psum(loss_part[0, 0], ("x", "y", "c"))
    return (loss, grad_x[None], *[grads[k] for k in ORDER], *[delta[k] for k in ORDER],
            *[new_m[k] for k in ORDER], *[new_v[k] for k in ORDER])
```
